```python
import math
import jax
import jax.numpy as jnp
from jax import lax
import numpy as np

D_MODEL = 1024
BATCH = 16
SEQ = 4096
DEPTH = 2

CHUNK = 64
EPS = 1e-6
NEG_INF = -1e30
Q_BLOCK = 128
N_MIXERS = 4
GROUP_W = D_MODEL // N_MIXERS
D_MIX = N_MIXERS * GROUP_W
HEAD_DIM = 64
POOL_WINDOWS = (2, 4, 8, 16)
N_POOL_GROUPS = len(POOL_WINDOWS)
POOL_GROUP = GROUP_W // N_POOL_GROUPS
CA_HEADS = GROUP_W // HEAD_DIM
CA_LEFT_CHUNKS = 8
CA_MAX_REL = 256
CA_REL_SIZE = CHUNK + CA_MAX_REL
SA_HEADS = GROUP_W // HEAD_DIM
IDX_HEADS = 8
IDX_DIM = 64
TOPK_MAX = 256
MLA_HEADS = 4
MLA_NOPE = 64
MLA_ROPE = 32
MLA_V = GROUP_W // MLA_HEADS
Q_LORA = 256
KV_LORA = 128
ROPE_BASE = 10000.0
T5_BUCKETS = 32
T5_MAX_DIST = 128
D_FF = -(-8 * D_MODEL // (3 * 256)) * 256

IN_SPLITS = (
    ('pool_u', GROUP_W),
    ('ca_q', GROUP_W), ('ca_k', GROUP_W), ('ca_v', GROUP_W),
    ('sa_q', GROUP_W), ('sa_k', HEAD_DIM), ('sa_v', HEAD_DIM),
    ('idx_q', IDX_HEADS * IDX_DIM), ('idx_k', IDX_DIM), ('idx_w', IDX_HEADS),
    ('mla_cq', Q_LORA), ('mla_ckv', KV_LORA), ('mla_kr', MLA_ROPE),
)
IN_NAMES = tuple(n for n, _ in IN_SPLITS)
IN_OFFSETS = tuple(int(o) for o in np.cumsum([w for _, w in IN_SPLITS])[:-1])
IN_WIDTH = sum(w for _, w in IN_SPLITS)

kernel_name = 'hybrid_parallel_head_group_streaming_encoder'


def rmsnorm(x, g):
    xf = x.astype(jnp.float32)
    y = xf * lax.rsqrt(jnp.mean(xf * xf, axis=-1, keepdims=True) + EPS)
    return (y * g.astype(jnp.float32)).astype(x.dtype)


def group_rmsnorm(y, g):
    B, S, W = y.shape
    yg = y.reshape(B, S, N_MIXERS, W // N_MIXERS).astype(jnp.float32)
    yg = yg * lax.rsqrt(jnp.mean(yg * yg, axis=-1, keepdims=True) + EPS)
    return (yg.reshape(B, S, W) * g.astype(jnp.float32)).astype(y.dtype)


def rope(x, pos):
    half = x.shape[-1] // 2
    freqs = ROPE_BASE ** (-jnp.arange(half, dtype=jnp.float32) / half)
    ang = pos.astype(jnp.float32)[:, None] * freqs[None, :]
    ang = ang.reshape((1, x.shape[1]) + (1,) * (x.ndim - 3) + (half,))
    cos, sin = jnp.cos(ang), jnp.sin(ang)
    x1 = x[..., :half].astype(jnp.float32)
    x2 = x[..., half:].astype(jnp.float32)
    return jnp.concatenate([x1 * cos - x2 * sin, x1 * sin + x2 * cos], axis=-1).astype(x.dtype)


def t5_bucket(rel):
    nb = T5_BUCKETS // 2
    max_exact = nb // 2
    ret = jnp.where(rel > 0, nb, 0)
    n = jnp.abs(rel)
    nf = jnp.maximum(n, 1).astype(jnp.float32)
    large = max_exact + (jnp.log(nf / max_exact) / math.log(T5_MAX_DIST / max_exact)
                         * (nb - max_exact)).astype(jnp.int32)
    large = jnp.minimum(large, nb - 1)
    return ret + jnp.where(n < max_exact, n, large)


def pool_mixer(u, w_grp, scale):
    B, S, _ = u.shape
    ug = u.reshape(B, S, N_POOL_GROUPS, POOL_GROUP).astype(jnp.float32)
    cs = jnp.cumsum(ug, axis=1, dtype=jnp.float32)
    t = jnp.arange(S)
    means = []
    for gi, w in enumerate(POOL_WINDOWS):
        c = cs[:, :, gi]
        lag = jnp.pad(c, ((0, 0), (w, 0), (0, 0)))[:, :S]
        cnt = jnp.minimum(t + 1, w).astype(jnp.float32)[None, :, None]
        means.append((c - lag) / cnt)
    d = (jnp.stack(means, axis=2) - ug).astype(u.dtype)
    y = jnp.einsum('bsgc,gcd->bsgd', d, w_grp).reshape(B, S, GROUP_W)
    return y * scale


def chunk_band_attention(q, k, v, rel_table):
    B, S, H, Dh = q.shape
    NC = S // CHUNK
    NB = CA_LEFT_CHUNKS + 1
    qc = q.reshape(B, NC, CHUNK, H, Dh)

    def band(a):
        ac = a.reshape(B, NC, CHUNK, H, Dh)
        ap = jnp.pad(ac, ((0, 0), (CA_LEFT_CHUNKS, 0), (0, 0), (0, 0), (0, 0)))
        return jnp.concatenate([ap[:, j:j + NC] for j in range(NB)], axis=2)

    kb, vb = band(k), band(v)
    s = jnp.einsum('bnqhd,bnkhd->bnhqk', qc, kb).astype(jnp.float32) * (Dh ** -0.5)
    qi = jnp.arange(CHUNK)
    kj = jnp.arange(NB * CHUNK)
    dist = (CA_LEFT_CHUNKS * CHUNK + qi[:, None]) - kj[None, :]
    ridx = jnp.clip(dist, -(CHUNK - 1), CA_MAX_REL) + (CHUNK - 1)
    bias = rel_table[:, ridx].astype(jnp.float32)
    key_chunk = jnp.arange(NC)[:, None] + (kj // CHUNK)[None, :] - CA_LEFT_CHUNKS
    valid = key_chunk >= 0
    s = jnp.where(valid[None, :, None, None, :], s + bias[None, None], NEG_INF)
    p = jax.nn.softmax(s, axis=-1).astype(v.dtype)
    o = jnp.einsum('bnhqk,bnkhd->bnqhd', p, vb)
    return o.reshape(B, S, H * Dh)


def dsa_attention(q, k, v, iq, ik, iw, t5_table):
    B, S, H, Dh = q.shape
    topk = min(TOPK_MAX, S // 4)
    n_blocks = S // Q_BLOCK
    kpos = jnp.arange(S, dtype=jnp.int32)
    iw = iw.astype(jnp.float32) * ((IDX_HEADS ** -0.5) * (IDX_DIM ** -0.5))
    gather = jax.vmap(lambda a, i: a[i])

    def block(bi):
        t0 = bi * Q_BLOCK
        qb = lax.dynamic_slice_in_dim(q, t0, Q_BLOCK, axis=1)
        iqb = lax.dynamic_slice_in_dim(iq, t0, Q_BLOCK, axis=1)
        iwb = lax.dynamic_slice_in_dim(iw, t0, Q_BLOCK, axis=1)
        qpos = t0 + jnp.arange(Q_BLOCK, dtype=jnp.int32)
        logits = jnp.einsum('bthd,bsd->bths', iqb, ik).astype(jnp.float32)
        score = jnp.einsum('bth,bths->bts', iwb, jax.nn.relu(logits))
        adm = (kpos[None, :] // CHUNK) <= (qpos[:, None] // CHUNK)
        score = jnp.where(adm[None], score, -jnp.inf)
        _, sel = lax.top_k(score, topk)
        kg = gather(k, sel)
        vg = gather(v, sel)
        s = jnp.einsum('bthd,btkd->bhtk', qb, kg).astype(jnp.float32) * (Dh ** -0.5)
        bias = t5_table[t5_bucket(sel - qpos[None, :, None])].astype(jnp.float32)
        s = s + jnp.moveaxis(bias, -1, 1)
        valid = (sel // CHUNK) <= (qpos[None, :, None] // CHUNK)
        s = jnp.where(valid[:, None], s, NEG_INF)
        p = jax.nn.softmax(s, axis=-1).astype(vg.dtype)
        return jnp.einsum('bhtk,btkd->bthd', p, vg)

    out = lax.map(block, jnp.arange(n_blocks))
    return jnp.moveaxis(out, 0, 1).reshape(B, S, H * Dh)


def mla_attention(cq, ckv, kr, g_cq, g_ckv, w_uq, w_ukv, pos):
    B, S, _ = cq.shape
    q = jnp.einsum('bsr,rhd->bshd', rmsnorm(cq, g_cq), w_uq)
    q_nope = q[..., :MLA_NOPE]
    q_rope = rope(q[..., MLA_NOPE:], pos)
    kv = jnp.einsum('bsr,rhd->bshd', rmsnorm(ckv, g_ckv), w_ukv)
    k_nope = kv[..., :MLA_NOPE]
    v = kv[..., MLA_NOPE:]
    k_rope = rope(kr, pos)
    scale = (MLA_NOPE + MLA_ROPE) ** -0.5
    kchunk = jnp.arange(S) // CHUNK

    def block(bi):
        t0 = bi * Q_BLOCK
        qn = lax.dynamic_slice_in_dim(q_nope, t0, Q_BLOCK, axis=1)
        qr = lax.dynamic_slice_in_dim(q_rope, t0, Q_BLOCK, axis=1)
        s = (jnp.einsum('bthd,bshd->bhts', qn, k_nope)
             + jnp.einsum('bthd,bsd->bhts', qr, k_rope)).astype(jnp.float32) * scale
        qchunk = (t0 + jnp.arange(Q_BLOCK)) // CHUNK
        mask = kchunk[None, :] <= qchunk[:, None]
        s = jnp.where(mask, s, NEG_INF)
        p = jax.nn.softmax(s, axis=-1).astype(v.dtype)
        return jnp.einsum('bhts,bshd->bthd', p, v)

    out = lax.map(block, jnp.arange(S // Q_BLOCK))
    return jnp.moveaxis(out, 0, 1).reshape(B, S, MLA_HEADS * MLA_V)


def setup_inputs(seed: int = 0) -> dict:
    key = jax.random.key(seed)
    ks = jax.random.split(key, 24)
    f32 = jnp.float32

    def nrm(k, shape, s):
        return jax.random.normal(k, shape, f32) * s

    def gain(k, shape):
        return 1.0 + 0.05 * jax.random.normal(k, shape, f32)

    L = DEPTH
    return {
        'x': nrm(ks[0], (BATCH, SEQ, D_MODEL), 1.0),
        'c': nrm(ks[1], (BATCH, D_MODEL), 1.0),
        't5_table': nrm(ks[2], (T5_BUCKETS, SA_HEADS), 0.5),
        'w_mod': nrm(ks[3], (L, D_MODEL, 6 * D_MODEL), 0.5 * D_MODEL ** -0.5),
        'b_mod': nrm(ks[4], (L, 6 * D_MODEL), 0.02),
        'g_mix': gain(ks[5], (L, D_MODEL)),
        'w_in': nrm(ks[6], (L, D_MODEL, IN_WIDTH), D_MODEL ** -0.5),
        'pool_w': nrm(ks[7], (L, N_POOL_GROUPS, POOL_GROUP, POOL_GROUP), POOL_GROUP ** -0.5),
        'pool_scale': 1.0 + 0.1 * jax.random.normal(ks[8], (L, GROUP_W), f32),
        'ca_rel': nrm(ks[9], (L, CA_HEADS, CA_REL_SIZE), 0.5),
        'mla_g_cq': gain(ks[10], (L, Q_LORA)),
        'mla_g_ckv': gain(ks[11], (L, KV_LORA)),
        'mla_w_uq': nrm(ks[12], (L, Q_LORA, MLA_HEADS, MLA_NOPE + MLA_ROPE), Q_LORA ** -0.5),
        'mla_w_ukv': nrm(ks[13], (L, KV_LORA, MLA_HEADS, MLA_NOPE + MLA_V), KV_LORA ** -0.5),
        'g_group': gain(ks[14], (L, D_MIX)),
        'w_out': nrm(ks[15], (L, D_MIX, D_MODEL), D_MIX ** -0.5),
        'g_ffn': gain(ks[16], (L, D_MODEL)),
        'ffn_w1': nrm(ks[17], (L, D_MODEL, D_FF), D_MODEL ** -0.5),
        'ffn_w3': nrm(ks[18], (L, D_MODEL, D_FF), D_MODEL ** -0.5),
        'ffn_w2': nrm(ks[19], (L, D_FF, D_MODEL), D_FF ** -0.5),
        'g_final': gain(ks[20], (D_MODEL,)),
    }


def reference(x, c, t5_table, w_mod, b_mod, g_mix, w_in, pool_w, pool_scale, ca_rel,
              mla_g_cq, mla_g_ckv, mla_w_uq, mla_w_ukv, g_group, w_out,
              g_ffn, ffn_w1, ffn_w3, ffn_w2, g_final):
    B, S, D = x.shape
    pos = jnp.arange(S, dtype=jnp.int32)
    c_act = jax.nn.silu(c)
    for l in range(DEPTH):
        mod = c_act @ w_mod[l] + b_mod[l]
        sh1, sc1, gt1, sh2, sc2, gt2 = jnp.split(mod[:, None, :], 6, axis=-1)
        h = rmsnorm(x, g_mix[l]) * (1.0 + sc1) + sh1
        z = h @ w_in[l]
        parts = dict(zip(IN_NAMES, jnp.split(z, IN_OFFSETS, axis=-1)))
        y_a = pool_mixer(parts['pool_u'], pool_w[l], pool_scale[l])
        y_b = chunk_band_attention(
            parts['ca_q'].reshape(B, S, CA_HEADS, HEAD_DIM),
            parts['ca_k'].reshape(B, S, CA_HEADS, HEAD_DIM),
            parts['ca_v'].reshape(B, S, CA_HEADS, HEAD_DIM),
            ca_rel[l])
        y_c = dsa_attention(
            parts['sa_q'].reshape(B, S, SA_HEADS, HEAD_DIM),
            parts['sa_k'], parts['sa_v'],
            parts['idx_q'].reshape(B, S, IDX_HEADS, IDX_DIM),
            parts['idx_k'], parts['idx_w'], t5_table)
        y_d = mla_attention(parts['mla_cq'], parts['mla_ckv'], parts['mla_kr'],
                            mla_g_cq[l], mla_g_ckv[l], mla_w_uq[l], mla_w_ukv[l], pos)
        mix = group_rmsnorm(jnp.concatenate([y_a, y_b, y_c, y_d], axis=-1), g_group[l])
        x = x + gt1 * (mix @ w_out[l])
        h = rmsnorm(x, g_ffn[l]) * (1.0 + sc2) + sh2
        f = (jax.nn.silu(h @ ffn_w1[l]) * (h @ ffn_w3[l])) @ ffn_w2[l]
        x = x + gt2 * f
    return rmsnorm(x, g_final)
```

```python
import functools
import math

import jax
import jax.numpy as jnp
from jax import lax
import numpy as np
from jax.experimental import pallas as pl
from jax.experimental.pallas import tpu as pltpu

F32 = jnp.float32
BF16 = jnp.bfloat16

D_MODEL = 1024
DEPTH = 2
CHUNK = 64
EPS = 1e-6
NEG_INF = -1e30
GROUP_W = 256
HEAD_DIM = 64
POOL_WINDOWS = (2, 4, 8, 16)
POOL_GROUP = 64
POOL_HALO = 16
CA_HEADS = 4
CA_LEFT_CHUNKS = 8
CA_MAX_REL = 256
SA_HEADS = 4
IDX_HEADS = 8
IDX_DIM = 64
TOPK_MAX = 256
MLA_HEADS = 4
MLA_NOPE = 64
MLA_ROPE = 32
MLA_V = 64
Q_LORA = 256
KV_LORA = 128
ROPE_BASE = 10000.0
T5_BUCKETS = 32
T5_MAX_DIST = 128
D_FF = 2816
IN_WIDTHS = (256, 256, 256, 256, 256, 64, 64, 512, 64, 8, 256, 128, 32)
IN_OFFS = tuple(int(v) for v in np.cumsum((0,) + IN_WIDTHS))

LANES = 128
VMEM_LIMIT = 56 * 1024 * 1024

TM_PROJ = 512
TM_FFN = 512
TP_POOL = 512
TQ_CA = 128
CA_WIN = TQ_CA + CA_LEFT_CHUNKS * CHUNK
TQ_SA = 256
KB_SA = 256
TQ_MLA = 256
FF_CHUNK = 256

C_POOL = 0
C_CA = C_POOL + 256
C_SAQ = C_CA + 768
C_SAKV = C_SAQ + 512
C_IQ = C_SAKV + 128
C_IK = C_IQ + 512
C_IW = C_IK + 256
C_CQ = C_IW + 128
C_CKV = C_CQ + 256
C_KRF = C_CKV + 128
C_KRS = C_KRF + 128
C_END = C_KRS + 128

INT_MIN = -2 ** 31
KEY_ALL = INT_MIN - int(np.array(-np.inf, np.float32).view(np.int32)) + 1


def _cparams(n_axes):
    return pltpu.CompilerParams(dimension_semantics=("arbitrary",) * n_axes,
                                vmem_limit_bytes=VMEM_LIMIT)


def _rms(x, g):
    return x * lax.rsqrt(jnp.mean(x * x, axis=-1, keepdims=True) + EPS) * g


def _dot(a, b):
    return jnp.dot(a, b, preferred_element_type=F32)


def _dot_t(a, b):
    return lax.dot_general(a, b, (((1,), (1,)), ((), ())), preferred_element_type=F32)


def _mod_kernel(c_ref, w_ref, b_ref, o_ref):
    c = c_ref[...]
    act = c * jax.nn.sigmoid(c)
    o_ref[0] = jnp.dot(act, w_ref[0], precision=lax.Precision.HIGHEST,
                       preferred_element_type=F32) + b_ref[0]


def _modulation(c, w_mod, b_mod):
    L, D, W = w_mod.shape
    B = c.shape[0]
    nj = W // D
    return pl.pallas_call(
        _mod_kernel,
        grid=(L, nj),
        in_specs=[pl.BlockSpec((B, D), lambda l, j: (0, 0)),
                  pl.BlockSpec((1, D, D), lambda l, j: (l, 0, j)),
                  pl.BlockSpec((1, 1, D), lambda l, j: (l, 0, j))],
        out_specs=pl.BlockSpec((1, B, D), lambda l, j: (l, 0, j)),
        out_shape=jax.ShapeDtypeStruct((L, B, W), F32),
        compiler_params=_cparams(2),
        name="modulation",
    )(c, w_mod, b_mod.reshape(L, 1, W))


def _inproj_kernel(x_ref, mod_ref, gmix_ref, w_ref, gcq_ref, gckv_ref, wq_ref, wqs_ref, wkv_ref,
                   cosq_ref, sinq_ref, cosk_ref, sink_ref,
                   pool_o, ca_o, saq_o, sakv_o, iq_o, ik_o, iw_o, mq_o, mk_o, mv_o):
    sh1 = mod_ref[0, 0:1, :]
    sc1 = mod_ref[0, 1:2, :]
    h = (_rms(x_ref[...], gmix_ref[...]) * (1.0 + sc1) + sh1).astype(BF16)

    def seg(a, b):
        return _dot(h, w_ref[:, a:b])

    pool_o[...] = seg(C_POOL, C_CA)
    ca_o[...] = seg(C_CA, C_SAQ).astype(BF16)
    saq_o[...] = seg(C_SAQ, C_SAKV).astype(BF16)
    sakv_o[...] = seg(C_SAKV, C_IQ).astype(BF16)
    iq_o[...] = seg(C_IQ, C_IK).astype(BF16)
    ik_o[...] = seg(C_IK, C_IW).astype(BF16)
    iw_o[...] = seg(C_IW, C_CQ) * ((IDX_HEADS ** -0.5) * (IDX_DIM ** -0.5))

    qn = _rms(seg(C_CQ, C_CKV), gcq_ref[...]).astype(BF16)
    qf = _dot(qn, wq_ref[...])
    qs = _dot(qn, wqs_ref[...])
    cosq = jnp.concatenate([cosq_ref[...]] * MLA_HEADS, axis=1)
    sinq = jnp.concatenate([sinq_ref[...]] * MLA_HEADS, axis=1)
    mq_o[...] = (qf * cosq + qs * sinq).astype(BF16)

    kvn = _rms(seg(C_CKV, C_KRF), gckv_ref[...]).astype(BF16)
    kvf = _dot(kvn, wkv_ref[...])
    krope = seg(C_KRF, C_KRS) * cosk_ref[...] + seg(C_KRS, C_END) * sink_ref[...]
    for hd in range(MLA_HEADS):
        mk_o[:, hd * LANES:(hd + 1) * LANES] = (kvf[:, hd * LANES:(hd + 1) * LANES] + krope).astype(BF16)
    mv_o[...] = kvf[:, MLA_HEADS * LANES:].astype(BF16)


def _inproj(x2, mod, gmix, w1, gcq, gckv, wq, wqs, wkv, rope_tabs, S):
    N, D = x2.shape
    TM = TM_PROJ
    nt = S // TM
    cosq, sinq, cosk, sink = rope_tabs

    def full(a):
        return pl.BlockSpec(a.shape, lambda i: (0,) * a.ndim)

    def tok(w):
        return pl.BlockSpec((TM, w), lambda i: (i, 0))

    tab = pl.BlockSpec((TM, LANES), lambda i: (i % nt, 0))
    outs = [(256, F32), (768, BF16), (512, BF16), (128, BF16), (512, BF16), (256, BF16),
            (128, F32), (512, BF16), (512, BF16), (256, BF16)]
    return pl.pallas_call(
        _inproj_kernel,
        grid=(N // TM,),
        in_specs=[tok(D),
                  pl.BlockSpec((1, 6, D), lambda i: (i // nt, 0, 0)),
                  full(gmix), full(w1), full(gcq), full(gckv), full(wq), full(wqs), full(wkv),
                  tab, tab, tab, tab],
        out_specs=[tok(w) for w, _ in outs],
        out_shape=[jax.ShapeDtypeStruct((N, w), dt) for w, dt in outs],
        compiler_params=_cparams(1),
        name="inproj",
    )(x2, mod, gmix, w1, gcq, gckv, wq, wqs, wkv, cosq, sinq, cosk, sink)


def _pool_kernel(u_ref, halo_ref, w_ref, scale_ref, g_ref, o_ref, pad_scr):
    i = pl.program_id(1)
    TP = u_ref.shape[1]
    u = u_ref[0]
    pad_scr[0:POOL_HALO, :] = jnp.where(i > 0, halo_ref[0], 0.0)
    pad_scr[POOL_HALO:, :] = u

    def shifted(j):
        return pad_scr[POOL_HALO - j:POOL_HALO - j + TP, :]

    lane = lax.broadcasted_iota(jnp.int32, (TP, GROUP_W), 1)
    w2 = u + shifted(1)
    w4 = w2 + shifted(2) + shifted(3)
    w8 = w4
    for j in range(4, 8):
        w8 = w8 + shifted(j)
    w16 = w8
    for j in range(8, 16):
        w16 = w16 + shifted(j)
    win = jnp.where(lane < 64, w2, jnp.where(lane < 128, w4, jnp.where(lane < 192, w8, w16)))
    wlen = jnp.where(lane < 64, 2, jnp.where(lane < 128, 4, jnp.where(lane < 192, 8, 16)))
    t = i * TP + lax.broadcasted_iota(jnp.int32, (TP, GROUP_W), 0)
    cnt = jnp.minimum(t + 1, wlen).astype(F32)
    d = (win / cnt - u).astype(BF16)
    y = _dot(d, w_ref[...]) * scale_ref[...]
    o_ref[0] = _rms(y, g_ref[...]).astype(BF16)


def _pool(u, wbd, scale, g):
    B, S, W = u.shape
    TP = TP_POOL
    hb = TP // POOL_HALO
    return pl.pallas_call(
        _pool_kernel,
        grid=(B, S // TP),
        in_specs=[pl.BlockSpec((1, TP, W), lambda b, i: (b, i, 0)),
                  pl.BlockSpec((1, POOL_HALO, W), lambda b, i: (b, jnp.maximum(i * hb - 1, 0), 0)),
                  pl.BlockSpec((W, W), lambda b, i: (0, 0)),
                  pl.BlockSpec((1, W), lambda b, i: (0, 0)),
                  pl.BlockSpec((1, W), lambda b, i: (0, 0))],
        out_specs=pl.BlockSpec((1, TP, W), lambda b, i: (b, i, 0)),
        out_shape=jax.ShapeDtypeStruct((B, S, W), BF16),
        scratch_shapes=[pltpu.VMEM((POOL_HALO + TP, W), F32)],
        compiler_params=_cparams(2),
        name="pool_mixer",
    )(u, u, wbd, scale, g)


def _ca_kernel(q_ref, k_ref, v_ref, bias_ref, g_ref, o_ref, y_scr):
    i = pl.program_id(1)
    TQ = TQ_CA
    nblk = CA_WIN // TQ
    lane = lax.broadcasted_iota(jnp.int32, (TQ, LANES), 1)
    for pair in range(CA_HEADS // 2):
        cols = slice(pair * LANES, (pair + 1) * LANES)
        qp = q_ref[0, :, cols].astype(F32)
        outs = []
        for e in range(2):
            hd = 2 * pair + e
            keep = (lane < HEAD_DIM) if e == 0 else (lane >= HEAD_DIM)
            qh = jnp.where(keep, qp, 0.0).astype(BF16)
            parts = []
            for j in range(nblk):
                kb = i - (nblk - 1) + j
                start = pl.multiple_of(jnp.maximum(kb, 0) * TQ, TQ)
                s = _dot_t(qh, k_ref[0, pl.ds(start, TQ), cols]) + bias_ref[hd, :, j * TQ:(j + 1) * TQ]
                parts.append(jnp.where(kb >= 0, s, NEG_INF))
            m = parts[0].max(axis=1, keepdims=True)
            for s in parts[1:]:
                m = jnp.maximum(m, s.max(axis=1, keepdims=True))
            l = jnp.zeros((TQ, 1), F32)
            acc = jnp.zeros((TQ, LANES), F32)
            for j in range(nblk):
                kb = i - (nblk - 1) + j
                start = pl.multiple_of(jnp.maximum(kb, 0) * TQ, TQ)
                p = jnp.exp(parts[j] - m)
                l = l + p.sum(axis=1, keepdims=True)
                acc = acc + _dot(p.astype(BF16), v_ref[0, pl.ds(start, TQ), cols])
            outs.append(acc / l)
        y_scr[:, cols] = jnp.where(lane < HEAD_DIM, outs[0], outs[1])
    o_ref[0] = _rms(y_scr[...], g_ref[...]).astype(BF16)


def _chunk_attention(ca, bias, g):
    B, S, _ = ca.shape
    W = GROUP_W
    TQ = TQ_CA
    return pl.pallas_call(
        _ca_kernel,
        grid=(B, S // TQ),
        in_specs=[pl.BlockSpec((1, TQ, W), lambda b, i: (b, i, 0)),
                  pl.BlockSpec((1, S, W), lambda b, i: (b, 0, 1)),
                  pl.BlockSpec((1, S, W), lambda b, i: (b, 0, 2)),
                  pl.BlockSpec(bias.shape, lambda b, i: (0, 0, 0)),
                  pl.BlockSpec((1, W), lambda b, i: (0, 0))],
        out_specs=pl.BlockSpec((1, TQ, W), lambda b, i: (b, i, 0)),
        out_shape=jax.ShapeDtypeStruct((B, S, W), BF16),
        scratch_shapes=[pltpu.VMEM((TQ, W), F32)],
        compiler_params=_cparams(2),
        name="band_attention",
    )(ca, ca, ca, bias, g)


def _score_key(score):
    b = lax.bitcast_convert_type(score, jnp.int32)
    return jnp.where(b < 0, jnp.int32(INT_MIN) - b, b)


def _sa_kernel(q_ref, kv_ref, iq_ref, ik_ref, iw_ref, nbias_ref, fbias_ref, g_ref, o_ref, key_scr, y_scr):
    i = pl.program_id(1)
    TQ, KB = TQ_SA, KB_SA
    K = float(TOPK_MAX)
    nb = i + 1
    q0 = i * TQ
    qchunk = (q0 + lax.broadcasted_iota(jnp.int32, (TQ, KB), 0)) // CHUNK
    kcol = lax.broadcasted_iota(jnp.int32, (TQ, KB), 1)

    iw = iw_ref[0]
    wcols = [jnp.broadcast_to(iw[:, hd:hd + 1], (TQ, KB)) for hd in range(IDX_HEADS)]
    iqp = [iq_ref[0, :, p * LANES:(p + 1) * LANES] for p in range(IDX_HEADS // 2)]

    def score_block(j, carry):
        k0 = pl.multiple_of(j * KB, KB)
        ik = ik_ref[0, pl.ds(k0, KB), :]
        ik_even, ik_odd = ik[:, :LANES], ik[:, LANES:]
        sc = jnp.zeros((TQ, KB), F32)
        for p in range(IDX_HEADS // 2):
            sc = sc + wcols[2 * p] * jnp.maximum(_dot_t(iqp[p], ik_even), 0.0)
            sc = sc + wcols[2 * p + 1] * jnp.maximum(_dot_t(iqp[p], ik_odd), 0.0)
        adm = (k0 + kcol) // CHUNK <= qchunk
        key_scr[:, pl.ds(k0, KB)] = _score_key(jnp.where(adm, sc, -jnp.inf))
        return carry

    lax.fori_loop(0, nb, score_block, 0)

    def count_ge(cand):
        def body(j, acc):
            blk = key_scr[:, pl.ds(pl.multiple_of(j * KB, KB), KB)]
            for c in range(KB // LANES):
                acc = acc + jnp.where(blk[:, c * LANES:(c + 1) * LANES] >= cand, 1.0, 0.0)
            return acc
        acc = lax.fori_loop(0, nb, body, jnp.zeros((TQ, LANES), F32))
        return acc.sum(axis=1, keepdims=True)

    def search():
        def cond(st):
            bit, _, cnt_t = st
            return jnp.logical_and(bit >= 0, jnp.max(cnt_t) > K)

        def body(st):
            bit, t, cnt_t = st
            cand = t + jnp.left_shift(jnp.int32(1), bit)
            cnt = count_ge(cand)
            ge = cnt >= K
            return bit - 1, jnp.where(ge, cand, t), jnp.where(ge, cnt, cnt_t)

        t0 = jnp.full((TQ, 1), INT_MIN, jnp.int32)
        c0 = jnp.full((TQ, 1), 1.0, F32) * (nb * KB).astype(F32)
        _, t, cnt_t = lax.while_loop(cond, body, (jnp.int32(31), t0, c0))
        return t, cnt_t

    def no_search():
        return jnp.full((TQ, 1), KEY_ALL, jnp.int32), jnp.full((TQ, 1), K, F32)

    t, cnt_t = lax.cond(i > 0, search, no_search)
    t = jnp.maximum(t, KEY_ALL)

    @pl.when(jnp.max(cnt_t) > K)
    def _():
        allowed = K - count_ge(t + 1)
        r = lax.broadcasted_iota(jnp.int32, (LANES, LANES), 0)
        c = lax.broadcasted_iota(jnp.int32, (LANES, LANES), 1)
        upper = jnp.where(r < c, 1.0, 0.0).astype(BF16)

        def body(j, seen):
            sl = pl.ds(pl.multiple_of(j * LANES, LANES), LANES)
            blk = key_scr[:, sl]
            eq = jnp.where(blk == t, 1.0, 0.0)
            rank = _dot(eq.astype(BF16), upper) + seen
            demote = eq * jnp.where(rank >= allowed, 1.0, 0.0)
            key_scr[:, sl] = jnp.where(demote > 0.5, t - 1, blk)
            return seen + eq.sum(axis=1, keepdims=True)

        lax.fori_loop(0, nb * (KB // LANES), body, jnp.zeros((TQ, 1), F32))

    qs = [q_ref[0, :, hd * LANES:(hd + 1) * LANES] for hd in range(SA_HEADS)]

    def attend(k0, thr, bias_of, st):
        kv = kv_ref[0, pl.ds(k0, KB), :]
        sel = key_scr[:, pl.ds(k0, KB)] >= thr
        out = []
        for hd in range(SA_HEADS):
            m, l, acc = st[hd]
            s = jnp.where(sel, _dot_t(qs[hd], kv) + bias_of(hd), NEG_INF)
            mn = jnp.maximum(m, s.max(axis=1, keepdims=True))
            p = jnp.where(sel, jnp.exp(s - mn), 0.0)
            alpha = jnp.exp(m - mn)
            out.append((mn, alpha * l + p.sum(axis=1, keepdims=True),
                        alpha * acc + _dot(p.astype(BF16), kv)))
        return tuple(out)

    st = tuple((jnp.full((TQ, 1), NEG_INF, F32), jnp.zeros((TQ, 1), F32), jnp.zeros((TQ, LANES), F32))
               for _ in range(SA_HEADS))
    st = lax.fori_loop(
        0, jnp.maximum(i - 1, 0),
        lambda j, s_: attend(pl.multiple_of(j * KB, KB), t, lambda hd: fbias_ref[hd:hd + 1, :], s_), st)
    thr_prev = jnp.where(i > 0, t, jnp.int32(2 ** 31 - 1))
    st = attend(pl.multiple_of(jnp.maximum(i - 1, 0) * KB, KB), thr_prev,
                lambda hd: nbias_ref[hd, :, 0:KB], st)
    st = attend(pl.multiple_of(i * KB, KB), t, lambda hd: nbias_ref[hd, :, KB:2 * KB], st)

    lane = lax.broadcasted_iota(jnp.int32, (TQ, LANES), 1)
    for pair in range(SA_HEADS // 2):
        o_even = st[2 * pair][2] / st[2 * pair][1]
        o_odd = st[2 * pair + 1][2] / st[2 * pair + 1][1]
        y_scr[:, pair * LANES:(pair + 1) * LANES] = jnp.where(
            lane < HEAD_DIM, pltpu.roll(o_even, HEAD_DIM, axis=1), o_odd)
    o_ref[0] = _rms(y_scr[...], g_ref[...]).astype(BF16)


def _sparse_attention(saq, sakv, iq, ik, iw, nbias, fbias, g):
    B, S, _ = saq.shape
    TQ = TQ_SA
    W = GROUP_W
    return pl.pallas_call(
        _sa_kernel,
        grid=(B, S // TQ),
        in_specs=[pl.BlockSpec((1, TQ, 512), lambda b, i: (b, i, 0)),
                  pl.BlockSpec((1, S, 128), lambda b, i: (b, 0, 0)),
                  pl.BlockSpec((1, TQ, 512), lambda b, i: (b, i, 0)),
                  pl.BlockSpec((1, S, 256), lambda b, i: (b, 0, 0)),
                  pl.BlockSpec((1, TQ, 128), lambda b, i: (b, i, 0)),
                  pl.BlockSpec(nbias.shape, lambda b, i: (0, 0, 0)),
                  pl.BlockSpec(fbias.shape, lambda b, i: (0, 0)),
                  pl.BlockSpec((1, W), lambda b, i: (0, 0))],
        out_specs=pl.BlockSpec((1, TQ, W), lambda b, i: (b, i, 0)),
        out_shape=jax.ShapeDtypeStruct((B, S, W), BF16),
        scratch_shapes=[pltpu.VMEM((TQ, S), jnp.int32), pltpu.VMEM((TQ, W), F32)],
        compiler_params=_cparams(2),
        name="sparse_attention",
    )(saq, sakv, iq, ik, iw, nbias, fbias, g)


def _mla_kernel(q_ref, k_ref, v_ref, g_ref, o_ref, y_scr):
    i = pl.program_id(1)
    TQ = TQ_MLA
    r = lax.broadcasted_iota(jnp.int32, (TQ, TQ), 0) // CHUNK
    c = lax.broadcasted_iota(jnp.int32, (TQ, TQ), 1) // CHUNK
    causal = c <= r
    lane = lax.broadcasted_iota(jnp.int32, (TQ, LANES), 1)
    outs = []
    for hd in range(MLA_HEADS):
        qh = q_ref[0, :, hd * LANES:(hd + 1) * LANES]
        vcols = slice((hd // 2) * LANES, (hd // 2 + 1) * LANES)

        def block(j, st, diag):
            m, l, acc = st
            k0 = pl.multiple_of(j * TQ, TQ)
            s = _dot_t(qh, k_ref[0, pl.ds(k0, TQ), hd * LANES:(hd + 1) * LANES])
            if diag:
                s = jnp.where(causal, s, NEG_INF)
            mn = jnp.maximum(m, s.max(axis=1, keepdims=True))
            p = jnp.exp(s - mn)
            alpha = jnp.exp(m - mn)
            return (mn, alpha * l + p.sum(axis=1, keepdims=True),
                    alpha * acc + _dot(p.astype(BF16), v_ref[0, pl.ds(k0, TQ), vcols]))

        st = (jnp.full((TQ, 1), NEG_INF, F32), jnp.zeros((TQ, 1), F32), jnp.zeros((TQ, LANES), F32))
        st = lax.fori_loop(0, i, lambda j, s_: block(j, s_, False), st)
        _, l, acc = block(i, st, True)
        outs.append(acc / l)
    for pair in range(MLA_HEADS // 2):
        y_scr[:, pair * LANES:(pair + 1) * LANES] = jnp.where(lane < MLA_V, outs[2 * pair], outs[2 * pair + 1])
    o_ref[0] = _rms(y_scr[...], g_ref[...]).astype(BF16)


def _latent_attention(mq, mk, mv, g):
    B, S, _ = mq.shape
    TQ = TQ_MLA
    W = GROUP_W
    return pl.pallas_call(
        _mla_kernel,
        grid=(B, S // TQ),
        in_specs=[pl.BlockSpec((1, TQ, 512), lambda b, i: (b, i, 0)),
                  pl.BlockSpec((1, S, 512), lambda b, i: (b, 0, 0)),
                  pl.BlockSpec((1, S, W), lambda b, i: (b, 0, 0)),
                  pl.BlockSpec((1, W), lambda b, i: (0, 0))],
        out_specs=pl.BlockSpec((1, TQ, W), lambda b, i: (b, i, 0)),
        out_shape=jax.ShapeDtypeStruct((B, S, W), BF16),
        scratch_shapes=[pltpu.VMEM((TQ, W), F32)],
        compiler_params=_cparams(2),
        name="latent_attention",
    )(mq, mk, mv, g)


def _ffn_kernel(ya_ref, yb_ref, yc_ref, yd_ref, x_ref, mod_ref, wout_ref, gffn_ref, w1_ref, w3_ref, w2_ref,
                gfin_ref, o_ref, acc_scr, *, final):
    gt1 = mod_ref[0, 2:3, :]
    sh2 = mod_ref[0, 3:4, :]
    sc2 = mod_ref[0, 4:5, :]
    gt2 = mod_ref[0, 5:6, :]
    attn = _dot(ya_ref[...], wout_ref[0:GROUP_W, :])
    for gi, y_ref in enumerate((yb_ref, yc_ref, yd_ref), start=1):
        attn = attn + _dot(y_ref[...], wout_ref[gi * GROUP_W:(gi + 1) * GROUP_W, :])
    x1 = x_ref[...] + gt1 * attn
    h = (_rms(x1, gffn_ref[...]) * (1.0 + sc2) + sh2).astype(BF16)
    for ci in range(D_FF // FF_CHUNK):
        cols = slice(ci * FF_CHUNK, (ci + 1) * FF_CHUNK)
        a = _dot(h, w1_ref[:, cols])
        gate = (a * jax.nn.sigmoid(a) * _dot(h, w3_ref[:, cols])).astype(BF16)
        part = _dot(gate, w2_ref[cols, :])
        if ci == 0:
            acc_scr[...] = part
        else:
            acc_scr[...] += part
    x2 = x1 + gt2 * acc_scr[...]
    o_ref[...] = _rms(x2, gfin_ref[...]) if final else x2


def _out_ffn(ys, x2, mod, wout, gffn, w1, w3, w2, gfin, S, final):
    N, D = x2.shape
    TM = TM_FFN
    nt = S // TM

    def full(a):
        return pl.BlockSpec(a.shape, lambda i: (0,) * a.ndim, pipeline_mode=pl.Buffered(1))

    def tok(w):
        return pl.BlockSpec((TM, w), lambda i: (i, 0))

    return pl.pallas_call(
        functools.partial(_ffn_kernel, final=final),
        grid=(N // TM,),
        in_specs=[tok(GROUP_W)] * 4 + [tok(D), pl.BlockSpec((1, 6, D), lambda i: (i // nt, 0, 0)),
                                       full(wout), full(gffn), full(w1), full(w3), full(w2), full(gfin)],
        out_specs=tok(D),
        out_shape=jax.ShapeDtypeStruct((N, D), F32),
        scratch_shapes=[pltpu.VMEM((TM, D), F32)],
        compiler_params=_cparams(1),
        name="out_ffn_final" if final else "out_ffn",
    )(*ys, x2, mod, wout, gffn, w1, w3, w2, gfin)


def _t5_bucket(rel):
    nb = T5_BUCKETS // 2
    max_exact = nb // 2
    ret = jnp.where(rel > 0, nb, 0)
    n = jnp.abs(rel)
    nf = jnp.maximum(n, 1).astype(jnp.float32)
    large = max_exact + (jnp.log(nf / max_exact) / math.log(T5_MAX_DIST / max_exact)
                         * (nb - max_exact)).astype(jnp.int32)
    large = jnp.minimum(large, nb - 1)
    return ret + jnp.where(n < max_exact, n, large)


def _rope_tables(S):
    half = MLA_ROPE // 2
    freqs = ROPE_BASE ** (-jnp.arange(half, dtype=F32) / half)
    ang = jnp.arange(S, dtype=jnp.int32).astype(F32)[:, None] * freqs[None, :]
    cos, sin = jnp.cos(ang), jnp.sin(ang)
    cos2 = jnp.concatenate([cos, cos], axis=1)
    sin2 = jnp.concatenate([-sin, sin], axis=1)
    zeros = jnp.zeros((S, LANES - MLA_NOPE - MLA_ROPE), F32)
    scale = (MLA_NOPE + MLA_ROPE) ** -0.5
    cosq = jnp.concatenate([jnp.full((S, MLA_NOPE), scale, F32), cos2 * scale, zeros], axis=1)
    sinq = jnp.concatenate([jnp.zeros((S, MLA_NOPE), F32), sin2 * scale, zeros], axis=1)
    cosk = jnp.concatenate([jnp.zeros((S, MLA_NOPE), F32), cos2, zeros], axis=1)
    sink = jnp.concatenate([jnp.zeros((S, MLA_NOPE), F32), sin2, zeros], axis=1)
    return cosq, sinq, cosk, sink


def _pack_in_weight(w):
    part = {n: w[:, IN_OFFS[k]:IN_OFFS[k + 1]] for k, n in enumerate(
        ('pool_u', 'ca_q', 'ca_k', 'ca_v', 'sa_q', 'sa_k', 'sa_v', 'idx_q', 'idx_k', 'idx_w',
         'mla_cq', 'mla_ckv', 'mla_kr'))}
    D = w.shape[0]
    z = lambda n: jnp.zeros((D, n), F32)
    qscale = HEAD_DIM ** -0.5
    saq = part['sa_q'].reshape(D, SA_HEADS, HEAD_DIM) * qscale
    saq = jnp.concatenate([saq, jnp.zeros_like(saq)], axis=2).reshape(D, SA_HEADS * LANES)
    kr = part['mla_kr']
    kr_swap = jnp.concatenate([kr[:, MLA_ROPE // 2:], kr[:, :MLA_ROPE // 2]], axis=1)
    pad_r = LANES - MLA_NOPE - MLA_ROPE
    cols = [part['pool_u'], part['ca_q'] * qscale, part['ca_k'], part['ca_v'], saq,
            part['sa_k'], part['sa_v'], part['idx_q'],
            part['idx_k'], z(IDX_DIM), z(IDX_DIM), part['idx_k'],
            part['idx_w'], z(LANES - IDX_HEADS),
            part['mla_cq'], part['mla_ckv'],
            z(MLA_NOPE), kr, z(pad_r), z(MLA_NOPE), kr_swap, z(pad_r)]
    out = jnp.concatenate(cols, axis=1)
    assert out.shape[1] == C_END
    return out.astype(BF16)


def _pack_mla_weights(w_uq, w_ukv):
    R = w_uq.shape[0]
    pad = jnp.zeros((R, MLA_HEADS, LANES - MLA_NOPE - MLA_ROPE), F32)
    rope_w = w_uq[:, :, MLA_NOPE:]
    rope_sw = jnp.concatenate([rope_w[:, :, MLA_ROPE // 2:], rope_w[:, :, :MLA_ROPE // 2]], axis=2)
    wq = jnp.concatenate([w_uq, pad], axis=2).reshape(R, MLA_HEADS * LANES)
    wqs = jnp.concatenate([jnp.zeros((R, MLA_HEADS, MLA_NOPE), F32), rope_sw, pad],
                          axis=2).reshape(R, MLA_HEADS * LANES)
    Rk = w_ukv.shape[0]
    wk = jnp.concatenate([w_ukv[:, :, :MLA_NOPE], jnp.zeros((Rk, MLA_HEADS, LANES - MLA_NOPE), F32)],
                         axis=2).reshape(Rk, MLA_HEADS * LANES)
    wv = w_ukv[:, :, MLA_NOPE:].reshape(Rk, MLA_HEADS * MLA_V)
    return wq.astype(BF16), wqs.astype(BF16), jnp.concatenate([wk, wv], axis=1).astype(BF16)


def _band_bias(rel_table):
    r = jnp.arange(TQ_CA)[:, None]
    c = jnp.arange(CA_WIN)[None, :]
    dist = CA_LEFT_CHUNKS * CHUNK + r - c
    ridx = jnp.clip(dist, -(CHUNK - 1), CA_MAX_REL) + (CHUNK - 1)
    qc = r // CHUNK + CA_LEFT_CHUNKS
    kc = c // CHUNK
    valid = (kc <= qc) & (kc >= qc - CA_LEFT_CHUNKS)
    return jnp.where(valid[None], rel_table[:, ridx].astype(F32), NEG_INF)


def _t5_bias(t5_table):
    r = jnp.arange(TQ_SA, dtype=jnp.int32)[:, None]
    c = jnp.arange(2 * TQ_SA, dtype=jnp.int32)[None, :]
    near = jnp.moveaxis(t5_table[_t5_bucket(c - TQ_SA - r)].astype(F32), -1, 0)
    far = t5_table[_t5_bucket(jnp.int32(-(TQ_SA + 1)))].astype(F32)
    far = jnp.broadcast_to(far[:, None], (SA_HEADS, KB_SA))
    return near, jnp.concatenate([far, jnp.zeros((8 - SA_HEADS, KB_SA), F32)], axis=0)


def kernel(x, c, t5_table, w_mod, b_mod, g_mix, w_in, pool_w, pool_scale, ca_rel, mla_g_cq, mla_g_ckv,
           mla_w_uq, mla_w_ukv, g_group, w_out, g_ffn, ffn_w1, ffn_w3, ffn_w2, g_final):
    B, S, D = x.shape
    assert D == D_MODEL and S % TM_PROJ == 0 and S % TQ_SA == 0 and S >= 4 * TOPK_MAX
    N = B * S
    mod_all = _modulation(c, w_mod, b_mod)
    rope_tabs = _rope_tables(S)
    nbias, fbias = _t5_bias(t5_table)
    row = lambda v: v.reshape(1, -1).astype(F32)
    x2 = x.reshape(N, D)
    for l in range(DEPTH):
        mod = mod_all[l].reshape(B, 6, D)
        w1 = _pack_in_weight(w_in[l])
        wq, wqs, wkv = _pack_mla_weights(mla_w_uq[l], mla_w_ukv[l])
        (pool_u, ca, saq, sakv, iq, ik, iw, mq, mk, mv) = _inproj(
            x2, mod, row(g_mix[l]), w1, row(mla_g_cq[l]), row(mla_g_ckv[l]), wq, wqs, wkv, rope_tabs, S)
        gg = g_group[l].reshape(4, 1, GROUP_W).astype(F32)
        wbd = jax.scipy.linalg.block_diag(*[pool_w[l, gi] for gi in range(len(POOL_WINDOWS))]).astype(BF16)
        bsw = lambda a: a.reshape(B, S, a.shape[-1])
        y_a = _pool(bsw(pool_u), wbd, row(pool_scale[l]), gg[0])
        y_b = _chunk_attention(bsw(ca), _band_bias(ca_rel[l]), gg[1])
        y_c = _sparse_attention(bsw(saq), bsw(sakv), bsw(iq), bsw(ik), bsw(iw), nbias, fbias, gg[2])
        y_d = _latent_attention(bsw(mq), bsw(mk), bsw(mv), gg[3])
        ys = [y.reshape(N, GROUP_W) for y in (y_a, y_b, y_c, y_d)]
        x2 = _out_ffn(ys, x2, mod, w_out[l].astype(BF16), row(g_ffn[l]), ffn_w1[l].astype(BF16),
                      ffn_w3[l].astype(BF16), ffn_w2[l].astype(BF16), row(g_final), S,
                      final=(l == DEPTH - 1))
    return x2.reshape(B, S, D)
```

```python
import functools
import math

import jax
import jax.numpy as jnp
from jax import lax
import numpy as np
from jax.experimental import pallas as pl
from jax.experimental.pallas import tpu as pltpu

F32 = jnp.float32
BF16 = jnp.bfloat16

D_MODEL = 1024
DEPTH = 2
CHUNK = 64
EPS = 1e-6
NEG_INF = -1e30
GROUP_W = 256
HEAD_DIM = 64
POOL_WINDOWS = (2, 4, 8, 16)
POOL_GROUP = 64
POOL_HALO = 16
CA_HEADS = 4
CA_LEFT_CHUNKS = 8
CA_MAX_REL = 256
SA_HEADS = 4
IDX_HEADS = 8
IDX_DIM = 64
TOPK_MAX = 256
MLA_HEADS = 4
MLA_NOPE = 64
MLA_ROPE = 32
MLA_V = 64
Q_LORA = 256
KV_LORA = 128
ROPE_BASE = 10000.0
T5_BUCKETS = 32
T5_MAX_DIST = 128
D_FF = 2816
IN_WIDTHS = (256, 256, 256, 256, 256, 64, 64, 512, 64, 8, 256, 128, 32)
IN_OFFS = tuple(int(v) for v in np.cumsum((0,) + IN_WIDTHS))

LANES = 128
VMEM_LIMIT = 56 * 1024 * 1024

TM_PROJ = 512
TM_FFN = 512
TP_POOL = 512
TQ_CA = 128
CA_WIN = TQ_CA + CA_LEFT_CHUNKS * CHUNK
IWT_ROWS = 16
TQ_SA = 256
KB_SA = 256
TQ_MLA = 256
FF_CHUNK = 256

C_POOL = 0
C_CA = C_POOL + 256
C_SAQ = C_CA + 768
C_SAKV = C_SAQ + 512
C_IQ = C_SAKV + 128
C_IK = C_IQ + 512
C_CQ = C_IK + 256
C_CKV = C_CQ + 256
C_KRF = C_CKV + 128
C_KRS = C_KRF + 128
C_END = C_KRS + 128

INT_MIN = -2 ** 31
KEY_ALL = INT_MIN - int(np.array(-np.inf, np.float32).view(np.int32)) + 1
KEY_TOP = int(np.array(np.inf, np.float32).view(np.int32)) + 1
GUESS_MARGIN = 1 << 16


def _cparams(n_axes):
    return pltpu.CompilerParams(dimension_semantics=("arbitrary",) * n_axes,
                                vmem_limit_bytes=VMEM_LIMIT)


def _rms(x, g):
    return x * lax.rsqrt(jnp.mean(x * x, axis=-1, keepdims=True) + EPS) * g


def _dot(a, b):
    return jnp.dot(a, b, preferred_element_type=F32)


def _dot_t(a, b):
    return lax.dot_general(a, b, (((1,), (1,)), ((), ())), preferred_element_type=F32)


def _mod_kernel(c_ref, w_ref, b_ref, o_ref):
    c = c_ref[...]
    act = c * jax.nn.sigmoid(c)
    o_ref[0] = jnp.dot(act, w_ref[0], precision=lax.Precision.HIGHEST,
                       preferred_element_type=F32) + b_ref[0]


def _modulation(c, w_mod, b_mod):
    L, D, W = w_mod.shape
    B = c.shape[0]
    nj = W // D
    return pl.pallas_call(
        _mod_kernel,
        grid=(L, nj),
        in_specs=[pl.BlockSpec((B, D), lambda l, j: (0, 0)),
                  pl.BlockSpec((1, D, D), lambda l, j: (l, 0, j)),
                  pl.BlockSpec((1, 1, D), lambda l, j: (l, 0, j))],
        out_specs=pl.BlockSpec((1, B, D), lambda l, j: (l, 0, j)),
        out_shape=jax.ShapeDtypeStruct((L, B, W), F32),
        compiler_params=_cparams(2),
        name="modulation",
    )(c, w_mod, b_mod.reshape(L, 1, W))


def _inproj_kernel(x_ref, mod_ref, gmix_ref, w_ref, wt_ref, gcq_ref, gckv_ref, wq_ref, wqs_ref, wk_ref, wvt_ref,
                   cosq_ref, sinq_ref, cosk_ref, sink_ref,
                   pool_o, ca_o, saq_o, sakv_o, iq_o, ik_o, iwt_o, svt_o, mq_o, mk_o, mvt_o):
    sh1 = mod_ref[0, 0:1, :]
    sc1 = mod_ref[0, 1:2, :]
    h = (_rms(x_ref[...], gmix_ref[...]) * (1.0 + sc1) + sh1).astype(BF16)

    def seg(a, b):
        return _dot(h, w_ref[:, a:b])

    pool_o[...] = seg(C_POOL, C_CA)
    ca_o[...] = seg(C_CA, C_SAQ).astype(BF16)
    saq_o[...] = seg(C_SAQ, C_SAKV).astype(BF16)
    sakv_o[...] = seg(C_SAKV, C_IQ).astype(BF16)
    iq_o[...] = seg(C_IQ, C_IK).astype(BF16)
    ik_o[...] = seg(C_IK, C_CQ).astype(BF16)
    tr = _dot_t(wt_ref[...], h)
    iwt_o[...] = tr[0:IWT_ROWS] * ((IDX_HEADS ** -0.5) * (IDX_DIM ** -0.5))
    svt_o[...] = tr[IWT_ROWS:].astype(BF16)

    qn = _rms(seg(C_CQ, C_CKV), gcq_ref[...]).astype(BF16)
    qf = _dot(qn, wq_ref[...])
    qs = _dot(qn, wqs_ref[...])
    cosq = jnp.concatenate([cosq_ref[...]] * MLA_HEADS, axis=1)
    sinq = jnp.concatenate([sinq_ref[...]] * MLA_HEADS, axis=1)
    mq_o[...] = (qf * cosq + qs * sinq).astype(BF16)

    kvn = _rms(seg(C_CKV, C_KRF), gckv_ref[...]).astype(BF16)
    kvf = _dot(kvn, wk_ref[...])
    krope = seg(C_KRF, C_KRS) * cosk_ref[...] + seg(C_KRS, C_END) * sink_ref[...]
    for hd in range(MLA_HEADS):
        mk_o[:, hd * LANES:(hd + 1) * LANES] = (kvf[:, hd * LANES:(hd + 1) * LANES] + krope).astype(BF16)
    mvt_o[...] = _dot_t(wvt_ref[...], kvn).astype(BF16)


def _inproj(x2, mod, gmix, w1, wt, gcq, gckv, wq, wqs, wk, wvt, rope_tabs, S):
    N, D = x2.shape
    TM = TM_PROJ
    nt = S // TM
    cosq, sinq, cosk, sink = rope_tabs

    def full(a):
        return pl.BlockSpec(a.shape, lambda i: (0,) * a.ndim)

    def tok(w):
        return pl.BlockSpec((TM, w), lambda i: (i, 0))

    tab = pl.BlockSpec((TM, LANES), lambda i: (i % nt, 0))
    def tokt(rows):
        return pl.BlockSpec((rows, TM), lambda i: (0, i))

    outs = [(256, F32, True), (768, BF16, True), (512, BF16, True), (128, BF16, True), (512, BF16, True),
            (256, BF16, True), (IWT_ROWS, F32, False), (HEAD_DIM, BF16, False), (512, BF16, True),
            (512, BF16, True), (GROUP_W, BF16, False)]
    return pl.pallas_call(
        _inproj_kernel,
        grid=(N // TM,),
        in_specs=[tok(D),
                  pl.BlockSpec((1, 6, D), lambda i: (i // nt, 0, 0)),
                  full(gmix), full(w1), full(wt), full(gcq), full(gckv), full(wq), full(wqs), full(wk), full(wvt),
                  tab, tab, tab, tab],
        out_specs=[tok(w) if tm else tokt(w) for w, _, tm in outs],
        out_shape=[jax.ShapeDtypeStruct((N, w) if tm else (w, N), dt) for w, dt, tm in outs],
        compiler_params=_cparams(1),
        name="inproj",
    )(x2, mod, gmix, w1, wt, gcq, gckv, wq, wqs, wk, wvt, cosq, sinq, cosk, sink)


def _pool_kernel(u_ref, halo_ref, w_ref, scale_ref, g_ref, o_ref, pad_scr):
    i = pl.program_id(1)
    TP = u_ref.shape[1]
    u = u_ref[0]
    pad_scr[0:POOL_HALO, :] = jnp.where(i > 0, halo_ref[0], 0.0)
    pad_scr[POOL_HALO:, :] = u

    def shifted(j):
        return pad_scr[POOL_HALO - j:POOL_HALO - j + TP, :]

    lane = lax.broadcasted_iota(jnp.int32, (TP, GROUP_W), 1)
    w2 = u + shifted(1)
    w4 = w2 + shifted(2) + shifted(3)
    w8 = w4
    for j in range(4, 8):
        w8 = w8 + shifted(j)
    w16 = w8
    for j in range(8, 16):
        w16 = w16 + shifted(j)
    win = jnp.where(lane < 64, w2, jnp.where(lane < 128, w4, jnp.where(lane < 192, w8, w16)))
    wlen = jnp.where(lane < 64, 2, jnp.where(lane < 128, 4, jnp.where(lane < 192, 8, 16)))
    t = i * TP + lax.broadcasted_iota(jnp.int32, (TP, GROUP_W), 0)
    cnt = jnp.minimum(t + 1, wlen).astype(F32)
    d = (win / cnt - u).astype(BF16)
    y = _dot(d, w_ref[...]) * scale_ref[...]
    o_ref[0] = _rms(y, g_ref[...]).astype(BF16)


def _pool(u, wbd, scale, g):
    B, S, W = u.shape
    TP = TP_POOL
    hb = TP // POOL_HALO
    return pl.pallas_call(
        _pool_kernel,
        grid=(B, S // TP),
        in_specs=[pl.BlockSpec((1, TP, W), lambda b, i: (b, i, 0)),
                  pl.BlockSpec((1, POOL_HALO, W), lambda b, i: (b, jnp.maximum(i * hb - 1, 0), 0)),
                  pl.BlockSpec((W, W), lambda b, i: (0, 0)),
                  pl.BlockSpec((1, W), lambda b, i: (0, 0)),
                  pl.BlockSpec((1, W), lambda b, i: (0, 0))],
        out_specs=pl.BlockSpec((1, TP, W), lambda b, i: (b, i, 0)),
        out_shape=jax.ShapeDtypeStruct((B, S, W), BF16),
        scratch_shapes=[pltpu.VMEM((POOL_HALO + TP, W), F32)],
        compiler_params=_cparams(2),
        name="pool_mixer",
    )(u, u, wbd, scale, g)


def _ca_kernel(q_ref, k_ref, v_ref, bias_ref, g_ref, o_ref, y_scr):
    i = pl.program_id(1)
    TQ = TQ_CA
    nblk = CA_WIN // TQ
    lane = lax.broadcasted_iota(jnp.int32, (TQ, LANES), 1)
    for pair in range(CA_HEADS // 2):
        cols = slice(pair * LANES, (pair + 1) * LANES)
        qp = q_ref[0, :, cols].astype(F32)
        outs = []
        for e in range(2):
            hd = 2 * pair + e
            keep = (lane < HEAD_DIM) if e == 0 else (lane >= HEAD_DIM)
            qh = jnp.where(keep, qp, 0.0).astype(BF16)
            parts = []
            for j in range(nblk):
                kb = i - (nblk - 1) + j
                start = pl.multiple_of(jnp.maximum(kb, 0) * TQ, TQ)
                s = _dot_t(qh, k_ref[0, pl.ds(start, TQ), cols]) + bias_ref[hd, :, j * TQ:(j + 1) * TQ]
                parts.append(jnp.where(kb >= 0, s, NEG_INF))
            m = parts[0].max(axis=1, keepdims=True)
            for s in parts[1:]:
                m = jnp.maximum(m, s.max(axis=1, keepdims=True))
            l = jnp.zeros((TQ, 1), F32)
            acc = jnp.zeros((TQ, LANES), F32)
            for j in range(nblk):
                kb = i - (nblk - 1) + j
                start = pl.multiple_of(jnp.maximum(kb, 0) * TQ, TQ)
                p = jnp.exp(parts[j] - m)
                l = l + p.sum(axis=1, keepdims=True)
                acc = acc + _dot(p.astype(BF16), v_ref[0, pl.ds(start, TQ), cols])
            outs.append(acc / l)
        y_scr[:, cols] = jnp.where(lane < HEAD_DIM, outs[0], outs[1])
    o_ref[0] = _rms(y_scr[...], g_ref[...]).astype(BF16)


def _chunk_attention(ca, bias, g):
    B, S, _ = ca.shape
    W = GROUP_W
    TQ = TQ_CA
    return pl.pallas_call(
        _ca_kernel,
        grid=(B, S // TQ),
        in_specs=[pl.BlockSpec((1, TQ, W), lambda b, i: (b, i, 0)),
                  pl.BlockSpec((1, S, W), lambda b, i: (b, 0, 1)),
                  pl.BlockSpec((1, S, W), lambda b, i: (b, 0, 2)),
                  pl.BlockSpec(bias.shape, lambda b, i: (0, 0, 0)),
                  pl.BlockSpec((1, W), lambda b, i: (0, 0))],
        out_specs=pl.BlockSpec((1, TQ, W), lambda b, i: (b, i, 0)),
        out_shape=jax.ShapeDtypeStruct((B, S, W), BF16),
        scratch_shapes=[pltpu.VMEM((TQ, W), F32)],
        compiler_params=_cparams(2),
        name="band_attention",
    )(ca, ca, ca, bias, g)


def _score_key(score):
    b = lax.bitcast_convert_type(score, jnp.int32)
    return jnp.where(b < 0, jnp.int32(INT_MIN) - b, b)


def _sa_kernel(q_ref, kv_ref, vt_ref, iq_ref, ik_ref, iwt_ref, nbias_ref, g_ref, o_ref, key_scr, tprev_scr):
    i = pl.program_id(1)
    TQ, KB = TQ_SA, KB_SA
    K = float(TOPK_MAX)
    nb = i + 1
    q0 = i * TQ
    krow = lax.broadcasted_iota(jnp.int32, (KB, TQ), 0)
    qchunk = (q0 + lax.broadcasted_iota(jnp.int32, (KB, TQ), 1)) // CHUNK

    iwt = iwt_ref[...]

    def score_block(j, carry):
        k0 = pl.multiple_of(j * KB, KB)
        ik = ik_ref[0, pl.ds(k0, KB), :]
        ik_even, ik_odd = ik[:, :LANES], ik[:, LANES:]
        sc = jnp.zeros((KB, TQ), F32)
        for p in range(IDX_HEADS // 2):
            iqp = iq_ref[0, :, p * LANES:(p + 1) * LANES]
            sc = sc + iwt[2 * p:2 * p + 1, :] * jnp.maximum(_dot_t(ik_even, iqp), 0.0)
            sc = sc + iwt[2 * p + 1:2 * p + 2, :] * jnp.maximum(_dot_t(ik_odd, iqp), 0.0)
        adm = (k0 + krow) // CHUNK <= qchunk
        key_scr[pl.ds(k0, KB), :] = _score_key(jnp.where(adm, sc, -jnp.inf))
        return carry

    lax.fori_loop(0, nb, score_block, 0)

    def count_ge(cand):
        def body(j, acc):
            blk = key_scr[pl.ds(pl.multiple_of(j * KB, KB), KB), :]
            return acc + jnp.sum(jnp.where(blk >= cand, 1.0, 0.0), axis=0, keepdims=True)
        return lax.fori_loop(0, nb, body, jnp.zeros((1, TQ), F32))

    def search():
        prev = tprev_scr[0:1, :]
        guess_hi = jnp.zeros((1, TQ), jnp.int32) + (jnp.max(prev) + GUESS_MARGIN)
        guess_lo = jnp.zeros((1, TQ), jnp.int32) + (jnp.min(prev) - GUESS_MARGIN)
        have_guess = i >= 2

        def is_active(lo, hi, clo):
            return jnp.logical_and(clo > K, hi > lo + 1)

        def cond(st):
            _, lo, hi, clo, _ = st
            return jnp.max(jnp.where(is_active(lo, hi, clo), 1, 0)) > 0

        def body(st):
            it, lo, hi, clo, chi = st
            active = is_active(lo, hi, clo)
            mid = (lo >> 1) + (hi >> 1) + (lo & hi & 1)
            lf, hf = lo.astype(F32), hi.astype(F32)
            frac = (clo - K + 0.5) / jnp.maximum(clo - chi, 1.0)
            interp = (lf + frac * (hf - lf)).astype(jnp.int32)
            cand = jnp.where(it % 2 == 0, interp, mid)
            cand = jnp.where(jnp.logical_and(have_guess, it == 0), guess_hi, cand)
            cand = jnp.where(jnp.logical_and(have_guess, it == 1), guess_lo, cand)
            cand = jnp.where(active, jnp.clip(cand, lo + 1, hi - 1), lo)
            cnt = count_ge(cand)
            up = jnp.logical_and(active, cnt >= K)
            down = jnp.logical_and(active, cnt < K)
            return (it + 1, jnp.where(up, cand, lo), jnp.where(down, cand, hi),
                    jnp.where(up, cnt, clo), jnp.where(down, cnt, chi))

        lo0 = jnp.full((1, TQ), KEY_ALL - 1, jnp.int32)
        hi0 = jnp.full((1, TQ), KEY_TOP, jnp.int32)
        clo0 = jnp.zeros((1, TQ), F32) + (nb * KB).astype(F32)
        _, lo, _, clo, _ = lax.while_loop(cond, body, (jnp.int32(0), lo0, hi0, clo0, jnp.zeros((1, TQ), F32)))
        return lo, clo

    def no_search():
        return jnp.full((1, TQ), KEY_ALL, jnp.int32), jnp.full((1, TQ), K, F32)

    t, cnt_t = lax.cond(i > 0, search, no_search)
    t = jnp.maximum(t, KEY_ALL)
    tprev_scr[...] = jnp.broadcast_to(t, tprev_scr.shape)

    @pl.when(jnp.max(cnt_t) > K)
    def _():
        allowed = K - count_ge(t + 1)
        r = lax.broadcasted_iota(jnp.int32, (KB, KB), 0)
        c = lax.broadcasted_iota(jnp.int32, (KB, KB), 1)
        earlier = jnp.where(c < r, 1.0, 0.0).astype(BF16)

        def body(j, seen):
            sl = pl.ds(pl.multiple_of(j * KB, KB), KB)
            blk = key_scr[sl, :]
            eq = jnp.where(blk == t, 1.0, 0.0)
            rank = _dot(earlier, eq.astype(BF16)) + seen
            demote = eq * jnp.where(rank >= allowed, 1.0, 0.0)
            key_scr[sl, :] = jnp.where(demote > 0.5, t - 1, blk)
            return seen + eq.sum(axis=0, keepdims=True)

        lax.fori_loop(0, nb, body, jnp.zeros((1, TQ), F32))

    def attend(k0, thr, bias_of, st):
        kblk = kv_ref[0, pl.ds(k0, KB), :]
        vt = vt_ref[:, pl.ds(k0, KB)]
        sel = key_scr[pl.ds(k0, KB), :] >= thr
        out = []
        for hd in range(SA_HEADS):
            m, l, acc = st[hd]
            s = _dot_t(kblk, q_ref[0, :, hd * LANES:(hd + 1) * LANES])
            if bias_of is not None:
                s = s + bias_of(hd)
            s = jnp.where(sel, s, NEG_INF)
            mn = jnp.maximum(m, s.max(axis=0, keepdims=True))
            p = jnp.exp(s - mn)
            alpha = jnp.exp(m - mn)
            out.append((mn, alpha * l + p.sum(axis=0, keepdims=True), alpha * acc + _dot(vt, p.astype(BF16))))
        return tuple(out)

    st = tuple((jnp.full((1, TQ), NEG_INF, F32), jnp.zeros((1, TQ), F32), jnp.zeros((HEAD_DIM, TQ), F32))
               for _ in range(SA_HEADS))
    st = lax.fori_loop(0, jnp.maximum(i - 1, 0),
                       lambda j, s_: attend(pl.multiple_of(j * KB, KB), t, None, s_), st)
    thr_prev = jnp.where(i > 0, t, jnp.int32(2 ** 31 - 1))
    st = attend(pl.multiple_of(jnp.maximum(i - 1, 0) * KB, KB), thr_prev,
                lambda hd: nbias_ref[hd, 0:KB, :], st)
    st = attend(pl.multiple_of(i * KB, KB), t, lambda hd: nbias_ref[hd, KB:2 * KB, :], st)
    y_t = jnp.concatenate([acc / l for _, l, acc in st], axis=0)
    o_ref[0] = _group_norm_t(y_t, g_ref[...])


def _sparse_attention(saq, sakv, svt, iq, ik, iwt, nbias, g):
    B, S, _ = saq.shape
    TQ = TQ_SA
    W = GROUP_W
    nt = S // TQ
    return pl.pallas_call(
        _sa_kernel,
        grid=(B, nt),
        in_specs=[pl.BlockSpec((1, TQ, 512), lambda b, i: (b, i, 0)),
                  pl.BlockSpec((1, S, 128), lambda b, i: (b, 0, 0)),
                  pl.BlockSpec((HEAD_DIM, S), lambda b, i: (0, b)),
                  pl.BlockSpec((1, TQ, 512), lambda b, i: (b, i, 0)),
                  pl.BlockSpec((1, S, 256), lambda b, i: (b, 0, 0)),
                  pl.BlockSpec((IWT_ROWS, TQ), lambda b, i: (0, b * nt + i)),
                  pl.BlockSpec(nbias.shape, lambda b, i: (0, 0, 0)),
                  pl.BlockSpec((1, W), lambda b, i: (0, 0))],
        out_specs=pl.BlockSpec((1, TQ, W), lambda b, i: (b, i, 0)),
        out_shape=jax.ShapeDtypeStruct((B, S, W), BF16),
        scratch_shapes=[pltpu.VMEM((S, TQ), jnp.int32), pltpu.VMEM((8, TQ), jnp.int32)],
        compiler_params=_cparams(2),
        name="sparse_attention",
    )(saq, sakv, svt, iq, ik, iwt, nbias, g)


def _group_norm_t(y_t, g):
    inv = lax.rsqrt(jnp.mean(y_t * y_t, axis=0, keepdims=True) + EPS)
    return ((y_t * inv).T * g).astype(BF16)


def _mla_kernel(q_ref, k_ref, vt_ref, g_ref, o_ref):
    i = pl.program_id(1)
    TQ = TQ_MLA
    kch = lax.broadcasted_iota(jnp.int32, (TQ, TQ), 0) // CHUNK
    qch = lax.broadcasted_iota(jnp.int32, (TQ, TQ), 1) // CHUNK
    causal = kch <= qch

    def block(j, st, diag):
        k0 = pl.multiple_of(j * TQ, TQ)
        out = []
        for hd in range(MLA_HEADS):
            m, l, acc = st[hd]
            cols = slice(hd * LANES, (hd + 1) * LANES)
            s = _dot_t(k_ref[0, pl.ds(k0, TQ), cols], q_ref[0, :, cols])
            if diag:
                s = jnp.where(causal, s, NEG_INF)
            mn = jnp.maximum(m, s.max(axis=0, keepdims=True))
            p = jnp.exp(s - mn)
            alpha = jnp.exp(m - mn)
            vt = vt_ref[hd * MLA_V:(hd + 1) * MLA_V, pl.ds(k0, TQ)]
            out.append((mn, alpha * l + p.sum(axis=0, keepdims=True),
                        alpha * acc + _dot(vt, p.astype(BF16))))
        return tuple(out)

    st = tuple((jnp.full((1, TQ), NEG_INF, F32), jnp.zeros((1, TQ), F32), jnp.zeros((MLA_V, TQ), F32))
               for _ in range(MLA_HEADS))
    st = lax.fori_loop(0, i, lambda j, s_: block(j, s_, False), st)
    st = block(i, st, True)
    y_t = jnp.concatenate([acc / l for _, l, acc in st], axis=0)
    o_ref[0] = _group_norm_t(y_t, g_ref[...])


def _latent_attention(mq, mk, mvt, g):
    B, S, _ = mq.shape
    TQ = TQ_MLA
    W = GROUP_W
    return pl.pallas_call(
        _mla_kernel,
        grid=(B, S // TQ),
        in_specs=[pl.BlockSpec((1, TQ, 512), lambda b, i: (b, i, 0)),
                  pl.BlockSpec((1, S, 512), lambda b, i: (b, 0, 0)),
                  pl.BlockSpec((W, S), lambda b, i: (0, b)),
                  pl.BlockSpec((1, W), lambda b, i: (0, 0))],
        out_specs=pl.BlockSpec((1, TQ, W), lambda b, i: (b, i, 0)),
        out_shape=jax.ShapeDtypeStruct((B, S, W), BF16),
        compiler_params=_cparams(2),
        name="latent_attention",
    )(mq, mk, mvt, g)


def _ffn_kernel(ya_ref, yb_ref, yc_ref, yd_ref, x_ref, mod_ref, wout_ref, gffn_ref, w1_ref, w3_ref, w2_ref,
                gfin_ref, o_ref, acc_scr, *, final):
    gt1 = mod_ref[0, 2:3, :]
    sh2 = mod_ref[0, 3:4, :]
    sc2 = mod_ref[0, 4:5, :]
    gt2 = mod_ref[0, 5:6, :]
    attn = _dot(ya_ref[...], wout_ref[0:GROUP_W, :])
    for gi, y_ref in enumerate((yb_ref, yc_ref, yd_ref), start=1):
        attn = attn + _dot(y_ref[...], wout_ref[gi * GROUP_W:(gi + 1) * GROUP_W, :])
    x1 = x_ref[...] + gt1 * attn
    h = (_rms(x1, gffn_ref[...]) * (1.0 + sc2) + sh2).astype(BF16)
    for ci in range(D_FF // FF_CHUNK):
        cols = slice(ci * FF_CHUNK, (ci + 1) * FF_CHUNK)
        a = _dot(h, w1_ref[:, cols])
        gate = (a * jax.nn.sigmoid(a) * _dot(h, w3_ref[:, cols])).astype(BF16)
        part = _dot(gate, w2_ref[cols, :])
        if ci == 0:
            acc_scr[...] = part
        else:
            acc_scr[...] += part
    x2 = x1 + gt2 * acc_scr[...]
    o_ref[...] = _rms(x2, gfin_ref[...]) if final else x2


def _out_ffn(ys, x2, mod, wout, gffn, w1, w3, w2, gfin, S, final):
    N, D = x2.shape
    TM = TM_FFN
    nt = S // TM

    def full(a):
        return pl.BlockSpec(a.shape, lambda i: (0,) * a.ndim, pipeline_mode=pl.Buffered(1))

    def tok(w):
        return pl.BlockSpec((TM, w), lambda i: (i, 0))

    return pl.pallas_call(
        functools.partial(_ffn_kernel, final=final),
        grid=(N // TM,),
        in_specs=[tok(GROUP_W)] * 4 + [tok(D), pl.BlockSpec((1, 6, D), lambda i: (i // nt, 0, 0)),
                                       full(wout), full(gffn), full(w1), full(w3), full(w2), full(gfin)],
        out_specs=tok(D),
        out_shape=jax.ShapeDtypeStruct((N, D), F32),
        scratch_shapes=[pltpu.VMEM((TM, D), F32)],
        compiler_params=_cparams(1),
        name="out_ffn_final" if final else "out_ffn",
    )(*ys, x2, mod, wout, gffn, w1, w3, w2, gfin)


def _t5_bucket(rel):
    nb = T5_BUCKETS // 2
    max_exact = nb // 2
    ret = jnp.where(rel > 0, nb, 0)
    n = jnp.abs(rel)
    nf = jnp.maximum(n, 1).astype(jnp.float32)
    large = max_exact + (jnp.log(nf / max_exact) / math.log(T5_MAX_DIST / max_exact)
                         * (nb - max_exact)).astype(jnp.int32)
    large = jnp.minimum(large, nb - 1)
    return ret + jnp.where(n < max_exact, n, large)


def _rope_tables(S):
    half = MLA_ROPE // 2
    freqs = ROPE_BASE ** (-jnp.arange(half, dtype=F32) / half)
    ang = jnp.arange(S, dtype=jnp.int32).astype(F32)[:, None] * freqs[None, :]
    cos, sin = jnp.cos(ang), jnp.sin(ang)
    cos2 = jnp.concatenate([cos, cos], axis=1)
    sin2 = jnp.concatenate([-sin, sin], axis=1)
    zeros = jnp.zeros((S, LANES - MLA_NOPE - MLA_ROPE), F32)
    scale = (MLA_NOPE + MLA_ROPE) ** -0.5
    cosq = jnp.concatenate([jnp.full((S, MLA_NOPE), scale, F32), cos2 * scale, zeros], axis=1)
    sinq = jnp.concatenate([jnp.zeros((S, MLA_NOPE), F32), sin2 * scale, zeros], axis=1)
    cosk = jnp.concatenate([jnp.zeros((S, MLA_NOPE), F32), cos2, zeros], axis=1)
    sink = jnp.concatenate([jnp.zeros((S, MLA_NOPE), F32), sin2, zeros], axis=1)
    return cosq, sinq, cosk, sink


def _pack_in_weight(w):
    part = {n: w[:, IN_OFFS[k]:IN_OFFS[k + 1]] for k, n in enumerate(
        ('pool_u', 'ca_q', 'ca_k', 'ca_v', 'sa_q', 'sa_k', 'sa_v', 'idx_q', 'idx_k', 'idx_w',
         'mla_cq', 'mla_ckv', 'mla_kr'))}
    D = w.shape[0]
    z = lambda n: jnp.zeros((D, n), F32)
    qscale = HEAD_DIM ** -0.5
    saq = part['sa_q'].reshape(D, SA_HEADS, HEAD_DIM) * qscale
    saq = jnp.concatenate([saq, jnp.zeros_like(saq)], axis=2).reshape(D, SA_HEADS * LANES)
    kr = part['mla_kr']
    kr_swap = jnp.concatenate([kr[:, MLA_ROPE // 2:], kr[:, :MLA_ROPE // 2]], axis=1)
    pad_r = LANES - MLA_NOPE - MLA_ROPE
    cols = [part['pool_u'], part['ca_q'] * qscale, part['ca_k'], part['ca_v'], saq,
            part['sa_k'], part['sa_v'], part['idx_q'],
            part['idx_k'], z(IDX_DIM), z(IDX_DIM), part['idx_k'],
            part['mla_cq'], part['mla_ckv'],
            z(MLA_NOPE), kr, z(pad_r), z(MLA_NOPE), kr_swap, z(pad_r)]
    out = jnp.concatenate(cols, axis=1)
    assert out.shape[1] == C_END
    wt = jnp.concatenate([part['idx_w'].T, jnp.zeros((IWT_ROWS - IDX_HEADS, D), F32), part['sa_v'].T], axis=0)
    return out.astype(BF16), wt.astype(BF16)


def _pack_mla_weights(w_uq, w_ukv):
    R = w_uq.shape[0]
    pad = jnp.zeros((R, MLA_HEADS, LANES - MLA_NOPE - MLA_ROPE), F32)
    rope_w = w_uq[:, :, MLA_NOPE:]
    rope_sw = jnp.concatenate([rope_w[:, :, MLA_ROPE // 2:], rope_w[:, :, :MLA_ROPE // 2]], axis=2)
    wq = jnp.concatenate([w_uq, pad], axis=2).reshape(R, MLA_HEADS * LANES)
    wqs = jnp.concatenate([jnp.zeros((R, MLA_HEADS, MLA_NOPE), F32), rope_sw, pad],
                          axis=2).reshape(R, MLA_HEADS * LANES)
    Rk = w_ukv.shape[0]
    wk = jnp.concatenate([w_ukv[:, :, :MLA_NOPE], jnp.zeros((Rk, MLA_HEADS, LANES - MLA_NOPE), F32)],
                         axis=2).reshape(Rk, MLA_HEADS * LANES)
    wvt = w_ukv[:, :, MLA_NOPE:].reshape(Rk, MLA_HEADS * MLA_V).T
    return wq.astype(BF16), wqs.astype(BF16), wk.astype(BF16), wvt.astype(BF16)


def _toeplitz(vec, rows, cols):
    L = vec.shape[-1]
    assert cols <= L - 1
    flat = jnp.tile(vec, (1, rows))[:, :rows * (L - 1)]
    return flat.reshape(vec.shape[0], rows, L - 1)[:, :, :cols]


def _signed_mod_range(L, hi):
    d = np.arange(L)
    return np.where(d <= hi, d, d - L)


def _band_bias(rel_table):
    L = 768
    d = _signed_mod_range(L, CA_WIN - 1)
    ridx = np.clip(CA_LEFT_CHUNKS * CHUNK - d, -(CHUNK - 1), CA_MAX_REL) + (CHUNK - 1)
    bias = _toeplitz(rel_table[:, ridx].astype(F32), TQ_CA, CA_WIN)
    r = np.arange(TQ_CA)[:, None]
    c = np.arange(CA_WIN)[None, :]
    qc = r // CHUNK + CA_LEFT_CHUNKS
    kc = c // CHUNK
    valid = (kc <= qc) & (kc >= qc - CA_LEFT_CHUNKS)
    return jnp.where(valid[None], bias, NEG_INF)


def _t5_bias(t5_table):
    TQ = TQ_SA
    L = 3 * TQ
    e = _signed_mod_range(L, TQ - 1)
    rel = jnp.asarray(-e - TQ, jnp.int32)
    far = t5_table[_t5_bucket(jnp.int32(-(TQ + 1)))].astype(F32)
    vec = (t5_table[_t5_bucket(rel)].astype(F32) - far[None, :]).T
    return _toeplitz(vec, 2 * TQ, TQ)


def kernel(x, c, t5_table, w_mod, b_mod, g_mix, w_in, pool_w, pool_scale, ca_rel, mla_g_cq, mla_g_ckv,
           mla_w_uq, mla_w_ukv, g_group, w_out, g_ffn, ffn_w1, ffn_w3, ffn_w2, g_final):
    B, S, D = x.shape
    assert D == D_MODEL and S % TM_PROJ == 0 and S % TQ_SA == 0 and S >= 4 * TOPK_MAX
    N = B * S
    mod_all = _modulation(c, w_mod, b_mod)
    rope_tabs = _rope_tables(S)
    nbias = _t5_bias(t5_table)
    row = lambda v: v.reshape(1, -1).astype(F32)
    x2 = x.reshape(N, D)
    for l in range(DEPTH):
        mod = mod_all[l].reshape(B, 6, D)
        w1, wt = _pack_in_weight(w_in[l])
        wq, wqs, wk, wvt = _pack_mla_weights(mla_w_uq[l], mla_w_ukv[l])
        (pool_u, ca, saq, sakv, iq, ik, iwt, svt, mq, mk, mvt) = _inproj(
            x2, mod, row(g_mix[l]), w1, wt, row(mla_g_cq[l]), row(mla_g_ckv[l]), wq, wqs, wk, wvt, rope_tabs, S)
        gg = g_group[l].reshape(4, 1, GROUP_W).astype(F32)
        wbd = jax.scipy.linalg.block_diag(*[pool_w[l, gi] for gi in range(len(POOL_WINDOWS))]).astype(BF16)
        bsw = lambda a: a.reshape(B, S, a.shape[-1])
        y_a = _pool(bsw(pool_u), wbd, row(pool_scale[l]), gg[0])
        y_b = _chunk_attention(bsw(ca), _band_bias(ca_rel[l]), gg[1])
        y_c = _sparse_attention(bsw(saq), bsw(sakv), svt, bsw(iq), bsw(ik), iwt, nbias, gg[2])
        y_d = _latent_attention(bsw(mq), bsw(mk), mvt, gg[3])
        ys = [y.reshape(N, GROUP_W) for y in (y_a, y_b, y_c, y_d)]
        x2 = _out_ffn(ys, x2, mod, w_out[l].astype(BF16), row(g_ffn[l]), ffn_w1[l].astype(BF16),
                      ffn_w3[l].astype(BF16), ffn_w2[l].astype(BF16), row(g_final), S,
                      final=(l == DEPTH - 1))
    return x2.reshape(B, S, D)
```

```python
import functools
import math

import jax
import jax.numpy as jnp
from jax import lax
import numpy as np
from jax.experimental import pallas as pl
from jax.experimental.pallas import tpu as pltpu

F32 = jnp.float32
BF16 = jnp.bfloat16

D_MODEL = 1024
DEPTH = 2
CHUNK = 64
EPS = 1e-6
NEG_INF = -1e30
GROUP_W = 256
HEAD_DIM = 64
POOL_WINDOWS = (2, 4, 8, 16)
POOL_GROUP = 64
POOL_HALO = 16
CA_HEADS = 4
CA_LEFT_CHUNKS = 8
CA_MAX_REL = 256
SA_HEADS = 4
IDX_HEADS = 8
IDX_DIM = 64
TOPK_MAX = 256
MLA_HEADS = 4
MLA_NOPE = 64
MLA_ROPE = 32
MLA_V = 64
Q_LORA = 256
KV_LORA = 128
ROPE_BASE = 10000.0
T5_BUCKETS = 32
T5_MAX_DIST = 128
D_FF = 2816
IN_WIDTHS = (256, 256, 256, 256, 256, 64, 64, 512, 64, 8, 256, 128, 32)
IN_OFFS = tuple(int(v) for v in np.cumsum((0,) + IN_WIDTHS))

LANES = 128
VMEM_LIMIT = 56 * 1024 * 1024

TM_PROJ = 512
TM_FFN = 512
TP_POOL = 512
TQ_CA = 128
CA_WIN = TQ_CA + CA_LEFT_CHUNKS * CHUNK
IWT_ROWS = 16
TQ_SA = 256
KB_SA = 256
TQ_MLA = 256
FF_CHUNK = 256

C_POOL = 0
C_CA = C_POOL + 256
C_SAQ = C_CA + 768
C_SAKV = C_SAQ + 512
C_IQ = C_SAKV + 128
C_IK = C_IQ + 512
C_CQ = C_IK + 256
C_CKV = C_CQ + 256
C_KRF = C_CKV + 128
C_KRS = C_KRF + 128
C_END = C_KRS + 128

INT_MIN = -2 ** 31
KEY_ALL = INT_MIN - int(np.array(-np.inf, np.float32).view(np.int32)) + 1


def _cparams(n_axes):
    return pltpu.CompilerParams(dimension_semantics=("arbitrary",) * n_axes,
                                vmem_limit_bytes=VMEM_LIMIT)


def _rms(x, g):
    return x * lax.rsqrt(jnp.mean(x * x, axis=-1, keepdims=True) + EPS) * g


def _dot(a, b):
    return jnp.dot(a, b, preferred_element_type=F32)


def _dot_t(a, b):
    return lax.dot_general(a, b, (((1,), (1,)), ((), ())), preferred_element_type=F32)


def _mod_kernel(c_ref, w_ref, b_ref, o_ref):
    c = c_ref[...]
    act = c * jax.nn.sigmoid(c)
    o_ref[0] = jnp.dot(act, w_ref[0], precision=lax.Precision.HIGHEST,
                       preferred_element_type=F32) + b_ref[0]


def _modulation(c, w_mod, b_mod):
    L, D, W = w_mod.shape
    B = c.shape[0]
    nj = W // D
    return pl.pallas_call(
        _mod_kernel,
        grid=(L, nj),
        in_specs=[pl.BlockSpec((B, D), lambda l, j: (0, 0)),
                  pl.BlockSpec((1, D, D), lambda l, j: (l, 0, j)),
                  pl.BlockSpec((1, 1, D), lambda l, j: (l, 0, j))],
        out_specs=pl.BlockSpec((1, B, D), lambda l, j: (l, 0, j)),
        out_shape=jax.ShapeDtypeStruct((L, B, W), F32),
        compiler_params=_cparams(2),
        name="modulation",
    )(c, w_mod, b_mod.reshape(L, 1, W))


def _inproj_kernel(x_ref, mod_ref, gmix_ref, w_ref, wt_ref, gcq_ref, gckv_ref, wq_ref, wqs_ref, wk_ref, wvt_ref,
                   cosq_ref, sinq_ref, cosk_ref, sink_ref,
                   pool_o, ca_o, saq_o, sakv_o, iq_o, ik_o, iwt_o, svt_o, mq_o, mk_o, mvt_o):
    sh1 = mod_ref[0, 0:1, :]
    sc1 = mod_ref[0, 1:2, :]
    h = (_rms(x_ref[...], gmix_ref[...]) * (1.0 + sc1) + sh1).astype(BF16)

    def seg(a, b):
        return _dot(h, w_ref[:, a:b])

    pool_o[...] = seg(C_POOL, C_CA)
    ca_o[...] = seg(C_CA, C_SAQ).astype(BF16)
    saq_o[...] = seg(C_SAQ, C_SAKV).astype(BF16)
    sakv_o[...] = seg(C_SAKV, C_IQ).astype(BF16)
    iq_o[...] = seg(C_IQ, C_IK).astype(BF16)
    ik_o[...] = seg(C_IK, C_CQ).astype(BF16)
    tr = _dot_t(wt_ref[...], h)
    iwt_o[...] = tr[0:IWT_ROWS] * ((IDX_HEADS ** -0.5) * (IDX_DIM ** -0.5))
    svt_o[...] = tr[IWT_ROWS:].astype(BF16)

    qn = _rms(seg(C_CQ, C_CKV), gcq_ref[...]).astype(BF16)
    qf = _dot(qn, wq_ref[...])
    qs = _dot(qn, wqs_ref[...])
    cosq = jnp.concatenate([cosq_ref[...]] * MLA_HEADS, axis=1)
    sinq = jnp.concatenate([sinq_ref[...]] * MLA_HEADS, axis=1)
    mq_o[...] = (qf * cosq + qs * sinq).astype(BF16)

    kvn = _rms(seg(C_CKV, C_KRF), gckv_ref[...]).astype(BF16)
    kvf = _dot(kvn, wk_ref[...])
    krope = seg(C_KRF, C_KRS) * cosk_ref[...] + seg(C_KRS, C_END) * sink_ref[...]
    for hd in range(MLA_HEADS):
        mk_o[:, hd * LANES:(hd + 1) * LANES] = (kvf[:, hd * LANES:(hd + 1) * LANES] + krope).astype(BF16)
    mvt_o[...] = _dot_t(wvt_ref[...], kvn).astype(BF16)


def _inproj(x2, mod, gmix, w1, wt, gcq, gckv, wq, wqs, wk, wvt, rope_tabs, S):
    N, D = x2.shape
    TM = TM_PROJ
    nt = S // TM
    cosq, sinq, cosk, sink = rope_tabs

    def full(a):
        return pl.BlockSpec(a.shape, lambda i: (0,) * a.ndim)

    def tok(w):
        return pl.BlockSpec((TM, w), lambda i: (i, 0))

    tab = pl.BlockSpec((TM, LANES), lambda i: (i % nt, 0))
    def tokt(rows):
        return pl.BlockSpec((rows, TM), lambda i: (0, i))

    outs = [(256, F32, True), (768, BF16, True), (512, BF16, True), (128, BF16, True), (512, BF16, True),
            (256, BF16, True), (IWT_ROWS, F32, False), (HEAD_DIM, BF16, False), (512, BF16, True),
            (512, BF16, True), (GROUP_W, BF16, False)]
    return pl.pallas_call(
        _inproj_kernel,
        grid=(N // TM,),
        in_specs=[tok(D),
                  pl.BlockSpec((1, 6, D), lambda i: (i // nt, 0, 0)),
                  full(gmix), full(w1), full(wt), full(gcq), full(gckv), full(wq), full(wqs), full(wk), full(wvt),
                  tab, tab, tab, tab],
        out_specs=[tok(w) if tm else tokt(w) for w, _, tm in outs],
        out_shape=[jax.ShapeDtypeStruct((N, w) if tm else (w, N), dt) for w, dt, tm in outs],
        compiler_params=_cparams(1),
        name="inproj",
    )(x2, mod, gmix, w1, wt, gcq, gckv, wq, wqs, wk, wvt, cosq, sinq, cosk, sink)


def _pool_kernel(u_ref, halo_ref, w_ref, scale_ref, g_ref, o_ref, pad_scr):
    i = pl.program_id(1)
    TP = u_ref.shape[1]
    u = u_ref[0]
    pad_scr[0:POOL_HALO, :] = jnp.where(i > 0, halo_ref[0], 0.0)
    pad_scr[POOL_HALO:, :] = u

    def shifted(j):
        return pad_scr[POOL_HALO - j:POOL_HALO - j + TP, :]

    lane = lax.broadcasted_iota(jnp.int32, (TP, GROUP_W), 1)
    w2 = u + shifted(1)
    w4 = w2 + shifted(2) + shifted(3)
    w8 = w4
    for j in range(4, 8):
        w8 = w8 + shifted(j)
    w16 = w8
    for j in range(8, 16):
        w16 = w16 + shifted(j)
    win = jnp.where(lane < 64, w2, jnp.where(lane < 128, w4, jnp.where(lane < 192, w8, w16)))
    wlen = jnp.where(lane < 64, 2, jnp.where(lane < 128, 4, jnp.where(lane < 192, 8, 16)))
    t = i * TP + lax.broadcasted_iota(jnp.int32, (TP, GROUP_W), 0)
    cnt = jnp.minimum(t + 1, wlen).astype(F32)
    d = (win / cnt - u).astype(BF16)
    y = _dot(d, w_ref[...]) * scale_ref[...]
    o_ref[0] = _rms(y, g_ref[...]).astype(BF16)


def _pool(u, wbd, scale, g):
    B, S, W = u.shape
    TP = TP_POOL
    hb = TP // POOL_HALO
    return pl.pallas_call(
        _pool_kernel,
        grid=(B, S // TP),
        in_specs=[pl.BlockSpec((1, TP, W), lambda b, i: (b, i, 0)),
                  pl.BlockSpec((1, POOL_HALO, W), lambda b, i: (b, jnp.maximum(i * hb - 1, 0), 0)),
                  pl.BlockSpec((W, W), lambda b, i: (0, 0)),
                  pl.BlockSpec((1, W), lambda b, i: (0, 0)),
                  pl.BlockSpec((1, W), lambda b, i: (0, 0))],
        out_specs=pl.BlockSpec((1, TP, W), lambda b, i: (b, i, 0)),
        out_shape=jax.ShapeDtypeStruct((B, S, W), BF16),
        scratch_shapes=[pltpu.VMEM((POOL_HALO + TP, W), F32)],
        compiler_params=_cparams(2),
        name="pool_mixer",
    )(u, u, wbd, scale, g)


def _ca_kernel(q_ref, k_ref, v_ref, bias_ref, g_ref, o_ref, y_scr):
    i = pl.program_id(1)
    TQ = TQ_CA
    nblk = CA_WIN // TQ
    lane = lax.broadcasted_iota(jnp.int32, (TQ, LANES), 1)
    for pair in range(CA_HEADS // 2):
        cols = slice(pair * LANES, (pair + 1) * LANES)
        qp = q_ref[0, :, cols].astype(F32)
        outs = []
        for e in range(2):
            hd = 2 * pair + e
            keep = (lane < HEAD_DIM) if e == 0 else (lane >= HEAD_DIM)
            qh = jnp.where(keep, qp, 0.0).astype(BF16)
            parts = []
            for j in range(nblk):
                kb = i - (nblk - 1) + j
                start = pl.multiple_of(jnp.maximum(kb, 0) * TQ, TQ)
                s = _dot_t(qh, k_ref[0, pl.ds(start, TQ), cols]) + bias_ref[hd, :, j * TQ:(j + 1) * TQ]
                parts.append(jnp.where(kb >= 0, s, NEG_INF))
            m = parts[0].max(axis=1, keepdims=True)
            for s in parts[1:]:
                m = jnp.maximum(m, s.max(axis=1, keepdims=True))
            l = jnp.zeros((TQ, 1), F32)
            acc = jnp.zeros((TQ, LANES), F32)
            for j in range(nblk):
                kb = i - (nblk - 1) + j
                start = pl.multiple_of(jnp.maximum(kb, 0) * TQ, TQ)
                p = jnp.exp(parts[j] - m)
                l = l + p.sum(axis=1, keepdims=True)
                acc = acc + _dot(p.astype(BF16), v_ref[0, pl.ds(start, TQ), cols])
            outs.append(acc / l)
        y_scr[:, cols] = jnp.where(lane < HEAD_DIM, outs[0], outs[1])
    o_ref[0] = _rms(y_scr[...], g_ref[...]).astype(BF16)


def _chunk_attention(ca, bias, g):
    B, S, _ = ca.shape
    W = GROUP_W
    TQ = TQ_CA
    return pl.pallas_call(
        _ca_kernel,
        grid=(B, S // TQ),
        in_specs=[pl.BlockSpec((1, TQ, W), lambda b, i: (b, i, 0)),
                  pl.BlockSpec((1, S, W), lambda b, i: (b, 0, 1)),
                  pl.BlockSpec((1, S, W), lambda b, i: (b, 0, 2)),
                  pl.BlockSpec(bias.shape, lambda b, i: (0, 0, 0)),
                  pl.BlockSpec((1, W), lambda b, i: (0, 0))],
        out_specs=pl.BlockSpec((1, TQ, W), lambda b, i: (b, i, 0)),
        out_shape=jax.ShapeDtypeStruct((B, S, W), BF16),
        scratch_shapes=[pltpu.VMEM((TQ, W), F32)],
        compiler_params=_cparams(2),
        name="band_attention",
    )(ca, ca, ca, bias, g)


def _score_key(score):
    b = lax.bitcast_convert_type(score, jnp.int32)
    return jnp.where(b < 0, jnp.int32(INT_MIN) - b, b)


def _sa_kernel(q_ref, kv_ref, vt_ref, iq_ref, ik_ref, iwt_ref, nbias_ref, g_ref, o_ref, key_scr):
    i = pl.program_id(1)
    TQ, KB = TQ_SA, KB_SA
    K = float(TOPK_MAX)
    nb = i + 1
    q0 = i * TQ
    krow = lax.broadcasted_iota(jnp.int32, (KB, TQ), 0)
    qchunk = (q0 + lax.broadcasted_iota(jnp.int32, (KB, TQ), 1)) // CHUNK

    iwt = iwt_ref[...]

    def score_block(j, carry):
        smax, smin = carry
        k0 = pl.multiple_of(j * KB, KB)
        ik = ik_ref[0, pl.ds(k0, KB), :]
        ik_even, ik_odd = ik[:, :LANES], ik[:, LANES:]
        sc = jnp.zeros((KB, TQ), F32)
        for p in range(IDX_HEADS // 2):
            iqp = iq_ref[0, :, p * LANES:(p + 1) * LANES]
            sc = sc + iwt[2 * p:2 * p + 1, :] * jnp.maximum(_dot_t(ik_even, iqp), 0.0)
            sc = sc + iwt[2 * p + 1:2 * p + 2, :] * jnp.maximum(_dot_t(ik_odd, iqp), 0.0)
        adm = (k0 + krow) // CHUNK <= qchunk
        sc = jnp.where(adm, sc, -jnp.inf)
        key_scr[pl.ds(k0, KB), :] = _score_key(sc)
        smax = jnp.maximum(smax, sc.max(axis=0, keepdims=True))
        smin = jnp.where(j < i, jnp.minimum(smin, sc.min(axis=0, keepdims=True)), smin)
        return smax, smin

    smax, smin = lax.fori_loop(0, nb, score_block,
                               (jnp.full((1, TQ), -jnp.inf, F32), jnp.full((1, TQ), jnp.inf, F32)))

    def count_ge(cand):
        def body(j, acc):
            blk = key_scr[pl.ds(pl.multiple_of(j * KB, KB), KB), :]
            ones = jnp.where(blk >= cand, 1.0, 0.0)
            return acc + ones.reshape(KB // 8, 8, TQ).sum(axis=0)
        return lax.fori_loop(0, nb, body, jnp.zeros((8, TQ), F32)).sum(axis=0, keepdims=True)

    def search():
        def unkey(k):
            return lax.bitcast_convert_type(jnp.where(k < 0, jnp.int32(INT_MIN) - k, k), F32)

        def is_active(lo, hi, clo):
            return jnp.logical_and(clo > K, hi > lo + 1)

        def cond(st):
            _, lo, hi, clo, _ = st
            return jnp.max(jnp.where(is_active(lo, hi, clo), 1, 0)) > 0

        first = _score_key(smin)

        def body(st):
            it, lo, hi, clo, chi = st
            active = is_active(lo, hi, clo)
            lf, hf = unkey(lo), unkey(hi)
            frac = jnp.clip((clo - K + 0.5) / jnp.maximum(clo - chi, 1.0), 0.1, 0.9)
            cand = _score_key(lf + frac * (hf - lf))
            cand = jnp.where(it % 3 == 2, (lo >> 1) + (hi >> 1) + (lo & hi & 1), cand)
            cand = jnp.where(it == 0, first, cand)
            cand = jnp.where(active, jnp.clip(cand, lo + 1, hi - 1), lo)
            cnt = count_ge(cand)
            up = jnp.logical_and(active, cnt >= K)
            down = jnp.logical_and(active, cnt < K)
            return (it + 1, jnp.where(up, cand, lo), jnp.where(down, cand, hi),
                    jnp.where(up, cnt, clo), jnp.where(down, cnt, chi))

        lo0 = jnp.full((1, TQ), KEY_ALL - 1, jnp.int32)
        hi0 = _score_key(smax) + 1
        clo0 = jnp.zeros((1, TQ), F32) + (nb * KB).astype(F32)
        _, lo, _, clo, _ = lax.while_loop(cond, body, (jnp.int32(0), lo0, hi0, clo0, jnp.zeros((1, TQ), F32)))
        return lo, clo

    def no_search():
        return jnp.full((1, TQ), KEY_ALL, jnp.int32), jnp.full((1, TQ), K, F32)

    t, cnt_t = lax.cond(i > 0, search, no_search)
    t = jnp.maximum(t, KEY_ALL)

    @pl.when(jnp.max(cnt_t) > K)
    def _():
        allowed = K - count_ge(t + 1)
        r = lax.broadcasted_iota(jnp.int32, (KB, KB), 0)
        c = lax.broadcasted_iota(jnp.int32, (KB, KB), 1)
        earlier = jnp.where(c < r, 1.0, 0.0).astype(BF16)

        def body(j, seen):
            sl = pl.ds(pl.multiple_of(j * KB, KB), KB)
            blk = key_scr[sl, :]
            eq = jnp.where(blk == t, 1.0, 0.0)
            rank = _dot(earlier, eq.astype(BF16)) + seen
            demote = eq * jnp.where(rank >= allowed, 1.0, 0.0)
            key_scr[sl, :] = jnp.where(demote > 0.5, t - 1, blk)
            return seen + eq.sum(axis=0, keepdims=True)

        lax.fori_loop(0, nb, body, jnp.zeros((1, TQ), F32))

    def scores(j):
        k0 = pl.multiple_of(j * KB, KB)
        boff = pl.multiple_of(jnp.clip(j - (i - 2), 0, 2) * KB, KB)
        kblk = kv_ref[0, pl.ds(k0, KB), :]
        sel = key_scr[pl.ds(k0, KB), :] >= t
        out = []
        for hd in range(SA_HEADS):
            s = _dot_t(kblk, q_ref[0, :, hd * LANES:(hd + 1) * LANES]) + nbias_ref[hd, pl.ds(boff, KB), :]
            out.append(jnp.where(sel, s, NEG_INF))
        return tuple(out)

    def softmax_pv(j, s_all, st):
        vt = vt_ref[:, pl.ds(pl.multiple_of(j * KB, KB), KB)]
        out = []
        for hd in range(SA_HEADS):
            m, l, acc = st[hd]
            s = s_all[hd]
            mn = jnp.maximum(m, s.max(axis=0, keepdims=True))
            p = jnp.exp(s - mn)
            alpha = jnp.exp(m - mn)
            out.append((mn, alpha * l + p.sum(axis=0, keepdims=True), alpha * acc + _dot(vt, p.astype(BF16))))
        return tuple(out)

    def body(j, carry):
        s_cur, st = carry
        s_next = scores(j + 1)
        return s_next, softmax_pv(j, s_cur, st)

    st = tuple((jnp.full((1, TQ), NEG_INF, F32), jnp.zeros((1, TQ), F32), jnp.zeros((HEAD_DIM, TQ), F32))
               for _ in range(SA_HEADS))
    s_last, st = lax.fori_loop(0, i, body, (scores(0), st))
    st = softmax_pv(i, s_last, st)
    y_t = jnp.concatenate([acc / l for _, l, acc in st], axis=0)
    o_ref[0] = _group_norm_t(y_t, g_ref[...])


def _sparse_attention(saq, sakv, svt, iq, ik, iwt, nbias, g):
    B, S, _ = saq.shape
    TQ = TQ_SA
    W = GROUP_W
    nt = S // TQ
    return pl.pallas_call(
        _sa_kernel,
        grid=(B, nt),
        in_specs=[pl.BlockSpec((1, TQ, 512), lambda b, i: (b, i, 0)),
                  pl.BlockSpec((1, S, 128), lambda b, i: (b, 0, 0)),
                  pl.BlockSpec((HEAD_DIM, S), lambda b, i: (0, b)),
                  pl.BlockSpec((1, TQ, 512), lambda b, i: (b, i, 0)),
                  pl.BlockSpec((1, S, 256), lambda b, i: (b, 0, 0)),
                  pl.BlockSpec((IWT_ROWS, TQ), lambda b, i: (0, b * nt + i)),
                  pl.BlockSpec(nbias.shape, lambda b, i: (0, 0, 0)),
                  pl.BlockSpec((1, W), lambda b, i: (0, 0))],
        out_specs=pl.BlockSpec((1, TQ, W), lambda b, i: (b, i, 0)),
        out_shape=jax.ShapeDtypeStruct((B, S, W), BF16),
        scratch_shapes=[pltpu.VMEM((S, TQ), jnp.int32)],
        compiler_params=_cparams(2),
        name="sparse_attention",
    )(saq, sakv, svt, iq, ik, iwt, nbias, g)


def _group_norm_t(y_t, g):
    inv = lax.rsqrt(jnp.mean(y_t * y_t, axis=0, keepdims=True) + EPS)
    return ((y_t * inv).T * g).astype(BF16)


def _mla_kernel(q_ref, k_ref, vt_ref, g_ref, o_ref):
    i = pl.program_id(1)
    TQ = TQ_MLA
    kch = lax.broadcasted_iota(jnp.int32, (TQ, TQ), 0) // CHUNK
    qch = lax.broadcasted_iota(jnp.int32, (TQ, TQ), 1) // CHUNK

    def scores(j):
        k0 = pl.multiple_of(j * TQ, TQ)
        causal = kch <= qch + (i - j) * (TQ // CHUNK)
        out = []
        for hd in range(MLA_HEADS):
            cols = slice(hd * LANES, (hd + 1) * LANES)
            s = _dot_t(k_ref[0, pl.ds(k0, TQ), cols], q_ref[0, :, cols])
            out.append(jnp.where(causal, s, NEG_INF))
        return tuple(out)

    def softmax_pv(j, s_all, st):
        k0 = pl.multiple_of(j * TQ, TQ)
        out = []
        for hd in range(MLA_HEADS):
            m, l, acc = st[hd]
            s = s_all[hd]
            mn = jnp.maximum(m, s.max(axis=0, keepdims=True))
            p = jnp.exp(s - mn)
            alpha = jnp.exp(m - mn)
            vt = vt_ref[hd * MLA_V:(hd + 1) * MLA_V, pl.ds(k0, TQ)]
            out.append((mn, alpha * l + p.sum(axis=0, keepdims=True),
                        alpha * acc + _dot(vt, p.astype(BF16))))
        return tuple(out)

    def body(j, carry):
        s_cur, st = carry
        s_next = scores(j + 1)
        return s_next, softmax_pv(j, s_cur, st)

    st = tuple((jnp.full((1, TQ), NEG_INF, F32), jnp.zeros((1, TQ), F32), jnp.zeros((MLA_V, TQ), F32))
               for _ in range(MLA_HEADS))
    s_last, st = lax.fori_loop(0, i, body, (scores(0), st))
    st = softmax_pv(i, s_last, st)
    y_t = jnp.concatenate([acc / l for _, l, acc in st], axis=0)
    o_ref[0] = _group_norm_t(y_t, g_ref[...])


def _latent_attention(mq, mk, mvt, g):
    B, S, _ = mq.shape
    TQ = TQ_MLA
    W = GROUP_W
    return pl.pallas_call(
        _mla_kernel,
        grid=(B, S // TQ),
        in_specs=[pl.BlockSpec((1, TQ, 512), lambda b, i: (b, i, 0)),
                  pl.BlockSpec((1, S, 512), lambda b, i: (b, 0, 0)),
                  pl.BlockSpec((W, S), lambda b, i: (0, b)),
                  pl.BlockSpec((1, W), lambda b, i: (0, 0))],
        out_specs=pl.BlockSpec((1, TQ, W), lambda b, i: (b, i, 0)),
        out_shape=jax.ShapeDtypeStruct((B, S, W), BF16),
        compiler_params=_cparams(2),
        name="latent_attention",
    )(mq, mk, mvt, g)


def _ffn_kernel(ya_ref, yb_ref, yc_ref, yd_ref, x_ref, mod_ref, wout_ref, gffn_ref, w1_ref, w3_ref, w2_ref,
                gfin_ref, o_ref, acc_scr, *, final):
    gt1 = mod_ref[0, 2:3, :]
    sh2 = mod_ref[0, 3:4, :]
    sc2 = mod_ref[0, 4:5, :]
    gt2 = mod_ref[0, 5:6, :]
    attn = _dot(ya_ref[...], wout_ref[0:GROUP_W, :])
    for gi, y_ref in enumerate((yb_ref, yc_ref, yd_ref), start=1):
        attn = attn + _dot(y_ref[...], wout_ref[gi * GROUP_W:(gi + 1) * GROUP_W, :])
    x1 = x_ref[...] + gt1 * attn
    h = (_rms(x1, gffn_ref[...]) * (1.0 + sc2) + sh2).astype(BF16)
    for ci in range(D_FF // FF_CHUNK):
        cols = slice(ci * FF_CHUNK, (ci + 1) * FF_CHUNK)
        a = _dot(h, w1_ref[:, cols])
        gate = (a * jax.nn.sigmoid(a) * _dot(h, w3_ref[:, cols])).astype(BF16)
        part = _dot(gate, w2_ref[cols, :])
        if ci == 0:
            acc_scr[...] = part
        else:
            acc_scr[...] += part
    x2 = x1 + gt2 * acc_scr[...]
    o_ref[...] = _rms(x2, gfin_ref[...]) if final else x2


def _out_ffn(ys, x2, mod, wout, gffn, w1, w3, w2, gfin, S, final):
    N, D = x2.shape
    TM = TM_FFN
    nt = S // TM

    def full(a):
        return pl.BlockSpec(a.shape, lambda i: (0,) * a.ndim, pipeline_mode=pl.Buffered(1))

    def tok(w):
        return pl.BlockSpec((TM, w), lambda i: (i, 0))

    return pl.pallas_call(
        functools.partial(_ffn_kernel, final=final),
        grid=(N // TM,),
        in_specs=[tok(GROUP_W)] * 4 + [tok(D), pl.BlockSpec((1, 6, D), lambda i: (i // nt, 0, 0)),
                                       full(wout), full(gffn), full(w1), full(w3), full(w2), full(gfin)],
        out_specs=tok(D),
        out_shape=jax.ShapeDtypeStruct((N, D), F32),
        scratch_shapes=[pltpu.VMEM((TM, D), F32)],
        compiler_params=_cparams(1),
        name="out_ffn_final" if final else "out_ffn",
    )(*ys, x2, mod, wout, gffn, w1, w3, w2, gfin)


def _t5_bucket(rel):
    nb = T5_BUCKETS // 2
    max_exact = nb // 2
    ret = jnp.where(rel > 0, nb, 0)
    n = jnp.abs(rel)
    nf = jnp.maximum(n, 1).astype(jnp.float32)
    large = max_exact + (jnp.log(nf / max_exact) / math.log(T5_MAX_DIST / max_exact)
                         * (nb - max_exact)).astype(jnp.int32)
    large = jnp.minimum(large, nb - 1)
    return ret + jnp.where(n < max_exact, n, large)


def _rope_tables(S):
    half = MLA_ROPE // 2
    freqs = ROPE_BASE ** (-jnp.arange(half, dtype=F32) / half)
    ang = jnp.arange(S, dtype=jnp.int32).astype(F32)[:, None] * freqs[None, :]
    cos, sin = jnp.cos(ang), jnp.sin(ang)
    cos2 = jnp.concatenate([cos, cos], axis=1)
    sin2 = jnp.concatenate([-sin, sin], axis=1)
    zeros = jnp.zeros((S, LANES - MLA_NOPE - MLA_ROPE), F32)
    scale = (MLA_NOPE + MLA_ROPE) ** -0.5
    cosq = jnp.concatenate([jnp.full((S, MLA_NOPE), scale, F32), cos2 * scale, zeros], axis=1)
    sinq = jnp.concatenate([jnp.zeros((S, MLA_NOPE), F32), sin2 * scale, zeros], axis=1)
    cosk = jnp.concatenate([jnp.zeros((S, MLA_NOPE), F32), cos2, zeros], axis=1)
    sink = jnp.concatenate([jnp.zeros((S, MLA_NOPE), F32), sin2, zeros], axis=1)
    return cosq, sinq, cosk, sink


def _pack_in_weight(w):
    part = {n: w[:, IN_OFFS[k]:IN_OFFS[k + 1]] for k, n in enumerate(
        ('pool_u', 'ca_q', 'ca_k', 'ca_v', 'sa_q', 'sa_k', 'sa_v', 'idx_q', 'idx_k', 'idx_w',
         'mla_cq', 'mla_ckv', 'mla_kr'))}
    D = w.shape[0]
    z = lambda n: jnp.zeros((D, n), F32)
    qscale = HEAD_DIM ** -0.5
    saq = part['sa_q'].reshape(D, SA_HEADS, HEAD_DIM) * qscale
    saq = jnp.concatenate([saq, jnp.zeros_like(saq)], axis=2).reshape(D, SA_HEADS * LANES)
    kr = part['mla_kr']
    kr_swap = jnp.concatenate([kr[:, MLA_ROPE // 2:], kr[:, :MLA_ROPE // 2]], axis=1)
    pad_r = LANES - MLA_NOPE - MLA_ROPE
    cols = [part['pool_u'], part['ca_q'] * qscale, part['ca_k'], part['ca_v'], saq,
            part['sa_k'], part['sa_v'], part['idx_q'],
            part['idx_k'], z(IDX_DIM), z(IDX_DIM), part['idx_k'],
            part['mla_cq'], part['mla_ckv'],
            z(MLA_NOPE), kr, z(pad_r), z(MLA_NOPE), kr_swap, z(pad_r)]
    out = jnp.concatenate(cols, axis=1)
    assert out.shape[1] == C_END
    wt = jnp.concatenate([part['idx_w'].T, jnp.zeros((IWT_ROWS - IDX_HEADS, D), F32), part['sa_v'].T], axis=0)
    return out.astype(BF16), wt.astype(BF16)


def _pack_mla_weights(w_uq, w_ukv):
    R = w_uq.shape[0]
    pad = jnp.zeros((R, MLA_HEADS, LANES - MLA_NOPE - MLA_ROPE), F32)
    rope_w = w_uq[:, :, MLA_NOPE:]
    rope_sw = jnp.concatenate([rope_w[:, :, MLA_ROPE // 2:], rope_w[:, :, :MLA_ROPE // 2]], axis=2)
    wq = jnp.concatenate([w_uq, pad], axis=2).reshape(R, MLA_HEADS * LANES)
    wqs = jnp.concatenate([jnp.zeros((R, MLA_HEADS, MLA_NOPE), F32), rope_sw, pad],
                          axis=2).reshape(R, MLA_HEADS * LANES)
    Rk = w_ukv.shape[0]
    wk = jnp.concatenate([w_ukv[:, :, :MLA_NOPE], jnp.zeros((Rk, MLA_HEADS, LANES - MLA_NOPE), F32)],
                         axis=2).reshape(Rk, MLA_HEADS * LANES)
    wvt = w_ukv[:, :, MLA_NOPE:].reshape(Rk, MLA_HEADS * MLA_V).T
    return wq.astype(BF16), wqs.astype(BF16), wk.astype(BF16), wvt.astype(BF16)


def _toeplitz(vec, rows, cols):
    L = vec.shape[-1]
    assert cols <= L - 1
    flat = jnp.tile(vec, (1, rows))[:, :rows * (L - 1)]
    return flat.reshape(vec.shape[0], rows, L - 1)[:, :, :cols]


def _signed_mod_range(L, hi):
    d = np.arange(L)
    return np.where(d <= hi, d, d - L)


def _band_bias(rel_table):
    L = 768
    d = _signed_mod_range(L, CA_WIN - 1)
    ridx = np.clip(CA_LEFT_CHUNKS * CHUNK - d, -(CHUNK - 1), CA_MAX_REL) + (CHUNK - 1)
    bias = _toeplitz(rel_table[:, ridx].astype(F32), TQ_CA, CA_WIN)
    r = np.arange(TQ_CA)[:, None]
    c = np.arange(CA_WIN)[None, :]
    qc = r // CHUNK + CA_LEFT_CHUNKS
    kc = c // CHUNK
    valid = (kc <= qc) & (kc >= qc - CA_LEFT_CHUNKS)
    return jnp.where(valid[None], bias, NEG_INF)


def _t5_bias(t5_table):
    TQ = TQ_SA
    L = 3 * TQ
    e = _signed_mod_range(L, TQ - 1)
    rel = jnp.asarray(-e - TQ, jnp.int32)
    far = t5_table[_t5_bucket(jnp.int32(-(TQ + 1)))].astype(F32)
    vec = (t5_table[_t5_bucket(rel)].astype(F32) - far[None, :]).T
    near = _toeplitz(vec, 2 * TQ, TQ)
    return jnp.concatenate([jnp.zeros((SA_HEADS, TQ, TQ), F32), near], axis=1)


def kernel(x, c, t5_table, w_mod, b_mod, g_mix, w_in, pool_w, pool_scale, ca_rel, mla_g_cq, mla_g_ckv,
           mla_w_uq, mla_w_ukv, g_group, w_out, g_ffn, ffn_w1, ffn_w3, ffn_w2, g_final):
    B, S, D = x.shape
    assert D == D_MODEL and S % TM_PROJ == 0 and S % TQ_SA == 0 and S >= 4 * TOPK_MAX
    N = B * S
    mod_all = _modulation(c, w_mod, b_mod)
    rope_tabs = _rope_tables(S)
    nbias = _t5_bias(t5_table)
    row = lambda v: v.reshape(1, -1).astype(F32)
    x2 = x.reshape(N, D)
    for l in range(DEPTH):
        mod = mod_all[l].reshape(B, 6, D)
        w1, wt = _pack_in_weight(w_in[l])
        wq, wqs, wk, wvt = _pack_mla_weights(mla_w_uq[l], mla_w_ukv[l])
        (pool_u, ca, saq, sakv, iq, ik, iwt, svt, mq, mk, mvt) = _inproj(
            x2, mod, row(g_mix[l]), w1, wt, row(mla_g_cq[l]), row(mla_g_ckv[l]), wq, wqs, wk, wvt, rope_tabs, S)
        gg = g_group[l].reshape(4, 1, GROUP_W).astype(F32)
        wbd = jax.scipy.linalg.block_diag(*[pool_w[l, gi] for gi in range(len(POOL_WINDOWS))]).astype(BF16)
        bsw = lambda a: a.reshape(B, S, a.shape[-1])
        y_a = _pool(bsw(pool_u), wbd, row(pool_scale[l]), gg[0])
        y_b = _chunk_attention(bsw(ca), _band_bias(ca_rel[l]), gg[1])
        y_c = _sparse_attention(bsw(saq), bsw(sakv), svt, bsw(iq), bsw(ik), iwt, nbias, gg[2])
        y_d = _latent_attention(bsw(mq), bsw(mk), mvt, gg[3])
        ys = [y.reshape(N, GROUP_W) for y in (y_a, y_b, y_c, y_d)]
        x2 = _out_ffn(ys, x2, mod, w_out[l].astype(BF16), row(g_ffn[l]), ffn_w1[l].astype(BF16),
                      ffn_w3[l].astype(BF16), ffn_w2[l].astype(BF16), row(g_final), S,
                      final=(l == DEPTH - 1))
    return x2.reshape(B, S, D)
```

```python
import functools
import math

import jax
import jax.numpy as jnp
from jax import lax
import numpy as np
from jax.experimental import pallas as pl
from jax.experimental.pallas import tpu as pltpu

F32 = jnp.float32
BF16 = jnp.bfloat16

D_MODEL = 1024
DEPTH = 2
CHUNK = 64
EPS = 1e-6
NEG_INF = -1e30
GROUP_W = 256
HEAD_DIM = 64
POOL_WINDOWS = (2, 4, 8, 16)
POOL_GROUP = 64
POOL_HALO = 16
CA_HEADS = 4
CA_LEFT_CHUNKS = 8
CA_MAX_REL = 256
SA_HEADS = 4
IDX_HEADS = 8
IDX_DIM = 64
TOPK_MAX = 256
MLA_HEADS = 4
MLA_NOPE = 64
MLA_ROPE = 32
MLA_V = 64
Q_LORA = 256
KV_LORA = 128
ROPE_BASE = 10000.0
T5_BUCKETS = 32
T5_MAX_DIST = 128
D_FF = 2816
IN_WIDTHS = (256, 256, 256, 256, 256, 64, 64, 512, 64, 8, 256, 128, 32)
IN_OFFS = tuple(int(v) for v in np.cumsum((0,) + IN_WIDTHS))

LANES = 128
VMEM_LIMIT = 56 * 1024 * 1024

TM_PROJ = 512
TM_FFN = 512
TP_POOL = 512
TQ_CA = 128
CA_WIN = TQ_CA + CA_LEFT_CHUNKS * CHUNK
IWT_ROWS = 16
TQ_SA = 256
KB_SA = 256
COUNT_CHAINS = 8
SEARCH_FIRST_ROUND = 12
SEARCH_ROUND = 4
TQ_MLA = 256
FF_CHUNK = 256

C_POOL = 0
C_CA = C_POOL + 256
C_SAQ = C_CA + 768
C_SAKV = C_SAQ + 512
C_IQ = C_SAKV + 128
C_IK = C_IQ + 512
C_CQ = C_IK + 256
C_CKV = C_CQ + 256
C_KRF = C_CKV + 128
C_KRS = C_KRF + 128
C_END = C_KRS + 128

INT_MIN = -2 ** 31
KEY_ALL = INT_MIN - int(np.array(-np.inf, np.float32).view(np.int32)) + 1


def _cparams(n_axes):
    return pltpu.CompilerParams(dimension_semantics=("arbitrary",) * n_axes,
                                vmem_limit_bytes=VMEM_LIMIT)


def _rms(x, g):
    return x * lax.rsqrt(jnp.mean(x * x, axis=-1, keepdims=True) + EPS) * g


def _dot(a, b):
    return jnp.dot(a, b, preferred_element_type=F32)


def _dot_t(a, b):
    return lax.dot_general(a, b, (((1,), (1,)), ((), ())), preferred_element_type=F32)


def _mod_kernel(c_ref, w_ref, b_ref, o_ref):
    c = c_ref[...]
    act = c * jax.nn.sigmoid(c)
    o_ref[0] = jnp.dot(act, w_ref[0], precision=lax.Precision.HIGHEST,
                       preferred_element_type=F32) + b_ref[0]


def _modulation(c, w_mod, b_mod):
    L, D, W = w_mod.shape
    B = c.shape[0]
    nj = W // D
    return pl.pallas_call(
        _mod_kernel,
        grid=(L, nj),
        in_specs=[pl.BlockSpec((B, D), lambda l, j: (0, 0)),
                  pl.BlockSpec((1, D, D), lambda l, j: (l, 0, j)),
                  pl.BlockSpec((1, 1, D), lambda l, j: (l, 0, j))],
        out_specs=pl.BlockSpec((1, B, D), lambda l, j: (l, 0, j)),
        out_shape=jax.ShapeDtypeStruct((L, B, W), F32),
        compiler_params=_cparams(2),
        name="modulation",
    )(c, w_mod, b_mod.reshape(L, 1, W))


def _inproj_kernel(x_ref, mod_ref, gmix_ref, w_ref, wt_ref, gcq_ref, gckv_ref, wq_ref, wqs_ref, wk_ref, wvt_ref,
                   cosq_ref, sinq_ref, cosk_ref, sink_ref,
                   pool_o, ca_o, saq_o, sakv_o, iq_o, ik_o, iwt_o, svt_o, mq_o, mk_o, mvt_o):
    sh1 = mod_ref[0, 0:1, :]
    sc1 = mod_ref[0, 1:2, :]
    h = (_rms(x_ref[...], gmix_ref[...]) * (1.0 + sc1) + sh1).astype(BF16)

    def seg(a, b):
        return _dot(h, w_ref[:, a:b])

    pool_o[...] = seg(C_POOL, C_CA)
    ca_o[...] = seg(C_CA, C_SAQ).astype(BF16)
    saq_o[...] = seg(C_SAQ, C_SAKV).astype(BF16)
    sakv_o[...] = seg(C_SAKV, C_IQ).astype(BF16)
    iq_o[...] = seg(C_IQ, C_IK).astype(BF16)
    ik_o[...] = seg(C_IK, C_CQ).astype(BF16)
    tr = _dot_t(wt_ref[...], h)
    iwt_o[...] = tr[0:IWT_ROWS] * ((IDX_HEADS ** -0.5) * (IDX_DIM ** -0.5))
    svt_o[...] = tr[IWT_ROWS:].astype(BF16)

    qn = _rms(seg(C_CQ, C_CKV), gcq_ref[...]).astype(BF16)
    qf = _dot(qn, wq_ref[...])
    qs = _dot(qn, wqs_ref[...])
    cosq = jnp.concatenate([cosq_ref[...]] * MLA_HEADS, axis=1)
    sinq = jnp.concatenate([sinq_ref[...]] * MLA_HEADS, axis=1)
    mq_o[...] = (qf * cosq + qs * sinq).astype(BF16)

    kvn = _rms(seg(C_CKV, C_KRF), gckv_ref[...]).astype(BF16)
    kvf = _dot(kvn, wk_ref[...])
    krope = seg(C_KRF, C_KRS) * cosk_ref[...] + seg(C_KRS, C_END) * sink_ref[...]
    for hd in range(MLA_HEADS):
        mk_o[:, hd * LANES:(hd + 1) * LANES] = (kvf[:, hd * LANES:(hd + 1) * LANES] + krope).astype(BF16)
    mvt_o[...] = _dot_t(wvt_ref[...], kvn).astype(BF16)


def _inproj(x2, mod, gmix, w1, wt, gcq, gckv, wq, wqs, wk, wvt, rope_tabs, S):
    N, D = x2.shape
    TM = TM_PROJ
    nt = S // TM
    cosq, sinq, cosk, sink = rope_tabs

    def full(a):
        return pl.BlockSpec(a.shape, lambda i: (0,) * a.ndim)

    def tok(w):
        return pl.BlockSpec((TM, w), lambda i: (i, 0))

    tab = pl.BlockSpec((TM, LANES), lambda i: (i % nt, 0))
    def tokt(rows):
        return pl.BlockSpec((rows, TM), lambda i: (0, i))

    outs = [(256, F32, True), (768, BF16, True), (512, BF16, True), (128, BF16, True), (512, BF16, True),
            (256, BF16, True), (IWT_ROWS, F32, False), (HEAD_DIM, BF16, False), (512, BF16, True),
            (512, BF16, True), (GROUP_W, BF16, False)]
    return pl.pallas_call(
        _inproj_kernel,
        grid=(N // TM,),
        in_specs=[tok(D),
                  pl.BlockSpec((1, 6, D), lambda i: (i // nt, 0, 0)),
                  full(gmix), full(w1), full(wt), full(gcq), full(gckv), full(wq), full(wqs), full(wk), full(wvt),
                  tab, tab, tab, tab],
        out_specs=[tok(w) if tm else tokt(w) for w, _, tm in outs],
        out_shape=[jax.ShapeDtypeStruct((N, w) if tm else (w, N), dt) for w, dt, tm in outs],
        compiler_params=_cparams(1),
        name="inproj",
    )(x2, mod, gmix, w1, wt, gcq, gckv, wq, wqs, wk, wvt, cosq, sinq, cosk, sink)


def _pool_kernel(u_ref, halo_ref, w_ref, scale_ref, g_ref, o_ref, pad_scr):
    i = pl.program_id(1)
    TP = u_ref.shape[1]
    u = u_ref[0]
    pad_scr[0:POOL_HALO, :] = jnp.where(i > 0, halo_ref[0], 0.0)
    pad_scr[POOL_HALO:, :] = u

    def shifted(j):
        return pad_scr[POOL_HALO - j:POOL_HALO - j + TP, :]

    lane = lax.broadcasted_iota(jnp.int32, (TP, GROUP_W), 1)
    w2 = u + shifted(1)
    w4 = w2 + shifted(2) + shifted(3)
    w8 = w4
    for j in range(4, 8):
        w8 = w8 + shifted(j)
    w16 = w8
    for j in range(8, 16):
        w16 = w16 + shifted(j)
    win = jnp.where(lane < 64, w2, jnp.where(lane < 128, w4, jnp.where(lane < 192, w8, w16)))
    wlen = jnp.where(lane < 64, 2, jnp.where(lane < 128, 4, jnp.where(lane < 192, 8, 16)))
    t = i * TP + lax.broadcasted_iota(jnp.int32, (TP, GROUP_W), 0)
    cnt = jnp.minimum(t + 1, wlen).astype(F32)
    d = (win / cnt - u).astype(BF16)
    y = _dot(d, w_ref[...]) * scale_ref[...]
    o_ref[0] = _rms(y, g_ref[...]).astype(BF16)


def _pool(u, wbd, scale, g):
    B, S, W = u.shape
    TP = TP_POOL
    hb = TP // POOL_HALO
    return pl.pallas_call(
        _pool_kernel,
        grid=(B, S // TP),
        in_specs=[pl.BlockSpec((1, TP, W), lambda b, i: (b, i, 0)),
                  pl.BlockSpec((1, POOL_HALO, W), lambda b, i: (b, jnp.maximum(i * hb - 1, 0), 0)),
                  pl.BlockSpec((W, W), lambda b, i: (0, 0)),
                  pl.BlockSpec((1, W), lambda b, i: (0, 0)),
                  pl.BlockSpec((1, W), lambda b, i: (0, 0))],
        out_specs=pl.BlockSpec((1, TP, W), lambda b, i: (b, i, 0)),
        out_shape=jax.ShapeDtypeStruct((B, S, W), BF16),
        scratch_shapes=[pltpu.VMEM((POOL_HALO + TP, W), F32)],
        compiler_params=_cparams(2),
        name="pool_mixer",
    )(u, u, wbd, scale, g)


def _ca_kernel(q_ref, k_ref, v_ref, bias_ref, g_ref, o_ref, y_scr):
    i = pl.program_id(1)
    TQ = TQ_CA
    nblk = CA_WIN // TQ
    lane = lax.broadcasted_iota(jnp.int32, (TQ, LANES), 1)
    for pair in range(CA_HEADS // 2):
        cols = slice(pair * LANES, (pair + 1) * LANES)
        qp = q_ref[0, :, cols].astype(F32)
        outs = []
        for e in range(2):
            hd = 2 * pair + e
            keep = (lane < HEAD_DIM) if e == 0 else (lane >= HEAD_DIM)
            qh = jnp.where(keep, qp, 0.0).astype(BF16)
            parts = []
            for j in range(nblk):
                kb = i - (nblk - 1) + j
                start = pl.multiple_of(jnp.maximum(kb, 0) * TQ, TQ)
                s = _dot_t(qh, k_ref[0, pl.ds(start, TQ), cols]) + bias_ref[hd, :, j * TQ:(j + 1) * TQ]
                parts.append(jnp.where(kb >= 0, s, NEG_INF))
            m = parts[0].max(axis=1, keepdims=True)
            for s in parts[1:]:
                m = jnp.maximum(m, s.max(axis=1, keepdims=True))
            l = jnp.zeros((TQ, 1), F32)
            acc = jnp.zeros((TQ, LANES), F32)
            for j in range(nblk):
                kb = i - (nblk - 1) + j
                start = pl.multiple_of(jnp.maximum(kb, 0) * TQ, TQ)
                p = jnp.exp(parts[j] - m)
                l = l + p.sum(axis=1, keepdims=True)
                acc = acc + _dot(p.astype(BF16), v_ref[0, pl.ds(start, TQ), cols])
            outs.append(acc / l)
        y_scr[:, cols] = jnp.where(lane < HEAD_DIM, outs[0], outs[1])
    o_ref[0] = _rms(y_scr[...], g_ref[...]).astype(BF16)


def _chunk_attention(ca, bias, g):
    B, S, _ = ca.shape
    W = GROUP_W
    TQ = TQ_CA
    return pl.pallas_call(
        _ca_kernel,
        grid=(B, S // TQ),
        in_specs=[pl.BlockSpec((1, TQ, W), lambda b, i: (b, i, 0)),
                  pl.BlockSpec((1, S, W), lambda b, i: (b, 0, 1)),
                  pl.BlockSpec((1, S, W), lambda b, i: (b, 0, 2)),
                  pl.BlockSpec(bias.shape, lambda b, i: (0, 0, 0)),
                  pl.BlockSpec((1, W), lambda b, i: (0, 0))],
        out_specs=pl.BlockSpec((1, TQ, W), lambda b, i: (b, i, 0)),
        out_shape=jax.ShapeDtypeStruct((B, S, W), BF16),
        scratch_shapes=[pltpu.VMEM((TQ, W), F32)],
        compiler_params=_cparams(2),
        name="band_attention",
    )(ca, ca, ca, bias, g)


def _score_key(score):
    b = lax.bitcast_convert_type(score, jnp.int32)
    return jnp.where(b < 0, jnp.int32(INT_MIN) - b, b)


def _sa_kernel(q_ref, kv_ref, vt_ref, iq_ref, ik_ref, iwt_ref, nbias_ref, g_ref, o_ref, key_scr):
    i = pl.program_id(1)
    TQ, KB = TQ_SA, KB_SA
    K = float(TOPK_MAX)
    nb = i + 1
    q0 = i * TQ
    krow = lax.broadcasted_iota(jnp.int32, (KB, TQ), 0)
    qchunk = (q0 + lax.broadcasted_iota(jnp.int32, (KB, TQ), 1)) // CHUNK

    iwt = iwt_ref[...]

    def score_block(j, carry):
        smax, smin = carry
        k0 = pl.multiple_of(j * KB, KB)
        ik = ik_ref[0, pl.ds(k0, KB), :]
        ik_even, ik_odd = ik[:, :LANES], ik[:, LANES:]
        sc = jnp.zeros((KB, TQ), F32)
        for p in range(IDX_HEADS // 2):
            iqp = iq_ref[0, :, p * LANES:(p + 1) * LANES]
            sc = sc + iwt[2 * p:2 * p + 1, :] * jnp.maximum(_dot_t(ik_even, iqp), 0.0)
            sc = sc + iwt[2 * p + 1:2 * p + 2, :] * jnp.maximum(_dot_t(ik_odd, iqp), 0.0)
        adm = (k0 + krow) // CHUNK <= qchunk
        sc = jnp.where(adm, sc, -jnp.inf)
        key_scr[pl.ds(k0, KB), :] = _score_key(sc)
        smax = jnp.maximum(smax, sc.max(axis=0, keepdims=True))
        smin = jnp.where(j < i, jnp.minimum(smin, sc.min(axis=0, keepdims=True)), smin)
        return smax, smin

    smax, smin = lax.fori_loop(0, nb, score_block,
                               (jnp.full((1, TQ), -jnp.inf, F32), jnp.full((1, TQ), jnp.inf, F32)))

    @pl.when(nb % 2 == 1)
    def _():
        key_scr[pl.ds(pl.multiple_of(nb * KB, KB), KB), :] = jnp.full((KB, TQ), KEY_ALL - 1, jnp.int32)

    def count_ge(cand):
        def body(j, acc):
            blk = key_scr[pl.ds(pl.multiple_of(j * (2 * KB), 2 * KB), 2 * KB), :]
            ones = jnp.where(blk >= cand, 1.0, 0.0)
            return acc + ones.reshape(COUNT_CHAINS, -1, 8, TQ).sum(axis=1)
        acc = lax.fori_loop(0, (nb + 1) // 2, body, jnp.zeros((COUNT_CHAINS, 8, TQ), F32))
        return acc.sum(axis=0).sum(axis=0, keepdims=True)

    def search():
        def unkey(k):
            return lax.bitcast_convert_type(jnp.where(k < 0, jnp.int32(INT_MIN) - k, k), F32)

        def is_active(lo, hi, clo):
            return jnp.logical_and(clo > K, hi > lo + 1)

        def cond(st):
            _, lo, hi, clo, _ = st
            act = jnp.where(is_active(lo, hi, clo), 1.0, 0.0)
            return jnp.max(jnp.maximum(act[:, :LANES], act[:, LANES:])) > 0.0

        first = _score_key(smin)

        def step(_, st):
            it, lo, hi, clo, chi = st
            active = is_active(lo, hi, clo)
            lf, hf = unkey(lo), unkey(hi)
            lc = jnp.log(clo)
            frac = jnp.clip((lc - math.log(K - 0.5)) / (lc - jnp.log(jnp.maximum(chi, 0.5))), 0.05, 0.95)
            cand = _score_key(lf + frac * (hf - lf))
            cand = jnp.where(it % 3 == 2, (lo >> 1) + (hi >> 1) + (lo & hi & 1), cand)
            cand = jnp.where(it == 0, first, cand)
            cand = jnp.where(active, jnp.clip(cand, lo + 1, hi - 1), lo)
            cnt = count_ge(cand)
            up = jnp.logical_and(active, cnt >= K)
            down = jnp.logical_and(active, cnt < K)
            return (it + 1, jnp.where(up, cand, lo), jnp.where(down, cand, hi),
                    jnp.where(up, cnt, clo), jnp.where(down, cnt, chi))

        lo0 = jnp.full((1, TQ), KEY_ALL - 1, jnp.int32)
        hi0 = _score_key(smax) + 1
        clo0 = jnp.zeros((1, TQ), F32) + ((nb + 1) // 2 * (2 * KB)).astype(F32)
        st = (jnp.int32(0), lo0, hi0, clo0, jnp.zeros((1, TQ), F32))
        st = lax.fori_loop(0, SEARCH_FIRST_ROUND, step, st)
        st = lax.while_loop(cond, lambda s: lax.fori_loop(0, SEARCH_ROUND, step, s), st)
        return st[1], st[3]

    def no_search():
        return jnp.full((1, TQ), KEY_ALL, jnp.int32), jnp.full((1, TQ), K, F32)

    t, cnt_t = lax.cond(i > 0, search, no_search)
    t = jnp.maximum(t, KEY_ALL)

    @pl.when(jnp.max(cnt_t) > K)
    def _():
        allowed = K - count_ge(t + 1)
        r = lax.broadcasted_iota(jnp.int32, (KB, KB), 0)
        c = lax.broadcasted_iota(jnp.int32, (KB, KB), 1)
        earlier = jnp.where(c < r, 1.0, 0.0).astype(BF16)

        def body(j, seen):
            sl = pl.ds(pl.multiple_of(j * KB, KB), KB)
            blk = key_scr[sl, :]
            eq = jnp.where(blk == t, 1.0, 0.0)
            rank = _dot(earlier, eq.astype(BF16)) + seen
            demote = eq * jnp.where(rank >= allowed, 1.0, 0.0)
            key_scr[sl, :] = jnp.where(demote > 0.5, t - 1, blk)
            return seen + eq.sum(axis=0, keepdims=True)

        lax.fori_loop(0, nb, body, jnp.zeros((1, TQ), F32))

    def scores(j):
        k0 = pl.multiple_of(j * KB, KB)
        boff = pl.multiple_of(jnp.clip(j - (i - 2), 0, 2) * KB, KB)
        kblk = kv_ref[0, pl.ds(k0, KB), :]
        sel = key_scr[pl.ds(k0, KB), :] >= t
        out = []
        for hd in range(SA_HEADS):
            s = _dot_t(kblk, q_ref[0, :, hd * LANES:(hd + 1) * LANES]) + nbias_ref[hd, pl.ds(boff, KB), :]
            out.append(jnp.where(sel, s, NEG_INF))
        return tuple(out)

    def softmax_pv(j, s_all, st):
        vt = vt_ref[:, pl.ds(pl.multiple_of(j * KB, KB), KB)]
        out = []
        for hd in range(SA_HEADS):
            m, l, acc = st[hd]
            s = s_all[hd]
            mn = jnp.maximum(m, s.max(axis=0, keepdims=True))
            p = jnp.exp(s - mn)
            alpha = jnp.exp(m - mn)
            out.append((mn, alpha * l + p.sum(axis=0, keepdims=True), alpha * acc + _dot(vt, p.astype(BF16))))
        return tuple(out)

    def body(j, carry):
        s_cur, st = carry
        s_next = scores(j + 1)
        return s_next, softmax_pv(j, s_cur, st)

    st = tuple((jnp.full((1, TQ), NEG_INF, F32), jnp.zeros((1, TQ), F32), jnp.zeros((HEAD_DIM, TQ), F32))
               for _ in range(SA_HEADS))
    s_last, st = lax.fori_loop(0, i, body, (scores(0), st))
    st = softmax_pv(i, s_last, st)
    y_t = jnp.concatenate([acc / l for _, l, acc in st], axis=0)
    o_ref[0] = _group_norm_t(y_t, g_ref[...])


def _sparse_attention(saq, sakv, svt, iq, ik, iwt, nbias, g):
    B, S, _ = saq.shape
    TQ = TQ_SA
    W = GROUP_W
    nt = S // TQ
    return pl.pallas_call(
        _sa_kernel,
        grid=(B, nt),
        in_specs=[pl.BlockSpec((1, TQ, 512), lambda b, i: (b, i, 0)),
                  pl.BlockSpec((1, S, 128), lambda b, i: (b, 0, 0)),
                  pl.BlockSpec((HEAD_DIM, S), lambda b, i: (0, b)),
                  pl.BlockSpec((1, TQ, 512), lambda b, i: (b, i, 0)),
                  pl.BlockSpec((1, S, 256), lambda b, i: (b, 0, 0)),
                  pl.BlockSpec((IWT_ROWS, TQ), lambda b, i: (0, b * nt + i)),
                  pl.BlockSpec(nbias.shape, lambda b, i: (0, 0, 0)),
                  pl.BlockSpec((1, W), lambda b, i: (0, 0))],
        out_specs=pl.BlockSpec((1, TQ, W), lambda b, i: (b, i, 0)),
        out_shape=jax.ShapeDtypeStruct((B, S, W), BF16),
        scratch_shapes=[pltpu.VMEM((S, TQ), jnp.int32)],
        compiler_params=_cparams(2),
        name="sparse_attention",
    )(saq, sakv, svt, iq, ik, iwt, nbias, g)


def _group_norm_t(y_t, g):
    inv = lax.rsqrt(jnp.mean(y_t * y_t, axis=0, keepdims=True) + EPS)
    return ((y_t * inv).T * g).astype(BF16)


def _mla_kernel(q_ref, k_ref, vt_ref, g_ref, o_ref):
    i = pl.program_id(1)
    TQ = TQ_MLA
    kch = lax.broadcasted_iota(jnp.int32, (TQ, TQ), 0) // CHUNK
    qch = lax.broadcasted_iota(jnp.int32, (TQ, TQ), 1) // CHUNK

    def scores(j):
        k0 = pl.multiple_of(j * TQ, TQ)
        causal = kch <= qch + (i - j) * (TQ // CHUNK)
        out = []
        for hd in range(MLA_HEADS):
            cols = slice(hd * LANES, (hd + 1) * LANES)
            s = _dot_t(k_ref[0, pl.ds(k0, TQ), cols], q_ref[0, :, cols])
            out.append(jnp.where(causal, s, NEG_INF))
        return tuple(out)

    def softmax_pv(j, s_all, st):
        k0 = pl.multiple_of(j * TQ, TQ)
        out = []
        for hd in range(MLA_HEADS):
            m, l, acc = st[hd]
            s = s_all[hd]
            mn = jnp.maximum(m, s.max(axis=0, keepdims=True))
            p = jnp.exp(s - mn)
            alpha = jnp.exp(m - mn)
            vt = vt_ref[hd * MLA_V:(hd + 1) * MLA_V, pl.ds(k0, TQ)]
            out.append((mn, alpha * l + p.sum(axis=0, keepdims=True),
                        alpha * acc + _dot(vt, p.astype(BF16))))
        return tuple(out)

    def body(j, carry):
        s_cur, st = carry
        s_next = scores(j + 1)
        return s_next, softmax_pv(j, s_cur, st)

    st = tuple((jnp.full((1, TQ), NEG_INF, F32), jnp.zeros((1, TQ), F32), jnp.zeros((MLA_V, TQ), F32))
               for _ in range(MLA_HEADS))
    s_last, st = lax.fori_loop(0, i, body, (scores(0), st))
    st = softmax_pv(i, s_last, st)
    y_t = jnp.concatenate([acc / l for _, l, acc in st], axis=0)
    o_ref[0] = _group_norm_t(y_t, g_ref[...])


def _latent_attention(mq, mk, mvt, g):
    B, S, _ = mq.shape
    TQ = TQ_MLA
    W = GROUP_W
    return pl.pallas_call(
        _mla_kernel,
        grid=(B, S // TQ),
        in_specs=[pl.BlockSpec((1, TQ, 512), lambda b, i: (b, i, 0)),
                  pl.BlockSpec((1, S, 512), lambda b, i: (b, 0, 0)),
                  pl.BlockSpec((W, S), lambda b, i: (0, b)),
                  pl.BlockSpec((1, W), lambda b, i: (0, 0))],
        out_specs=pl.BlockSpec((1, TQ, W), lambda b, i: (b, i, 0)),
        out_shape=jax.ShapeDtypeStruct((B, S, W), BF16),
        compiler_params=_cparams(2),
        name="latent_attention",
    )(mq, mk, mvt, g)


def _ffn_kernel(ya_ref, yb_ref, yc_ref, yd_ref, x_ref, mod_ref, wout_ref, gffn_ref, w1_ref, w3_ref, w2_ref,
                gfin_ref, o_ref, acc_scr, *, final):
    gt1 = mod_ref[0, 2:3, :]
    sh2 = mod_ref[0, 3:4, :]
    sc2 = mod_ref[0, 4:5, :]
    gt2 = mod_ref[0, 5:6, :]
    attn = _dot(ya_ref[...], wout_ref[0:GROUP_W, :])
    for gi, y_ref in enumerate((yb_ref, yc_ref, yd_ref), start=1):
        attn = attn + _dot(y_ref[...], wout_ref[gi * GROUP_W:(gi + 1) * GROUP_W, :])
    x1 = x_ref[...] + gt1 * attn
    h = (_rms(x1, gffn_ref[...]) * (1.0 + sc2) + sh2).astype(BF16)
    for ci in range(D_FF // FF_CHUNK):
        cols = slice(ci * FF_CHUNK, (ci + 1) * FF_CHUNK)
        a = _dot(h, w1_ref[:, cols])
        gate = (a * jax.nn.sigmoid(a) * _dot(h, w3_ref[:, cols])).astype(BF16)
        part = _dot(gate, w2_ref[cols, :])
        if ci == 0:
            acc_scr[...] = part
        else:
            acc_scr[...] += part
    x2 = x1 + gt2 * acc_scr[...]
    o_ref[...] = _rms(x2, gfin_ref[...]) if final else x2


def _out_ffn(ys, x2, mod, wout, gffn, w1, w3, w2, gfin, S, final):
    N, D = x2.shape
    TM = TM_FFN
    nt = S // TM

    def full(a):
        return pl.BlockSpec(a.shape, lambda i: (0,) * a.ndim, pipeline_mode=pl.Buffered(1))

    def tok(w):
        return pl.BlockSpec((TM, w), lambda i: (i, 0))

    return pl.pallas_call(
        functools.partial(_ffn_kernel, final=final),
        grid=(N // TM,),
        in_specs=[tok(GROUP_W)] * 4 + [tok(D), pl.BlockSpec((1, 6, D), lambda i: (i // nt, 0, 0)),
                                       full(wout), full(gffn), full(w1), full(w3), full(w2), full(gfin)],
        out_specs=tok(D),
        out_shape=jax.ShapeDtypeStruct((N, D), F32),
        scratch_shapes=[pltpu.VMEM((TM, D), F32)],
        compiler_params=_cparams(1),
        name="out_ffn_final" if final else "out_ffn",
    )(*ys, x2, mod, wout, gffn, w1, w3, w2, gfin)


def _t5_bucket(rel):
    nb = T5_BUCKETS // 2
    max_exact = nb // 2
    ret = jnp.where(rel > 0, nb, 0)
    n = jnp.abs(rel)
    nf = jnp.maximum(n, 1).astype(jnp.float32)
    large = max_exact + (jnp.log(nf / max_exact) / math.log(T5_MAX_DIST / max_exact)
                         * (nb - max_exact)).astype(jnp.int32)
    large = jnp.minimum(large, nb - 1)
    return ret + jnp.where(n < max_exact, n, large)


def _rope_tables(S):
    half = MLA_ROPE // 2
    freqs = ROPE_BASE ** (-jnp.arange(half, dtype=F32) / half)
    ang = jnp.arange(S, dtype=jnp.int32).astype(F32)[:, None] * freqs[None, :]
    cos, sin = jnp.cos(ang), jnp.sin(ang)
    cos2 = jnp.concatenate([cos, cos], axis=1)
    sin2 = jnp.concatenate([-sin, sin], axis=1)
    zeros = jnp.zeros((S, LANES - MLA_NOPE - MLA_ROPE), F32)
    scale = (MLA_NOPE + MLA_ROPE) ** -0.5
    cosq = jnp.concatenate([jnp.full((S, MLA_NOPE), scale, F32), cos2 * scale, zeros], axis=1)
    sinq = jnp.concatenate([jnp.zeros((S, MLA_NOPE), F32), sin2 * scale, zeros], axis=1)
    cosk = jnp.concatenate([jnp.zeros((S, MLA_NOPE), F32), cos2, zeros], axis=1)
    sink = jnp.concatenate([jnp.zeros((S, MLA_NOPE), F32), sin2, zeros], axis=1)
    return cosq, sinq, cosk, sink


def _pack_in_weight(w):
    part = {n: w[:, IN_OFFS[k]:IN_OFFS[k + 1]] for k, n in enumerate(
        ('pool_u', 'ca_q', 'ca_k', 'ca_v', 'sa_q', 'sa_k', 'sa_v', 'idx_q', 'idx_k', 'idx_w',
         'mla_cq', 'mla_ckv', 'mla_kr'))}
    D = w.shape[0]
    z = lambda n: jnp.zeros((D, n), F32)
    qscale = HEAD_DIM ** -0.5
    saq = part['sa_q'].reshape(D, SA_HEADS, HEAD_DIM) * qscale
    saq = jnp.concatenate([saq, jnp.zeros_like(saq)], axis=2).reshape(D, SA_HEADS * LANES)
    kr = part['mla_kr']
    kr_swap = jnp.concatenate([kr[:, MLA_ROPE // 2:], kr[:, :MLA_ROPE // 2]], axis=1)
    pad_r = LANES - MLA_NOPE - MLA_ROPE
    cols = [part['pool_u'], part['ca_q'] * qscale, part['ca_k'], part['ca_v'], saq,
            part['sa_k'], part['sa_v'], part['idx_q'],
            part['idx_k'], z(IDX_DIM), z(IDX_DIM), part['idx_k'],
            part['mla_cq'], part['mla_ckv'],
            z(MLA_NOPE), kr, z(pad_r), z(MLA_NOPE), kr_swap, z(pad_r)]
    out = jnp.concatenate(cols, axis=1)
    assert out.shape[1] == C_END
    wt = jnp.concatenate([part['idx_w'].T, jnp.zeros((IWT_ROWS - IDX_HEADS, D), F32), part['sa_v'].T], axis=0)
    return out.astype(BF16), wt.astype(BF16)


def _pack_mla_weights(w_uq, w_ukv):
    R = w_uq.shape[0]
    pad = jnp.zeros((R, MLA_HEADS, LANES - MLA_NOPE - MLA_ROPE), F32)
    rope_w = w_uq[:, :, MLA_NOPE:]
    rope_sw = jnp.concatenate([rope_w[:, :, MLA_ROPE // 2:], rope_w[:, :, :MLA_ROPE // 2]], axis=2)
    wq = jnp.concatenate([w_uq, pad], axis=2).reshape(R, MLA_HEADS * LANES)
    wqs = jnp.concatenate([jnp.zeros((R, MLA_HEADS, MLA_NOPE), F32), rope_sw, pad],
                          axis=2).reshape(R, MLA_HEADS * LANES)
    Rk = w_ukv.shape[0]
    wk = jnp.concatenate([w_ukv[:, :, :MLA_NOPE], jnp.zeros((Rk, MLA_HEADS, LANES - MLA_NOPE), F32)],
                         axis=2).reshape(Rk, MLA_HEADS * LANES)
    wvt = w_ukv[:, :, MLA_NOPE:].reshape(Rk, MLA_HEADS * MLA_V).T
    return wq.astype(BF16), wqs.astype(BF16), wk.astype(BF16), wvt.astype(BF16)


def _toeplitz(vec, rows, cols):
    L = vec.shape[-1]
    assert cols <= L - 1
    flat = jnp.tile(vec, (1, rows))[:, :rows * (L - 1)]
    return flat.reshape(vec.shape[0], rows, L - 1)[:, :, :cols]


def _signed_mod_range(L, hi):
    d = np.arange(L)
    return np.where(d <= hi, d, d - L)


def _band_bias(rel_table):
    L = 768
    d = _signed_mod_range(L, CA_WIN - 1)
    ridx = np.clip(CA_LEFT_CHUNKS * CHUNK - d, -(CHUNK - 1), CA_MAX_REL) + (CHUNK - 1)
    bias = _toeplitz(rel_table[:, ridx].astype(F32), TQ_CA, CA_WIN)
    r = np.arange(TQ_CA)[:, None]
    c = np.arange(CA_WIN)[None, :]
    qc = r // CHUNK + CA_LEFT_CHUNKS
    kc = c // CHUNK
    valid = (kc <= qc) & (kc >= qc - CA_LEFT_CHUNKS)
    return jnp.where(valid[None], bias, NEG_INF)


def _t5_bias(t5_table):
    TQ = TQ_SA
    L = 3 * TQ
    e = _signed_mod_range(L, TQ - 1)
    rel = jnp.asarray(-e - TQ, jnp.int32)
    far = t5_table[_t5_bucket(jnp.int32(-(TQ + 1)))].astype(F32)
    vec = (t5_table[_t5_bucket(rel)].astype(F32) - far[None, :]).T
    near = _toeplitz(vec, 2 * TQ, TQ)
    return jnp.concatenate([jnp.zeros((SA_HEADS, TQ, TQ), F32), near], axis=1)


def kernel(x, c, t5_table, w_mod, b_mod, g_mix, w_in, pool_w, pool_scale, ca_rel, mla_g_cq, mla_g_ckv,
           mla_w_uq, mla_w_ukv, g_group, w_out, g_ffn, ffn_w1, ffn_w3, ffn_w2, g_final):
    B, S, D = x.shape
    assert D == D_MODEL and S % TM_PROJ == 0 and S % TQ_SA == 0 and S >= 4 * TOPK_MAX
    N = B * S
    mod_all = _modulation(c, w_mod, b_mod)
    rope_tabs = _rope_tables(S)
    nbias = _t5_bias(t5_table)
    row = lambda v: v.reshape(1, -1).astype(F32)
    x2 = x.reshape(N, D)
    for l in range(DEPTH):
        mod = mod_all[l].reshape(B, 6, D)
        w1, wt = _pack_in_weight(w_in[l])
        wq, wqs, wk, wvt = _pack_mla_weights(mla_w_uq[l], mla_w_ukv[l])
        (pool_u, ca, saq, sakv, iq, ik, iwt, svt, mq, mk, mvt) = _inproj(
            x2, mod, row(g_mix[l]), w1, wt, row(mla_g_cq[l]), row(mla_g_ckv[l]), wq, wqs, wk, wvt, rope_tabs, S)
        gg = g_group[l].reshape(4, 1, GROUP_W).astype(F32)
        wbd = jax.scipy.linalg.block_diag(*[pool_w[l, gi] for gi in range(len(POOL_WINDOWS))]).astype(BF16)
        bsw = lambda a: a.reshape(B, S, a.shape[-1])
        y_a = _pool(bsw(pool_u), wbd, row(pool_scale[l]), gg[0])
        y_b = _chunk_attention(bsw(ca), _band_bias(ca_rel[l]), gg[1])
        y_c = _sparse_attention(bsw(saq), bsw(sakv), svt, bsw(iq), bsw(ik), iwt, nbias, gg[2])
        y_d = _latent_attention(bsw(mq), bsw(mk), mvt, gg[3])
        ys = [y.reshape(N, GROUP_W) for y in (y_a, y_b, y_c, y_d)]
        x2 = _out_ffn(ys, x2, mod, w_out[l].astype(BF16), row(g_ffn[l]), ffn_w1[l].astype(BF16),
                      ffn_w3[l].astype(BF16), ffn_w2[l].astype(BF16), row(g_final), S,
                      final=(l == DEPTH - 1))
    return x2.reshape(B, S, D)
```

```python
import functools
import math

import jax
import jax.numpy as jnp
from jax import lax
import numpy as np
from jax.experimental import pallas as pl
from jax.experimental.pallas import tpu as pltpu

F32 = jnp.float32
BF16 = jnp.bfloat16

D_MODEL = 1024
DEPTH = 2
CHUNK = 64
EPS = 1e-6
NEG_INF = -1e30
GROUP_W = 256
HEAD_DIM = 64
POOL_WINDOWS = (2, 4, 8, 16)
POOL_GROUP = 64
POOL_HALO = 16
CA_HEADS = 4
CA_LEFT_CHUNKS = 8
CA_MAX_REL = 256
SA_HEADS = 4
IDX_HEADS = 8
IDX_DIM = 64
TOPK_MAX = 256
MLA_HEADS = 4
MLA_NOPE = 64
MLA_ROPE = 32
MLA_V = 64
Q_LORA = 256
KV_LORA = 128
ROPE_BASE = 10000.0
T5_BUCKETS = 32
T5_MAX_DIST = 128
D_FF = 2816
IN_WIDTHS = (256, 256, 256, 256, 256, 64, 64, 512, 64, 8, 256, 128, 32)
IN_OFFS = tuple(int(v) for v in np.cumsum((0,) + IN_WIDTHS))

LANES = 128
VMEM_LIMIT = 56 * 1024 * 1024

TM_PROJ = 512
TM_FFN = 512
TP_POOL = 512
TQ_CA = 256
CA_WIN = TQ_CA + CA_LEFT_CHUNKS * CHUNK
CA_NBLK = CA_WIN // TQ_CA
IWT_ROWS = 16
TQ_SA = 256
KB_SA = 256
COUNT_CHAINS = 8
SEARCH_FIRST_ROUND = 12
SEARCH_ROUND = 4
TQ_MLA = 256
FF_CHUNK = 256

C_POOL = 0
C_CA = C_POOL + 256
C_SAQ = C_CA + 512
C_SAKV = C_SAQ + 512
C_IQ = C_SAKV + 128
C_IK = C_IQ + 512
C_CQ = C_IK + 256
C_CKV = C_CQ + 256
C_KRF = C_CKV + 128
C_KRS = C_KRF + 128
C_END = C_KRS + 128

INT_MIN = -2 ** 31
KEY_ALL = INT_MIN - int(np.array(-np.inf, np.float32).view(np.int32)) + 1


def _cparams(n_axes):
    return pltpu.CompilerParams(dimension_semantics=("arbitrary",) * n_axes,
                                vmem_limit_bytes=VMEM_LIMIT)


def _rms(x, g):
    return x * lax.rsqrt(jnp.mean(x * x, axis=-1, keepdims=True) + EPS) * g


def _dot(a, b):
    return jnp.dot(a, b, preferred_element_type=F32)


def _dot_t(a, b):
    return lax.dot_general(a, b, (((1,), (1,)), ((), ())), preferred_element_type=F32)


def _mod_kernel(c_ref, w_ref, b_ref, o_ref):
    c = c_ref[...]
    act = c * jax.nn.sigmoid(c)
    o_ref[0] = jnp.dot(act, w_ref[0], precision=lax.Precision.HIGHEST,
                       preferred_element_type=F32) + b_ref[0]


def _modulation(c, w_mod, b_mod):
    L, D, W = w_mod.shape
    B = c.shape[0]
    nj = W // D
    return pl.pallas_call(
        _mod_kernel,
        grid=(L, nj),
        in_specs=[pl.BlockSpec((B, D), lambda l, j: (0, 0)),
                  pl.BlockSpec((1, D, D), lambda l, j: (l, 0, j)),
                  pl.BlockSpec((1, 1, D), lambda l, j: (l, 0, j))],
        out_specs=pl.BlockSpec((1, B, D), lambda l, j: (l, 0, j)),
        out_shape=jax.ShapeDtypeStruct((L, B, W), F32),
        compiler_params=_cparams(2),
        name="modulation",
    )(c, w_mod, b_mod.reshape(L, 1, W))


def _inproj_kernel(x_ref, mod_ref, gmix_ref, w_ref, wt_ref, gcq_ref, gckv_ref, wq_ref, wqs_ref, wk_ref, wvt_ref,
                   cosq_ref, sinq_ref, cosk_ref, sink_ref,
                   pool_o, ca_o, saq_o, sakv_o, iq_o, ik_o, iwt_o, svt_o, cavt_o, mq_o, mk_o, mvt_o):
    sh1 = mod_ref[0, 0:1, :]
    sc1 = mod_ref[0, 1:2, :]
    h = (_rms(x_ref[...], gmix_ref[...]) * (1.0 + sc1) + sh1).astype(BF16)

    def seg(a, b):
        return _dot(h, w_ref[:, a:b])

    pool_o[...] = seg(C_POOL, C_CA)
    ca_o[...] = seg(C_CA, C_SAQ).astype(BF16)
    saq_o[...] = seg(C_SAQ, C_SAKV).astype(BF16)
    sakv_o[...] = seg(C_SAKV, C_IQ).astype(BF16)
    iq_o[...] = seg(C_IQ, C_IK).astype(BF16)
    ik_o[...] = seg(C_IK, C_CQ).astype(BF16)
    tr = _dot_t(wt_ref[...], h)
    iwt_o[...] = tr[0:IWT_ROWS] * ((IDX_HEADS ** -0.5) * (IDX_DIM ** -0.5))
    svt_o[...] = tr[IWT_ROWS:IWT_ROWS + HEAD_DIM].astype(BF16)
    cavt_o[...] = tr[IWT_ROWS + HEAD_DIM:].astype(BF16)

    qn = _rms(seg(C_CQ, C_CKV), gcq_ref[...]).astype(BF16)
    qf = _dot(qn, wq_ref[...])
    qs = _dot(qn, wqs_ref[...])
    cosq = jnp.concatenate([cosq_ref[...]] * MLA_HEADS, axis=1)
    sinq = jnp.concatenate([sinq_ref[...]] * MLA_HEADS, axis=1)
    mq_o[...] = (qf * cosq + qs * sinq).astype(BF16)

    kvn = _rms(seg(C_CKV, C_KRF), gckv_ref[...]).astype(BF16)
    kvf = _dot(kvn, wk_ref[...])
    krope = seg(C_KRF, C_KRS) * cosk_ref[...] + seg(C_KRS, C_END) * sink_ref[...]
    for hd in range(MLA_HEADS):
        mk_o[:, hd * LANES:(hd + 1) * LANES] = (kvf[:, hd * LANES:(hd + 1) * LANES] + krope).astype(BF16)
    mvt_o[...] = _dot_t(wvt_ref[...], kvn).astype(BF16)


def _inproj(x2, mod, gmix, w1, wt, gcq, gckv, wq, wqs, wk, wvt, rope_tabs, S):
    N, D = x2.shape
    TM = TM_PROJ
    nt = S // TM
    cosq, sinq, cosk, sink = rope_tabs

    def full(a):
        return pl.BlockSpec(a.shape, lambda i: (0,) * a.ndim)

    def tok(w):
        return pl.BlockSpec((TM, w), lambda i: (i, 0))

    tab = pl.BlockSpec((TM, LANES), lambda i: (i % nt, 0))
    def tokt(rows):
        return pl.BlockSpec((rows, TM), lambda i: (0, i))

    outs = [(256, F32, True), (512, BF16, True), (512, BF16, True), (128, BF16, True), (512, BF16, True),
            (256, BF16, True), (IWT_ROWS, F32, False), (HEAD_DIM, BF16, False), (GROUP_W, BF16, False),
            (512, BF16, True), (512, BF16, True), (GROUP_W, BF16, False)]
    return pl.pallas_call(
        _inproj_kernel,
        grid=(N // TM,),
        in_specs=[tok(D),
                  pl.BlockSpec((1, 6, D), lambda i: (i // nt, 0, 0)),
                  full(gmix), full(w1), full(wt), full(gcq), full(gckv), full(wq), full(wqs), full(wk), full(wvt),
                  tab, tab, tab, tab],
        out_specs=[tok(w) if tm else tokt(w) for w, _, tm in outs],
        out_shape=[jax.ShapeDtypeStruct((N, w) if tm else (w, N), dt) for w, dt, tm in outs],
        compiler_params=_cparams(1),
        name="inproj",
    )(x2, mod, gmix, w1, wt, gcq, gckv, wq, wqs, wk, wvt, cosq, sinq, cosk, sink)


def _pool_kernel(u_ref, halo_ref, w_ref, scale_ref, g_ref, o_ref, pad_scr):
    i = pl.program_id(1)
    TP = u_ref.shape[1]
    u = u_ref[0]
    pad_scr[0:POOL_HALO, :] = jnp.where(i > 0, halo_ref[0], 0.0)
    pad_scr[POOL_HALO:, :] = u

    def shifted(j):
        return pad_scr[POOL_HALO - j:POOL_HALO - j + TP, :]

    lane = lax.broadcasted_iota(jnp.int32, (TP, GROUP_W), 1)
    w2 = u + shifted(1)
    w4 = w2 + shifted(2) + shifted(3)
    w8 = w4
    for j in range(4, 8):
        w8 = w8 + shifted(j)
    w16 = w8
    for j in range(8, 16):
        w16 = w16 + shifted(j)
    win = jnp.where(lane < 64, w2, jnp.where(lane < 128, w4, jnp.where(lane < 192, w8, w16)))
    wlen = jnp.where(lane < 64, 2, jnp.where(lane < 128, 4, jnp.where(lane < 192, 8, 16)))
    t = i * TP + lax.broadcasted_iota(jnp.int32, (TP, GROUP_W), 0)
    cnt = jnp.minimum(t + 1, wlen).astype(F32)
    d = (win / cnt - u).astype(BF16)
    y = _dot(d, w_ref[...]) * scale_ref[...]
    o_ref[0] = _rms(y, g_ref[...]).astype(BF16)


def _pool(u, wbd, scale, g):
    B, S, W = u.shape
    TP = TP_POOL
    hb = TP // POOL_HALO
    return pl.pallas_call(
        _pool_kernel,
        grid=(B, S // TP),
        in_specs=[pl.BlockSpec((1, TP, W), lambda b, i: (b, i, 0)),
                  pl.BlockSpec((1, POOL_HALO, W), lambda b, i: (b, jnp.maximum(i * hb - 1, 0), 0)),
                  pl.BlockSpec((W, W), lambda b, i: (0, 0)),
                  pl.BlockSpec((1, W), lambda b, i: (0, 0)),
                  pl.BlockSpec((1, W), lambda b, i: (0, 0))],
        out_specs=pl.BlockSpec((1, TP, W), lambda b, i: (b, i, 0)),
        out_shape=jax.ShapeDtypeStruct((B, S, W), BF16),
        scratch_shapes=[pltpu.VMEM((POOL_HALO + TP, W), F32)],
        compiler_params=_cparams(2),
        name="pool_mixer",
    )(u, u, wbd, scale, g)


def _ca_kernel(q_ref, k_ref, vt_ref, bias_ref, g_ref, o_ref):
    i = pl.program_id(1)
    TQ = TQ_CA
    lane = lax.broadcasted_iota(jnp.int32, (TQ, LANES), 1)
    outs = []
    for hd in range(CA_HEADS):
        cols = slice((hd // 2) * LANES, (hd // 2 + 1) * LANES)
        keep = (lane < HEAD_DIM) if hd % 2 == 0 else (lane >= HEAD_DIM)
        qh = jnp.where(keep, q_ref[0, :, cols].astype(F32), 0.0).astype(BF16)
        parts = []
        for j in range(CA_NBLK):
            kb = i - (CA_NBLK - 1) + j
            start = pl.multiple_of(jnp.maximum(kb, 0) * TQ, TQ)
            s = _dot_t(k_ref[0, pl.ds(start, TQ), cols], qh) + bias_ref[hd, j * TQ:(j + 1) * TQ, :]
            parts.append(jnp.where(kb >= 0, s, NEG_INF))
        m = parts[0].max(axis=0, keepdims=True)
        for s in parts[1:]:
            m = jnp.maximum(m, s.max(axis=0, keepdims=True))
        l = jnp.zeros((1, TQ), F32)
        acc = jnp.zeros((HEAD_DIM, TQ), F32)
        for j in range(CA_NBLK):
            kb = i - (CA_NBLK - 1) + j
            start = pl.multiple_of(jnp.maximum(kb, 0) * TQ, TQ)
            p = jnp.exp(parts[j] - m)
            l = l + p.sum(axis=0, keepdims=True)
            acc = acc + _dot(vt_ref[hd * HEAD_DIM:(hd + 1) * HEAD_DIM, pl.ds(start, TQ)], p.astype(BF16))
        outs.append(acc / l)
    o_ref[0] = _group_norm_t(jnp.concatenate(outs, axis=0), g_ref[...])


def _chunk_attention(caqk, cavt, bias, g):
    B, S, _ = caqk.shape
    W = GROUP_W
    TQ = TQ_CA
    return pl.pallas_call(
        _ca_kernel,
        grid=(B, S // TQ),
        in_specs=[pl.BlockSpec((1, TQ, W), lambda b, i: (b, i, 0)),
                  pl.BlockSpec((1, S, W), lambda b, i: (b, 0, 1)),
                  pl.BlockSpec((W, S), lambda b, i: (0, b)),
                  pl.BlockSpec(bias.shape, lambda b, i: (0, 0, 0)),
                  pl.BlockSpec((1, W), lambda b, i: (0, 0))],
        out_specs=pl.BlockSpec((1, TQ, W), lambda b, i: (b, i, 0)),
        out_shape=jax.ShapeDtypeStruct((B, S, W), BF16),
        compiler_params=_cparams(2),
        name="band_attention",
    )(caqk, caqk, cavt, bias, g)


def _score_key(score):
    b = lax.bitcast_convert_type(score, jnp.int32)
    return jnp.where(b < 0, jnp.int32(INT_MIN) - b, b)


def _sa_kernel(q_ref, kv_ref, vt_ref, iq_ref, ik_ref, iwt_ref, nbias_ref, g_ref, o_ref, key_scr):
    i = pl.program_id(1)
    TQ, KB = TQ_SA, KB_SA
    K = float(TOPK_MAX)
    nb = i + 1
    q0 = i * TQ
    krow = lax.broadcasted_iota(jnp.int32, (KB, TQ), 0)
    qchunk = (q0 + lax.broadcasted_iota(jnp.int32, (KB, TQ), 1)) // CHUNK

    iwt = iwt_ref[...]

    def score_block(j, carry):
        smax, smin = carry
        k0 = pl.multiple_of(j * KB, KB)
        ik = ik_ref[0, pl.ds(k0, KB), :]
        ik2 = jnp.concatenate([ik[:, :LANES], ik[:, LANES:]], axis=0)
        sc = jnp.zeros((KB, TQ), F32)
        for p in range(IDX_HEADS // 2):
            logits = _dot_t(ik2, iq_ref[0, :, p * LANES:(p + 1) * LANES])
            sc = sc + iwt[2 * p:2 * p + 1, :] * jnp.maximum(logits[:KB], 0.0)
            sc = sc + iwt[2 * p + 1:2 * p + 2, :] * jnp.maximum(logits[KB:], 0.0)
        adm = (k0 + krow) // CHUNK <= qchunk
        sc = jnp.where(adm, sc, -jnp.inf)
        key_scr[pl.ds(k0, KB), :] = _score_key(sc)
        smax = jnp.maximum(smax, sc.max(axis=0, keepdims=True))
        smin = jnp.where(j < i, jnp.minimum(smin, sc.min(axis=0, keepdims=True)), smin)
        return smax, smin

    smax, smin = lax.fori_loop(
        0, (nb + 1) // 2, lambda j, c: score_block(2 * j + 1, score_block(2 * j, c)),
        (jnp.full((1, TQ), -jnp.inf, F32), jnp.full((1, TQ), jnp.inf, F32)))

    def count_ge(cand):
        def body(j, acc):
            blk = key_scr[pl.ds(pl.multiple_of(j * (2 * KB), 2 * KB), 2 * KB), :]
            ones = jnp.where(blk >= cand, 1.0, 0.0)
            return acc + ones.reshape(COUNT_CHAINS, -1, 8, TQ).sum(axis=1)
        acc = lax.fori_loop(0, (nb + 1) // 2, body, jnp.zeros((COUNT_CHAINS, 8, TQ), F32))
        return acc.sum(axis=0).sum(axis=0, keepdims=True)

    def search():
        def unkey(k):
            return lax.bitcast_convert_type(jnp.where(k < 0, jnp.int32(INT_MIN) - k, k), F32)

        def is_active(lo, hi, clo):
            return jnp.logical_and(clo > K, hi > lo + 1)

        def cond(st):
            _, lo, hi, clo, _ = st
            act = jnp.where(is_active(lo, hi, clo), 1.0, 0.0)
            return jnp.max(jnp.maximum(act[:, :LANES], act[:, LANES:])) > 0.0

        first = _score_key(smin)

        def step(_, st):
            it, lo, hi, clo, chi = st
            active = is_active(lo, hi, clo)
            lf, hf = unkey(lo), unkey(hi)
            lc = jnp.log(clo)
            frac = jnp.clip((lc - math.log(K - 0.5)) / (lc - jnp.log(jnp.maximum(chi, 0.5))), 0.05, 0.95)
            cand = _score_key(lf + frac * (hf - lf))
            cand = jnp.where(it % 3 == 2, (lo >> 1) + (hi >> 1) + (lo & hi & 1), cand)
            cand = jnp.where(it == 0, first, cand)
            cand = jnp.where(active, jnp.clip(cand, lo + 1, hi - 1), lo)
            cnt = count_ge(cand)
            up = jnp.logical_and(active, cnt >= K)
            down = jnp.logical_and(active, cnt < K)
            return (it + 1, jnp.where(up, cand, lo), jnp.where(down, cand, hi),
                    jnp.where(up, cnt, clo), jnp.where(down, cnt, chi))

        lo0 = jnp.full((1, TQ), KEY_ALL - 1, jnp.int32)
        hi0 = _score_key(smax) + 1
        clo0 = jnp.zeros((1, TQ), F32) + ((nb + 1) // 2 * (2 * KB)).astype(F32)
        st = (jnp.int32(0), lo0, hi0, clo0, jnp.zeros((1, TQ), F32))
        st = lax.fori_loop(0, SEARCH_FIRST_ROUND, step, st)
        st = lax.while_loop(cond, lambda s: lax.fori_loop(0, SEARCH_ROUND, step, s), st)
        return st[1], st[3]

    def no_search():
        return jnp.full((1, TQ), KEY_ALL, jnp.int32), jnp.full((1, TQ), K, F32)

    t, cnt_t = lax.cond(i > 0, search, no_search)
    t = jnp.maximum(t, KEY_ALL)

    @pl.when(jnp.max(cnt_t) > K)
    def _():
        allowed = K - count_ge(t + 1)
        r = lax.broadcasted_iota(jnp.int32, (KB, KB), 0)
        c = lax.broadcasted_iota(jnp.int32, (KB, KB), 1)
        earlier = jnp.where(c < r, 1.0, 0.0).astype(BF16)

        def body(j, seen):
            sl = pl.ds(pl.multiple_of(j * KB, KB), KB)
            blk = key_scr[sl, :]
            eq = jnp.where(blk == t, 1.0, 0.0)
            rank = _dot(earlier, eq.astype(BF16)) + seen
            demote = eq * jnp.where(rank >= allowed, 1.0, 0.0)
            key_scr[sl, :] = jnp.where(demote > 0.5, t - 1, blk)
            return seen + eq.sum(axis=0, keepdims=True)

        lax.fori_loop(0, nb, body, jnp.zeros((1, TQ), F32))

    def scores(j):
        k0 = pl.multiple_of(j * KB, KB)
        boff = pl.multiple_of(jnp.clip(j - (i - 2), 0, 2) * KB, KB)
        kblk = kv_ref[0, pl.ds(k0, KB), :]
        sel = key_scr[pl.ds(k0, KB), :] >= t
        out = []
        for hd in range(SA_HEADS):
            s = _dot_t(kblk, q_ref[0, :, hd * LANES:(hd + 1) * LANES]) + nbias_ref[hd, pl.ds(boff, KB), :]
            out.append(jnp.where(sel, s, NEG_INF))
        return tuple(out)

    def softmax_pv(j, s_all, st):
        vt = vt_ref[:, pl.ds(pl.multiple_of(j * KB, KB), KB)]
        out = []
        for hd in range(SA_HEADS):
            m, l, acc = st[hd]
            s = s_all[hd]
            mn = jnp.maximum(m, s.max(axis=0, keepdims=True))
            p = jnp.exp(s - mn)
            alpha = jnp.exp(m - mn)
            out.append((mn, alpha * l + p.sum(axis=0, keepdims=True), alpha * acc + _dot(vt, p.astype(BF16))))
        return tuple(out)

    def body(j, carry):
        s_cur, st = carry
        s_next = scores(j + 1)
        return s_next, softmax_pv(j, s_cur, st)

    st = tuple((jnp.full((1, TQ), NEG_INF, F32), jnp.zeros((1, TQ), F32), jnp.zeros((HEAD_DIM, TQ), F32))
               for _ in range(SA_HEADS))
    s_last, st = lax.fori_loop(0, i, body, (scores(0), st))
    st = softmax_pv(i, s_last, st)
    y_t = jnp.concatenate([acc / l for _, l, acc in st], axis=0)
    o_ref[0] = _group_norm_t(y_t, g_ref[...])


def _sparse_attention(saq, sakv, svt, iq, ik, iwt, nbias, g):
    B, S, _ = saq.shape
    TQ = TQ_SA
    W = GROUP_W
    nt = S // TQ
    return pl.pallas_call(
        _sa_kernel,
        grid=(B, nt),
        in_specs=[pl.BlockSpec((1, TQ, 512), lambda b, i: (b, i, 0)),
                  pl.BlockSpec((1, S, 128), lambda b, i: (b, 0, 0)),
                  pl.BlockSpec((HEAD_DIM, S), lambda b, i: (0, b)),
                  pl.BlockSpec((1, TQ, 512), lambda b, i: (b, i, 0)),
                  pl.BlockSpec((1, S, 256), lambda b, i: (b, 0, 0)),
                  pl.BlockSpec((IWT_ROWS, TQ), lambda b, i: (0, b * nt + i)),
                  pl.BlockSpec(nbias.shape, lambda b, i: (0, 0, 0)),
                  pl.BlockSpec((1, W), lambda b, i: (0, 0))],
        out_specs=pl.BlockSpec((1, TQ, W), lambda b, i: (b, i, 0)),
        out_shape=jax.ShapeDtypeStruct((B, S, W), BF16),
        scratch_shapes=[pltpu.VMEM((S, TQ), jnp.int32)],
        compiler_params=_cparams(2),
        name="sparse_attention",
    )(saq, sakv, svt, iq, ik, iwt, nbias, g)


def _group_norm_t(y_t, g):
    inv = lax.rsqrt(jnp.mean(y_t * y_t, axis=0, keepdims=True) + EPS)
    return ((y_t * inv).T * g).astype(BF16)


def _mla_kernel(q_ref, k_ref, vt_ref, g_ref, o_ref):
    i = pl.program_id(1)
    TQ = TQ_MLA
    kch = lax.broadcasted_iota(jnp.int32, (TQ, TQ), 0) // CHUNK
    qch = lax.broadcasted_iota(jnp.int32, (TQ, TQ), 1) // CHUNK

    def scores(j):
        k0 = pl.multiple_of(j * TQ, TQ)
        causal = kch <= qch + (i - j) * (TQ // CHUNK)
        out = []
        for hd in range(MLA_HEADS):
            cols = slice(hd * LANES, (hd + 1) * LANES)
            s = _dot_t(k_ref[0, pl.ds(k0, TQ), cols], q_ref[0, :, cols])
            out.append(jnp.where(causal, s, NEG_INF))
        return tuple(out)

    def softmax_pv(j, s_all, st):
        k0 = pl.multiple_of(j * TQ, TQ)
        out = []
        for hd in range(MLA_HEADS):
            m, l, acc = st[hd]
            s = s_all[hd]
            mn = jnp.maximum(m, s.max(axis=0, keepdims=True))
            p = jnp.exp(s - mn)
            alpha = jnp.exp(m - mn)
            vt = vt_ref[hd * MLA_V:(hd + 1) * MLA_V, pl.ds(k0, TQ)]
            out.append((mn, alpha * l + p.sum(axis=0, keepdims=True),
                        alpha * acc + _dot(vt, p.astype(BF16))))
        return tuple(out)

    def body(j, carry):
        s_cur, st = carry
        s_next = scores(j + 1)
        return s_next, softmax_pv(j, s_cur, st)

    st = tuple((jnp.full((1, TQ), NEG_INF, F32), jnp.zeros((1, TQ), F32), jnp.zeros((MLA_V, TQ), F32))
               for _ in range(MLA_HEADS))
    s_last, st = lax.fori_loop(0, i, body, (scores(0), st))
    st = softmax_pv(i, s_last, st)
    y_t = jnp.concatenate([acc / l for _, l, acc in st], axis=0)
    o_ref[0] = _group_norm_t(y_t, g_ref[...])


def _latent_attention(mq, mk, mvt, g):
    B, S, _ = mq.shape
    TQ = TQ_MLA
    W = GROUP_W
    return pl.pallas_call(
        _mla_kernel,
        grid=(B, S // TQ),
        in_specs=[pl.BlockSpec((1, TQ, 512), lambda b, i: (b, i, 0)),
                  pl.BlockSpec((1, S, 512), lambda b, i: (b, 0, 0)),
                  pl.BlockSpec((W, S), lambda b, i: (0, b)),
                  pl.BlockSpec((1, W), lambda b, i: (0, 0))],
        out_specs=pl.BlockSpec((1, TQ, W), lambda b, i: (b, i, 0)),
        out_shape=jax.ShapeDtypeStruct((B, S, W), BF16),
        compiler_params=_cparams(2),
        name="latent_attention",
    )(mq, mk, mvt, g)


def _ffn_kernel(ya_ref, yb_ref, yc_ref, yd_ref, x_ref, mod_ref, wout_ref, gffn_ref, w1_ref, w3_ref, w2_ref,
                gfin_ref, o_ref, acc_scr, *, final):
    gt1 = mod_ref[0, 2:3, :]
    sh2 = mod_ref[0, 3:4, :]
    sc2 = mod_ref[0, 4:5, :]
    gt2 = mod_ref[0, 5:6, :]
    attn = _dot(ya_ref[...], wout_ref[0:GROUP_W, :])
    for gi, y_ref in enumerate((yb_ref, yc_ref, yd_ref), start=1):
        attn = attn + _dot(y_ref[...], wout_ref[gi * GROUP_W:(gi + 1) * GROUP_W, :])
    x1 = x_ref[...] + gt1 * attn
    h = (_rms(x1, gffn_ref[...]) * (1.0 + sc2) + sh2).astype(BF16)
    for ci in range(D_FF // FF_CHUNK):
        cols = slice(ci * FF_CHUNK, (ci + 1) * FF_CHUNK)
        a = _dot(h, w1_ref[:, cols])
        gate = (a * jax.nn.sigmoid(a) * _dot(h, w3_ref[:, cols])).astype(BF16)
        part = _dot(gate, w2_ref[cols, :])
        if ci == 0:
            acc_scr[...] = part
        else:
            acc_scr[...] += part
    x2 = x1 + gt2 * acc_scr[...]
    o_ref[...] = _rms(x2, gfin_ref[...]) if final else x2


def _out_ffn(ys, x2, mod, wout, gffn, w1, w3, w2, gfin, S, final):
    N, D = x2.shape
    TM = TM_FFN
    nt = S // TM

    def full(a):
        return pl.BlockSpec(a.shape, lambda i: (0,) * a.ndim, pipeline_mode=pl.Buffered(1))

    def tok(w):
        return pl.BlockSpec((TM, w), lambda i: (i, 0))

    return pl.pallas_call(
        functools.partial(_ffn_kernel, final=final),
        grid=(N // TM,),
        in_specs=[tok(GROUP_W)] * 4 + [tok(D), pl.BlockSpec((1, 6, D), lambda i: (i // nt, 0, 0)),
                                       full(wout), full(gffn), full(w1), full(w3), full(w2), full(gfin)],
        out_specs=tok(D),
        out_shape=jax.ShapeDtypeStruct((N, D), F32),
        scratch_shapes=[pltpu.VMEM((TM, D), F32)],
        compiler_params=_cparams(1),
        name="out_ffn_final" if final else "out_ffn",
    )(*ys, x2, mod, wout, gffn, w1, w3, w2, gfin)


def _t5_bucket(rel):
    nb = T5_BUCKETS // 2
    max_exact = nb // 2
    ret = jnp.where(rel > 0, nb, 0)
    n = jnp.abs(rel)
    nf = jnp.maximum(n, 1).astype(jnp.float32)
    large = max_exact + (jnp.log(nf / max_exact) / math.log(T5_MAX_DIST / max_exact)
                         * (nb - max_exact)).astype(jnp.int32)
    large = jnp.minimum(large, nb - 1)
    return ret + jnp.where(n < max_exact, n, large)


def _rope_tables(S):
    half = MLA_ROPE // 2
    freqs = ROPE_BASE ** (-jnp.arange(half, dtype=F32) / half)
    ang = jnp.arange(S, dtype=jnp.int32).astype(F32)[:, None] * freqs[None, :]
    cos, sin = jnp.cos(ang), jnp.sin(ang)
    cos2 = jnp.concatenate([cos, cos], axis=1)
    sin2 = jnp.concatenate([-sin, sin], axis=1)
    zeros = jnp.zeros((S, LANES - MLA_NOPE - MLA_ROPE), F32)
    scale = (MLA_NOPE + MLA_ROPE) ** -0.5
    cosq = jnp.concatenate([jnp.full((S, MLA_NOPE), scale, F32), cos2 * scale, zeros], axis=1)
    sinq = jnp.concatenate([jnp.zeros((S, MLA_NOPE), F32), sin2 * scale, zeros], axis=1)
    cosk = jnp.concatenate([jnp.zeros((S, MLA_NOPE), F32), cos2, zeros], axis=1)
    sink = jnp.concatenate([jnp.zeros((S, MLA_NOPE), F32), sin2, zeros], axis=1)
    return cosq, sinq, cosk, sink


def _pack_in_weight(w):
    part = {n: w[:, IN_OFFS[k]:IN_OFFS[k + 1]] for k, n in enumerate(
        ('pool_u', 'ca_q', 'ca_k', 'ca_v', 'sa_q', 'sa_k', 'sa_v', 'idx_q', 'idx_k', 'idx_w',
         'mla_cq', 'mla_ckv', 'mla_kr'))}
    D = w.shape[0]
    z = lambda n: jnp.zeros((D, n), F32)
    qscale = HEAD_DIM ** -0.5
    saq = part['sa_q'].reshape(D, SA_HEADS, HEAD_DIM) * qscale
    saq = jnp.concatenate([saq, jnp.zeros_like(saq)], axis=2).reshape(D, SA_HEADS * LANES)
    kr = part['mla_kr']
    kr_swap = jnp.concatenate([kr[:, MLA_ROPE // 2:], kr[:, :MLA_ROPE // 2]], axis=1)
    pad_r = LANES - MLA_NOPE - MLA_ROPE
    cols = [part['pool_u'], part['ca_q'] * qscale, part['ca_k'], saq,
            part['sa_k'], part['sa_v'], part['idx_q'],
            part['idx_k'], z(IDX_DIM), z(IDX_DIM), part['idx_k'],
            part['mla_cq'], part['mla_ckv'],
            z(MLA_NOPE), kr, z(pad_r), z(MLA_NOPE), kr_swap, z(pad_r)]
    out = jnp.concatenate(cols, axis=1)
    assert out.shape[1] == C_END
    wt = jnp.concatenate([part['idx_w'].T, jnp.zeros((IWT_ROWS - IDX_HEADS, D), F32), part['sa_v'].T,
                          part['ca_v'].T], axis=0)
    return out.astype(BF16), wt.astype(BF16)


def _pack_mla_weights(w_uq, w_ukv):
    R = w_uq.shape[0]
    pad = jnp.zeros((R, MLA_HEADS, LANES - MLA_NOPE - MLA_ROPE), F32)
    rope_w = w_uq[:, :, MLA_NOPE:]
    rope_sw = jnp.concatenate([rope_w[:, :, MLA_ROPE // 2:], rope_w[:, :, :MLA_ROPE // 2]], axis=2)
    wq = jnp.concatenate([w_uq, pad], axis=2).reshape(R, MLA_HEADS * LANES)
    wqs = jnp.concatenate([jnp.zeros((R, MLA_HEADS, MLA_NOPE), F32), rope_sw, pad],
                          axis=2).reshape(R, MLA_HEADS * LANES)
    Rk = w_ukv.shape[0]
    wk = jnp.concatenate([w_ukv[:, :, :MLA_NOPE], jnp.zeros((Rk, MLA_HEADS, LANES - MLA_NOPE), F32)],
                         axis=2).reshape(Rk, MLA_HEADS * LANES)
    wvt = w_ukv[:, :, MLA_NOPE:].reshape(Rk, MLA_HEADS * MLA_V).T
    return wq.astype(BF16), wqs.astype(BF16), wk.astype(BF16), wvt.astype(BF16)


def _toeplitz(vec, rows, cols):
    L = vec.shape[-1]
    assert cols <= L - 1
    flat = jnp.tile(vec, (1, rows))[:, :rows * (L - 1)]
    return flat.reshape(vec.shape[0], rows, L - 1)[:, :, :cols]


def _signed_mod_range(L, hi):
    d = np.arange(L)
    return np.where(d <= hi, d, d - L)


def _band_bias(rel_table):
    L = CA_WIN + TQ_CA
    e = _signed_mod_range(L, TQ_CA - 1)
    ridx = np.clip(CA_LEFT_CHUNKS * CHUNK + e, -(CHUNK - 1), CA_MAX_REL) + (CHUNK - 1)
    bias = _toeplitz(rel_table[:, ridx].astype(F32), CA_WIN, TQ_CA)
    kc = np.arange(CA_WIN)[:, None] // CHUNK
    qc = np.arange(TQ_CA)[None, :] // CHUNK + CA_LEFT_CHUNKS
    valid = (kc <= qc) & (kc >= qc - CA_LEFT_CHUNKS)
    return jnp.where(valid[None], bias, NEG_INF)


def _t5_bias(t5_table):
    TQ = TQ_SA
    L = 3 * TQ
    e = _signed_mod_range(L, TQ - 1)
    rel = jnp.asarray(-e - TQ, jnp.int32)
    far = t5_table[_t5_bucket(jnp.int32(-(TQ + 1)))].astype(F32)
    vec = (t5_table[_t5_bucket(rel)].astype(F32) - far[None, :]).T
    near = _toeplitz(vec, 2 * TQ, TQ)
    return jnp.concatenate([jnp.zeros((SA_HEADS, TQ, TQ), F32), near], axis=1)


def kernel(x, c, t5_table, w_mod, b_mod, g_mix, w_in, pool_w, pool_scale, ca_rel, mla_g_cq, mla_g_ckv,
           mla_w_uq, mla_w_ukv, g_group, w_out, g_ffn, ffn_w1, ffn_w3, ffn_w2, g_final):
    B, S, D = x.shape
    assert D == D_MODEL and S % TM_PROJ == 0 and S % TQ_SA == 0 and S >= 4 * TOPK_MAX
    N = B * S
    mod_all = _modulation(c, w_mod, b_mod)
    rope_tabs = _rope_tables(S)
    nbias = _t5_bias(t5_table)
    row = lambda v: v.reshape(1, -1).astype(F32)
    x2 = x.reshape(N, D)
    for l in range(DEPTH):
        mod = mod_all[l].reshape(B, 6, D)
        w1, wt = _pack_in_weight(w_in[l])
        wq, wqs, wk, wvt = _pack_mla_weights(mla_w_uq[l], mla_w_ukv[l])
        (pool_u, ca, saq, sakv, iq, ik, iwt, svt, cavt, mq, mk, mvt) = _inproj(
            x2, mod, row(g_mix[l]), w1, wt, row(mla_g_cq[l]), row(mla_g_ckv[l]), wq, wqs, wk, wvt, rope_tabs, S)
        gg = g_group[l].reshape(4, 1, GROUP_W).astype(F32)
        wbd = jax.scipy.linalg.block_diag(*[pool_w[l, gi] for gi in range(len(POOL_WINDOWS))]).astype(BF16)
        bsw = lambda a: a.reshape(B, S, a.shape[-1])
        y_a = _pool(bsw(pool_u), wbd, row(pool_scale[l]), gg[0])
        y_b = _chunk_attention(bsw(ca), cavt, _band_bias(ca_rel[l]), gg[1])
        y_c = _sparse_attention(bsw(saq), bsw(sakv), svt, bsw(iq), bsw(ik), iwt, nbias, gg[2])
        y_d = _latent_attention(bsw(mq), bsw(mk), mvt, gg[3])
        ys = [y.reshape(N, GROUP_W) for y in (y_a, y_b, y_c, y_d)]
        x2 = _out_ffn(ys, x2, mod, w_out[l].astype(BF16), row(g_ffn[l]), ffn_w1[l].astype(BF16),
                      ffn_w3[l].astype(BF16), ffn_w2[l].astype(BF16), row(g_final), S,
                      final=(l == DEPTH - 1))
    return x2.reshape(B, S, D)
```

```python
import functools
import math

import jax
import jax.numpy as jnp
from jax import lax
import numpy as np
from jax.experimental import pallas as pl
from jax.experimental.pallas import tpu as pltpu

F32 = jnp.float32
BF16 = jnp.bfloat16

D_MODEL = 1024
DEPTH = 2
CHUNK = 64
EPS = 1e-6
NEG_INF = -1e30
GROUP_W = 256
HEAD_DIM = 64
POOL_WINDOWS = (2, 4, 8, 16)
POOL_GROUP = 64
POOL_HALO = 16
CA_HEADS = 4
CA_LEFT_CHUNKS = 8
CA_MAX_REL = 256
SA_HEADS = 4
IDX_HEADS = 8
IDX_DIM = 64
TOPK_MAX = 256
MLA_HEADS = 4
MLA_NOPE = 64
MLA_ROPE = 32
MLA_V = 64
Q_LORA = 256
KV_LORA = 128
ROPE_BASE = 10000.0
T5_BUCKETS = 32
T5_MAX_DIST = 128
D_FF = 2816
IN_WIDTHS = (256, 256, 256, 256, 256, 64, 64, 512, 64, 8, 256, 128, 32)
IN_OFFS = tuple(int(v) for v in np.cumsum((0,) + IN_WIDTHS))

LANES = 128
VMEM_LIMIT = 56 * 1024 * 1024

TM_PROJ = 512
TM_FFN = 512
TP_POOL = 512
TQ_CA = 256
CA_WIN = TQ_CA + CA_LEFT_CHUNKS * CHUNK
CA_NBLK = CA_WIN // TQ_CA
IWT_ROWS = 16
TQ_SA = 256
KB_SA = 256
COUNT_CHAINS = 2
SEARCH_FIRST_ROUND = 12
SEARCH_ROUND = 4
TQ_MLA = 256
FF_CHUNK = 256

C_POOL = 0
C_CA = C_POOL + 256
C_SAQ = C_CA + 512
C_SAKV = C_SAQ + 512
C_IQ = C_SAKV + 128
C_IK = C_IQ + 512
C_CQ = C_IK + 256
C_CKV = C_CQ + 256
C_KRF = C_CKV + 128
C_KRS = C_KRF + 128
C_END = C_KRS + 128

INT_MIN = -2 ** 31
KEY_ALL = INT_MIN - int(np.array(-np.inf, np.float32).view(np.int32)) + 1


def _cparams(n_axes):
    return pltpu.CompilerParams(dimension_semantics=("arbitrary",) * n_axes,
                                vmem_limit_bytes=VMEM_LIMIT)


def _rms(x, g):
    return x * lax.rsqrt(jnp.mean(x * x, axis=-1, keepdims=True) + EPS) * g


def _dot(a, b):
    return jnp.dot(a, b, preferred_element_type=F32)


def _dot_t(a, b):
    return lax.dot_general(a, b, (((1,), (1,)), ((), ())), preferred_element_type=F32)


def _mod_kernel(c_ref, w_ref, b_ref, o_ref):
    c = c_ref[...]
    act = c * jax.nn.sigmoid(c)
    o_ref[0] = jnp.dot(act, w_ref[0], precision=lax.Precision.HIGHEST,
                       preferred_element_type=F32) + b_ref[0]


def _modulation(c, w_mod, b_mod):
    L, D, W = w_mod.shape
    B = c.shape[0]
    nj = W // D
    return pl.pallas_call(
        _mod_kernel,
        grid=(L, nj),
        in_specs=[pl.BlockSpec((B, D), lambda l, j: (0, 0)),
                  pl.BlockSpec((1, D, D), lambda l, j: (l, 0, j)),
                  pl.BlockSpec((1, 1, D), lambda l, j: (l, 0, j))],
        out_specs=pl.BlockSpec((1, B, D), lambda l, j: (l, 0, j)),
        out_shape=jax.ShapeDtypeStruct((L, B, W), F32),
        compiler_params=_cparams(2),
        name="modulation",
    )(c, w_mod, b_mod.reshape(L, 1, W))


def _inproj_kernel(x_ref, mod_ref, gmix_ref, w_ref, wt_ref, gcq_ref, gckv_ref, wq_ref, wqs_ref, wk_ref, wvt_ref,
                   cosq_ref, sinq_ref, cosk_ref, sink_ref,
                   pool_o, ca_o, saq_o, sakv_o, iq_o, ik_o, iwt_o, svt_o, cavt_o, mq_o, mk_o, mvt_o):
    sh1 = mod_ref[0, 0:1, :]
    sc1 = mod_ref[0, 1:2, :]
    h = (_rms(x_ref[...], gmix_ref[...]) * (1.0 + sc1) + sh1).astype(BF16)

    def seg(a, b):
        return _dot(h, w_ref[:, a:b])

    pool_o[...] = seg(C_POOL, C_CA)
    ca_o[...] = seg(C_CA, C_SAQ).astype(BF16)
    saq_o[...] = seg(C_SAQ, C_SAKV).astype(BF16)
    sakv_o[...] = seg(C_SAKV, C_IQ).astype(BF16)
    iq_o[...] = seg(C_IQ, C_IK).astype(BF16)
    ik_o[...] = seg(C_IK, C_CQ).astype(BF16)
    tr = _dot_t(wt_ref[...], h)
    iwt_o[...] = tr[0:IWT_ROWS] * ((IDX_HEADS ** -0.5) * (IDX_DIM ** -0.5))
    svt_o[...] = tr[IWT_ROWS:IWT_ROWS + HEAD_DIM].astype(BF16)
    cavt_o[...] = tr[IWT_ROWS + HEAD_DIM:].astype(BF16)

    qn = _rms(seg(C_CQ, C_CKV), gcq_ref[...]).astype(BF16)
    qf = _dot(qn, wq_ref[...])
    qs = _dot(qn, wqs_ref[...])
    cosq = jnp.concatenate([cosq_ref[...]] * MLA_HEADS, axis=1)
    sinq = jnp.concatenate([sinq_ref[...]] * MLA_HEADS, axis=1)
    mq_o[...] = (qf * cosq + qs * sinq).astype(BF16)

    kvn = _rms(seg(C_CKV, C_KRF), gckv_ref[...]).astype(BF16)
    kvf = _dot(kvn, wk_ref[...])
    krope = seg(C_KRF, C_KRS) * cosk_ref[...] + seg(C_KRS, C_END) * sink_ref[...]
    for hd in range(MLA_HEADS):
        mk_o[:, hd * LANES:(hd + 1) * LANES] = (kvf[:, hd * LANES:(hd + 1) * LANES] + krope).astype(BF16)
    mvt_o[...] = _dot_t(wvt_ref[...], kvn).astype(BF16)


def _inproj(x2, mod, gmix, w1, wt, gcq, gckv, wq, wqs, wk, wvt, rope_tabs, S):
    N, D = x2.shape
    TM = TM_PROJ
    nt = S // TM
    cosq, sinq, cosk, sink = rope_tabs

    def full(a):
        return pl.BlockSpec(a.shape, lambda i: (0,) * a.ndim)

    def tok(w):
        return pl.BlockSpec((TM, w), lambda i: (i, 0))

    tab = pl.BlockSpec((TM, LANES), lambda i: (i % nt, 0))
    def tokt(rows):
        return pl.BlockSpec((rows, TM), lambda i: (0, i))

    outs = [(256, F32, True), (512, BF16, True), (512, BF16, True), (128, BF16, True), (512, BF16, True),
            (256, BF16, True), (IWT_ROWS, F32, False), (HEAD_DIM, BF16, False), (GROUP_W, BF16, False),
            (512, BF16, True), (512, BF16, True), (GROUP_W, BF16, False)]
    return pl.pallas_call(
        _inproj_kernel,
        grid=(N // TM,),
        in_specs=[tok(D),
                  pl.BlockSpec((1, 6, D), lambda i: (i // nt, 0, 0)),
                  full(gmix), full(w1), full(wt), full(gcq), full(gckv), full(wq), full(wqs), full(wk), full(wvt),
                  tab, tab, tab, tab],
        out_specs=[tok(w) if tm else tokt(w) for w, _, tm in outs],
        out_shape=[jax.ShapeDtypeStruct((N, w) if tm else (w, N), dt) for w, dt, tm in outs],
        compiler_params=_cparams(1),
        name="inproj",
    )(x2, mod, gmix, w1, wt, gcq, gckv, wq, wqs, wk, wvt, cosq, sinq, cosk, sink)


def _pool_kernel(u_ref, halo_ref, w_ref, scale_ref, g_ref, o_ref, pad_scr):
    i = pl.program_id(1)
    TP = u_ref.shape[1]
    u = u_ref[0]
    pad_scr[0:POOL_HALO, :] = jnp.where(i > 0, halo_ref[0], 0.0)
    pad_scr[POOL_HALO:, :] = u

    def shifted(j):
        return pad_scr[POOL_HALO - j:POOL_HALO - j + TP, :]

    lane = lax.broadcasted_iota(jnp.int32, (TP, GROUP_W), 1)
    w2 = u + shifted(1)
    w4 = w2 + shifted(2) + shifted(3)
    w8 = w4
    for j in range(4, 8):
        w8 = w8 + shifted(j)
    w16 = w8
    for j in range(8, 16):
        w16 = w16 + shifted(j)
    win = jnp.where(lane < 64, w2, jnp.where(lane < 128, w4, jnp.where(lane < 192, w8, w16)))
    wlen = jnp.where(lane < 64, 2, jnp.where(lane < 128, 4, jnp.where(lane < 192, 8, 16)))
    t = i * TP + lax.broadcasted_iota(jnp.int32, (TP, GROUP_W), 0)
    cnt = jnp.minimum(t + 1, wlen).astype(F32)
    d = (win / cnt - u).astype(BF16)
    y = _dot(d, w_ref[...]) * scale_ref[...]
    o_ref[0] = _rms(y, g_ref[...]).astype(BF16)


def _pool(u, wbd, scale, g):
    B, S, W = u.shape
    TP = TP_POOL
    hb = TP // POOL_HALO
    return pl.pallas_call(
        _pool_kernel,
        grid=(B, S // TP),
        in_specs=[pl.BlockSpec((1, TP, W), lambda b, i: (b, i, 0)),
                  pl.BlockSpec((1, POOL_HALO, W), lambda b, i: (b, jnp.maximum(i * hb - 1, 0), 0)),
                  pl.BlockSpec((W, W), lambda b, i: (0, 0)),
                  pl.BlockSpec((1, W), lambda b, i: (0, 0)),
                  pl.BlockSpec((1, W), lambda b, i: (0, 0))],
        out_specs=pl.BlockSpec((1, TP, W), lambda b, i: (b, i, 0)),
        out_shape=jax.ShapeDtypeStruct((B, S, W), BF16),
        scratch_shapes=[pltpu.VMEM((POOL_HALO + TP, W), F32)],
        compiler_params=_cparams(2),
        name="pool_mixer",
    )(u, u, wbd, scale, g)


def _ca_kernel(q_ref, k_ref, vt_ref, bias_ref, g_ref, o_ref):
    i = pl.program_id(1)
    TQ = TQ_CA
    lane = lax.broadcasted_iota(jnp.int32, (TQ, LANES), 1)
    outs = []
    for hd in range(CA_HEADS):
        cols = slice((hd // 2) * LANES, (hd // 2 + 1) * LANES)
        keep = (lane < HEAD_DIM) if hd % 2 == 0 else (lane >= HEAD_DIM)
        qh = jnp.where(keep, q_ref[0, :, cols].astype(F32), 0.0).astype(BF16)
        parts = []
        for j in range(CA_NBLK):
            kb = i - (CA_NBLK - 1) + j
            start = pl.multiple_of(jnp.maximum(kb, 0) * TQ, TQ)
            s = _dot_t(k_ref[0, pl.ds(start, TQ), cols], qh) + bias_ref[hd, j * TQ:(j + 1) * TQ, :]
            parts.append(jnp.where(kb >= 0, s, NEG_INF))
        m = parts[0].max(axis=0, keepdims=True)
        for s in parts[1:]:
            m = jnp.maximum(m, s.max(axis=0, keepdims=True))
        l = jnp.zeros((1, TQ), F32)
        acc = jnp.zeros((HEAD_DIM, TQ), F32)
        for j in range(CA_NBLK):
            kb = i - (CA_NBLK - 1) + j
            start = pl.multiple_of(jnp.maximum(kb, 0) * TQ, TQ)
            p = jnp.exp(parts[j] - m)
            l = l + p.sum(axis=0, keepdims=True)
            acc = acc + _dot(vt_ref[hd * HEAD_DIM:(hd + 1) * HEAD_DIM, pl.ds(start, TQ)], p.astype(BF16))
        outs.append(acc / l)
    o_ref[0] = _group_norm_t(jnp.concatenate(outs, axis=0), g_ref[...])


def _chunk_attention(caqk, cavt, bias, g):
    B, S, _ = caqk.shape
    W = GROUP_W
    TQ = TQ_CA
    return pl.pallas_call(
        _ca_kernel,
        grid=(B, S // TQ),
        in_specs=[pl.BlockSpec((1, TQ, W), lambda b, i: (b, i, 0)),
                  pl.BlockSpec((1, S, W), lambda b, i: (b, 0, 1)),
                  pl.BlockSpec((W, S), lambda b, i: (0, b)),
                  pl.BlockSpec(bias.shape, lambda b, i: (0, 0, 0)),
                  pl.BlockSpec((1, W), lambda b, i: (0, 0))],
        out_specs=pl.BlockSpec((1, TQ, W), lambda b, i: (b, i, 0)),
        out_shape=jax.ShapeDtypeStruct((B, S, W), BF16),
        compiler_params=_cparams(2),
        name="band_attention",
    )(caqk, caqk, cavt, bias, g)


def _score_key(score):
    b = lax.bitcast_convert_type(score, jnp.int32)
    return jnp.where(b < 0, jnp.int32(INT_MIN) - b, b)


def _sa_kernel(q_ref, kv_ref, vt_ref, iq_ref, ik_ref, iwt_ref, nbias_ref, g_ref, o_ref, key_scr):
    i = pl.program_id(1)
    TQ, KB = TQ_SA, KB_SA
    K = float(TOPK_MAX)
    nb = i + 1
    q0 = i * TQ
    krow = lax.broadcasted_iota(jnp.int32, (KB, TQ), 0)
    qchunk = (q0 + lax.broadcasted_iota(jnp.int32, (KB, TQ), 1)) // CHUNK

    iwt = iwt_ref[...]

    def score_block(j, carry):
        smax, smin = carry
        k0 = pl.multiple_of(j * KB, KB)
        ik = ik_ref[0, pl.ds(k0, KB), :]
        ik2 = jnp.concatenate([ik[:, :LANES], ik[:, LANES:]], axis=0)
        sc = jnp.zeros((KB, TQ), F32)
        for p in range(IDX_HEADS // 2):
            logits = _dot_t(ik2, iq_ref[0, :, p * LANES:(p + 1) * LANES])
            sc = sc + iwt[2 * p:2 * p + 1, :] * jnp.maximum(logits[:KB], 0.0)
            sc = sc + iwt[2 * p + 1:2 * p + 2, :] * jnp.maximum(logits[KB:], 0.0)
        adm = (k0 + krow) // CHUNK <= qchunk
        sc = jnp.where(adm, sc, -jnp.inf)
        key_scr[pl.ds(k0, KB), :] = _score_key(sc)
        smax = jnp.maximum(smax, sc.max(axis=0, keepdims=True))
        smin = jnp.where(j < i, jnp.minimum(smin, sc.min(axis=0, keepdims=True)), smin)
        return smax, smin

    smax, smin = lax.fori_loop(
        0, (nb + 1) // 2, lambda j, c: score_block(2 * j + 1, score_block(2 * j, c)),
        (jnp.full((1, TQ), -jnp.inf, F32), jnp.full((1, TQ), jnp.inf, F32)))

    def count_ge(cand):
        def body(j, acc):
            blk = key_scr[pl.ds(pl.multiple_of(j * (2 * KB), 2 * KB), 2 * KB), :]
            ones = jnp.where(blk >= cand, 1.0, 0.0)
            return acc + ones.reshape(COUNT_CHAINS, -1, 8, TQ).sum(axis=1)
        acc = lax.fori_loop(0, (nb + 1) // 2, body, jnp.zeros((COUNT_CHAINS, 8, TQ), F32))
        return acc.sum(axis=0).sum(axis=0, keepdims=True)

    def search():
        def unkey(k):
            return lax.bitcast_convert_type(jnp.where(k < 0, jnp.int32(INT_MIN) - k, k), F32)

        def is_active(lo, hi, clo):
            return jnp.logical_and(clo > K, hi > lo + 1)

        def cond(st):
            _, lo, hi, clo, _ = st
            act = jnp.where(is_active(lo, hi, clo), 1.0, 0.0)
            return jnp.max(jnp.maximum(act[:, :LANES], act[:, LANES:])) > 0.0

        first = _score_key(smin)

        def step(_, st):
            it, lo, hi, clo, chi = st
            active = is_active(lo, hi, clo)
            lf, hf = unkey(lo), unkey(hi)
            lc = jnp.log(clo)
            frac = jnp.clip((lc - math.log(K - 0.5)) / (lc - jnp.log(jnp.maximum(chi, 0.5))), 0.05, 0.95)
            cand = _score_key(lf + frac * (hf - lf))
            cand = jnp.where(it % 3 == 2, (lo >> 1) + (hi >> 1) + (lo & hi & 1), cand)
            cand = jnp.where(it == 0, first, cand)
            cand = jnp.where(active, jnp.clip(cand, lo + 1, hi - 1), lo)
            cnt = count_ge(cand)
            up = jnp.logical_and(active, cnt >= K)
            down = jnp.logical_and(active, cnt < K)
            return (it + 1, jnp.where(up, cand, lo), jnp.where(down, cand, hi),
                    jnp.where(up, cnt, clo), jnp.where(down, cnt, chi))

        lo0 = jnp.full((1, TQ), KEY_ALL - 1, jnp.int32)
        hi0 = _score_key(smax) + 1
        clo0 = jnp.zeros((1, TQ), F32) + ((nb + 1) // 2 * (2 * KB)).astype(F32)
        st = (jnp.int32(0), lo0, hi0, clo0, jnp.zeros((1, TQ), F32))
        st = lax.fori_loop(0, SEARCH_FIRST_ROUND, step, st)
        st = lax.while_loop(cond, lambda s: lax.fori_loop(0, SEARCH_ROUND, step, s), st)
        return st[1], st[3]

    def no_search():
        return jnp.full((1, TQ), KEY_ALL, jnp.int32), jnp.full((1, TQ), K, F32)

    t, cnt_t = lax.cond(i > 0, search, no_search)
    t = jnp.maximum(t, KEY_ALL)

    @pl.when(jnp.max(cnt_t) > K)
    def _():
        allowed = K - count_ge(t + 1)
        r = lax.broadcasted_iota(jnp.int32, (KB, KB), 0)
        c = lax.broadcasted_iota(jnp.int32, (KB, KB), 1)
        earlier = jnp.where(c < r, 1.0, 0.0).astype(BF16)

        def body(j, seen):
            sl = pl.ds(pl.multiple_of(j * KB, KB), KB)
            blk = key_scr[sl, :]
            eq = jnp.where(blk == t, 1.0, 0.0)
            rank = _dot(earlier, eq.astype(BF16)) + seen
            demote = eq * jnp.where(rank >= allowed, 1.0, 0.0)
            key_scr[sl, :] = jnp.where(demote > 0.5, t - 1, blk)
            return seen + eq.sum(axis=0, keepdims=True)

        lax.fori_loop(0, nb, body, jnp.zeros((1, TQ), F32))

    def scores(j):
        k0 = pl.multiple_of(j * KB, KB)
        boff = pl.multiple_of(jnp.clip(j - (i - 2), 0, 2) * KB, KB)
        kblk = kv_ref[0, pl.ds(k0, KB), :]
        sel = key_scr[pl.ds(k0, KB), :] >= t
        out = []
        for hd in range(SA_HEADS):
            s = _dot_t(kblk, q_ref[0, :, hd * LANES:(hd + 1) * LANES]) + nbias_ref[hd, pl.ds(boff, KB), :]
            s = jnp.where(sel, s, NEG_INF)
            out.append((s, s.max(axis=0, keepdims=True)))
        return tuple(out)

    def softmax_pv(j, s_all, st):
        vt = vt_ref[:, pl.ds(pl.multiple_of(j * KB, KB), KB)]
        out = []
        for hd in range(SA_HEADS):
            m, l, acc = st[hd]
            s, smax = s_all[hd]
            mn = jnp.maximum(m, smax)
            p = jnp.exp(s - mn)
            alpha = jnp.exp(m - mn)
            out.append((mn, alpha * l + p.sum(axis=0, keepdims=True), alpha * acc + _dot(vt, p.astype(BF16))))
        return tuple(out)

    def body(j, carry):
        s_cur, st = carry
        s_next = scores(j + 1)
        return s_next, softmax_pv(j, s_cur, st)

    st = tuple((jnp.full((1, TQ), NEG_INF, F32), jnp.zeros((1, TQ), F32), jnp.zeros((HEAD_DIM, TQ), F32))
               for _ in range(SA_HEADS))
    s_last, st = lax.fori_loop(0, i, body, (scores(0), st))
    st = softmax_pv(i, s_last, st)
    y_t = jnp.concatenate([acc / l for _, l, acc in st], axis=0)
    o_ref[0] = _group_norm_t(y_t, g_ref[...])


def _sparse_attention(saq, sakv, svt, iq, ik, iwt, nbias, g):
    B, S, _ = saq.shape
    TQ = TQ_SA
    W = GROUP_W
    nt = S // TQ
    return pl.pallas_call(
        _sa_kernel,
        grid=(B, nt),
        in_specs=[pl.BlockSpec((1, TQ, 512), lambda b, i: (b, i, 0)),
                  pl.BlockSpec((1, S, 128), lambda b, i: (b, 0, 0)),
                  pl.BlockSpec((HEAD_DIM, S), lambda b, i: (0, b)),
                  pl.BlockSpec((1, TQ, 512), lambda b, i: (b, i, 0)),
                  pl.BlockSpec((1, S, 256), lambda b, i: (b, 0, 0)),
                  pl.BlockSpec((IWT_ROWS, TQ), lambda b, i: (0, b * nt + i)),
                  pl.BlockSpec(nbias.shape, lambda b, i: (0, 0, 0)),
                  pl.BlockSpec((1, W), lambda b, i: (0, 0))],
        out_specs=pl.BlockSpec((1, TQ, W), lambda b, i: (b, i, 0)),
        out_shape=jax.ShapeDtypeStruct((B, S, W), BF16),
        scratch_shapes=[pltpu.VMEM((S, TQ), jnp.int32)],
        compiler_params=_cparams(2),
        name="sparse_attention",
    )(saq, sakv, svt, iq, ik, iwt, nbias, g)


def _group_norm_t(y_t, g):
    inv = lax.rsqrt(jnp.mean(y_t * y_t, axis=0, keepdims=True) + EPS)
    return ((y_t * inv).T * g).astype(BF16)


def _mla_kernel(q_ref, k_ref, vt_ref, g_ref, o_ref):
    i = pl.program_id(1)
    TQ = TQ_MLA
    kch = lax.broadcasted_iota(jnp.int32, (TQ, TQ), 0) // CHUNK
    qch = lax.broadcasted_iota(jnp.int32, (TQ, TQ), 1) // CHUNK

    def scores(j):
        k0 = pl.multiple_of(j * TQ, TQ)
        causal = kch <= qch + (i - j) * (TQ // CHUNK)
        out = []
        for hd in range(MLA_HEADS):
            cols = slice(hd * LANES, (hd + 1) * LANES)
            s = _dot_t(k_ref[0, pl.ds(k0, TQ), cols], q_ref[0, :, cols])
            s = jnp.where(causal, s, NEG_INF)
            out.append((s, s.max(axis=0, keepdims=True)))
        return tuple(out)

    def softmax_pv(j, s_all, st):
        k0 = pl.multiple_of(j * TQ, TQ)
        out = []
        for hd in range(MLA_HEADS):
            m, l, acc = st[hd]
            s, smax = s_all[hd]
            mn = jnp.maximum(m, smax)
            p = jnp.exp(s - mn)
            alpha = jnp.exp(m - mn)
            vt = vt_ref[hd * MLA_V:(hd + 1) * MLA_V, pl.ds(k0, TQ)]
            out.append((mn, alpha * l + p.sum(axis=0, keepdims=True),
                        alpha * acc + _dot(vt, p.astype(BF16))))
        return tuple(out)

    def body(j, carry):
        s_cur, st = carry
        s_next = scores(j + 1)
        return s_next, softmax_pv(j, s_cur, st)

    st = tuple((jnp.full((1, TQ), NEG_INF, F32), jnp.zeros((1, TQ), F32), jnp.zeros((MLA_V, TQ), F32))
               for _ in range(MLA_HEADS))
    s_last, st = lax.fori_loop(0, i, body, (scores(0), st))
    st = softmax_pv(i, s_last, st)
    y_t = jnp.concatenate([acc / l for _, l, acc in st], axis=0)
    o_ref[0] = _group_norm_t(y_t, g_ref[...])


def _latent_attention(mq, mk, mvt, g):
    B, S, _ = mq.shape
    TQ = TQ_MLA
    W = GROUP_W
    return pl.pallas_call(
        _mla_kernel,
        grid=(B, S // TQ),
        in_specs=[pl.BlockSpec((1, TQ, 512), lambda b, i: (b, i, 0)),
                  pl.BlockSpec((1, S, 512), lambda b, i: (b, 0, 0)),
                  pl.BlockSpec((W, S), lambda b, i: (0, b)),
                  pl.BlockSpec((1, W), lambda b, i: (0, 0))],
        out_specs=pl.BlockSpec((1, TQ, W), lambda b, i: (b, i, 0)),
        out_shape=jax.ShapeDtypeStruct((B, S, W), BF16),
        compiler_params=_cparams(2),
        name="latent_attention",
    )(mq, mk, mvt, g)


def _ffn_kernel(ya_ref, yb_ref, yc_ref, yd_ref, x_ref, mod_ref, wout_ref, gffn_ref, w1_ref, w3_ref, w2_ref,
                gfin_ref, o_ref, acc_scr, *, final):
    gt1 = mod_ref[0, 2:3, :]
    sh2 = mod_ref[0, 3:4, :]
    sc2 = mod_ref[0, 4:5, :]
    gt2 = mod_ref[0, 5:6, :]
    attn = _dot(ya_ref[...], wout_ref[0:GROUP_W, :])
    for gi, y_ref in enumerate((yb_ref, yc_ref, yd_ref), start=1):
        attn = attn + _dot(y_ref[...], wout_ref[gi * GROUP_W:(gi + 1) * GROUP_W, :])
    x1 = x_ref[...] + gt1 * attn
    h = (_rms(x1, gffn_ref[...]) * (1.0 + sc2) + sh2).astype(BF16)
    for ci in range(D_FF // FF_CHUNK):
        cols = slice(ci * FF_CHUNK, (ci + 1) * FF_CHUNK)
        a = _dot(h, w1_ref[:, cols])
        gate = (a * jax.nn.sigmoid(a) * _dot(h, w3_ref[:, cols])).astype(BF16)
        part = _dot(gate, w2_ref[cols, :])
        if ci == 0:
            acc_scr[...] = part
        else:
            acc_scr[...] += part
    x2 = x1 + gt2 * acc_scr[...]
    o_ref[...] = _rms(x2, gfin_ref[...]) if final else x2


def _out_ffn(ys, x2, mod, wout, gffn, w1, w3, w2, gfin, S, final):
    N, D = x2.shape
    TM = TM_FFN
    nt = S // TM

    def full(a):
        return pl.BlockSpec(a.shape, lambda i: (0,) * a.ndim, pipeline_mode=pl.Buffered(1))

    def tok(w):
        return pl.BlockSpec((TM, w), lambda i: (i, 0))

    return pl.pallas_call(
        functools.partial(_ffn_kernel, final=final),
        grid=(N // TM,),
        in_specs=[tok(GROUP_W)] * 4 + [tok(D), pl.BlockSpec((1, 6, D), lambda i: (i // nt, 0, 0)),
                                       full(wout), full(gffn), full(w1), full(w3), full(w2), full(gfin)],
        out_specs=tok(D),
        out_shape=jax.ShapeDtypeStruct((N, D), F32),
        scratch_shapes=[pltpu.VMEM((TM, D), F32)],
        compiler_params=_cparams(1),
        name="out_ffn_final" if final else "out_ffn",
    )(*ys, x2, mod, wout, gffn, w1, w3, w2, gfin)


def _t5_bucket(rel):
    nb = T5_BUCKETS // 2
    max_exact = nb // 2
    ret = jnp.where(rel > 0, nb, 0)
    n = jnp.abs(rel)
    nf = jnp.maximum(n, 1).astype(jnp.float32)
    large = max_exact + (jnp.log(nf / max_exact) / math.log(T5_MAX_DIST / max_exact)
                         * (nb - max_exact)).astype(jnp.int32)
    large = jnp.minimum(large, nb - 1)
    return ret + jnp.where(n < max_exact, n, large)


def _rope_tables(S):
    half = MLA_ROPE // 2
    freqs = ROPE_BASE ** (-jnp.arange(half, dtype=F32) / half)
    ang = jnp.arange(S, dtype=jnp.int32).astype(F32)[:, None] * freqs[None, :]
    cos, sin = jnp.cos(ang), jnp.sin(ang)
    cos2 = jnp.concatenate([cos, cos], axis=1)
    sin2 = jnp.concatenate([-sin, sin], axis=1)
    zeros = jnp.zeros((S, LANES - MLA_NOPE - MLA_ROPE), F32)
    scale = (MLA_NOPE + MLA_ROPE) ** -0.5
    cosq = jnp.concatenate([jnp.full((S, MLA_NOPE), scale, F32), cos2 * scale, zeros], axis=1)
    sinq = jnp.concatenate([jnp.zeros((S, MLA_NOPE), F32), sin2 * scale, zeros], axis=1)
    cosk = jnp.concatenate([jnp.zeros((S, MLA_NOPE), F32), cos2, zeros], axis=1)
    sink = jnp.concatenate([jnp.zeros((S, MLA_NOPE), F32), sin2, zeros], axis=1)
    return cosq, sinq, cosk, sink


def _pack_in_weight(w):
    part = {n: w[:, IN_OFFS[k]:IN_OFFS[k + 1]] for k, n in enumerate(
        ('pool_u', 'ca_q', 'ca_k', 'ca_v', 'sa_q', 'sa_k', 'sa_v', 'idx_q', 'idx_k', 'idx_w',
         'mla_cq', 'mla_ckv', 'mla_kr'))}
    D = w.shape[0]
    z = lambda n: jnp.zeros((D, n), F32)
    qscale = HEAD_DIM ** -0.5
    saq = part['sa_q'].reshape(D, SA_HEADS, HEAD_DIM) * qscale
    saq = jnp.concatenate([saq, jnp.zeros_like(saq)], axis=2).reshape(D, SA_HEADS * LANES)
    kr = part['mla_kr']
    kr_swap = jnp.concatenate([kr[:, MLA_ROPE // 2:], kr[:, :MLA_ROPE // 2]], axis=1)
    pad_r = LANES - MLA_NOPE - MLA_ROPE
    cols = [part['pool_u'], part['ca_q'] * qscale, part['ca_k'], saq,
            part['sa_k'], part['sa_v'], part['idx_q'],
            part['idx_k'], z(IDX_DIM), z(IDX_DIM), part['idx_k'],
            part['mla_cq'], part['mla_ckv'],
            z(MLA_NOPE), kr, z(pad_r), z(MLA_NOPE), kr_swap, z(pad_r)]
    out = jnp.concatenate(cols, axis=1)
    assert out.shape[1] == C_END
    wt = jnp.concatenate([part['idx_w'].T, jnp.zeros((IWT_ROWS - IDX_HEADS, D), F32), part['sa_v'].T,
                          part['ca_v'].T], axis=0)
    return out.astype(BF16), wt.astype(BF16)


def _pack_mla_weights(w_uq, w_ukv):
    R = w_uq.shape[0]
    pad = jnp.zeros((R, MLA_HEADS, LANES - MLA_NOPE - MLA_ROPE), F32)
    rope_w = w_uq[:, :, MLA_NOPE:]
    rope_sw = jnp.concatenate([rope_w[:, :, MLA_ROPE // 2:], rope_w[:, :, :MLA_ROPE // 2]], axis=2)
    wq = jnp.concatenate([w_uq, pad], axis=2).reshape(R, MLA_HEADS * LANES)
    wqs = jnp.concatenate([jnp.zeros((R, MLA_HEADS, MLA_NOPE), F32), rope_sw, pad],
                          axis=2).reshape(R, MLA_HEADS * LANES)
    Rk = w_ukv.shape[0]
    wk = jnp.concatenate([w_ukv[:, :, :MLA_NOPE], jnp.zeros((Rk, MLA_HEADS, LANES - MLA_NOPE), F32)],
                         axis=2).reshape(Rk, MLA_HEADS * LANES)
    wvt = w_ukv[:, :, MLA_NOPE:].reshape(Rk, MLA_HEADS * MLA_V).T
    return wq.astype(BF16), wqs.astype(BF16), wk.astype(BF16), wvt.astype(BF16)


def _toeplitz(vec, rows, cols):
    L = vec.shape[-1]
    assert cols <= L - 1
    flat = jnp.tile(vec, (1, rows))[:, :rows * (L - 1)]
    return flat.reshape(vec.shape[0], rows, L - 1)[:, :, :cols]


def _signed_mod_range(L, hi):
    d = np.arange(L)
    return np.where(d <= hi, d, d - L)


def _band_bias(rel_table):
    L = CA_WIN + TQ_CA
    e = _signed_mod_range(L, TQ_CA - 1)
    ridx = np.clip(CA_LEFT_CHUNKS * CHUNK + e, -(CHUNK - 1), CA_MAX_REL) + (CHUNK - 1)
    bias = _toeplitz(rel_table[:, ridx].astype(F32), CA_WIN, TQ_CA)
    kc = np.arange(CA_WIN)[:, None] // CHUNK
    qc = np.arange(TQ_CA)[None, :] // CHUNK + CA_LEFT_CHUNKS
    valid = (kc <= qc) & (kc >= qc - CA_LEFT_CHUNKS)
    return jnp.where(valid[None], bias, NEG_INF)


def _t5_bias(t5_table):
    TQ = TQ_SA
    L = 3 * TQ
    e = _signed_mod_range(L, TQ - 1)
    rel = jnp.asarray(-e - TQ, jnp.int32)
    far = t5_table[_t5_bucket(jnp.int32(-(TQ + 1)))].astype(F32)
    vec = (t5_table[_t5_bucket(rel)].astype(F32) - far[None, :]).T
    near = _toeplitz(vec, 2 * TQ, TQ)
    return jnp.concatenate([jnp.zeros((SA_HEADS, TQ, TQ), F32), near], axis=1)


def kernel(x, c, t5_table, w_mod, b_mod, g_mix, w_in, pool_w, pool_scale, ca_rel, mla_g_cq, mla_g_ckv,
           mla_w_uq, mla_w_ukv, g_group, w_out, g_ffn, ffn_w1, ffn_w3, ffn_w2, g_final):
    B, S, D = x.shape
    assert D == D_MODEL and S % TM_PROJ == 0 and S % TQ_SA == 0 and S >= 4 * TOPK_MAX
    N = B * S
    mod_all = _modulation(c, w_mod, b_mod)
    rope_tabs = _rope_tables(S)
    nbias = _t5_bias(t5_table)
    row = lambda v: v.reshape(1, -1).astype(F32)
    x2 = x.reshape(N, D)
    for l in range(DEPTH):
        mod = mod_all[l].reshape(B, 6, D)
        w1, wt = _pack_in_weight(w_in[l])
        wq, wqs, wk, wvt = _pack_mla_weights(mla_w_uq[l], mla_w_ukv[l])
        (pool_u, ca, saq, sakv, iq, ik, iwt, svt, cavt, mq, mk, mvt) = _inproj(
            x2, mod, row(g_mix[l]), w1, wt, row(mla_g_cq[l]), row(mla_g_ckv[l]), wq, wqs, wk, wvt, rope_tabs, S)
        gg = g_group[l].reshape(4, 1, GROUP_W).astype(F32)
        wbd = jax.scipy.linalg.block_diag(*[pool_w[l, gi] for gi in range(len(POOL_WINDOWS))]).astype(BF16)
        bsw = lambda a: a.reshape(B, S, a.shape[-1])
        y_a = _pool(bsw(pool_u), wbd, row(pool_scale[l]), gg[0])
        y_b = _chunk_attention(bsw(ca), cavt, _band_bias(ca_rel[l]), gg[1])
        y_c = _sparse_attention(bsw(saq), bsw(sakv), svt, bsw(iq), bsw(ik), iwt, nbias, gg[2])
        y_d = _latent_attention(bsw(mq), bsw(mk), mvt, gg[3])
        ys = [y.reshape(N, GROUP_W) for y in (y_a, y_b, y_c, y_d)]
        x2 = _out_ffn(ys, x2, mod, w_out[l].astype(BF16), row(g_ffn[l]), ffn_w1[l].astype(BF16),
                      ffn_w3[l].astype(BF16), ffn_w2[l].astype(BF16), row(g_final), S,
                      final=(l == DEPTH - 1))
    return x2.reshape(B, S, D)
```

```python
import functools
import math
from statistics import NormalDist

import jax
import jax.numpy as jnp
from jax import lax
import numpy as np
from jax.experimental import pallas as pl
from jax.experimental.pallas import tpu as pltpu

F32 = jnp.float32
BF16 = jnp.bfloat16

D_MODEL = 1024
DEPTH = 2
CHUNK = 64
EPS = 1e-6
NEG_INF = -1e30
GROUP_W = 256
HEAD_DIM = 64
POOL_WINDOWS = (2, 4, 8, 16)
POOL_HALO = 16
CA_HEADS = 4
CA_LEFT_CHUNKS = 8
CA_MAX_REL = 256
SA_HEADS = 4
IDX_HEADS = 8
IDX_DIM = 64
TOPK_MAX = 256
MLA_HEADS = 4
MLA_NOPE = 64
MLA_ROPE = 32
MLA_V = 64
ROPE_BASE = 10000.0
T5_BUCKETS = 32
T5_MAX_DIST = 128
D_FF = 2816
IN_WIDTHS = (256, 256, 256, 256, 256, 64, 64, 512, 64, 8, 256, 128, 32)
IN_OFFS = tuple(int(v) for v in np.cumsum((0,) + IN_WIDTHS))

LANES = 128
VMEM_LIMIT = 56 * 1024 * 1024

TM_PROJ = 512
TM_FFN = 512
TP_POOL = 512
TQ_CA = 256
CA_WIN = TQ_CA + CA_LEFT_CHUNKS * CHUNK
CA_NBLK = CA_WIN // TQ_CA
IWT_ROWS = 16
TQ_SA = 256
KB_SA = 256
COUNT_CHAINS = 2
SEARCH_FIRST_ROUND = 12
SEARCH_ROUND = 4
GUESS_SPREAD = 0.3
TQ_MLA = 256
MLA_HEADS_PER_PASS = 4
FF_CHUNK = 256

C_POOL = 0
C_CA = C_POOL + 256
C_SAQ = C_CA + 512
C_SAKV = C_SAQ + 512
C_IQ = C_SAKV + 128
C_IK = C_IQ + 512
C_CQ = C_IK + 256
C_CKV = C_CQ + 256
C_KRF = C_CKV + 128
C_KRS = C_KRF + 128
C_END = C_KRS + 128

INT_MIN = -2 ** 31
KEY_ALL = INT_MIN - int(np.array(-np.inf, np.float32).view(np.int32)) + 1


def _cparams(n_axes):
    return pltpu.CompilerParams(dimension_semantics=("arbitrary",) * n_axes,
                                vmem_limit_bytes=VMEM_LIMIT)


def _rms(x, g):
    return x * lax.rsqrt(jnp.mean(x * x, axis=-1, keepdims=True) + EPS) * g


def _dot(a, b):
    return jnp.dot(a, b, preferred_element_type=F32)


def _dot_t(a, b):
    return lax.dot_general(a, b, (((1,), (1,)), ((), ())), preferred_element_type=F32)


def _mod_kernel(c_ref, w_ref, b_ref, o_ref):
    c = c_ref[...]
    act = c * jax.nn.sigmoid(c)
    o_ref[0] = jnp.dot(act, w_ref[0], precision=lax.Precision.HIGHEST,
                       preferred_element_type=F32) + b_ref[0]


def _modulation(c, w_mod, b_mod):
    L, D, W = w_mod.shape
    B = c.shape[0]
    nj = W // D
    return pl.pallas_call(
        _mod_kernel,
        grid=(L, nj),
        in_specs=[pl.BlockSpec((B, D), lambda l, j: (0, 0)),
                  pl.BlockSpec((1, D, D), lambda l, j: (l, 0, j)),
                  pl.BlockSpec((1, 1, D), lambda l, j: (l, 0, j))],
        out_specs=pl.BlockSpec((1, B, D), lambda l, j: (l, 0, j)),
        out_shape=jax.ShapeDtypeStruct((L, B, W), F32),
        compiler_params=_cparams(2),
        name="modulation",
    )(c, w_mod, b_mod.reshape(L, 1, W))


def _inproj_kernel(x_ref, mod_ref, gmix_ref, w_ref, wt_ref, gcq_ref, gckv_ref, wq_ref, wqs_ref, wk_ref, wvt_ref,
                   cosq_ref, sinq_ref, cosk_ref, sink_ref,
                   pool_o, ca_o, saq_o, sakv_o, iq_o, ik_o, iwt_o, svt_o, cavt_o, mq_o, mk_o, mvt_o):
    sh1 = mod_ref[0, 0:1, :]
    sc1 = mod_ref[0, 1:2, :]
    h = (_rms(x_ref[...], gmix_ref[...]) * (1.0 + sc1) + sh1).astype(BF16)

    def seg(a, b):
        return _dot(h, w_ref[:, a:b])

    pool_o[...] = seg(C_POOL, C_CA)
    ca_o[...] = seg(C_CA, C_SAQ).astype(BF16)
    saq_o[...] = seg(C_SAQ, C_SAKV).astype(BF16)
    sakv_o[...] = seg(C_SAKV, C_IQ).astype(BF16)
    iq_o[...] = seg(C_IQ, C_IK).astype(BF16)
    ik_o[...] = seg(C_IK, C_CQ).astype(BF16)
    tr = _dot_t(wt_ref[...], h)
    iwt_o[...] = tr[0:IWT_ROWS] * ((IDX_HEADS ** -0.5) * (IDX_DIM ** -0.5))
    svt_o[...] = tr[IWT_ROWS:IWT_ROWS + HEAD_DIM].astype(BF16)
    cavt_o[...] = tr[IWT_ROWS + HEAD_DIM:].astype(BF16)

    qn = _rms(seg(C_CQ, C_CKV), gcq_ref[...]).astype(BF16)
    qf = _dot(qn, wq_ref[...])
    qs = _dot(qn, wqs_ref[...])
    cosq = jnp.concatenate([cosq_ref[...]] * MLA_HEADS, axis=1)
    sinq = jnp.concatenate([sinq_ref[...]] * MLA_HEADS, axis=1)
    mq_o[...] = (qf * cosq + qs * sinq).astype(BF16)

    kvn = _rms(seg(C_CKV, C_KRF), gckv_ref[...]).astype(BF16)
    kvf = _dot(kvn, wk_ref[...])
    krope = seg(C_KRF, C_KRS) * cosk_ref[...] + seg(C_KRS, C_END) * sink_ref[...]
    for hd in range(MLA_HEADS):
        mk_o[:, hd * LANES:(hd + 1) * LANES] = (kvf[:, hd * LANES:(hd + 1) * LANES] + krope).astype(BF16)
    mvt_o[...] = _dot_t(wvt_ref[...], kvn).astype(BF16)


def _inproj(x2, mod, gmix, w1, wt, gcq, gckv, wq, wqs, wk, wvt, rope_tabs, S):
    N, D = x2.shape
    TM = TM_PROJ
    nt = S // TM
    cosq, sinq, cosk, sink = rope_tabs

    def full(a):
        return pl.BlockSpec(a.shape, lambda i: (0,) * a.ndim)

    def tok(w):
        return pl.BlockSpec((TM, w), lambda i: (i, 0))

    tab = pl.BlockSpec((TM, LANES), lambda i: (i % nt, 0))
    def tokt(rows):
        return pl.BlockSpec((rows, TM), lambda i: (0, i))

    outs = [(256, F32, True), (512, BF16, True), (512, BF16, True), (128, BF16, True), (512, BF16, True),
            (256, BF16, True), (IWT_ROWS, F32, False), (HEAD_DIM, BF16, False), (GROUP_W, BF16, False),
            (512, BF16, True), (512, BF16, True), (GROUP_W, BF16, False)]
    return pl.pallas_call(
        _inproj_kernel,
        grid=(N // TM,),
        in_specs=[tok(D),
                  pl.BlockSpec((1, 6, D), lambda i: (i // nt, 0, 0)),
                  full(gmix), full(w1), full(wt), full(gcq), full(gckv), full(wq), full(wqs), full(wk), full(wvt),
                  tab, tab, tab, tab],
        out_specs=[tok(w) if tm else tokt(w) for w, _, tm in outs],
        out_shape=[jax.ShapeDtypeStruct((N, w) if tm else (w, N), dt) for w, dt, tm in outs],
        compiler_params=_cparams(1),
        name="inproj",
    )(x2, mod, gmix, w1, wt, gcq, gckv, wq, wqs, wk, wvt, cosq, sinq, cosk, sink)


def _pool_kernel(u_ref, halo_ref, w_ref, scale_ref, g_ref, o_ref, pad_scr):
    i = pl.program_id(1)
    TP = u_ref.shape[1]
    u = u_ref[0]
    pad_scr[0:POOL_HALO, :] = jnp.where(i > 0, halo_ref[0], 0.0)
    pad_scr[POOL_HALO:, :] = u

    def shifted(j):
        return pad_scr[POOL_HALO - j:POOL_HALO - j + TP, :]

    lane = lax.broadcasted_iota(jnp.int32, (TP, GROUP_W), 1)
    w2 = u + shifted(1)
    w4 = w2 + shifted(2) + shifted(3)
    w8 = w4
    for j in range(4, 8):
        w8 = w8 + shifted(j)
    w16 = w8
    for j in range(8, 16):
        w16 = w16 + shifted(j)
    win = jnp.where(lane < 64, w2, jnp.where(lane < 128, w4, jnp.where(lane < 192, w8, w16)))
    wlen = jnp.where(lane < 64, 2, jnp.where(lane < 128, 4, jnp.where(lane < 192, 8, 16)))
    t = i * TP + lax.broadcasted_iota(jnp.int32, (TP, GROUP_W), 0)
    cnt = jnp.minimum(t + 1, wlen).astype(F32)
    d = (win / cnt - u).astype(BF16)
    y = _dot(d, w_ref[...]) * scale_ref[...]
    o_ref[0] = _rms(y, g_ref[...]).astype(BF16)


def _pool(u, wbd, scale, g):
    B, S, W = u.shape
    TP = TP_POOL
    hb = TP // POOL_HALO
    return pl.pallas_call(
        _pool_kernel,
        grid=(B, S // TP),
        in_specs=[pl.BlockSpec((1, TP, W), lambda b, i: (b, i, 0)),
                  pl.BlockSpec((1, POOL_HALO, W), lambda b, i: (b, jnp.maximum(i * hb - 1, 0), 0)),
                  pl.BlockSpec((W, W), lambda b, i: (0, 0)),
                  pl.BlockSpec((1, W), lambda b, i: (0, 0)),
                  pl.BlockSpec((1, W), lambda b, i: (0, 0))],
        out_specs=pl.BlockSpec((1, TP, W), lambda b, i: (b, i, 0)),
        out_shape=jax.ShapeDtypeStruct((B, S, W), BF16),
        scratch_shapes=[pltpu.VMEM((POOL_HALO + TP, W), F32)],
        compiler_params=_cparams(2),
        name="pool_mixer",
    )(u, u, wbd, scale, g)


def _ca_kernel(q_ref, k_ref, vt_ref, bias_ref, g_ref, o_ref):
    i = pl.program_id(1)
    TQ = TQ_CA
    lane = lax.broadcasted_iota(jnp.int32, (TQ, LANES), 1)
    outs = []
    for hd in range(CA_HEADS):
        cols = slice((hd // 2) * LANES, (hd // 2 + 1) * LANES)
        keep = (lane < HEAD_DIM) if hd % 2 == 0 else (lane >= HEAD_DIM)
        qh = jnp.where(keep, q_ref[0, :, cols].astype(F32), 0.0).astype(BF16)
        parts = []
        for j in range(CA_NBLK):
            kb = i - (CA_NBLK - 1) + j
            start = pl.multiple_of(jnp.maximum(kb, 0) * TQ, TQ)
            s = _dot_t(k_ref[0, pl.ds(start, TQ), cols], qh) + bias_ref[hd, j * TQ:(j + 1) * TQ, :]
            parts.append(jnp.where(kb >= 0, s, NEG_INF))
        m = parts[0].max(axis=0, keepdims=True)
        for s in parts[1:]:
            m = jnp.maximum(m, s.max(axis=0, keepdims=True))
        l = jnp.zeros((1, TQ), F32)
        acc = jnp.zeros((HEAD_DIM, TQ), F32)
        for j in range(CA_NBLK):
            kb = i - (CA_NBLK - 1) + j
            start = pl.multiple_of(jnp.maximum(kb, 0) * TQ, TQ)
            p = jnp.exp(parts[j] - m)
            l = l + p.sum(axis=0, keepdims=True)
            acc = acc + _dot(vt_ref[hd * HEAD_DIM:(hd + 1) * HEAD_DIM, pl.ds(start, TQ)], p.astype(BF16))
        outs.append(acc / l)
    o_ref[0] = _group_norm_t(jnp.concatenate(outs, axis=0), g_ref[...])


def _chunk_attention(caqk, cavt, bias, g):
    B, S, _ = caqk.shape
    W = GROUP_W
    TQ = TQ_CA
    return pl.pallas_call(
        _ca_kernel,
        grid=(B, S // TQ),
        in_specs=[pl.BlockSpec((1, TQ, W), lambda b, i: (b, i, 0)),
                  pl.BlockSpec((1, S, W), lambda b, i: (b, 0, 1)),
                  pl.BlockSpec((W, S), lambda b, i: (0, b)),
                  pl.BlockSpec(bias.shape, lambda b, i: (0, 0, 0)),
                  pl.BlockSpec((1, W), lambda b, i: (0, 0))],
        out_specs=pl.BlockSpec((1, TQ, W), lambda b, i: (b, i, 0)),
        out_shape=jax.ShapeDtypeStruct((B, S, W), BF16),
        compiler_params=_cparams(2),
        name="band_attention",
    )(caqk, caqk, cavt, bias, g)


def _score_key(score):
    b = lax.bitcast_convert_type(score, jnp.int32)
    return jnp.where(b < 0, jnp.int32(INT_MIN) - b, b)


def _sa_kernel(q_ref, kv_ref, vt_ref, iq_ref, ik_ref, iwt_ref, zq_ref, nbias_ref, g_ref, o_ref, key_scr):
    i = pl.program_id(1)
    TQ, KB = TQ_SA, KB_SA
    K = float(TOPK_MAX)
    nb = i + 1
    q0 = i * TQ
    krow = lax.broadcasted_iota(jnp.int32, (KB, TQ), 0)
    qchunk = (q0 + lax.broadcasted_iota(jnp.int32, (KB, TQ), 1)) // CHUNK

    iwt = iwt_ref[...]

    def score_block(j, carry):
        smax, s1, s2 = carry
        k0 = pl.multiple_of(j * KB, KB)
        ik = ik_ref[0, pl.ds(k0, KB), :]
        ik2 = jnp.concatenate([ik[:, :LANES], ik[:, LANES:]], axis=0)
        sc = jnp.zeros((KB, TQ), F32)
        for p in range(IDX_HEADS // 2):
            logits = _dot_t(ik2, iq_ref[0, :, p * LANES:(p + 1) * LANES])
            sc = sc + iwt[2 * p:2 * p + 1, :] * jnp.maximum(logits[:KB], 0.0)
            sc = sc + iwt[2 * p + 1:2 * p + 2, :] * jnp.maximum(logits[KB:], 0.0)
        adm = (k0 + krow) // CHUNK <= qchunk
        sc = jnp.where(adm, sc, -jnp.inf)
        key_scr[pl.ds(k0, KB), :] = _score_key(sc)
        smax = jnp.maximum(smax, sc.max(axis=0, keepdims=True))
        full = j < i
        s1 = jnp.where(full, s1 + sc.sum(axis=0, keepdims=True), s1)
        s2 = jnp.where(full, s2 + (sc * sc).sum(axis=0, keepdims=True), s2)
        return smax, s1, s2

    smax, s1, s2 = lax.fori_loop(
        0, (nb + 1) // 2, lambda j, c: score_block(2 * j + 1, score_block(2 * j, c)),
        (jnp.full((1, TQ), -jnp.inf, F32), jnp.zeros((1, TQ), F32), jnp.zeros((1, TQ), F32)))

    def count_ge(cand):
        def body(j, acc):
            blk = key_scr[pl.ds(pl.multiple_of(j * (2 * KB), 2 * KB), 2 * KB), :]
            ones = jnp.where(blk >= cand, 1.0, 0.0)
            return acc + ones.reshape(COUNT_CHAINS, -1, 8, TQ).sum(axis=1)
        acc = lax.fori_loop(0, (nb + 1) // 2, body, jnp.zeros((COUNT_CHAINS, 8, TQ), F32))
        return acc.sum(axis=0).sum(axis=0, keepdims=True)

    def search():
        def unkey(k):
            return lax.bitcast_convert_type(jnp.where(k < 0, jnp.int32(INT_MIN) - k, k), F32)

        def is_active(lo, hi, clo):
            return jnp.logical_and(clo > K, hi > lo + 1)

        def cond(st):
            _, lo, hi, clo, _ = st
            act = jnp.where(is_active(lo, hi, clo), 1.0, 0.0)
            return jnp.max(jnp.maximum(act[:, :LANES], act[:, LANES:])) > 0.0

        n_full = (i * KB).astype(F32)
        mean = s1 / n_full
        std = jnp.sqrt(jnp.maximum(s2 / n_full - mean * mean, 0.0))
        zq = jnp.max(zq_ref[...], axis=0, keepdims=True)
        guess_lo = _score_key(mean + (zq - GUESS_SPREAD) * std)
        guess_hi = _score_key(mean + (zq + GUESS_SPREAD) * std)

        def step(_, st):
            it, lo, hi, clo, chi = st
            active = is_active(lo, hi, clo)
            lf, hf = unkey(lo), unkey(hi)
            lc = jnp.log(clo)
            frac = jnp.clip((lc - math.log(K - 0.5)) / (lc - jnp.log(jnp.maximum(chi, 0.5))), 0.05, 0.95)
            cand = _score_key(lf + frac * (hf - lf))
            cand = jnp.where(it % 3 == 2, (lo >> 1) + (hi >> 1) + (lo & hi & 1), cand)
            cand = jnp.where(it == 0, guess_lo, cand)
            cand = jnp.where(it == 1, guess_hi, cand)
            cand = jnp.where(active, jnp.clip(cand, lo + 1, hi - 1), lo)
            cnt = count_ge(cand)
            up = jnp.logical_and(active, cnt >= K)
            down = jnp.logical_and(active, cnt < K)
            return (it + 1, jnp.where(up, cand, lo), jnp.where(down, cand, hi),
                    jnp.where(up, cnt, clo), jnp.where(down, cnt, chi))

        lo0 = jnp.full((1, TQ), KEY_ALL - 1, jnp.int32)
        hi0 = _score_key(smax) + 1
        clo0 = jnp.zeros((1, TQ), F32) + ((nb + 1) // 2 * (2 * KB)).astype(F32)
        st = (jnp.int32(0), lo0, hi0, clo0, jnp.zeros((1, TQ), F32))
        st = lax.fori_loop(0, SEARCH_FIRST_ROUND, step, st)
        st = lax.while_loop(cond, lambda s: lax.fori_loop(0, SEARCH_ROUND, step, s), st)
        return st[1], st[3]

    def no_search():
        return jnp.full((1, TQ), KEY_ALL, jnp.int32), jnp.full((1, TQ), K, F32)

    t, cnt_t = lax.cond(i > 0, search, no_search)
    t = jnp.maximum(t, KEY_ALL)

    @pl.when(jnp.max(cnt_t) > K)
    def _():
        allowed = K - count_ge(t + 1)
        r = lax.broadcasted_iota(jnp.int32, (KB, KB), 0)
        c = lax.broadcasted_iota(jnp.int32, (KB, KB), 1)
        earlier = jnp.where(c < r, 1.0, 0.0).astype(BF16)

        def body(j, seen):
            sl = pl.ds(pl.multiple_of(j * KB, KB), KB)
            blk = key_scr[sl, :]
            eq = jnp.where(blk == t, 1.0, 0.0)
            rank = _dot(earlier, eq.astype(BF16)) + seen
            demote = eq * jnp.where(rank >= allowed, 1.0, 0.0)
            key_scr[sl, :] = jnp.where(demote > 0.5, t - 1, blk)
            return seen + eq.sum(axis=0, keepdims=True)

        lax.fori_loop(0, nb, body, jnp.zeros((1, TQ), F32))

    def scores(j):
        k0 = pl.multiple_of(j * KB, KB)
        boff = pl.multiple_of(jnp.clip(j - (i - 2), 0, 2) * KB, KB)
        kblk = kv_ref[0, pl.ds(k0, KB), :]
        sel = key_scr[pl.ds(k0, KB), :] >= t
        out = []
        for hd in range(SA_HEADS):
            s = _dot_t(kblk, q_ref[0, :, hd * LANES:(hd + 1) * LANES]) + nbias_ref[hd, pl.ds(boff, KB), :]
            s = jnp.where(sel, s, NEG_INF)
            out.append((s, s.max(axis=0, keepdims=True)))
        return tuple(out)

    def softmax_pv(j, s_all, st):
        vt = vt_ref[:, pl.ds(pl.multiple_of(j * KB, KB), KB)]
        out = []
        for hd in range(SA_HEADS):
            m, l, acc = st[hd]
            s, smax = s_all[hd]
            mn = jnp.maximum(m, smax)
            p = jnp.exp(s - mn)
            alpha = jnp.exp(m - mn)
            out.append((mn, alpha * l + p.sum(axis=0, keepdims=True), alpha * acc + _dot(vt, p.astype(BF16))))
        return tuple(out)

    def body(j, carry):
        s_cur, st = carry
        s_next = scores(j + 1)
        return s_next, softmax_pv(j, s_cur, st)

    st = tuple((jnp.full((1, TQ), NEG_INF, F32), jnp.zeros((1, TQ), F32), jnp.zeros((HEAD_DIM, TQ), F32))
               for _ in range(SA_HEADS))
    s_last, st = lax.fori_loop(0, i, body, (scores(0), st))
    st = softmax_pv(i, s_last, st)
    y_t = jnp.concatenate([acc / l for _, l, acc in st], axis=0)
    o_ref[0] = _group_norm_t(y_t, g_ref[...])


def _sparse_attention(saq, sakv, svt, iq, ik, iwt, nbias, g):
    B, S, _ = saq.shape
    TQ = TQ_SA
    W = GROUP_W
    nt = S // TQ
    n_adm = (np.arange(S) // CHUNK + 1) * CHUNK
    zq = np.array([NormalDist().inv_cdf(1.0 - TOPK_MAX / n) if n > TOPK_MAX else 0.0 for n in n_adm], np.float32)
    zq = jnp.asarray(np.tile(zq[None, :], (8, 1)))
    return pl.pallas_call(
        _sa_kernel,
        grid=(B, nt),
        in_specs=[pl.BlockSpec((1, TQ, 512), lambda b, i: (b, i, 0)),
                  pl.BlockSpec((1, S, 128), lambda b, i: (b, 0, 0)),
                  pl.BlockSpec((HEAD_DIM, S), lambda b, i: (0, b)),
                  pl.BlockSpec((1, TQ, 512), lambda b, i: (b, i, 0)),
                  pl.BlockSpec((1, S, 256), lambda b, i: (b, 0, 0)),
                  pl.BlockSpec((IWT_ROWS, TQ), lambda b, i: (0, b * nt + i)),
                  pl.BlockSpec((8, TQ), lambda b, i: (0, i)),
                  pl.BlockSpec(nbias.shape, lambda b, i: (0, 0, 0)),
                  pl.BlockSpec((1, W), lambda b, i: (0, 0))],
        out_specs=pl.BlockSpec((1, TQ, W), lambda b, i: (b, i, 0)),
        out_shape=jax.ShapeDtypeStruct((B, S, W), BF16),
        scratch_shapes=[pltpu.VMEM((S, TQ), jnp.int32)],
        compiler_params=_cparams(2),
        name="sparse_attention",
    )(saq, sakv, svt, iq, ik, iwt, zq, nbias, g)


def _group_norm_t(y_t, g):
    inv = lax.rsqrt(jnp.mean(y_t * y_t, axis=0, keepdims=True) + EPS)
    return ((y_t * inv).T * g).astype(BF16)


def _mla_kernel(q_ref, k_ref, vt_ref, g_ref, o_ref):
    i = pl.program_id(1)
    TQ = TQ_MLA
    kch = lax.broadcasted_iota(jnp.int32, (TQ, TQ), 0) // CHUNK
    qch = lax.broadcasted_iota(jnp.int32, (TQ, TQ), 1) // CHUNK

    def scores(j, heads):
        k0 = pl.multiple_of(j * TQ, TQ)
        causal = kch <= qch + (i - j) * (TQ // CHUNK)
        out = []
        for hd in heads:
            cols = slice(hd * LANES, (hd + 1) * LANES)
            s = _dot_t(k_ref[0, pl.ds(k0, TQ), cols], q_ref[0, :, cols])
            s = jnp.where(causal, s, NEG_INF)
            out.append((s, s.max(axis=0, keepdims=True)))
        return tuple(out)

    def softmax_pv(j, heads, s_all, st):
        k0 = pl.multiple_of(j * TQ, TQ)
        out = []
        for n, hd in enumerate(heads):
            m, l, acc = st[n]
            s, smax = s_all[n]
            mn = jnp.maximum(m, smax)
            p = jnp.exp(s - mn)
            alpha = jnp.exp(m - mn)
            vt = vt_ref[hd * MLA_V:(hd + 1) * MLA_V, pl.ds(k0, TQ)]
            out.append((mn, alpha * l + p.sum(axis=0, keepdims=True),
                        alpha * acc + _dot(vt, p.astype(BF16))))
        return tuple(out)

    outs = []
    for g0 in range(0, MLA_HEADS, MLA_HEADS_PER_PASS):
        heads = tuple(range(g0, g0 + MLA_HEADS_PER_PASS))

        def body(j, carry, heads=heads):
            s_cur, st = carry
            s_next = scores(j + 1, heads)
            return s_next, softmax_pv(j, heads, s_cur, st)

        st = tuple((jnp.full((1, TQ), NEG_INF, F32), jnp.zeros((1, TQ), F32), jnp.zeros((MLA_V, TQ), F32))
                   for _ in heads)
        s_last, st = lax.fori_loop(0, i, body, (scores(0, heads), st))
        outs += [acc / l for _, l, acc in softmax_pv(i, heads, s_last, st)]
    o_ref[0] = _group_norm_t(jnp.concatenate(outs, axis=0), g_ref[...])


def _latent_attention(mq, mk, mvt, g):
    B, S, _ = mq.shape
    TQ = TQ_MLA
    W = GROUP_W
    return pl.pallas_call(
        _mla_kernel,
        grid=(B, S // TQ),
        in_specs=[pl.BlockSpec((1, TQ, 512), lambda b, i: (b, i, 0)),
                  pl.BlockSpec((1, S, 512), lambda b, i: (b, 0, 0)),
                  pl.BlockSpec((W, S), lambda b, i: (0, b)),
                  pl.BlockSpec((1, W), lambda b, i: (0, 0))],
        out_specs=pl.BlockSpec((1, TQ, W), lambda b, i: (b, i, 0)),
        out_shape=jax.ShapeDtypeStruct((B, S, W), BF16),
        compiler_params=_cparams(2),
        name="latent_attention",
    )(mq, mk, mvt, g)


def _ffn_kernel(ya_ref, yb_ref, yc_ref, yd_ref, x_ref, mod_ref, wout_ref, gffn_ref, w1_ref, w3_ref, w2_ref,
                gfin_ref, o_ref, acc_scr, *, final):
    gt1 = mod_ref[0, 2:3, :]
    sh2 = mod_ref[0, 3:4, :]
    sc2 = mod_ref[0, 4:5, :]
    gt2 = mod_ref[0, 5:6, :]
    attn = _dot(ya_ref[...], wout_ref[0:GROUP_W, :])
    for gi, y_ref in enumerate((yb_ref, yc_ref, yd_ref), start=1):
        attn = attn + _dot(y_ref[...], wout_ref[gi * GROUP_W:(gi + 1) * GROUP_W, :])
    x1 = x_ref[...] + gt1 * attn
    h = (_rms(x1, gffn_ref[...]) * (1.0 + sc2) + sh2).astype(BF16)
    for ci in range(D_FF // FF_CHUNK):
        cols = slice(ci * FF_CHUNK, (ci + 1) * FF_CHUNK)
        a = _dot(h, w1_ref[:, cols])
        gate = (a * jax.nn.sigmoid(a) * _dot(h, w3_ref[:, cols])).astype(BF16)
        part = _dot(gate, w2_ref[cols, :])
        if ci == 0:
            acc_scr[...] = part
        else:
            acc_scr[...] += part
    x2 = x1 + gt2 * acc_scr[...]
    o_ref[...] = _rms(x2, gfin_ref[...]) if final else x2


def _out_ffn(ys, x2, mod, wout, gffn, w1, w3, w2, gfin, S, final):
    N, D = x2.shape
    TM = TM_FFN
    nt = S // TM

    def full(a):
        return pl.BlockSpec(a.shape, lambda i: (0,) * a.ndim, pipeline_mode=pl.Buffered(1))

    def tok(w):
        return pl.BlockSpec((TM, w), lambda i: (i, 0))

    return pl.pallas_call(
        functools.partial(_ffn_kernel, final=final),
        grid=(N // TM,),
        in_specs=[tok(GROUP_W)] * 4 + [tok(D), pl.BlockSpec((1, 6, D), lambda i: (i // nt, 0, 0)),
                                       full(wout), full(gffn), full(w1), full(w3), full(w2), full(gfin)],
        out_specs=tok(D),
        out_shape=jax.ShapeDtypeStruct((N, D), F32),
        scratch_shapes=[pltpu.VMEM((TM, D), F32)],
        compiler_params=_cparams(1),
        name="out_ffn_final" if final else "out_ffn",
    )(*ys, x2, mod, wout, gffn, w1, w3, w2, gfin)


def _t5_bucket(rel):
    nb = T5_BUCKETS // 2
    max_exact = nb // 2
    ret = jnp.where(rel > 0, nb, 0)
    n = jnp.abs(rel)
    nf = jnp.maximum(n, 1).astype(jnp.float32)
    large = max_exact + (jnp.log(nf / max_exact) / math.log(T5_MAX_DIST / max_exact)
                         * (nb - max_exact)).astype(jnp.int32)
    large = jnp.minimum(large, nb - 1)
    return ret + jnp.where(n < max_exact, n, large)


def _rope_tables(S):
    half = MLA_ROPE // 2
    freqs = ROPE_BASE ** (-jnp.arange(half, dtype=F32) / half)
    ang = jnp.arange(S, dtype=jnp.int32).astype(F32)[:, None] * freqs[None, :]
    cos, sin = jnp.cos(ang), jnp.sin(ang)
    cos2 = jnp.concatenate([cos, cos], axis=1)
    sin2 = jnp.concatenate([-sin, sin], axis=1)
    zeros = jnp.zeros((S, LANES - MLA_NOPE - MLA_ROPE), F32)
    scale = (MLA_NOPE + MLA_ROPE) ** -0.5
    cosq = jnp.concatenate([jnp.full((S, MLA_NOPE), scale, F32), cos2 * scale, zeros], axis=1)
    sinq = jnp.concatenate([jnp.zeros((S, MLA_NOPE), F32), sin2 * scale, zeros], axis=1)
    cosk = jnp.concatenate([jnp.zeros((S, MLA_NOPE), F32), cos2, zeros], axis=1)
    sink = jnp.concatenate([jnp.zeros((S, MLA_NOPE), F32), sin2, zeros], axis=1)
    return cosq, sinq, cosk, sink


def _pack_in_weight(w):
    part = {n: w[:, IN_OFFS[k]:IN_OFFS[k + 1]] for k, n in enumerate(
        ('pool_u', 'ca_q', 'ca_k', 'ca_v', 'sa_q', 'sa_k', 'sa_v', 'idx_q', 'idx_k', 'idx_w',
         'mla_cq', 'mla_ckv', 'mla_kr'))}
    D = w.shape[0]
    z = lambda n: jnp.zeros((D, n), F32)
    qscale = HEAD_DIM ** -0.5
    saq = part['sa_q'].reshape(D, SA_HEADS, HEAD_DIM) * qscale
    saq = jnp.concatenate([saq, jnp.zeros_like(saq)], axis=2).reshape(D, SA_HEADS * LANES)
    kr = part['mla_kr']
    kr_swap = jnp.concatenate([kr[:, MLA_ROPE // 2:], kr[:, :MLA_ROPE // 2]], axis=1)
    pad_r = LANES - MLA_NOPE - MLA_ROPE
    cols = [part['pool_u'], part['ca_q'] * qscale, part['ca_k'], saq,
            part['sa_k'], part['sa_v'], part['idx_q'],
            part['idx_k'], z(IDX_DIM), z(IDX_DIM), part['idx_k'],
            part['mla_cq'], part['mla_ckv'],
            z(MLA_NOPE), kr, z(pad_r), z(MLA_NOPE), kr_swap, z(pad_r)]
    out = jnp.concatenate(cols, axis=1)
    assert out.shape[1] == C_END
    wt = jnp.concatenate([part['idx_w'].T, jnp.zeros((IWT_ROWS - IDX_HEADS, D), F32), part['sa_v'].T,
                          part['ca_v'].T], axis=0)
    return out.astype(BF16), wt.astype(BF16)


def _pack_mla_weights(w_uq, w_ukv):
    R = w_uq.shape[0]
    pad = jnp.zeros((R, MLA_HEADS, LANES - MLA_NOPE - MLA_ROPE), F32)
    rope_w = w_uq[:, :, MLA_NOPE:]
    rope_sw = jnp.concatenate([rope_w[:, :, MLA_ROPE // 2:], rope_w[:, :, :MLA_ROPE // 2]], axis=2)
    wq = jnp.concatenate([w_uq, pad], axis=2).reshape(R, MLA_HEADS * LANES)
    wqs = jnp.concatenate([jnp.zeros((R, MLA_HEADS, MLA_NOPE), F32), rope_sw, pad],
                          axis=2).reshape(R, MLA_HEADS * LANES)
    Rk = w_ukv.shape[0]
    wk = jnp.concatenate([w_ukv[:, :, :MLA_NOPE], jnp.zeros((Rk, MLA_HEADS, LANES - MLA_NOPE), F32)],
                         axis=2).reshape(Rk, MLA_HEADS * LANES)
    wvt = w_ukv[:, :, MLA_NOPE:].reshape(Rk, MLA_HEADS * MLA_V).T
    return wq.astype(BF16), wqs.astype(BF16), wk.astype(BF16), wvt.astype(BF16)


def _toeplitz(vec, rows, cols):
    L = vec.shape[-1]
    assert cols <= L - 1
    flat = jnp.tile(vec, (1, rows))[:, :rows * (L - 1)]
    return flat.reshape(vec.shape[0], rows, L - 1)[:, :, :cols]


def _signed_mod_range(L, hi):
    d = np.arange(L)
    return np.where(d <= hi, d, d - L)


def _band_bias(rel_table):
    L = CA_WIN + TQ_CA
    e = _signed_mod_range(L, TQ_CA - 1)
    ridx = np.clip(CA_LEFT_CHUNKS * CHUNK + e, -(CHUNK - 1), CA_MAX_REL) + (CHUNK - 1)
    bias = _toeplitz(rel_table[:, ridx].astype(F32), CA_WIN, TQ_CA)
    kc = np.arange(CA_WIN)[:, None] // CHUNK
    qc = np.arange(TQ_CA)[None, :] // CHUNK + CA_LEFT_CHUNKS
    valid = (kc <= qc) & (kc >= qc - CA_LEFT_CHUNKS)
    return jnp.where(valid[None], bias, NEG_INF)


def _t5_bias(t5_table):
    TQ = TQ_SA
    L = 3 * TQ
    e = _signed_mod_range(L, TQ - 1)
    rel = jnp.asarray(-e - TQ, jnp.int32)
    far = t5_table[_t5_bucket(jnp.int32(-(TQ + 1)))].astype(F32)
    vec = (t5_table[_t5_bucket(rel)].astype(F32) - far[None, :]).T
    near = _toeplitz(vec, 2 * TQ, TQ)
    return jnp.concatenate([jnp.zeros((SA_HEADS, TQ, TQ), F32), near], axis=1)


def kernel(x, c, t5_table, w_mod, b_mod, g_mix, w_in, pool_w, pool_scale, ca_rel, mla_g_cq, mla_g_ckv,
           mla_w_uq, mla_w_ukv, g_group, w_out, g_ffn, ffn_w1, ffn_w3, ffn_w2, g_final):
    B, S, D = x.shape
    assert D == D_MODEL and S % TM_PROJ == 0 and S % TQ_SA == 0 and S >= 4 * TOPK_MAX
    N = B * S
    mod_all = _modulation(c, w_mod, b_mod)
    rope_tabs = _rope_tables(S)
    nbias = _t5_bias(t5_table)
    row = lambda v: v.reshape(1, -1).astype(F32)
    x2 = x.reshape(N, D)
    for l in range(DEPTH):
        mod = mod_all[l].reshape(B, 6, D)
        w1, wt = _pack_in_weight(w_in[l])
        wq, wqs, wk, wvt = _pack_mla_weights(mla_w_uq[l], mla_w_ukv[l])
        (pool_u, ca, saq, sakv, iq, ik, iwt, svt, cavt, mq, mk, mvt) = _inproj(
            x2, mod, row(g_mix[l]), w1, wt, row(mla_g_cq[l]), row(mla_g_ckv[l]), wq, wqs, wk, wvt, rope_tabs, S)
        gg = g_group[l].reshape(4, 1, GROUP_W).astype(F32)
        wbd = jax.scipy.linalg.block_diag(*[pool_w[l, gi] for gi in range(len(POOL_WINDOWS))]).astype(BF16)
        bsw = lambda a: a.reshape(B, S, a.shape[-1])
        y_a = _pool(bsw(pool_u), wbd, row(pool_scale[l]), gg[0])
        y_b = _chunk_attention(bsw(ca), cavt, _band_bias(ca_rel[l]), gg[1])
        y_c = _sparse_attention(bsw(saq), bsw(sakv), svt, bsw(iq), bsw(ik), iwt, nbias, gg[2])
        y_d = _latent_attention(bsw(mq), bsw(mk), mvt, gg[3])
        ys = [y.reshape(N, GROUP_W) for y in (y_a, y_b, y_c, y_d)]
        x2 = _out_ffn(ys, x2, mod, w_out[l].astype(BF16), row(g_ffn[l]), ffn_w1[l].astype(BF16),
                      ffn_w3[l].astype(BF16), ffn_w2[l].astype(BF16), row(g_final), S,
                      final=(l == DEPTH - 1))
    return x2.reshape(B, S, D)
```

```python
import functools
import math
from statistics import NormalDist

import jax
import jax.numpy as jnp
from jax import lax
import numpy as np
from jax.experimental import pallas as pl
from jax.experimental.pallas import tpu as pltpu

F32 = jnp.float32
BF16 = jnp.bfloat16

D_MODEL = 1024
DEPTH = 2
CHUNK = 64
EPS = 1e-6
NEG_INF = -1e30
GROUP_W = 256
HEAD_DIM = 64
POOL_WINDOWS = (2, 4, 8, 16)
POOL_HALO = 16
CA_HEADS = 4
CA_LEFT_CHUNKS = 8
CA_MAX_REL = 256
SA_HEADS = 4
IDX_HEADS = 8
IDX_DIM = 64
TOPK_MAX = 256
MLA_HEADS = 4
MLA_NOPE = 64
MLA_ROPE = 32
MLA_V = 64
ROPE_BASE = 10000.0
T5_BUCKETS = 32
T5_MAX_DIST = 128
D_FF = 2816
IN_WIDTHS = (256, 256, 256, 256, 256, 64, 64, 512, 64, 8, 256, 128, 32)
IN_OFFS = tuple(int(v) for v in np.cumsum((0,) + IN_WIDTHS))

LANES = 128
VMEM_LIMIT = 56 * 1024 * 1024

TM_PROJ = 512
TM_FFN = 512
TP_POOL = 512
TQ_CA = 256
CA_WIN = TQ_CA + CA_LEFT_CHUNKS * CHUNK
CA_NBLK = CA_WIN // TQ_CA
IWT_ROWS = 16
TQ_SA = 256
KB_SA = 256
COUNT_CHAINS = 2
SEARCH_FIRST_ROUND = 12
SEARCH_ROUND = 4
GUESS_SPREAD = 0.3
TQ_MLA = 256
MLA_HEADS_PER_PASS = 4
FF_CHUNK = 256

C_POOL = 0
C_CA = C_POOL + 256
C_SAQ = C_CA + 512
C_SAKV = C_SAQ + 512
C_IQ = C_SAKV + 128
C_IK = C_IQ + 512
C_CQ = C_IK + 256
C_CKV = C_CQ + 256
C_KRF = C_CKV + 128
C_KRS = C_KRF + 128
C_END = C_KRS + 128

INT_MIN = -2 ** 31
KEY_ALL = INT_MIN - int(np.array(-np.inf, np.float32).view(np.int32)) + 1


def _cparams(n_axes):
    return pltpu.CompilerParams(dimension_semantics=("arbitrary",) * n_axes,
                                vmem_limit_bytes=VMEM_LIMIT)


def _rms(x, g):
    return x * lax.rsqrt(jnp.mean(x * x, axis=-1, keepdims=True) + EPS) * g


def _dot(a, b):
    return jnp.dot(a, b, preferred_element_type=F32)


def _dot_t(a, b):
    return lax.dot_general(a, b, (((1,), (1,)), ((), ())), preferred_element_type=F32)


def _mod_kernel(c_ref, w_ref, b_ref, o_ref):
    c = c_ref[...]
    act = c * jax.nn.sigmoid(c)
    o_ref[0] = jnp.dot(act, w_ref[0], precision=lax.Precision.HIGHEST,
                       preferred_element_type=F32) + b_ref[0]


def _modulation(c, w_mod, b_mod):
    L, D, W = w_mod.shape
    B = c.shape[0]
    nj = W // D
    return pl.pallas_call(
        _mod_kernel,
        grid=(L, nj),
        in_specs=[pl.BlockSpec((B, D), lambda l, j: (0, 0)),
                  pl.BlockSpec((1, D, D), lambda l, j: (l, 0, j)),
                  pl.BlockSpec((1, 1, D), lambda l, j: (l, 0, j))],
        out_specs=pl.BlockSpec((1, B, D), lambda l, j: (l, 0, j)),
        out_shape=jax.ShapeDtypeStruct((L, B, W), F32),
        compiler_params=_cparams(2),
        name="modulation",
    )(c, w_mod, b_mod.reshape(L, 1, W))


def _inproj_kernel(x_ref, mod_ref, gmix_ref, w_ref, wt_ref, gcq_ref, gckv_ref, wq_ref, wqs_ref, wk_ref, wvt_ref,
                   cosq_ref, sinq_ref, cosk_ref, sink_ref,
                   pool_o, ca_o, saq_o, sakv_o, iq_o, ik_o, iwt_o, svt_o, cavt_o, mq_o, mk_o, mvt_o):
    sh1 = mod_ref[0, 0:1, :]
    sc1 = mod_ref[0, 1:2, :]
    h = (_rms(x_ref[...], gmix_ref[...]) * (1.0 + sc1) + sh1).astype(BF16)

    def seg(a, b):
        return _dot(h, w_ref[:, a:b])

    pool_o[...] = seg(C_POOL, C_CA)
    ca_o[...] = seg(C_CA, C_SAQ).astype(BF16)
    saq_o[...] = seg(C_SAQ, C_SAKV).astype(BF16)
    sakv_o[...] = seg(C_SAKV, C_IQ).astype(BF16)
    iq_o[...] = seg(C_IQ, C_IK).astype(BF16)
    ik_o[...] = seg(C_IK, C_CQ).astype(BF16)
    tr = _dot_t(wt_ref[...], h)
    iwt_o[...] = tr[0:IWT_ROWS] * ((IDX_HEADS ** -0.5) * (IDX_DIM ** -0.5))
    svt_o[...] = tr[IWT_ROWS:IWT_ROWS + HEAD_DIM].astype(BF16)
    cavt_o[...] = tr[IWT_ROWS + HEAD_DIM:].astype(BF16)

    qn = _rms(seg(C_CQ, C_CKV), gcq_ref[...]).astype(BF16)
    qf = _dot(qn, wq_ref[...])
    qs = _dot(qn, wqs_ref[...])
    cosq = jnp.concatenate([cosq_ref[...]] * MLA_HEADS, axis=1)
    sinq = jnp.concatenate([sinq_ref[...]] * MLA_HEADS, axis=1)
    mq_o[...] = (qf * cosq + qs * sinq).astype(BF16)

    kvn = _rms(seg(C_CKV, C_KRF), gckv_ref[...]).astype(BF16)
    kvf = _dot(kvn, wk_ref[...])
    krope = seg(C_KRF, C_KRS) * cosk_ref[...] + seg(C_KRS, C_END) * sink_ref[...]
    for hd in range(MLA_HEADS):
        mk_o[:, hd * LANES:(hd + 1) * LANES] = (kvf[:, hd * LANES:(hd + 1) * LANES] + krope).astype(BF16)
    mvt_o[...] = _dot_t(wvt_ref[...], kvn).astype(BF16)


def _inproj(x2, mod, gmix, w1, wt, gcq, gckv, wq, wqs, wk, wvt, rope_tabs, S):
    N, D = x2.shape
    TM = TM_PROJ
    nt = S // TM
    cosq, sinq, cosk, sink = rope_tabs

    def full(a):
        return pl.BlockSpec(a.shape, lambda i: (0,) * a.ndim)

    def tok(w):
        return pl.BlockSpec((TM, w), lambda i: (i, 0))

    tab = pl.BlockSpec((TM, LANES), lambda i: (i % nt, 0))
    def tokt(rows):
        return pl.BlockSpec((rows, TM), lambda i: (0, i))

    outs = [(256, F32, True), (512, BF16, True), (512, BF16, True), (128, BF16, True), (512, BF16, True),
            (256, BF16, True), (IWT_ROWS, F32, False), (HEAD_DIM, BF16, False), (GROUP_W, BF16, False),
            (512, BF16, True), (512, BF16, True), (GROUP_W, BF16, False)]
    return pl.pallas_call(
        _inproj_kernel,
        grid=(N // TM,),
        in_specs=[tok(D),
                  pl.BlockSpec((1, 6, D), lambda i: (i // nt, 0, 0)),
                  full(gmix), full(w1), full(wt), full(gcq), full(gckv), full(wq), full(wqs), full(wk), full(wvt),
                  tab, tab, tab, tab],
        out_specs=[tok(w) if tm else tokt(w) for w, _, tm in outs],
        out_shape=[jax.ShapeDtypeStruct((N, w) if tm else (w, N), dt) for w, dt, tm in outs],
        compiler_params=_cparams(1),
        name="inproj",
    )(x2, mod, gmix, w1, wt, gcq, gckv, wq, wqs, wk, wvt, cosq, sinq, cosk, sink)


def _pool_kernel(u_ref, halo_ref, w_ref, scale_ref, g_ref, o_ref, pad_scr):
    i = pl.program_id(1)
    TP = u_ref.shape[1]
    u = u_ref[0]
    pad_scr[0:POOL_HALO, :] = jnp.where(i > 0, halo_ref[0], 0.0)
    pad_scr[POOL_HALO:, :] = u

    def shifted(j):
        return pad_scr[POOL_HALO - j:POOL_HALO - j + TP, :]

    lane = lax.broadcasted_iota(jnp.int32, (TP, GROUP_W), 1)
    w2 = u + shifted(1)
    w4 = w2 + shifted(2) + shifted(3)
    w8 = w4
    for j in range(4, 8):
        w8 = w8 + shifted(j)
    w16 = w8
    for j in range(8, 16):
        w16 = w16 + shifted(j)
    win = jnp.where(lane < 64, w2, jnp.where(lane < 128, w4, jnp.where(lane < 192, w8, w16)))
    wlen = jnp.where(lane < 64, 2, jnp.where(lane < 128, 4, jnp.where(lane < 192, 8, 16)))
    t = i * TP + lax.broadcasted_iota(jnp.int32, (TP, GROUP_W), 0)
    cnt = jnp.minimum(t + 1, wlen).astype(F32)
    d = (win / cnt - u).astype(BF16)
    y = _dot(d, w_ref[...]) * scale_ref[...]
    o_ref[0] = _rms(y, g_ref[...]).astype(BF16)


def _pool(u, wbd, scale, g):
    B, S, W = u.shape
    TP = TP_POOL
    hb = TP // POOL_HALO
    return pl.pallas_call(
        _pool_kernel,
        grid=(B, S // TP),
        in_specs=[pl.BlockSpec((1, TP, W), lambda b, i: (b, i, 0)),
                  pl.BlockSpec((1, POOL_HALO, W), lambda b, i: (b, jnp.maximum(i * hb - 1, 0), 0)),
                  pl.BlockSpec((W, W), lambda b, i: (0, 0)),
                  pl.BlockSpec((1, W), lambda b, i: (0, 0)),
                  pl.BlockSpec((1, W), lambda b, i: (0, 0))],
        out_specs=pl.BlockSpec((1, TP, W), lambda b, i: (b, i, 0)),
        out_shape=jax.ShapeDtypeStruct((B, S, W), BF16),
        scratch_shapes=[pltpu.VMEM((POOL_HALO + TP, W), F32)],
        compiler_params=_cparams(2),
        name="pool_mixer",
    )(u, u, wbd, scale, g)


def _ca_kernel(q_ref, k_ref, vt_ref, bias_ref, g_ref, o_ref):
    i = pl.program_id(1)
    TQ = TQ_CA
    lane = lax.broadcasted_iota(jnp.int32, (TQ, LANES), 1)
    outs = []
    for hd in range(CA_HEADS):
        cols = slice((hd // 2) * LANES, (hd // 2 + 1) * LANES)
        keep = (lane < HEAD_DIM) if hd % 2 == 0 else (lane >= HEAD_DIM)
        qh = jnp.where(keep, q_ref[0, :, cols].astype(F32), 0.0).astype(BF16)
        parts = []
        for j in range(CA_NBLK):
            kb = i - (CA_NBLK - 1) + j
            start = pl.multiple_of(jnp.maximum(kb, 0) * TQ, TQ)
            s = _dot_t(k_ref[0, pl.ds(start, TQ), cols], qh) + bias_ref[hd, j * TQ:(j + 1) * TQ, :]
            parts.append(jnp.where(kb >= 0, s, NEG_INF))
        m = parts[0].max(axis=0, keepdims=True)
        for s in parts[1:]:
            m = jnp.maximum(m, s.max(axis=0, keepdims=True))
        l = jnp.zeros((1, TQ), F32)
        acc = jnp.zeros((HEAD_DIM, TQ), F32)
        for j in range(CA_NBLK):
            kb = i - (CA_NBLK - 1) + j
            start = pl.multiple_of(jnp.maximum(kb, 0) * TQ, TQ)
            p = jnp.exp(parts[j] - m)
            l = l + p.sum(axis=0, keepdims=True)
            acc = acc + _dot(vt_ref[hd * HEAD_DIM:(hd + 1) * HEAD_DIM, pl.ds(start, TQ)], p.astype(BF16))
        outs.append(acc / l)
    o_ref[0] = _group_norm_t(jnp.concatenate(outs, axis=0), g_ref[...])


def _chunk_attention(caqk, cavt, bias, g):
    B, S, _ = caqk.shape
    W = GROUP_W
    TQ = TQ_CA
    return pl.pallas_call(
        _ca_kernel,
        grid=(B, S // TQ),
        in_specs=[pl.BlockSpec((1, TQ, W), lambda b, i: (b, i, 0)),
                  pl.BlockSpec((1, S, W), lambda b, i: (b, 0, 1)),
                  pl.BlockSpec((W, S), lambda b, i: (0, b)),
                  pl.BlockSpec(bias.shape, lambda b, i: (0, 0, 0)),
                  pl.BlockSpec((1, W), lambda b, i: (0, 0))],
        out_specs=pl.BlockSpec((1, TQ, W), lambda b, i: (b, i, 0)),
        out_shape=jax.ShapeDtypeStruct((B, S, W), BF16),
        compiler_params=_cparams(2),
        name="band_attention",
    )(caqk, caqk, cavt, bias, g)


def _score_key(score):
    b = lax.bitcast_convert_type(score, jnp.int32)
    return jnp.where(b < 0, jnp.int32(INT_MIN) - b, b)


def _sa_kernel(q_ref, kv_ref, vt_ref, iq_ref, ik_ref, iwt_ref, zq_ref, nbias_ref, g_ref, o_ref, key_scr):
    i = pl.program_id(1)
    TQ, KB = TQ_SA, KB_SA
    K = float(TOPK_MAX)
    nb = i + 1
    q0 = i * TQ
    krow = lax.broadcasted_iota(jnp.int32, (KB, TQ), 0)
    qchunk = (q0 + lax.broadcasted_iota(jnp.int32, (KB, TQ), 1)) // CHUNK

    iwt = iwt_ref[...]

    def score_block(j, carry):
        smax, s1, s2 = carry
        k0 = pl.multiple_of(j * KB, KB)
        ik = ik_ref[0, pl.ds(k0, KB), :]
        ik2 = jnp.concatenate([ik[:, :LANES], ik[:, LANES:]], axis=0)
        sc = jnp.zeros((KB, TQ), F32)
        for p in range(IDX_HEADS // 2):
            logits = _dot_t(ik2, iq_ref[0, :, p * LANES:(p + 1) * LANES])
            sc = sc + iwt[2 * p:2 * p + 1, :] * jnp.maximum(logits[:KB], 0.0)
            sc = sc + iwt[2 * p + 1:2 * p + 2, :] * jnp.maximum(logits[KB:], 0.0)
        adm = (k0 + krow) // CHUNK <= qchunk
        sc = jnp.where(adm, sc, -jnp.inf)
        key_scr[pl.ds(k0, KB), :] = _score_key(sc)
        smax = jnp.maximum(smax, sc.max(axis=0, keepdims=True))
        full = j < i
        s1 = jnp.where(full, s1 + sc.sum(axis=0, keepdims=True), s1)
        s2 = jnp.where(full, s2 + (sc * sc).sum(axis=0, keepdims=True), s2)
        return smax, s1, s2

    smax, s1, s2 = lax.fori_loop(
        0, (nb + 1) // 2, lambda j, c: score_block(2 * j + 1, score_block(2 * j, c)),
        (jnp.full((1, TQ), -jnp.inf, F32), jnp.zeros((1, TQ), F32), jnp.zeros((1, TQ), F32)))

    def count_ge(cand):
        def body(j, acc):
            blk = key_scr[pl.ds(pl.multiple_of(j * (2 * KB), 2 * KB), 2 * KB), :]
            ones = jnp.where(blk >= cand, 1.0, 0.0)
            return acc + ones.reshape(COUNT_CHAINS, -1, 8, TQ).sum(axis=1)
        acc = lax.fori_loop(0, (nb + 1) // 2, body, jnp.zeros((COUNT_CHAINS, 8, TQ), F32))
        return acc.sum(axis=0).sum(axis=0, keepdims=True)

    def search():
        def unkey(k):
            return lax.bitcast_convert_type(jnp.where(k < 0, jnp.int32(INT_MIN) - k, k), F32)

        def is_active(lo, hi, clo):
            return jnp.logical_and(clo > K, hi > lo + 1)

        def cond(st):
            _, lo, hi, clo, _ = st
            act = jnp.where(is_active(lo, hi, clo), 1.0, 0.0)
            return jnp.max(jnp.maximum(act[:, :LANES], act[:, LANES:])) > 0.0

        n_full = (i * KB).astype(F32)
        mean = s1 / n_full
        std = jnp.sqrt(jnp.maximum(s2 / n_full - mean * mean, 0.0))
        zq = jnp.max(zq_ref[...], axis=0, keepdims=True)
        guess_lo = _score_key(mean + (zq - GUESS_SPREAD) * std)
        guess_hi = _score_key(mean + (zq + GUESS_SPREAD) * std)

        def step(_, st):
            it, lo, hi, clo, chi = st
            active = is_active(lo, hi, clo)
            lf, hf = unkey(lo), unkey(hi)
            lc = jnp.log(clo)
            frac = jnp.clip((lc - math.log(K - 0.5)) / (lc - jnp.log(jnp.maximum(chi, 0.5))), 0.05, 0.95)
            cand = _score_key(lf + frac * (hf - lf))
            cand = jnp.where(it % 3 == 2, (lo >> 1) + (hi >> 1) + (lo & hi & 1), cand)
            cand = jnp.where(it == 0, guess_lo, cand)
            cand = jnp.where(it == 1, guess_hi, cand)
            cand = jnp.where(active, jnp.clip(cand, lo + 1, hi - 1), lo)
            cnt = count_ge(cand)
            up = jnp.logical_and(active, cnt >= K)
            down = jnp.logical_and(active, cnt < K)
            return (it + 1, jnp.where(up, cand, lo), jnp.where(down, cand, hi),
                    jnp.where(up, cnt, clo), jnp.where(down, cnt, chi))

        lo0 = jnp.full((1, TQ), KEY_ALL - 1, jnp.int32)
        hi0 = _score_key(smax) + 1
        clo0 = jnp.zeros((1, TQ), F32) + ((nb + 1) // 2 * (2 * KB)).astype(F32)
        st = (jnp.int32(0), lo0, hi0, clo0, jnp.zeros((1, TQ), F32))
        st = lax.fori_loop(0, SEARCH_FIRST_ROUND, step, st)
        st = lax.while_loop(cond, lambda s: lax.fori_loop(0, SEARCH_ROUND, step, s), st)
        return st[1], st[3]

    def no_search():
        return jnp.full((1, TQ), KEY_ALL, jnp.int32), jnp.full((1, TQ), K, F32)

    t, cnt_t = lax.cond(i > 0, search, no_search)
    t = jnp.maximum(t, KEY_ALL)

    @pl.when(jnp.max(cnt_t) > K)
    def _():
        allowed = K - count_ge(t + 1)
        r = lax.broadcasted_iota(jnp.int32, (KB, KB), 0)
        c = lax.broadcasted_iota(jnp.int32, (KB, KB), 1)
        earlier = jnp.where(c < r, 1.0, 0.0).astype(BF16)

        def body(j, seen):
            sl = pl.ds(pl.multiple_of(j * KB, KB), KB)
            blk = key_scr[sl, :]
            eq = jnp.where(blk == t, 1.0, 0.0)
            rank = _dot(earlier, eq.astype(BF16)) + seen
            demote = eq * jnp.where(rank >= allowed, 1.0, 0.0)
            key_scr[sl, :] = jnp.where(demote > 0.5, t - 1, blk)
            return seen + eq.sum(axis=0, keepdims=True)

        lax.fori_loop(0, nb, body, jnp.zeros((1, TQ), F32))

    def scores(j, bias_rows, present):
        k0 = pl.multiple_of(j * KB, KB)
        kblk = kv_ref[0, pl.ds(k0, KB), :]
        sel = key_scr[pl.ds(k0, KB), :] >= jnp.where(present, t, jnp.int32(2 ** 31 - 1))
        out = []
        for hd in range(SA_HEADS):
            s = _dot_t(kblk, q_ref[0, :, hd * LANES:(hd + 1) * LANES])
            if bias_rows is not None:
                s = s + nbias_ref[hd, bias_rows, :]
            s = jnp.where(sel, s, NEG_INF)
            out.append((s, s.max(axis=0, keepdims=True)))
        return tuple(out)

    def softmax_pv(j, s_all, st):
        vt = vt_ref[:, pl.ds(pl.multiple_of(j * KB, KB), KB)]
        out = []
        for hd in range(SA_HEADS):
            m, l, acc = st[hd]
            s, smax = s_all[hd]
            mn = jnp.maximum(m, smax)
            p = jnp.exp(s - mn)
            alpha = jnp.exp(m - mn)
            out.append((mn, alpha * l + p.sum(axis=0, keepdims=True), alpha * acc + _dot(vt, p.astype(BF16))))
        return tuple(out)

    def body(j, carry):
        s_cur, st = carry
        s_next = scores(j + 1, None, True)
        return s_next, softmax_pv(j, s_cur, st)

    st = tuple((jnp.full((1, TQ), NEG_INF, F32), jnp.zeros((1, TQ), F32), jnp.zeros((HEAD_DIM, TQ), F32))
               for _ in range(SA_HEADS))
    last_far = jnp.maximum(i - 2, 0)
    left = jnp.maximum(i - 1, 0)
    s_far, st = lax.fori_loop(0, last_far, body, (scores(0, None, i >= 2), st))
    s_left = scores(left, slice(0, KB), i >= 1)
    st = softmax_pv(last_far, s_far, st)
    s_diag = scores(i, slice(KB, 2 * KB), True)
    st = softmax_pv(left, s_left, st)
    st = softmax_pv(i, s_diag, st)
    y_t = jnp.concatenate([acc / l for _, l, acc in st], axis=0)
    o_ref[0] = _group_norm_t(y_t, g_ref[...])


def _sparse_attention(saq, sakv, svt, iq, ik, iwt, nbias, g):
    B, S, _ = saq.shape
    TQ = TQ_SA
    W = GROUP_W
    nt = S // TQ
    n_adm = (np.arange(S) // CHUNK + 1) * CHUNK
    zq = np.array([NormalDist().inv_cdf(1.0 - TOPK_MAX / n) if n > TOPK_MAX else 0.0 for n in n_adm], np.float32)
    zq = jnp.asarray(np.tile(zq[None, :], (8, 1)))
    return pl.pallas_call(
        _sa_kernel,
        grid=(B, nt),
        in_specs=[pl.BlockSpec((1, TQ, 512), lambda b, i: (b, i, 0)),
                  pl.BlockSpec((1, S, 128), lambda b, i: (b, 0, 0)),
                  pl.BlockSpec((HEAD_DIM, S), lambda b, i: (0, b)),
                  pl.BlockSpec((1, TQ, 512), lambda b, i: (b, i, 0)),
                  pl.BlockSpec((1, S, 256), lambda b, i: (b, 0, 0)),
                  pl.BlockSpec((IWT_ROWS, TQ), lambda b, i: (0, b * nt + i)),
                  pl.BlockSpec((8, TQ), lambda b, i: (0, i)),
                  pl.BlockSpec(nbias.shape, lambda b, i: (0, 0, 0)),
                  pl.BlockSpec((1, W), lambda b, i: (0, 0))],
        out_specs=pl.BlockSpec((1, TQ, W), lambda b, i: (b, i, 0)),
        out_shape=jax.ShapeDtypeStruct((B, S, W), BF16),
        scratch_shapes=[pltpu.VMEM((S, TQ), jnp.int32)],
        compiler_params=_cparams(2),
        name="sparse_attention",
    )(saq, sakv, svt, iq, ik, iwt, zq, nbias, g)


def _group_norm_t(y_t, g):
    inv = lax.rsqrt(jnp.mean(y_t * y_t, axis=0, keepdims=True) + EPS)
    return ((y_t * inv).T * g).astype(BF16)


def _mla_kernel(q_ref, k_ref, vt_ref, g_ref, o_ref):
    i = pl.program_id(1)
    TQ = TQ_MLA
    kch = lax.broadcasted_iota(jnp.int32, (TQ, TQ), 0) // CHUNK
    qch = lax.broadcasted_iota(jnp.int32, (TQ, TQ), 1) // CHUNK

    def scores(j, heads, keep):
        k0 = pl.multiple_of(j * TQ, TQ)
        out = []
        for hd in heads:
            cols = slice(hd * LANES, (hd + 1) * LANES)
            s = _dot_t(k_ref[0, pl.ds(k0, TQ), cols], q_ref[0, :, cols])
            if keep is not None:
                s = jnp.where(keep, s, NEG_INF)
            out.append((s, s.max(axis=0, keepdims=True)))
        return tuple(out)

    def softmax_pv(j, heads, s_all, st):
        k0 = pl.multiple_of(j * TQ, TQ)
        out = []
        for n, hd in enumerate(heads):
            m, l, acc = st[n]
            s, smax = s_all[n]
            mn = jnp.maximum(m, smax)
            p = jnp.exp(s - mn)
            alpha = jnp.exp(m - mn)
            vt = vt_ref[hd * MLA_V:(hd + 1) * MLA_V, pl.ds(k0, TQ)]
            out.append((mn, alpha * l + p.sum(axis=0, keepdims=True),
                        alpha * acc + _dot(vt, p.astype(BF16))))
        return tuple(out)

    outs = []
    for g0 in range(0, MLA_HEADS, MLA_HEADS_PER_PASS):
        heads = tuple(range(g0, g0 + MLA_HEADS_PER_PASS))

        def body(j, carry, heads=heads):
            s_cur, st = carry
            s_next = scores(j + 1, heads, None)
            return s_next, softmax_pv(j, heads, s_cur, st)

        st = tuple((jnp.full((1, TQ), NEG_INF, F32), jnp.zeros((1, TQ), F32), jnp.zeros((MLA_V, TQ), F32))
                   for _ in heads)
        left = jnp.maximum(i - 1, 0)
        first_keep = kch >= jnp.where(i >= 1, 0, TQ)
        s_left, st = lax.fori_loop(0, left, body, (scores(0, heads, first_keep), st))
        s_diag = scores(i, heads, kch <= qch)
        st = softmax_pv(left, heads, s_left, st)
        outs += [acc / l for _, l, acc in softmax_pv(i, heads, s_diag, st)]
    o_ref[0] = _group_norm_t(jnp.concatenate(outs, axis=0), g_ref[...])


def _latent_attention(mq, mk, mvt, g):
    B, S, _ = mq.shape
    TQ = TQ_MLA
    W = GROUP_W
    return pl.pallas_call(
        _mla_kernel,
        grid=(B, S // TQ),
        in_specs=[pl.BlockSpec((1, TQ, 512), lambda b, i: (b, i, 0)),
                  pl.BlockSpec((1, S, 512), lambda b, i: (b, 0, 0)),
                  pl.BlockSpec((W, S), lambda b, i: (0, b)),
                  pl.BlockSpec((1, W), lambda b, i: (0, 0))],
        out_specs=pl.BlockSpec((1, TQ, W), lambda b, i: (b, i, 0)),
        out_shape=jax.ShapeDtypeStruct((B, S, W), BF16),
        compiler_params=_cparams(2),
        name="latent_attention",
    )(mq, mk, mvt, g)


def _ffn_kernel(ya_ref, yb_ref, yc_ref, yd_ref, x_ref, mod_ref, wout_ref, gffn_ref, w1_ref, w3_ref, w2_ref,
                gfin_ref, o_ref, acc_scr, *, final):
    gt1 = mod_ref[0, 2:3, :]
    sh2 = mod_ref[0, 3:4, :]
    sc2 = mod_ref[0, 4:5, :]
    gt2 = mod_ref[0, 5:6, :]
    attn = _dot(ya_ref[...], wout_ref[0:GROUP_W, :])
    for gi, y_ref in enumerate((yb_ref, yc_ref, yd_ref), start=1):
        attn = attn + _dot(y_ref[...], wout_ref[gi * GROUP_W:(gi + 1) * GROUP_W, :])
    x1 = x_ref[...] + gt1 * attn
    h = (_rms(x1, gffn_ref[...]) * (1.0 + sc2) + sh2).astype(BF16)
    for ci in range(D_FF // FF_CHUNK):
        cols = slice(ci * FF_CHUNK, (ci + 1) * FF_CHUNK)
        a = _dot(h, w1_ref[:, cols])
        gate = (a * jax.nn.sigmoid(a) * _dot(h, w3_ref[:, cols])).astype(BF16)
        part = _dot(gate, w2_ref[cols, :])
        if ci == 0:
            acc_scr[...] = part
        else:
            acc_scr[...] += part
    x2 = x1 + gt2 * acc_scr[...]
    o_ref[...] = _rms(x2, gfin_ref[...]) if final else x2


def _out_ffn(ys, x2, mod, wout, gffn, w1, w3, w2, gfin, S, final):
    N, D = x2.shape
    TM = TM_FFN
    nt = S // TM

    def full(a):
        return pl.BlockSpec(a.shape, lambda i: (0,) * a.ndim, pipeline_mode=pl.Buffered(1))

    def tok(w):
        return pl.BlockSpec((TM, w), lambda i: (i, 0))

    return pl.pallas_call(
        functools.partial(_ffn_kernel, final=final),
        grid=(N // TM,),
        in_specs=[tok(GROUP_W)] * 4 + [tok(D), pl.BlockSpec((1, 6, D), lambda i: (i // nt, 0, 0)),
                                       full(wout), full(gffn), full(w1), full(w3), full(w2), full(gfin)],
        out_specs=tok(D),
        out_shape=jax.ShapeDtypeStruct((N, D), F32),
        scratch_shapes=[pltpu.VMEM((TM, D), F32)],
        compiler_params=_cparams(1),
        name="out_ffn_final" if final else "out_ffn",
    )(*ys, x2, mod, wout, gffn, w1, w3, w2, gfin)


def _t5_bucket(rel):
    nb = T5_BUCKETS // 2
    max_exact = nb // 2
    ret = jnp.where(rel > 0, nb, 0)
    n = jnp.abs(rel)
    nf = jnp.maximum(n, 1).astype(jnp.float32)
    large = max_exact + (jnp.log(nf / max_exact) / math.log(T5_MAX_DIST / max_exact)
                         * (nb - max_exact)).astype(jnp.int32)
    large = jnp.minimum(large, nb - 1)
    return ret + jnp.where(n < max_exact, n, large)


def _rope_tables(S):
    half = MLA_ROPE // 2
    freqs = ROPE_BASE ** (-jnp.arange(half, dtype=F32) / half)
    ang = jnp.arange(S, dtype=jnp.int32).astype(F32)[:, None] * freqs[None, :]
    cos, sin = jnp.cos(ang), jnp.sin(ang)
    cos2 = jnp.concatenate([cos, cos], axis=1)
    sin2 = jnp.concatenate([-sin, sin], axis=1)
    zeros = jnp.zeros((S, LANES - MLA_NOPE - MLA_ROPE), F32)
    scale = (MLA_NOPE + MLA_ROPE) ** -0.5
    cosq = jnp.concatenate([jnp.full((S, MLA_NOPE), scale, F32), cos2 * scale, zeros], axis=1)
    sinq = jnp.concatenate([jnp.zeros((S, MLA_NOPE), F32), sin2 * scale, zeros], axis=1)
    cosk = jnp.concatenate([jnp.zeros((S, MLA_NOPE), F32), cos2, zeros], axis=1)
    sink = jnp.concatenate([jnp.zeros((S, MLA_NOPE), F32), sin2, zeros], axis=1)
    return cosq, sinq, cosk, sink


def _pack_in_weight(w):
    part = {n: w[:, IN_OFFS[k]:IN_OFFS[k + 1]] for k, n in enumerate(
        ('pool_u', 'ca_q', 'ca_k', 'ca_v', 'sa_q', 'sa_k', 'sa_v', 'idx_q', 'idx_k', 'idx_w',
         'mla_cq', 'mla_ckv', 'mla_kr'))}
    D = w.shape[0]
    z = lambda n: jnp.zeros((D, n), F32)
    qscale = HEAD_DIM ** -0.5
    saq = part['sa_q'].reshape(D, SA_HEADS, HEAD_DIM) * qscale
    saq = jnp.concatenate([saq, jnp.zeros_like(saq)], axis=2).reshape(D, SA_HEADS * LANES)
    kr = part['mla_kr']
    kr_swap = jnp.concatenate([kr[:, MLA_ROPE // 2:], kr[:, :MLA_ROPE // 2]], axis=1)
    pad_r = LANES - MLA_NOPE - MLA_ROPE
    cols = [part['pool_u'], part['ca_q'] * qscale, part['ca_k'], saq,
            part['sa_k'], part['sa_v'], part['idx_q'],
            part['idx_k'], z(IDX_DIM), z(IDX_DIM), part['idx_k'],
            part['mla_cq'], part['mla_ckv'],
            z(MLA_NOPE), kr, z(pad_r), z(MLA_NOPE), kr_swap, z(pad_r)]
    out = jnp.concatenate(cols, axis=1)
    assert out.shape[1] == C_END
    wt = jnp.concatenate([part['idx_w'].T, jnp.zeros((IWT_ROWS - IDX_HEADS, D), F32), part['sa_v'].T,
                          part['ca_v'].T], axis=0)
    return out.astype(BF16), wt.astype(BF16)


def _pack_mla_weights(w_uq, w_ukv):
    R = w_uq.shape[0]
    pad = jnp.zeros((R, MLA_HEADS, LANES - MLA_NOPE - MLA_ROPE), F32)
    rope_w = w_uq[:, :, MLA_NOPE:]
    rope_sw = jnp.concatenate([rope_w[:, :, MLA_ROPE // 2:], rope_w[:, :, :MLA_ROPE // 2]], axis=2)
    wq = jnp.concatenate([w_uq, pad], axis=2).reshape(R, MLA_HEADS * LANES)
    wqs = jnp.concatenate([jnp.zeros((R, MLA_HEADS, MLA_NOPE), F32), rope_sw, pad],
                          axis=2).reshape(R, MLA_HEADS * LANES)
    Rk = w_ukv.shape[0]
    wk = jnp.concatenate([w_ukv[:, :, :MLA_NOPE], jnp.zeros((Rk, MLA_HEADS, LANES - MLA_NOPE), F32)],
                         axis=2).reshape(Rk, MLA_HEADS * LANES)
    wvt = w_ukv[:, :, MLA_NOPE:].reshape(Rk, MLA_HEADS * MLA_V).T
    return wq.astype(BF16), wqs.astype(BF16), wk.astype(BF16), wvt.astype(BF16)


def _toeplitz(vec, rows, cols):
    L = vec.shape[-1]
    assert cols <= L - 1
    flat = jnp.tile(vec, (1, rows))[:, :rows * (L - 1)]
    return flat.reshape(vec.shape[0], rows, L - 1)[:, :, :cols]


def _signed_mod_range(L, hi):
    d = np.arange(L)
    return np.where(d <= hi, d, d - L)


def _band_bias(rel_table):
    L = CA_WIN + TQ_CA
    e = _signed_mod_range(L, TQ_CA - 1)
    ridx = np.clip(CA_LEFT_CHUNKS * CHUNK + e, -(CHUNK - 1), CA_MAX_REL) + (CHUNK - 1)
    bias = _toeplitz(rel_table[:, ridx].astype(F32), CA_WIN, TQ_CA)
    kc = np.arange(CA_WIN)[:, None] // CHUNK
    qc = np.arange(TQ_CA)[None, :] // CHUNK + CA_LEFT_CHUNKS
    valid = (kc <= qc) & (kc >= qc - CA_LEFT_CHUNKS)
    return jnp.where(valid[None], bias, NEG_INF)


def _t5_bias(t5_table):
    TQ = TQ_SA
    L = 3 * TQ
    e = _signed_mod_range(L, TQ - 1)
    rel = jnp.asarray(-e - TQ, jnp.int32)
    far = t5_table[_t5_bucket(jnp.int32(-(TQ + 1)))].astype(F32)
    vec = (t5_table[_t5_bucket(rel)].astype(F32) - far[None, :]).T
    return _toeplitz(vec, 2 * TQ, TQ)


def kernel(x, c, t5_table, w_mod, b_mod, g_mix, w_in, pool_w, pool_scale, ca_rel, mla_g_cq, mla_g_ckv,
           mla_w_uq, mla_w_ukv, g_group, w_out, g_ffn, ffn_w1, ffn_w3, ffn_w2, g_final):
    B, S, D = x.shape
    assert D == D_MODEL and S % TM_PROJ == 0 and S % TQ_SA == 0 and S >= 4 * TOPK_MAX
    N = B * S
    mod_all = _modulation(c, w_mod, b_mod)
    rope_tabs = _rope_tables(S)
    nbias = _t5_bias(t5_table)
    row = lambda v: v.reshape(1, -1).astype(F32)
    x2 = x.reshape(N, D)
    for l in range(DEPTH):
        mod = mod_all[l].reshape(B, 6, D)
        w1, wt = _pack_in_weight(w_in[l])
        wq, wqs, wk, wvt = _pack_mla_weights(mla_w_uq[l], mla_w_ukv[l])
        (pool_u, ca, saq, sakv, iq, ik, iwt, svt, cavt, mq, mk, mvt) = _inproj(
            x2, mod, row(g_mix[l]), w1, wt, row(mla_g_cq[l]), row(mla_g_ckv[l]), wq, wqs, wk, wvt, rope_tabs, S)
        gg = g_group[l].reshape(4, 1, GROUP_W).astype(F32)
        wbd = jax.scipy.linalg.block_diag(*[pool_w[l, gi] for gi in range(len(POOL_WINDOWS))]).astype(BF16)
        bsw = lambda a: a.reshape(B, S, a.shape[-1])
        y_a = _pool(bsw(pool_u), wbd, row(pool_scale[l]), gg[0])
        y_b = _chunk_attention(bsw(ca), cavt, _band_bias(ca_rel[l]), gg[1])
        y_c = _sparse_attention(bsw(saq), bsw(sakv), svt, bsw(iq), bsw(ik), iwt, nbias, gg[2])
        y_d = _latent_attention(bsw(mq), bsw(mk), mvt, gg[3])
        ys = [y.reshape(N, GROUP_W) for y in (y_a, y_b, y_c, y_d)]
        x2 = _out_ffn(ys, x2, mod, w_out[l].astype(BF16), row(g_ffn[l]), ffn_w1[l].astype(BF16),
                      ffn_w3[l].astype(BF16), ffn_w2[l].astype(BF16), row(g_final), S,
                      final=(l == DEPTH - 1))
    return x2.reshape(B, S, D)
```

```python
import functools
import math
from statistics import NormalDist

import jax
import jax.numpy as jnp
from jax import lax
import numpy as np
from jax.experimental import pallas as pl
from jax.experimental.pallas import tpu as pltpu

F32 = jnp.float32
BF16 = jnp.bfloat16

D_MODEL = 1024
DEPTH = 2
CHUNK = 64
EPS = 1e-6
NEG_INF = -1e30
GROUP_W = 256
HEAD_DIM = 64
POOL_WINDOWS = (2, 4, 8, 16)
POOL_HALO = 16
CA_HEADS = 4
CA_LEFT_CHUNKS = 8
CA_MAX_REL = 256
SA_HEADS = 4
IDX_HEADS = 8
IDX_DIM = 64
TOPK_MAX = 256
MLA_HEADS = 4
MLA_NOPE = 64
MLA_ROPE = 32
MLA_V = 64
ROPE_BASE = 10000.0
T5_BUCKETS = 32
T5_MAX_DIST = 128
D_FF = 2816
IN_WIDTHS = (256, 256, 256, 256, 256, 64, 64, 512, 64, 8, 256, 128, 32)
IN_OFFS = tuple(int(v) for v in np.cumsum((0,) + IN_WIDTHS))

LANES = 128
VMEM_LIMIT = 56 * 1024 * 1024

TM_PROJ = 512
TM_FFN = 512
TP_POOL = 512
TQ_CA = 256
CA_WIN = TQ_CA + CA_LEFT_CHUNKS * CHUNK
CA_NBLK = CA_WIN // TQ_CA
IWT_ROWS = 16
TQ_SA = 256
KB_SA = 256
COUNT_CHAINS = 2
SEARCH_FIRST_ROUND = 12
SEARCH_ROUND = 4
GUESS_SPREAD = 0.3
TQ_MLA = 256
MLA_HEADS_PER_PASS = 4
FF_CHUNK = 256

C_POOL = 0
C_CA = C_POOL + 256
C_SAQ = C_CA + 512
C_SAKV = C_SAQ + 512
C_IQ = C_SAKV + 128
C_IK = C_IQ + 512
C_CQ = C_IK + 256
C_CKV = C_CQ + 256
C_KRF = C_CKV + 128
C_KRS = C_KRF + 128
C_END = C_KRS + 128

INT_MIN = -2 ** 31
KEY_ALL = INT_MIN - int(np.array(-np.inf, np.float32).view(np.int32)) + 1


def _cparams(n_axes):
    return pltpu.CompilerParams(dimension_semantics=("arbitrary",) * n_axes,
                                vmem_limit_bytes=VMEM_LIMIT)


def _rms(x, g):
    return x * lax.rsqrt(jnp.mean(x * x, axis=-1, keepdims=True) + EPS) * g


def _dot(a, b):
    return jnp.dot(a, b, preferred_element_type=F32)


def _dot_t(a, b):
    return lax.dot_general(a, b, (((1,), (1,)), ((), ())), preferred_element_type=F32)


def _mod_kernel(c_ref, w_ref, b_ref, o_ref):
    c = c_ref[...]
    act = c * jax.nn.sigmoid(c)
    o_ref[0] = jnp.dot(act, w_ref[0], precision=lax.Precision.HIGHEST,
                       preferred_element_type=F32) + b_ref[0]


def _modulation(c, w_mod, b_mod):
    L, D, W = w_mod.shape
    B = c.shape[0]
    nj = W // D
    return pl.pallas_call(
        _mod_kernel,
        grid=(L, nj),
        in_specs=[pl.BlockSpec((B, D), lambda l, j: (0, 0)),
                  pl.BlockSpec((1, D, D), lambda l, j: (l, 0, j)),
                  pl.BlockSpec((1, 1, D), lambda l, j: (l, 0, j))],
        out_specs=pl.BlockSpec((1, B, D), lambda l, j: (l, 0, j)),
        out_shape=jax.ShapeDtypeStruct((L, B, W), F32),
        compiler_params=_cparams(2),
        name="modulation",
    )(c, w_mod, b_mod.reshape(L, 1, W))


def _inproj_kernel(x_ref, mod_ref, gmix_ref, w_ref, wt_ref, gcq_ref, gckv_ref, wq_ref, wqs_ref, wk_ref, wvt_ref,
                   cosq_ref, sinq_ref, cosk_ref, sink_ref,
                   pool_o, ca_o, saq_o, sakv_o, iq_o, ik_o, iwt_o, svt_o, cavt_o, mq_o, mk_o, mvt_o):
    sh1 = mod_ref[0, 0:1, :]
    sc1 = mod_ref[0, 1:2, :]
    h = (_rms(x_ref[...], gmix_ref[...]) * (1.0 + sc1) + sh1).astype(BF16)

    def seg(a, b):
        return _dot(h, w_ref[:, a:b])

    pool_o[...] = seg(C_POOL, C_CA)
    ca_o[...] = seg(C_CA, C_SAQ).astype(BF16)
    saq_o[...] = seg(C_SAQ, C_SAKV).astype(BF16)
    sakv_o[...] = seg(C_SAKV, C_IQ).astype(BF16)
    iq_o[...] = seg(C_IQ, C_IK).astype(BF16)
    ik_o[...] = seg(C_IK, C_CQ).astype(BF16)
    tr = _dot_t(wt_ref[...], h)
    iwt_o[...] = tr[0:IWT_ROWS] * ((IDX_HEADS ** -0.5) * (IDX_DIM ** -0.5))
    svt_o[...] = tr[IWT_ROWS:IWT_ROWS + HEAD_DIM].astype(BF16)
    cavt_o[...] = tr[IWT_ROWS + HEAD_DIM:].astype(BF16)

    qn = _rms(seg(C_CQ, C_CKV), gcq_ref[...]).astype(BF16)
    qf = _dot(qn, wq_ref[...])
    qs = _dot(qn, wqs_ref[...])
    cosq = jnp.concatenate([cosq_ref[...]] * MLA_HEADS, axis=1)
    sinq = jnp.concatenate([sinq_ref[...]] * MLA_HEADS, axis=1)
    mq_o[...] = (qf * cosq + qs * sinq).astype(BF16)

    kvn = _rms(seg(C_CKV, C_KRF), gckv_ref[...]).astype(BF16)
    kvf = _dot(kvn, wk_ref[...])
    krope = seg(C_KRF, C_KRS) * cosk_ref[...] + seg(C_KRS, C_END) * sink_ref[...]
    for hd in range(MLA_HEADS):
        mk_o[:, hd * LANES:(hd + 1) * LANES] = (kvf[:, hd * LANES:(hd + 1) * LANES] + krope).astype(BF16)
    mvt_o[...] = _dot_t(wvt_ref[...], kvn).astype(BF16)


def _inproj(x2, mod, gmix, w1, wt, gcq, gckv, wq, wqs, wk, wvt, rope_tabs, S):
    N, D = x2.shape
    TM = TM_PROJ
    nt = S // TM
    cosq, sinq, cosk, sink = rope_tabs

    def full(a):
        return pl.BlockSpec(a.shape, lambda i: (0,) * a.ndim)

    def tok(w):
        return pl.BlockSpec((TM, w), lambda i: (i, 0))

    tab = pl.BlockSpec((TM, LANES), lambda i: (i % nt, 0))
    def tokt(rows):
        return pl.BlockSpec((rows, TM), lambda i: (0, i))

    outs = [(256, F32, True), (512, BF16, True), (512, BF16, True), (128, BF16, True), (512, BF16, True),
            (256, BF16, True), (IWT_ROWS, F32, False), (HEAD_DIM, BF16, False), (GROUP_W, BF16, False),
            (512, BF16, True), (512, BF16, True), (GROUP_W, BF16, False)]
    return pl.pallas_call(
        _inproj_kernel,
        grid=(N // TM,),
        in_specs=[tok(D),
                  pl.BlockSpec((1, 6, D), lambda i: (i // nt, 0, 0)),
                  full(gmix), full(w1), full(wt), full(gcq), full(gckv), full(wq), full(wqs), full(wk), full(wvt),
                  tab, tab, tab, tab],
        out_specs=[tok(w) if tm else tokt(w) for w, _, tm in outs],
        out_shape=[jax.ShapeDtypeStruct((N, w) if tm else (w, N), dt) for w, dt, tm in outs],
        compiler_params=_cparams(1),
        name="inproj",
    )(x2, mod, gmix, w1, wt, gcq, gckv, wq, wqs, wk, wvt, cosq, sinq, cosk, sink)


def _pool_kernel(u_ref, halo_ref, w_ref, scale_ref, g_ref, o_ref, pad_scr):
    i = pl.program_id(1)
    TP = u_ref.shape[1]
    u = u_ref[0]
    pad_scr[0:POOL_HALO, :] = jnp.where(i > 0, halo_ref[0], 0.0)
    pad_scr[POOL_HALO:, :] = u

    def shifted(j):
        return pad_scr[POOL_HALO - j:POOL_HALO - j + TP, :]

    lane = lax.broadcasted_iota(jnp.int32, (TP, GROUP_W), 1)
    w2 = u + shifted(1)
    w4 = w2 + shifted(2) + shifted(3)
    w8 = w4
    for j in range(4, 8):
        w8 = w8 + shifted(j)
    w16 = w8
    for j in range(8, 16):
        w16 = w16 + shifted(j)
    win = jnp.where(lane < 64, w2, jnp.where(lane < 128, w4, jnp.where(lane < 192, w8, w16)))
    wlen = jnp.where(lane < 64, 2, jnp.where(lane < 128, 4, jnp.where(lane < 192, 8, 16)))
    t = i * TP + lax.broadcasted_iota(jnp.int32, (TP, GROUP_W), 0)
    cnt = jnp.minimum(t + 1, wlen).astype(F32)
    d = (win / cnt - u).astype(BF16)
    y = _dot(d, w_ref[...]) * scale_ref[...]
    o_ref[0] = _rms(y, g_ref[...]).astype(BF16)


def _pool(u, wbd, scale, g):
    B, S, W = u.shape
    TP = TP_POOL
    hb = TP // POOL_HALO
    return pl.pallas_call(
        _pool_kernel,
        grid=(B, S // TP),
        in_specs=[pl.BlockSpec((1, TP, W), lambda b, i: (b, i, 0)),
                  pl.BlockSpec((1, POOL_HALO, W), lambda b, i: (b, jnp.maximum(i * hb - 1, 0), 0)),
                  pl.BlockSpec((W, W), lambda b, i: (0, 0)),
                  pl.BlockSpec((1, W), lambda b, i: (0, 0)),
                  pl.BlockSpec((1, W), lambda b, i: (0, 0))],
        out_specs=pl.BlockSpec((1, TP, W), lambda b, i: (b, i, 0)),
        out_shape=jax.ShapeDtypeStruct((B, S, W), BF16),
        scratch_shapes=[pltpu.VMEM((POOL_HALO + TP, W), F32)],
        compiler_params=_cparams(2),
        name="pool_mixer",
    )(u, u, wbd, scale, g)


def _ca_kernel(q_ref, k_ref, vt_ref, bias_ref, g_ref, o_ref):
    i = pl.program_id(1)
    TQ = TQ_CA
    lane = lax.broadcasted_iota(jnp.int32, (TQ, LANES), 1)
    starts = []
    for j in range(CA_NBLK):
        kb = i - (CA_NBLK - 1) + j
        starts.append((kb >= 0, pl.multiple_of(jnp.maximum(kb, 0) * TQ, TQ)))
    scored = []
    for hd in range(CA_HEADS):
        cols = slice((hd // 2) * LANES, (hd // 2 + 1) * LANES)
        keep = (lane < HEAD_DIM) if hd % 2 == 0 else (lane >= HEAD_DIM)
        qh = jnp.where(keep, q_ref[0, :, cols].astype(F32), 0.0).astype(BF16)
        parts = []
        for j, (present, start) in enumerate(starts):
            s = _dot_t(k_ref[0, pl.ds(start, TQ), cols], qh) + bias_ref[hd, j * TQ:(j + 1) * TQ, :]
            parts.append(jnp.where(present, s, NEG_INF))
        m = parts[0].max(axis=0, keepdims=True)
        for s in parts[1:]:
            m = jnp.maximum(m, s.max(axis=0, keepdims=True))
        scored.append((parts, m))
    outs = []
    for hd, (parts, m) in enumerate(scored):
        l = jnp.zeros((1, TQ), F32)
        acc = jnp.zeros((HEAD_DIM, TQ), F32)
        for j, (_, start) in enumerate(starts):
            p = jnp.exp(parts[j] - m)
            l = l + p.sum(axis=0, keepdims=True)
            acc = acc + _dot(vt_ref[hd * HEAD_DIM:(hd + 1) * HEAD_DIM, pl.ds(start, TQ)], p.astype(BF16))
        outs.append(acc / l)
    o_ref[0] = _group_norm_t(jnp.concatenate(outs, axis=0), g_ref[...])


def _chunk_attention(caqk, cavt, bias, g):
    B, S, _ = caqk.shape
    W = GROUP_W
    TQ = TQ_CA
    return pl.pallas_call(
        _ca_kernel,
        grid=(B, S // TQ),
        in_specs=[pl.BlockSpec((1, TQ, W), lambda b, i: (b, i, 0)),
                  pl.BlockSpec((1, S, W), lambda b, i: (b, 0, 1)),
                  pl.BlockSpec((W, S), lambda b, i: (0, b)),
                  pl.BlockSpec(bias.shape, lambda b, i: (0, 0, 0)),
                  pl.BlockSpec((1, W), lambda b, i: (0, 0))],
        out_specs=pl.BlockSpec((1, TQ, W), lambda b, i: (b, i, 0)),
        out_shape=jax.ShapeDtypeStruct((B, S, W), BF16),
        compiler_params=_cparams(2),
        name="band_attention",
    )(caqk, caqk, cavt, bias, g)


def _score_key(score):
    b = lax.bitcast_convert_type(score, jnp.int32)
    return jnp.where(b < 0, jnp.int32(INT_MIN) - b, b)


def _sa_kernel(q_ref, kv_ref, vt_ref, iq_ref, ik_ref, iwt_ref, zq_ref, nbias_ref, g_ref, o_ref, key_scr):
    i = pl.program_id(1)
    TQ, KB = TQ_SA, KB_SA
    K = float(TOPK_MAX)
    nb = i + 1
    q0 = i * TQ
    krow = lax.broadcasted_iota(jnp.int32, (KB, TQ), 0)
    qchunk = (q0 + lax.broadcasted_iota(jnp.int32, (KB, TQ), 1)) // CHUNK

    iwt = iwt_ref[...]

    def score_block(j, carry):
        smax, s1, s2 = carry
        k0 = pl.multiple_of(j * KB, KB)
        ik = ik_ref[0, pl.ds(k0, KB), :]
        ik2 = jnp.concatenate([ik[:, :LANES], ik[:, LANES:]], axis=0)
        sc = jnp.zeros((KB, TQ), F32)
        for p in range(IDX_HEADS // 2):
            logits = _dot_t(ik2, iq_ref[0, :, p * LANES:(p + 1) * LANES])
            sc = sc + iwt[2 * p:2 * p + 1, :] * jnp.maximum(logits[:KB], 0.0)
            sc = sc + iwt[2 * p + 1:2 * p + 2, :] * jnp.maximum(logits[KB:], 0.0)
        adm = (k0 + krow) // CHUNK <= qchunk
        sc = jnp.where(adm, sc, -jnp.inf)
        key_scr[pl.ds(k0, KB), :] = _score_key(sc)
        smax = jnp.maximum(smax, sc.max(axis=0, keepdims=True))
        full = j < i
        s1 = jnp.where(full, s1 + sc.sum(axis=0, keepdims=True), s1)
        s2 = jnp.where(full, s2 + (sc * sc).sum(axis=0, keepdims=True), s2)
        return smax, s1, s2

    smax, s1, s2 = lax.fori_loop(
        0, (nb + 1) // 2, lambda j, c: score_block(2 * j + 1, score_block(2 * j, c)),
        (jnp.full((1, TQ), -jnp.inf, F32), jnp.zeros((1, TQ), F32), jnp.zeros((1, TQ), F32)))

    def count_ge(cand):
        def body(j, acc):
            blk = key_scr[pl.ds(pl.multiple_of(j * (2 * KB), 2 * KB), 2 * KB), :]
            ones = jnp.where(blk >= cand, 1.0, 0.0)
            return acc + ones.reshape(COUNT_CHAINS, -1, 8, TQ).sum(axis=1)
        acc = lax.fori_loop(0, (nb + 1) // 2, body, jnp.zeros((COUNT_CHAINS, 8, TQ), F32))
        return acc.sum(axis=0).sum(axis=0, keepdims=True)

    def search():
        def unkey(k):
            return lax.bitcast_convert_type(jnp.where(k < 0, jnp.int32(INT_MIN) - k, k), F32)

        def is_active(lo, hi, clo):
            return jnp.logical_and(clo > K, hi > lo + 1)

        def cond(st):
            _, lo, hi, clo, _ = st
            act = jnp.where(is_active(lo, hi, clo), 1.0, 0.0)
            return jnp.max(jnp.maximum(act[:, :LANES], act[:, LANES:])) > 0.0

        n_full = (i * KB).astype(F32)
        mean = s1 / n_full
        std = jnp.sqrt(jnp.maximum(s2 / n_full - mean * mean, 0.0))
        zq = jnp.max(zq_ref[...], axis=0, keepdims=True)
        guess_lo = _score_key(mean + (zq - GUESS_SPREAD) * std)
        guess_hi = _score_key(mean + (zq + GUESS_SPREAD) * std)

        def step(_, st):
            it, lo, hi, clo, chi = st
            active = is_active(lo, hi, clo)
            lf, hf = unkey(lo), unkey(hi)
            lc = jnp.log(clo)
            frac = jnp.clip((lc - math.log(K - 0.5)) / (lc - jnp.log(jnp.maximum(chi, 0.5))), 0.05, 0.95)
            cand = _score_key(lf + frac * (hf - lf))
            cand = jnp.where(it % 3 == 2, (lo >> 1) + (hi >> 1) + (lo & hi & 1), cand)
            cand = jnp.where(it == 0, guess_lo, cand)
            cand = jnp.where(it == 1, guess_hi, cand)
            cand = jnp.where(active, jnp.clip(cand, lo + 1, hi - 1), lo)
            cnt = count_ge(cand)
            up = jnp.logical_and(active, cnt >= K)
            down = jnp.logical_and(active, cnt < K)
            return (it + 1, jnp.where(up, cand, lo), jnp.where(down, cand, hi),
                    jnp.where(up, cnt, clo), jnp.where(down, cnt, chi))

        lo0 = jnp.full((1, TQ), KEY_ALL - 1, jnp.int32)
        hi0 = _score_key(smax) + 1
        clo0 = jnp.zeros((1, TQ), F32) + ((nb + 1) // 2 * (2 * KB)).astype(F32)
        st = (jnp.int32(0), lo0, hi0, clo0, jnp.zeros((1, TQ), F32))
        st = lax.fori_loop(0, SEARCH_FIRST_ROUND, step, st)
        st = lax.while_loop(cond, lambda s: lax.fori_loop(0, SEARCH_ROUND, step, s), st)
        return st[1], st[3]

    def no_search():
        return jnp.full((1, TQ), KEY_ALL, jnp.int32), jnp.full((1, TQ), K, F32)

    t, cnt_t = lax.cond(i > 0, search, no_search)
    t = jnp.maximum(t, KEY_ALL)

    @pl.when(jnp.max(cnt_t) > K)
    def _():
        allowed = K - count_ge(t + 1)
        r = lax.broadcasted_iota(jnp.int32, (KB, KB), 0)
        c = lax.broadcasted_iota(jnp.int32, (KB, KB), 1)
        earlier = jnp.where(c < r, 1.0, 0.0).astype(BF16)

        def body(j, seen):
            sl = pl.ds(pl.multiple_of(j * KB, KB), KB)
            blk = key_scr[sl, :]
            eq = jnp.where(blk == t, 1.0, 0.0)
            rank = _dot(earlier, eq.astype(BF16)) + seen
            demote = eq * jnp.where(rank >= allowed, 1.0, 0.0)
            key_scr[sl, :] = jnp.where(demote > 0.5, t - 1, blk)
            return seen + eq.sum(axis=0, keepdims=True)

        lax.fori_loop(0, nb, body, jnp.zeros((1, TQ), F32))

    def scores(j, bias_rows, present):
        k0 = pl.multiple_of(j * KB, KB)
        kblk = kv_ref[0, pl.ds(k0, KB), :]
        sel = key_scr[pl.ds(k0, KB), :] >= jnp.where(present, t, jnp.int32(2 ** 31 - 1))
        out = []
        for hd in range(SA_HEADS):
            s = _dot_t(kblk, q_ref[0, :, hd * LANES:(hd + 1) * LANES])
            if bias_rows is not None:
                s = s + nbias_ref[hd, bias_rows, :]
            s = jnp.where(sel, s, NEG_INF)
            out.append((s, s.max(axis=0, keepdims=True)))
        return tuple(out)

    def softmax_pv(j, s_all, st):
        vt = vt_ref[:, pl.ds(pl.multiple_of(j * KB, KB), KB)]
        out = []
        for hd in range(SA_HEADS):
            m, l, acc = st[hd]
            s, smax = s_all[hd]
            mn = jnp.maximum(m, smax)
            p = jnp.exp(s - mn)
            alpha = jnp.exp(m - mn)
            out.append((mn, alpha * l + p.sum(axis=0, keepdims=True), alpha * acc + _dot(vt, p.astype(BF16))))
        return tuple(out)

    def body(j, carry):
        s_cur, st = carry
        s_next = scores(j + 1, None, True)
        return s_next, softmax_pv(j, s_cur, st)

    st = tuple((jnp.full((1, TQ), NEG_INF, F32), jnp.zeros((1, TQ), F32), jnp.zeros((HEAD_DIM, TQ), F32))
               for _ in range(SA_HEADS))
    last_far = jnp.maximum(i - 2, 0)
    left = jnp.maximum(i - 1, 0)
    s_far, st = lax.fori_loop(0, last_far, body, (scores(0, None, i >= 2), st))
    s_left = scores(left, slice(0, KB), i >= 1)
    st = softmax_pv(last_far, s_far, st)
    s_diag = scores(i, slice(KB, 2 * KB), True)
    st = softmax_pv(left, s_left, st)
    st = softmax_pv(i, s_diag, st)
    y_t = jnp.concatenate([acc / l for _, l, acc in st], axis=0)
    o_ref[0] = _group_norm_t(y_t, g_ref[...])


def _sparse_attention(saq, sakv, svt, iq, ik, iwt, nbias, g):
    B, S, _ = saq.shape
    TQ = TQ_SA
    W = GROUP_W
    nt = S // TQ
    n_adm = (np.arange(S) // CHUNK + 1) * CHUNK
    zq = np.array([NormalDist().inv_cdf(1.0 - TOPK_MAX / n) if n > TOPK_MAX else 0.0 for n in n_adm], np.float32)
    zq = jnp.asarray(np.tile(zq[None, :], (8, 1)))
    return pl.pallas_call(
        _sa_kernel,
        grid=(B, nt),
        in_specs=[pl.BlockSpec((1, TQ, 512), lambda b, i: (b, i, 0)),
                  pl.BlockSpec((1, S, 128), lambda b, i: (b, 0, 0)),
                  pl.BlockSpec((HEAD_DIM, S), lambda b, i: (0, b)),
                  pl.BlockSpec((1, TQ, 512), lambda b, i: (b, i, 0)),
                  pl.BlockSpec((1, S, 256), lambda b, i: (b, 0, 0)),
                  pl.BlockSpec((IWT_ROWS, TQ), lambda b, i: (0, b * nt + i)),
                  pl.BlockSpec((8, TQ), lambda b, i: (0, i)),
                  pl.BlockSpec(nbias.shape, lambda b, i: (0, 0, 0)),
                  pl.BlockSpec((1, W), lambda b, i: (0, 0))],
        out_specs=pl.BlockSpec((1, TQ, W), lambda b, i: (b, i, 0)),
        out_shape=jax.ShapeDtypeStruct((B, S, W), BF16),
        scratch_shapes=[pltpu.VMEM((S, TQ), jnp.int32)],
        compiler_params=_cparams(2),
        name="sparse_attention",
    )(saq, sakv, svt, iq, ik, iwt, zq, nbias, g)


def _group_norm_t(y_t, g):
    inv = lax.rsqrt(jnp.mean(y_t * y_t, axis=0, keepdims=True) + EPS)
    return ((y_t * inv).T * g).astype(BF16)


def _mla_kernel(q_ref, k_ref, vt_ref, g_ref, o_ref):
    i = pl.program_id(1)
    TQ = TQ_MLA
    kch = lax.broadcasted_iota(jnp.int32, (TQ, TQ), 0) // CHUNK
    qch = lax.broadcasted_iota(jnp.int32, (TQ, TQ), 1) // CHUNK

    def scores(j, heads, keep):
        k0 = pl.multiple_of(j * TQ, TQ)
        out = []
        for hd in heads:
            cols = slice(hd * LANES, (hd + 1) * LANES)
            s = _dot_t(k_ref[0, pl.ds(k0, TQ), cols], q_ref[0, :, cols])
            if keep is not None:
                s = jnp.where(keep, s, NEG_INF)
            out.append((s, s.max(axis=0, keepdims=True)))
        return tuple(out)

    def softmax_pv(j, heads, s_all, st):
        k0 = pl.multiple_of(j * TQ, TQ)
        out = []
        for n, hd in enumerate(heads):
            m, l, acc = st[n]
            s, smax = s_all[n]
            mn = jnp.maximum(m, smax)
            p = jnp.exp(s - mn)
            alpha = jnp.exp(m - mn)
            vt = vt_ref[hd * MLA_V:(hd + 1) * MLA_V, pl.ds(k0, TQ)]
            out.append((mn, alpha * l + p.sum(axis=0, keepdims=True),
                        alpha * acc + _dot(vt, p.astype(BF16))))
        return tuple(out)

    outs = []
    for g0 in range(0, MLA_HEADS, MLA_HEADS_PER_PASS):
        heads = tuple(range(g0, g0 + MLA_HEADS_PER_PASS))

        def body(j, carry, heads=heads):
            s_cur, st = carry
            s_next = scores(j + 1, heads, None)
            return s_next, softmax_pv(j, heads, s_cur, st)

        st = tuple((jnp.full((1, TQ), NEG_INF, F32), jnp.zeros((1, TQ), F32), jnp.zeros((MLA_V, TQ), F32))
                   for _ in heads)
        left = jnp.maximum(i - 1, 0)
        first_keep = kch >= jnp.where(i >= 1, 0, TQ)
        s_left, st = lax.fori_loop(0, left, body, (scores(0, heads, first_keep), st))
        s_diag = scores(i, heads, kch <= qch)
        st = softmax_pv(left, heads, s_left, st)
        outs += [acc / l for _, l, acc in softmax_pv(i, heads, s_diag, st)]
    o_ref[0] = _group_norm_t(jnp.concatenate(outs, axis=0), g_ref[...])


def _latent_attention(mq, mk, mvt, g):
    B, S, _ = mq.shape
    TQ = TQ_MLA
    W = GROUP_W
    return pl.pallas_call(
        _mla_kernel,
        grid=(B, S // TQ),
        in_specs=[pl.BlockSpec((1, TQ, 512), lambda b, i: (b, i, 0)),
                  pl.BlockSpec((1, S, 512), lambda b, i: (b, 0, 0)),
                  pl.BlockSpec((W, S), lambda b, i: (0, b)),
                  pl.BlockSpec((1, W), lambda b, i: (0, 0))],
        out_specs=pl.BlockSpec((1, TQ, W), lambda b, i: (b, i, 0)),
        out_shape=jax.ShapeDtypeStruct((B, S, W), BF16),
        compiler_params=_cparams(2),
        name="latent_attention",
    )(mq, mk, mvt, g)


def _ffn_kernel(ya_ref, yb_ref, yc_ref, yd_ref, x_ref, mod_ref, wout_ref, gffn_ref, w1_ref, w3_ref, w2_ref,
                gfin_ref, o_ref, acc_scr, *, final):
    gt1 = mod_ref[0, 2:3, :]
    sh2 = mod_ref[0, 3:4, :]
    sc2 = mod_ref[0, 4:5, :]
    gt2 = mod_ref[0, 5:6, :]
    attn = _dot(ya_ref[...], wout_ref[0:GROUP_W, :])
    for gi, y_ref in enumerate((yb_ref, yc_ref, yd_ref), start=1):
        attn = attn + _dot(y_ref[...], wout_ref[gi * GROUP_W:(gi + 1) * GROUP_W, :])
    x1 = x_ref[...] + gt1 * attn
    h = (_rms(x1, gffn_ref[...]) * (1.0 + sc2) + sh2).astype(BF16)
    for ci in range(D_FF // FF_CHUNK):
        cols = slice(ci * FF_CHUNK, (ci + 1) * FF_CHUNK)
        a = _dot(h, w1_ref[:, cols])
        gate = (a * jax.nn.sigmoid(a) * _dot(h, w3_ref[:, cols])).astype(BF16)
        part = _dot(gate, w2_ref[cols, :])
        if ci == 0:
            acc_scr[...] = part
        else:
            acc_scr[...] += part
    x2 = x1 + gt2 * acc_scr[...]
    o_ref[...] = _rms(x2, gfin_ref[...]) if final else x2


def _out_ffn(ys, x2, mod, wout, gffn, w1, w3, w2, gfin, S, final):
    N, D = x2.shape
    TM = TM_FFN
    nt = S // TM

    def full(a):
        return pl.BlockSpec(a.shape, lambda i: (0,) * a.ndim, pipeline_mode=pl.Buffered(1))

    def tok(w):
        return pl.BlockSpec((TM, w), lambda i: (i, 0))

    return pl.pallas_call(
        functools.partial(_ffn_kernel, final=final),
        grid=(N // TM,),
        in_specs=[tok(GROUP_W)] * 4 + [tok(D), pl.BlockSpec((1, 6, D), lambda i: (i // nt, 0, 0)),
                                       full(wout), full(gffn), full(w1), full(w3), full(w2), full(gfin)],
        out_specs=tok(D),
        out_shape=jax.ShapeDtypeStruct((N, D), F32),
        scratch_shapes=[pltpu.VMEM((TM, D), F32)],
        compiler_params=_cparams(1),
        name="out_ffn_final" if final else "out_ffn",
    )(*ys, x2, mod, wout, gffn, w1, w3, w2, gfin)


def _t5_bucket(rel):
    nb = T5_BUCKETS // 2
    max_exact = nb // 2
    ret = jnp.where(rel > 0, nb, 0)
    n = jnp.abs(rel)
    nf = jnp.maximum(n, 1).astype(jnp.float32)
    large = max_exact + (jnp.log(nf / max_exact) / math.log(T5_MAX_DIST / max_exact)
                         * (nb - max_exact)).astype(jnp.int32)
    large = jnp.minimum(large, nb - 1)
    return ret + jnp.where(n < max_exact, n, large)


def _rope_tables(S):
    half = MLA_ROPE // 2
    freqs = ROPE_BASE ** (-jnp.arange(half, dtype=F32) / half)
    ang = jnp.arange(S, dtype=jnp.int32).astype(F32)[:, None] * freqs[None, :]
    cos, sin = jnp.cos(ang), jnp.sin(ang)
    cos2 = jnp.concatenate([cos, cos], axis=1)
    sin2 = jnp.concatenate([-sin, sin], axis=1)
    zeros = jnp.zeros((S, LANES - MLA_NOPE - MLA_ROPE), F32)
    scale = (MLA_NOPE + MLA_ROPE) ** -0.5
    cosq = jnp.concatenate([jnp.full((S, MLA_NOPE), scale, F32), cos2 * scale, zeros], axis=1)
    sinq = jnp.concatenate([jnp.zeros((S, MLA_NOPE), F32), sin2 * scale, zeros], axis=1)
    cosk = jnp.concatenate([jnp.zeros((S, MLA_NOPE), F32), cos2, zeros], axis=1)
    sink = jnp.concatenate([jnp.zeros((S, MLA_NOPE), F32), sin2, zeros], axis=1)
    return cosq, sinq, cosk, sink


def _pack_in_weight(w):
    part = {n: w[:, IN_OFFS[k]:IN_OFFS[k + 1]] for k, n in enumerate(
        ('pool_u', 'ca_q', 'ca_k', 'ca_v', 'sa_q', 'sa_k', 'sa_v', 'idx_q', 'idx_k', 'idx_w',
         'mla_cq', 'mla_ckv', 'mla_kr'))}
    D = w.shape[0]
    z = lambda n: jnp.zeros((D, n), F32)
    qscale = HEAD_DIM ** -0.5
    saq = part['sa_q'].reshape(D, SA_HEADS, HEAD_DIM) * qscale
    saq = jnp.concatenate([saq, jnp.zeros_like(saq)], axis=2).reshape(D, SA_HEADS * LANES)
    kr = part['mla_kr']
    kr_swap = jnp.concatenate([kr[:, MLA_ROPE // 2:], kr[:, :MLA_ROPE // 2]], axis=1)
    pad_r = LANES - MLA_NOPE - MLA_ROPE
    cols = [part['pool_u'], part['ca_q'] * qscale, part['ca_k'], saq,
            part['sa_k'], part['sa_v'], part['idx_q'],
            part['idx_k'], z(IDX_DIM), z(IDX_DIM), part['idx_k'],
            part['mla_cq'], part['mla_ckv'],
            z(MLA_NOPE), kr, z(pad_r), z(MLA_NOPE), kr_swap, z(pad_r)]
    out = jnp.concatenate(cols, axis=1)
    assert out.shape[1] == C_END
    wt = jnp.concatenate([part['idx_w'].T, jnp.zeros((IWT_ROWS - IDX_HEADS, D), F32), part['sa_v'].T,
                          part['ca_v'].T], axis=0)
    return out.astype(BF16), wt.astype(BF16)


def _pack_mla_weights(w_uq, w_ukv):
    R = w_uq.shape[0]
    pad = jnp.zeros((R, MLA_HEADS, LANES - MLA_NOPE - MLA_ROPE), F32)
    rope_w = w_uq[:, :, MLA_NOPE:]
    rope_sw = jnp.concatenate([rope_w[:, :, MLA_ROPE // 2:], rope_w[:, :, :MLA_ROPE // 2]], axis=2)
    wq = jnp.concatenate([w_uq, pad], axis=2).reshape(R, MLA_HEADS * LANES)
    wqs = jnp.concatenate([jnp.zeros((R, MLA_HEADS, MLA_NOPE), F32), rope_sw, pad],
                          axis=2).reshape(R, MLA_HEADS * LANES)
    Rk = w_ukv.shape[0]
    wk = jnp.concatenate([w_ukv[:, :, :MLA_NOPE], jnp.zeros((Rk, MLA_HEADS, LANES - MLA_NOPE), F32)],
                         axis=2).reshape(Rk, MLA_HEADS * LANES)
    wvt = w_ukv[:, :, MLA_NOPE:].reshape(Rk, MLA_HEADS * MLA_V).T
    return wq.astype(BF16), wqs.astype(BF16), wk.astype(BF16), wvt.astype(BF16)


def _toeplitz(vec, rows, cols):
    L = vec.shape[-1]
    assert cols <= L - 1
    flat = jnp.tile(vec, (1, rows))[:, :rows * (L - 1)]
    return flat.reshape(vec.shape[0], rows, L - 1)[:, :, :cols]


def _signed_mod_range(L, hi):
    d = np.arange(L)
    return np.where(d <= hi, d, d - L)


def _band_bias(rel_table):
    L = CA_WIN + TQ_CA
    e = _signed_mod_range(L, TQ_CA - 1)
    ridx = np.clip(CA_LEFT_CHUNKS * CHUNK + e, -(CHUNK - 1), CA_MAX_REL) + (CHUNK - 1)
    bias = _toeplitz(rel_table[:, ridx].astype(F32), CA_WIN, TQ_CA)
    kc = np.arange(CA_WIN)[:, None] // CHUNK
    qc = np.arange(TQ_CA)[None, :] // CHUNK + CA_LEFT_CHUNKS
    valid = (kc <= qc) & (kc >= qc - CA_LEFT_CHUNKS)
    return jnp.where(valid[None], bias, NEG_INF)


def _t5_bias(t5_table):
    TQ = TQ_SA
    L = 3 * TQ
    e = _signed_mod_range(L, TQ - 1)
    rel = jnp.asarray(-e - TQ, jnp.int32)
    far = t5_table[_t5_bucket(jnp.int32(-(TQ + 1)))].astype(F32)
    vec = (t5_table[_t5_bucket(rel)].astype(F32) - far[None, :]).T
    return _toeplitz(vec, 2 * TQ, TQ)


def kernel(x, c, t5_table, w_mod, b_mod, g_mix, w_in, pool_w, pool_scale, ca_rel, mla_g_cq, mla_g_ckv,
           mla_w_uq, mla_w_ukv, g_group, w_out, g_ffn, ffn_w1, ffn_w3, ffn_w2, g_final):
    B, S, D = x.shape
    assert D == D_MODEL and S % TM_PROJ == 0 and S % TQ_SA == 0 and S >= 4 * TOPK_MAX
    N = B * S
    mod_all = _modulation(c, w_mod, b_mod)
    rope_tabs = _rope_tables(S)
    nbias = _t5_bias(t5_table)
    row = lambda v: v.reshape(1, -1).astype(F32)
    x2 = x.reshape(N, D)
    for l in range(DEPTH):
        mod = mod_all[l].reshape(B, 6, D)
        w1, wt = _pack_in_weight(w_in[l])
        wq, wqs, wk, wvt = _pack_mla_weights(mla_w_uq[l], mla_w_ukv[l])
        (pool_u, ca, saq, sakv, iq, ik, iwt, svt, cavt, mq, mk, mvt) = _inproj(
            x2, mod, row(g_mix[l]), w1, wt, row(mla_g_cq[l]), row(mla_g_ckv[l]), wq, wqs, wk, wvt, rope_tabs, S)
        gg = g_group[l].reshape(4, 1, GROUP_W).astype(F32)
        wbd = jax.scipy.linalg.block_diag(*[pool_w[l, gi] for gi in range(len(POOL_WINDOWS))]).astype(BF16)
        bsw = lambda a: a.reshape(B, S, a.shape[-1])
        y_a = _pool(bsw(pool_u), wbd, row(pool_scale[l]), gg[0])
        y_b = _chunk_attention(bsw(ca), cavt, _band_bias(ca_rel[l]), gg[1])
        y_c = _sparse_attention(bsw(saq), bsw(sakv), svt, bsw(iq), bsw(ik), iwt, nbias, gg[2])
        y_d = _latent_attention(bsw(mq), bsw(mk), mvt, gg[3])
        ys = [y.reshape(N, GROUP_W) for y in (y_a, y_b, y_c, y_d)]
        x2 = _out_ffn(ys, x2, mod, w_out[l].astype(BF16), row(g_ffn[l]), ffn_w1[l].astype(BF16),
                      ffn_w3[l].astype(BF16), ffn_w2[l].astype(BF16), row(g_final), S,
                      final=(l == DEPTH - 1))
    return x2.reshape(B, S, D)
```

```python
import functools
import math
from statistics import NormalDist

import jax
import jax.numpy as jnp
from jax import lax
import numpy as np
from jax.experimental import pallas as pl
from jax.experimental.pallas import tpu as pltpu

F32 = jnp.float32
BF16 = jnp.bfloat16

D_MODEL = 1024
DEPTH = 2
CHUNK = 64
EPS = 1e-6
NEG_INF = -1e30
GROUP_W = 256
HEAD_DIM = 64
POOL_WINDOWS = (2, 4, 8, 16)
POOL_HALO = 16
CA_HEADS = 4
CA_LEFT_CHUNKS = 8
CA_MAX_REL = 256
SA_HEADS = 4
IDX_HEADS = 8
IDX_DIM = 64
TOPK_MAX = 256
MLA_HEADS = 4
MLA_NOPE = 64
MLA_ROPE = 32
MLA_V = 64
ROPE_BASE = 10000.0
T5_BUCKETS = 32
T5_MAX_DIST = 128
D_FF = 2816
IN_WIDTHS = (256, 256, 256, 256, 256, 64, 64, 512, 64, 8, 256, 128, 32)
IN_OFFS = tuple(int(v) for v in np.cumsum((0,) + IN_WIDTHS))

LANES = 128
VMEM_LIMIT = 56 * 1024 * 1024

TM_PROJ = 512
TM_FFN = 512
TP_POOL = 512
TQ_CA = 256
CA_WIN = TQ_CA + CA_LEFT_CHUNKS * CHUNK
CA_NBLK = CA_WIN // TQ_CA
IWT_ROWS = 16
TQ_SA = 256
KB_SA = 256
COUNT_CHAINS = 2
SEARCH_FIRST_ROUND = 16
SEARCH_ROUND = 4
GUESS_SPREAD = 0.3
TQ_MLA = 256
MLA_HEADS_PER_PASS = 4
FF_CHUNK = 256

C_POOL = 0
C_CA = C_POOL + 256
C_SAQ = C_CA + 2 * GROUP_W
C_SAKV = C_SAQ + SA_HEADS * LANES
C_IQ = C_SAKV + LANES
C_IK = C_IQ + IDX_HEADS * IDX_DIM
C_CQ = C_IK + 2 * LANES
C_CKV = C_CQ + 256
C_KRF = C_CKV + LANES
C_KRS = C_KRF + LANES
C_END = C_KRS + LANES

INT_MIN = -2 ** 31
KEY_ALL = INT_MIN - int(np.array(-np.inf, np.float32).view(np.int32)) + 1


def _cparams(n_axes):
    return pltpu.CompilerParams(dimension_semantics=("arbitrary",) * n_axes,
                                vmem_limit_bytes=VMEM_LIMIT)


def _rms(x, g):
    return x * lax.rsqrt(jnp.mean(x * x, axis=-1, keepdims=True) + EPS) * g


def _dot(a, b):
    return jnp.dot(a, b, preferred_element_type=F32)


def _dot_t(a, b):
    return lax.dot_general(a, b, (((1,), (1,)), ((), ())), preferred_element_type=F32)


def _mod_kernel(c_ref, w_ref, b_ref, o_ref):
    c = c_ref[...]
    act = c * jax.nn.sigmoid(c)
    o_ref[0] = jnp.dot(act, w_ref[0], precision=lax.Precision.HIGHEST,
                       preferred_element_type=F32) + b_ref[0]


def _modulation(c, w_mod, b_mod):
    L, D, W = w_mod.shape
    B = c.shape[0]
    nj = W // D
    return pl.pallas_call(
        _mod_kernel,
        grid=(L, nj),
        in_specs=[pl.BlockSpec((B, D), lambda l, j: (0, 0)),
                  pl.BlockSpec((1, D, D), lambda l, j: (l, 0, j)),
                  pl.BlockSpec((1, 1, D), lambda l, j: (l, 0, j))],
        out_specs=pl.BlockSpec((1, B, D), lambda l, j: (l, 0, j)),
        out_shape=jax.ShapeDtypeStruct((L, B, W), F32),
        compiler_params=_cparams(2),
        name="modulation",
    )(c, w_mod, b_mod.reshape(L, 1, W))


def _inproj_kernel(x_ref, mod_ref, gmix_ref, w_ref, wt_ref, gcq_ref, gckv_ref, wq_ref, wqs_ref, wk_ref, wvt_ref,
                   cosq_ref, sinq_ref, cosk_ref, sink_ref,
                   pool_o, ca_o, saq_o, sakv_o, iq_o, ik_o, iwt_o, svt_o, cavt_o, mq_o, mk_o, mvt_o):
    sh1 = mod_ref[0, 0:1, :]
    sc1 = mod_ref[0, 1:2, :]
    h = (_rms(x_ref[...], gmix_ref[...]) * (1.0 + sc1) + sh1).astype(BF16)

    def seg(a, b):
        return _dot(h, w_ref[:, a:b])

    pool_o[...] = seg(C_POOL, C_CA)
    ca_o[...] = seg(C_CA, C_SAQ).astype(BF16)
    saq_o[...] = seg(C_SAQ, C_SAKV).astype(BF16)
    sakv_o[...] = seg(C_SAKV, C_IQ).astype(BF16)
    iq_o[...] = seg(C_IQ, C_IK).astype(BF16)
    ik_o[...] = seg(C_IK, C_CQ).astype(BF16)
    tr = _dot_t(wt_ref[...], h)
    iwt_o[...] = tr[0:IWT_ROWS] * ((IDX_HEADS ** -0.5) * (IDX_DIM ** -0.5))
    svt_o[...] = tr[IWT_ROWS:IWT_ROWS + HEAD_DIM].astype(BF16)
    cavt_o[...] = tr[IWT_ROWS + HEAD_DIM:].astype(BF16)

    qn = _rms(seg(C_CQ, C_CKV), gcq_ref[...]).astype(BF16)
    qf = _dot(qn, wq_ref[...])
    qs = _dot(qn, wqs_ref[...])
    cosq = jnp.concatenate([cosq_ref[...]] * MLA_HEADS, axis=1)
    sinq = jnp.concatenate([sinq_ref[...]] * MLA_HEADS, axis=1)
    mq_o[...] = (qf * cosq + qs * sinq).astype(BF16)

    kvn = _rms(seg(C_CKV, C_KRF), gckv_ref[...]).astype(BF16)
    kvf = _dot(kvn, wk_ref[...])
    krope = seg(C_KRF, C_KRS) * cosk_ref[...] + seg(C_KRS, C_END) * sink_ref[...]
    for hd in range(MLA_HEADS):
        mk_o[:, hd * LANES:(hd + 1) * LANES] = (kvf[:, hd * LANES:(hd + 1) * LANES] + krope).astype(BF16)
    mvt_o[...] = _dot_t(wvt_ref[...], kvn).astype(BF16)


def _inproj(x2, mod, gmix, w1, wt, gcq, gckv, wq, wqs, wk, wvt, rope_tabs, S):
    N, D = x2.shape
    TM = TM_PROJ
    nt = S // TM
    cosq, sinq, cosk, sink = rope_tabs

    def full(a):
        return pl.BlockSpec(a.shape, lambda i: (0,) * a.ndim)

    def tok(w):
        return pl.BlockSpec((TM, w), lambda i: (i, 0))

    tab = pl.BlockSpec((TM, LANES), lambda i: (i % nt, 0))
    def tokt(rows):
        return pl.BlockSpec((rows, TM), lambda i: (0, i))

    outs = [(C_CA - C_POOL, F32, True), (C_SAQ - C_CA, BF16, True), (C_SAKV - C_SAQ, BF16, True),
            (C_IQ - C_SAKV, BF16, True), (C_IK - C_IQ, BF16, True), (C_CQ - C_IK, BF16, True),
            (IWT_ROWS, F32, False), (HEAD_DIM, BF16, False), (GROUP_W, BF16, False),
            (MLA_HEADS * LANES, BF16, True), (MLA_HEADS * LANES, BF16, True), (GROUP_W, BF16, False)]
    return pl.pallas_call(
        _inproj_kernel,
        grid=(N // TM,),
        in_specs=[tok(D),
                  pl.BlockSpec((1, 6, D), lambda i: (i // nt, 0, 0)),
                  full(gmix), full(w1), full(wt), full(gcq), full(gckv), full(wq), full(wqs), full(wk), full(wvt),
                  tab, tab, tab, tab],
        out_specs=[tok(w) if tm else tokt(w) for w, _, tm in outs],
        out_shape=[jax.ShapeDtypeStruct((N, w) if tm else (w, N), dt) for w, dt, tm in outs],
        compiler_params=_cparams(1),
        name="inproj",
    )(x2, mod, gmix, w1, wt, gcq, gckv, wq, wqs, wk, wvt, cosq, sinq, cosk, sink)


def _pool_kernel(u_ref, halo_ref, w_ref, scale_ref, g_ref, o_ref, pad_scr):
    i = pl.program_id(1)
    TP = u_ref.shape[1]
    u = u_ref[0]
    pad_scr[0:POOL_HALO, :] = jnp.where(i > 0, halo_ref[0], 0.0)
    pad_scr[POOL_HALO:, :] = u

    def shifted(j):
        return pad_scr[POOL_HALO - j:POOL_HALO - j + TP, :]

    lane = lax.broadcasted_iota(jnp.int32, (TP, GROUP_W), 1)
    w2 = u + shifted(1)
    w4 = w2 + shifted(2) + shifted(3)
    w8 = w4
    for j in range(4, 8):
        w8 = w8 + shifted(j)
    w16 = w8
    for j in range(8, 16):
        w16 = w16 + shifted(j)
    win = jnp.where(lane < 64, w2, jnp.where(lane < 128, w4, jnp.where(lane < 192, w8, w16)))
    wlen = jnp.where(lane < 64, 2, jnp.where(lane < 128, 4, jnp.where(lane < 192, 8, 16)))
    t = i * TP + lax.broadcasted_iota(jnp.int32, (TP, GROUP_W), 0)
    cnt = jnp.minimum(t + 1, wlen).astype(F32)
    d = (win / cnt - u).astype(BF16)
    y = _dot(d, w_ref[...]) * scale_ref[...]
    o_ref[0] = _rms(y, g_ref[...]).astype(BF16)


def _pool(u, wbd, scale, g):
    B, S, W = u.shape
    TP = TP_POOL
    hb = TP // POOL_HALO
    return pl.pallas_call(
        _pool_kernel,
        grid=(B, S // TP),
        in_specs=[pl.BlockSpec((1, TP, W), lambda b, i: (b, i, 0)),
                  pl.BlockSpec((1, POOL_HALO, W), lambda b, i: (b, jnp.maximum(i * hb - 1, 0), 0)),
                  pl.BlockSpec((W, W), lambda b, i: (0, 0)),
                  pl.BlockSpec((1, W), lambda b, i: (0, 0)),
                  pl.BlockSpec((1, W), lambda b, i: (0, 0))],
        out_specs=pl.BlockSpec((1, TP, W), lambda b, i: (b, i, 0)),
        out_shape=jax.ShapeDtypeStruct((B, S, W), BF16),
        scratch_shapes=[pltpu.VMEM((POOL_HALO + TP, W), F32)],
        compiler_params=_cparams(2),
        name="pool_mixer",
    )(u, u, wbd, scale, g)


def _ca_kernel(q_ref, k_ref, vt_ref, bias_ref, g_ref, o_ref):
    i = pl.program_id(1)
    TQ = TQ_CA
    lane = lax.broadcasted_iota(jnp.int32, (TQ, LANES), 1)
    starts = []
    for j in range(CA_NBLK):
        kb = i - (CA_NBLK - 1) + j
        starts.append((kb >= 0, pl.multiple_of(jnp.maximum(kb, 0) * TQ, TQ)))
    scored = []
    for hd in range(CA_HEADS):
        cols = slice((hd // 2) * LANES, (hd // 2 + 1) * LANES)
        keep = (lane < HEAD_DIM) if hd % 2 == 0 else (lane >= HEAD_DIM)
        qh = jnp.where(keep, q_ref[0, :, cols].astype(F32), 0.0).astype(BF16)
        parts = []
        for j, (present, start) in enumerate(starts):
            s = _dot_t(k_ref[0, pl.ds(start, TQ), cols], qh) + bias_ref[hd, j * TQ:(j + 1) * TQ, :]
            parts.append(jnp.where(present, s, NEG_INF))
        m = parts[0].max(axis=0, keepdims=True)
        for s in parts[1:]:
            m = jnp.maximum(m, s.max(axis=0, keepdims=True))
        scored.append((parts, m))
    outs = []
    for hd, (parts, m) in enumerate(scored):
        l = jnp.zeros((1, TQ), F32)
        acc = jnp.zeros((HEAD_DIM, TQ), F32)
        for j, (_, start) in enumerate(starts):
            p = jnp.exp(parts[j] - m)
            l = l + p.sum(axis=0, keepdims=True)
            acc = acc + _dot(vt_ref[hd * HEAD_DIM:(hd + 1) * HEAD_DIM, pl.ds(start, TQ)], p.astype(BF16))
        outs.append(acc / l)
    o_ref[0] = _group_norm_t(jnp.concatenate(outs, axis=0), g_ref[...])


def _chunk_attention(caqk, cavt, bias, g):
    B, S, _ = caqk.shape
    W = GROUP_W
    TQ = TQ_CA
    return pl.pallas_call(
        _ca_kernel,
        grid=(B, S // TQ),
        in_specs=[pl.BlockSpec((1, TQ, W), lambda b, i: (b, i, 0)),
                  pl.BlockSpec((1, S, W), lambda b, i: (b, 0, 1)),
                  pl.BlockSpec((W, S), lambda b, i: (0, b)),
                  pl.BlockSpec(bias.shape, lambda b, i: (0, 0, 0)),
                  pl.BlockSpec((1, W), lambda b, i: (0, 0))],
        out_specs=pl.BlockSpec((1, TQ, W), lambda b, i: (b, i, 0)),
        out_shape=jax.ShapeDtypeStruct((B, S, W), BF16),
        compiler_params=_cparams(2),
        name="band_attention",
    )(caqk, caqk, cavt, bias, g)


def _score_key(score):
    b = lax.bitcast_convert_type(score, jnp.int32)
    return jnp.where(b < 0, jnp.int32(INT_MIN) - b, b)


def _sa_kernel(q_ref, kv_ref, vt_ref, iq_ref, ik_ref, iwt_ref, zq_ref, nbias_ref, g_ref, o_ref, key_scr):
    i = pl.program_id(1)
    TQ, KB = TQ_SA, KB_SA
    K = float(TOPK_MAX)
    nb = i + 1
    q0 = i * TQ
    krow = lax.broadcasted_iota(jnp.int32, (KB, TQ), 0)
    qchunk = (q0 + lax.broadcasted_iota(jnp.int32, (KB, TQ), 1)) // CHUNK

    iwt = iwt_ref[...]

    def score_block(j, carry):
        smax, s1, s2 = carry
        k0 = pl.multiple_of(j * KB, KB)
        ik = ik_ref[0, pl.ds(k0, KB), :]
        ik2 = jnp.concatenate([ik[:, :LANES], ik[:, LANES:]], axis=0)
        sc = jnp.zeros((KB, TQ), F32)
        for p in range(IDX_HEADS // 2):
            logits = _dot_t(ik2, iq_ref[0, :, p * LANES:(p + 1) * LANES])
            sc = sc + iwt[2 * p:2 * p + 1, :] * jnp.maximum(logits[:KB], 0.0)
            sc = sc + iwt[2 * p + 1:2 * p + 2, :] * jnp.maximum(logits[KB:], 0.0)
        adm = (k0 + krow) // CHUNK <= qchunk
        sc = jnp.where(adm, sc, -jnp.inf)
        key_scr[pl.ds(k0, KB), :] = _score_key(sc)
        smax = jnp.maximum(smax, sc.max(axis=0, keepdims=True))
        full = j < i
        s1 = jnp.where(full, s1 + sc.sum(axis=0, keepdims=True), s1)
        s2 = jnp.where(full, s2 + (sc * sc).sum(axis=0, keepdims=True), s2)
        return smax, s1, s2

    smax, s1, s2 = lax.fori_loop(
        0, (nb + 1) // 2, lambda j, c: score_block(2 * j + 1, score_block(2 * j, c)),
        (jnp.full((1, TQ), -jnp.inf, F32), jnp.zeros((1, TQ), F32), jnp.zeros((1, TQ), F32)))

    def count_ge(cand):
        def body(j, acc):
            blk = key_scr[pl.ds(pl.multiple_of(j * (2 * KB), 2 * KB), 2 * KB), :]
            ones = jnp.where(blk >= cand, 1.0, 0.0)
            return acc + ones.reshape(COUNT_CHAINS, -1, 8, TQ).sum(axis=1)
        acc = lax.fori_loop(0, (nb + 1) // 2, body, jnp.zeros((COUNT_CHAINS, 8, TQ), F32))
        return acc.sum(axis=0).sum(axis=0, keepdims=True)

    def search():
        def unkey(k):
            return lax.bitcast_convert_type(jnp.where(k < 0, jnp.int32(INT_MIN) - k, k), F32)

        def is_active(lo, hi, clo):
            return jnp.logical_and(clo > K, hi > lo + 1)

        def cond(st):
            _, lo, hi, clo, _ = st
            act = jnp.where(is_active(lo, hi, clo), 1.0, 0.0)
            return jnp.max(jnp.maximum(act[:, :LANES], act[:, LANES:])) > 0.0

        n_full = (i * KB).astype(F32)
        mean = s1 / n_full
        std = jnp.sqrt(jnp.maximum(s2 / n_full - mean * mean, 0.0))
        zq = jnp.max(zq_ref[...], axis=0, keepdims=True)
        guess_lo = _score_key(mean + (zq - GUESS_SPREAD) * std)
        guess_hi = _score_key(mean + (zq + GUESS_SPREAD) * std)

        def step(_, st):
            it, lo, hi, clo, chi = st
            active = is_active(lo, hi, clo)
            lf, hf = unkey(lo), unkey(hi)
            lc = jnp.log(clo)
            frac = jnp.clip((lc - math.log(K - 0.5)) / (lc - jnp.log(jnp.maximum(chi, 0.5))), 0.05, 0.95)
            cand = _score_key(lf + frac * (hf - lf))
            cand = jnp.where(it % 3 == 2, (lo >> 1) + (hi >> 1) + (lo & hi & 1), cand)
            cand = jnp.where(it == 0, guess_lo, cand)
            cand = jnp.where(it == 1, guess_hi, cand)
            cand = jnp.where(active, jnp.clip(cand, lo + 1, hi - 1), lo)
            cnt = count_ge(cand)
            up = jnp.logical_and(active, cnt >= K)
            down = jnp.logical_and(active, cnt < K)
            return (it + 1, jnp.where(up, cand, lo), jnp.where(down, cand, hi),
                    jnp.where(up, cnt, clo), jnp.where(down, cnt, chi))

        lo0 = jnp.full((1, TQ), KEY_ALL - 1, jnp.int32)
        hi0 = _score_key(smax) + 1
        clo0 = jnp.zeros((1, TQ), F32) + ((nb + 1) // 2 * (2 * KB)).astype(F32)
        st = (jnp.int32(0), lo0, hi0, clo0, jnp.zeros((1, TQ), F32))
        st = lax.fori_loop(0, SEARCH_FIRST_ROUND, step, st)
        st = lax.while_loop(cond, lambda s: lax.fori_loop(0, SEARCH_ROUND, step, s), st)
        return st[1], st[3]

    def no_search():
        return jnp.full((1, TQ), KEY_ALL, jnp.int32), jnp.full((1, TQ), K, F32)

    t, cnt_t = lax.cond(i > 0, search, no_search)
    t = jnp.maximum(t, KEY_ALL)

    @pl.when(jnp.max(cnt_t) > K)
    def _():
        allowed = K - count_ge(t + 1)
        r = lax.broadcasted_iota(jnp.int32, (KB, KB), 0)
        c = lax.broadcasted_iota(jnp.int32, (KB, KB), 1)
        earlier = jnp.where(c < r, 1.0, 0.0).astype(BF16)

        def body(j, seen):
            sl = pl.ds(pl.multiple_of(j * KB, KB), KB)
            blk = key_scr[sl, :]
            eq = jnp.where(blk == t, 1.0, 0.0)
            rank = _dot(earlier, eq.astype(BF16)) + seen
            demote = eq * jnp.where(rank >= allowed, 1.0, 0.0)
            key_scr[sl, :] = jnp.where(demote > 0.5, t - 1, blk)
            return seen + eq.sum(axis=0, keepdims=True)

        lax.fori_loop(0, nb, body, jnp.zeros((1, TQ), F32))

    def scores(j, bias_rows, present):
        k0 = pl.multiple_of(j * KB, KB)
        kblk = kv_ref[0, pl.ds(k0, KB), :]
        sel = key_scr[pl.ds(k0, KB), :] >= jnp.where(present, t, jnp.int32(2 ** 31 - 1))
        out = []
        for hd in range(SA_HEADS):
            s = _dot_t(kblk, q_ref[0, :, hd * LANES:(hd + 1) * LANES])
            if bias_rows is not None:
                s = s + nbias_ref[hd, bias_rows, :]
            s = jnp.where(sel, s, NEG_INF)
            out.append((s, s.max(axis=0, keepdims=True)))
        return tuple(out)

    def softmax_pv(j, s_all, st):
        vt = vt_ref[:, pl.ds(pl.multiple_of(j * KB, KB), KB)]
        out = []
        for hd in range(SA_HEADS):
            m, l, acc = st[hd]
            s, smax = s_all[hd]
            mn = jnp.maximum(m, smax)
            p = jnp.exp(s - mn)
            alpha = jnp.exp(m - mn)
            out.append((mn, alpha * l + p.sum(axis=0, keepdims=True), alpha * acc + _dot(vt, p.astype(BF16))))
        return tuple(out)

    def body(j, carry):
        s_cur, st = carry
        s_next = scores(j + 1, None, True)
        return s_next, softmax_pv(j, s_cur, st)

    st = tuple((jnp.full((1, TQ), NEG_INF, F32), jnp.zeros((1, TQ), F32), jnp.zeros((HEAD_DIM, TQ), F32))
               for _ in range(SA_HEADS))
    last_far = jnp.maximum(i - 2, 0)
    left = jnp.maximum(i - 1, 0)
    s_far, st = lax.fori_loop(0, last_far, body, (scores(0, None, i >= 2), st))
    s_left = scores(left, slice(0, KB), i >= 1)
    st = softmax_pv(last_far, s_far, st)
    s_diag = scores(i, slice(KB, 2 * KB), True)
    st = softmax_pv(left, s_left, st)
    st = softmax_pv(i, s_diag, st)
    y_t = jnp.concatenate([acc / l for _, l, acc in st], axis=0)
    o_ref[0] = _group_norm_t(y_t, g_ref[...])


def _sparse_attention(saq, sakv, svt, iq, ik, iwt, nbias, g):
    B, S, _ = saq.shape
    TQ = TQ_SA
    W = GROUP_W
    nt = S // TQ
    n_adm = (np.arange(S) // CHUNK + 1) * CHUNK
    zq = np.array([NormalDist().inv_cdf(1.0 - TOPK_MAX / n) if n > TOPK_MAX else 0.0 for n in n_adm], np.float32)
    zq = jnp.asarray(np.tile(zq[None, :], (8, 1)))
    return pl.pallas_call(
        _sa_kernel,
        grid=(B, nt),
        in_specs=[pl.BlockSpec((1, TQ, saq.shape[2]), lambda b, i: (b, i, 0)),
                  pl.BlockSpec((1, S, sakv.shape[2]), lambda b, i: (b, 0, 0)),
                  pl.BlockSpec((HEAD_DIM, S), lambda b, i: (0, b)),
                  pl.BlockSpec((1, TQ, iq.shape[2]), lambda b, i: (b, i, 0)),
                  pl.BlockSpec((1, S, ik.shape[2]), lambda b, i: (b, 0, 0)),
                  pl.BlockSpec((IWT_ROWS, TQ), lambda b, i: (0, b * nt + i)),
                  pl.BlockSpec((8, TQ), lambda b, i: (0, i)),
                  pl.BlockSpec(nbias.shape, lambda b, i: (0, 0, 0)),
                  pl.BlockSpec((1, W), lambda b, i: (0, 0))],
        out_specs=pl.BlockSpec((1, TQ, W), lambda b, i: (b, i, 0)),
        out_shape=jax.ShapeDtypeStruct((B, S, W), BF16),
        scratch_shapes=[pltpu.VMEM((S, TQ), jnp.int32)],
        compiler_params=_cparams(2),
        name="sparse_attention",
    )(saq, sakv, svt, iq, ik, iwt, zq, nbias, g)


def _group_norm_t(y_t, g):
    inv = lax.rsqrt(jnp.mean(y_t * y_t, axis=0, keepdims=True) + EPS)
    return ((y_t * inv).T * g).astype(BF16)


def _mla_kernel(q_ref, k_ref, vt_ref, g_ref, o_ref):
    i = pl.program_id(1)
    TQ = TQ_MLA
    kch = lax.broadcasted_iota(jnp.int32, (TQ, TQ), 0) // CHUNK
    qch = lax.broadcasted_iota(jnp.int32, (TQ, TQ), 1) // CHUNK

    def scores(j, heads, keep):
        k0 = pl.multiple_of(j * TQ, TQ)
        out = []
        for hd in heads:
            cols = slice(hd * LANES, (hd + 1) * LANES)
            s = _dot_t(k_ref[0, pl.ds(k0, TQ), cols], q_ref[0, :, cols])
            if keep is not None:
                s = jnp.where(keep, s, NEG_INF)
            out.append((s, s.max(axis=0, keepdims=True)))
        return tuple(out)

    def softmax_pv(j, heads, s_all, st):
        k0 = pl.multiple_of(j * TQ, TQ)
        out = []
        for n, hd in enumerate(heads):
            m, l, acc = st[n]
            s, smax = s_all[n]
            mn = jnp.maximum(m, smax)
            p = jnp.exp(s - mn)
            alpha = jnp.exp(m - mn)
            vt = vt_ref[hd * MLA_V:(hd + 1) * MLA_V, pl.ds(k0, TQ)]
            out.append((mn, alpha * l + p.sum(axis=0, keepdims=True),
                        alpha * acc + _dot(vt, p.astype(BF16))))
        return tuple(out)

    outs = []
    for g0 in range(0, MLA_HEADS, MLA_HEADS_PER_PASS):
        heads = tuple(range(g0, g0 + MLA_HEADS_PER_PASS))

        def body(j, carry, heads=heads):
            s_cur, st = carry
            s_next = scores(j + 1, heads, None)
            return s_next, softmax_pv(j, heads, s_cur, st)

        st = tuple((jnp.full((1, TQ), NEG_INF, F32), jnp.zeros((1, TQ), F32), jnp.zeros((MLA_V, TQ), F32))
                   for _ in heads)
        left = jnp.maximum(i - 1, 0)
        first_keep = kch >= jnp.where(i >= 1, 0, TQ)
        s_left, st = lax.fori_loop(0, left, body, (scores(0, heads, first_keep), st))
        s_diag = scores(i, heads, kch <= qch)
        st = softmax_pv(left, heads, s_left, st)
        outs += [acc / l for _, l, acc in softmax_pv(i, heads, s_diag, st)]
    o_ref[0] = _group_norm_t(jnp.concatenate(outs, axis=0), g_ref[...])


def _latent_attention(mq, mk, mvt, g):
    B, S, _ = mq.shape
    TQ = TQ_MLA
    W = GROUP_W
    return pl.pallas_call(
        _mla_kernel,
        grid=(B, S // TQ),
        in_specs=[pl.BlockSpec((1, TQ, mq.shape[2]), lambda b, i: (b, i, 0)),
                  pl.BlockSpec((1, S, mk.shape[2]), lambda b, i: (b, 0, 0)),
                  pl.BlockSpec((W, S), lambda b, i: (0, b)),
                  pl.BlockSpec((1, W), lambda b, i: (0, 0))],
        out_specs=pl.BlockSpec((1, TQ, W), lambda b, i: (b, i, 0)),
        out_shape=jax.ShapeDtypeStruct((B, S, W), BF16),
        compiler_params=_cparams(2),
        name="latent_attention",
    )(mq, mk, mvt, g)


def _ffn_kernel(ya_ref, yb_ref, yc_ref, yd_ref, x_ref, mod_ref, wout_ref, gffn_ref, w1_ref, w3_ref, w2_ref,
                gfin_ref, o_ref, acc_scr, *, final):
    gt1 = mod_ref[0, 2:3, :]
    sh2 = mod_ref[0, 3:4, :]
    sc2 = mod_ref[0, 4:5, :]
    gt2 = mod_ref[0, 5:6, :]
    attn = _dot(ya_ref[...], wout_ref[0:GROUP_W, :])
    for gi, y_ref in enumerate((yb_ref, yc_ref, yd_ref), start=1):
        attn = attn + _dot(y_ref[...], wout_ref[gi * GROUP_W:(gi + 1) * GROUP_W, :])
    x1 = x_ref[...] + gt1 * attn
    h = (_rms(x1, gffn_ref[...]) * (1.0 + sc2) + sh2).astype(BF16)
    for ci in range(D_FF // FF_CHUNK):
        cols = slice(ci * FF_CHUNK, (ci + 1) * FF_CHUNK)
        a = _dot(h, w1_ref[:, cols])
        gate = (a * jax.nn.sigmoid(a) * _dot(h, w3_ref[:, cols])).astype(BF16)
        part = _dot(gate, w2_ref[cols, :])
        if ci == 0:
            acc_scr[...] = part
        else:
            acc_scr[...] += part
    x2 = x1 + gt2 * acc_scr[...]
    o_ref[...] = _rms(x2, gfin_ref[...]) if final else x2


def _out_ffn(ys, x2, mod, wout, gffn, w1, w3, w2, gfin, S, final):
    N, D = x2.shape
    TM = TM_FFN
    nt = S // TM

    def full(a):
        return pl.BlockSpec(a.shape, lambda i: (0,) * a.ndim, pipeline_mode=pl.Buffered(1))

    def tok(w):
        return pl.BlockSpec((TM, w), lambda i: (i, 0))

    return pl.pallas_call(
        functools.partial(_ffn_kernel, final=final),
        grid=(N // TM,),
        in_specs=[tok(GROUP_W)] * 4 + [tok(D), pl.BlockSpec((1, 6, D), lambda i: (i // nt, 0, 0)),
                                       full(wout), full(gffn), full(w1), full(w3), full(w2), full(gfin)],
        out_specs=tok(D),
        out_shape=jax.ShapeDtypeStruct((N, D), F32),
        scratch_shapes=[pltpu.VMEM((TM, D), F32)],
        compiler_params=_cparams(1),
        name="out_ffn_final" if final else "out_ffn",
    )(*ys, x2, mod, wout, gffn, w1, w3, w2, gfin)


def _t5_bucket(rel):
    nb = T5_BUCKETS // 2
    max_exact = nb // 2
    ret = jnp.where(rel > 0, nb, 0)
    n = jnp.abs(rel)
    nf = jnp.maximum(n, 1).astype(jnp.float32)
    large = max_exact + (jnp.log(nf / max_exact) / math.log(T5_MAX_DIST / max_exact)
                         * (nb - max_exact)).astype(jnp.int32)
    large = jnp.minimum(large, nb - 1)
    return ret + jnp.where(n < max_exact, n, large)


def _rope_tables(S):
    half = MLA_ROPE // 2
    freqs = ROPE_BASE ** (-jnp.arange(half, dtype=F32) / half)
    ang = jnp.arange(S, dtype=jnp.int32).astype(F32)[:, None] * freqs[None, :]
    cos, sin = jnp.cos(ang), jnp.sin(ang)
    cos2 = jnp.concatenate([cos, cos], axis=1)
    sin2 = jnp.concatenate([-sin, sin], axis=1)
    zeros = jnp.zeros((S, LANES - MLA_NOPE - MLA_ROPE), F32)
    scale = (MLA_NOPE + MLA_ROPE) ** -0.5
    cosq = jnp.concatenate([jnp.full((S, MLA_NOPE), scale, F32), cos2 * scale, zeros], axis=1)
    sinq = jnp.concatenate([jnp.zeros((S, MLA_NOPE), F32), sin2 * scale, zeros], axis=1)
    cosk = jnp.concatenate([jnp.zeros((S, MLA_NOPE), F32), cos2, zeros], axis=1)
    sink = jnp.concatenate([jnp.zeros((S, MLA_NOPE), F32), sin2, zeros], axis=1)
    return cosq, sinq, cosk, sink


def _pack_in_weight(w):
    part = {n: w[:, IN_OFFS[k]:IN_OFFS[k + 1]] for k, n in enumerate(
        ('pool_u', 'ca_q', 'ca_k', 'ca_v', 'sa_q', 'sa_k', 'sa_v', 'idx_q', 'idx_k', 'idx_w',
         'mla_cq', 'mla_ckv', 'mla_kr'))}
    D = w.shape[0]
    z = lambda n: jnp.zeros((D, n), F32)
    qscale = HEAD_DIM ** -0.5
    saq = part['sa_q'].reshape(D, SA_HEADS, HEAD_DIM) * qscale
    saq = jnp.concatenate([saq, jnp.zeros_like(saq)], axis=2).reshape(D, SA_HEADS * LANES)
    kr = part['mla_kr']
    kr_swap = jnp.concatenate([kr[:, MLA_ROPE // 2:], kr[:, :MLA_ROPE // 2]], axis=1)
    pad_r = LANES - MLA_NOPE - MLA_ROPE
    cols = [part['pool_u'], part['ca_q'] * qscale, part['ca_k'], saq,
            part['sa_k'], part['sa_v'], part['idx_q'],
            part['idx_k'], z(IDX_DIM), z(IDX_DIM), part['idx_k'],
            part['mla_cq'], part['mla_ckv'],
            z(MLA_NOPE), kr, z(pad_r), z(MLA_NOPE), kr_swap, z(pad_r)]
    out = jnp.concatenate(cols, axis=1)
    assert out.shape[1] == C_END
    wt = jnp.concatenate([part['idx_w'].T, jnp.zeros((IWT_ROWS - IDX_HEADS, D), F32), part['sa_v'].T,
                          part['ca_v'].T], axis=0)
    return out.astype(BF16), wt.astype(BF16)


def _pack_mla_weights(w_uq, w_ukv):
    R = w_uq.shape[0]
    pad = jnp.zeros((R, MLA_HEADS, LANES - MLA_NOPE - MLA_ROPE), F32)
    rope_w = w_uq[:, :, MLA_NOPE:]
    rope_sw = jnp.concatenate([rope_w[:, :, MLA_ROPE // 2:], rope_w[:, :, :MLA_ROPE // 2]], axis=2)
    wq = jnp.concatenate([w_uq, pad], axis=2).reshape(R, MLA_HEADS * LANES)
    wqs = jnp.concatenate([jnp.zeros((R, MLA_HEADS, MLA_NOPE), F32), rope_sw, pad],
                          axis=2).reshape(R, MLA_HEADS * LANES)
    Rk = w_ukv.shape[0]
    wk = jnp.concatenate([w_ukv[:, :, :MLA_NOPE], jnp.zeros((Rk, MLA_HEADS, LANES - MLA_NOPE), F32)],
                         axis=2).reshape(Rk, MLA_HEADS * LANES)
    wvt = w_ukv[:, :, MLA_NOPE:].reshape(Rk, MLA_HEADS * MLA_V).T
    return wq.astype(BF16), wqs.astype(BF16), wk.astype(BF16), wvt.astype(BF16)


def _toeplitz(vec, rows, cols):
    L = vec.shape[-1]
    assert cols <= L - 1
    flat = jnp.tile(vec, (1, rows))[:, :rows * (L - 1)]
    return flat.reshape(vec.shape[0], rows, L - 1)[:, :, :cols]


def _signed_mod_range(L, hi):
    d = np.arange(L)
    return np.where(d <= hi, d, d - L)


def _band_bias(rel_table):
    L = CA_WIN + TQ_CA
    e = _signed_mod_range(L, TQ_CA - 1)
    ridx = np.clip(CA_LEFT_CHUNKS * CHUNK + e, -(CHUNK - 1), CA_MAX_REL) + (CHUNK - 1)
    bias = _toeplitz(rel_table[:, ridx].astype(F32), CA_WIN, TQ_CA)
    kc = np.arange(CA_WIN)[:, None] // CHUNK
    qc = np.arange(TQ_CA)[None, :] // CHUNK + CA_LEFT_CHUNKS
    valid = (kc <= qc) & (kc >= qc - CA_LEFT_CHUNKS)
    return jnp.where(valid[None], bias, NEG_INF)


def _t5_bias(t5_table):
    TQ = TQ_SA
    L = 3 * TQ
    e = _signed_mod_range(L, TQ - 1)
    rel = jnp.asarray(-e - TQ, jnp.int32)
    far = t5_table[_t5_bucket(jnp.int32(-(TQ + 1)))].astype(F32)
    vec = (t5_table[_t5_bucket(rel)].astype(F32) - far[None, :]).T
    return _toeplitz(vec, 2 * TQ, TQ)


def kernel(x, c, t5_table, w_mod, b_mod, g_mix, w_in, pool_w, pool_scale, ca_rel, mla_g_cq, mla_g_ckv,
           mla_w_uq, mla_w_ukv, g_group, w_out, g_ffn, ffn_w1, ffn_w3, ffn_w2, g_final):
    B, S, D = x.shape
    assert D == D_MODEL and S % TM_PROJ == 0 and S % TQ_SA == 0 and S >= 4 * TOPK_MAX
    N = B * S
    mod_all = _modulation(c, w_mod, b_mod)
    rope_tabs = _rope_tables(S)
    nbias = _t5_bias(t5_table)
    row = lambda v: v.reshape(1, -1).astype(F32)
    x2 = x.reshape(N, D)
    for l in range(DEPTH):
        mod = mod_all[l].reshape(B, 6, D)
        w1, wt = _pack_in_weight(w_in[l])
        wq, wqs, wk, wvt = _pack_mla_weights(mla_w_uq[l], mla_w_ukv[l])
        (pool_u, ca, saq, sakv, iq, ik, iwt, svt, cavt, mq, mk, mvt) = _inproj(
            x2, mod, row(g_mix[l]), w1, wt, row(mla_g_cq[l]), row(mla_g_ckv[l]), wq, wqs, wk, wvt, rope_tabs, S)
        gg = g_group[l].reshape(4, 1, GROUP_W).astype(F32)
        wbd = jax.scipy.linalg.block_diag(*[pool_w[l, gi] for gi in range(len(POOL_WINDOWS))]).astype(BF16)
        bsw = lambda a: a.reshape(B, S, a.shape[-1])
        y_a = _pool(bsw(pool_u), wbd, row(pool_scale[l]), gg[0])
        y_b = _chunk_attention(bsw(ca), cavt, _band_bias(ca_rel[l]), gg[1])
        y_c = _sparse_attention(bsw(saq), bsw(sakv), svt, bsw(iq), bsw(ik), iwt, nbias, gg[2])
        y_d = _latent_attention(bsw(mq), bsw(mk), mvt, gg[3])
        ys = [y.reshape(N, GROUP_W) for y in (y_a, y_b, y_c, y_d)]
        x2 = _out_ffn(ys, x2, mod, w_out[l].astype(BF16), row(g_ffn[l]), ffn_w1[l].astype(BF16),
                      ffn_w3[l].astype(BF16), ffn_w2[l].astype(BF16), row(g_final), S,
                      final=(l == DEPTH - 1))
    return x2.reshape(B, S, D)
```

```python
import functools
import math
from statistics import NormalDist

import jax
import jax.numpy as jnp
from jax import lax
import numpy as np
from jax.experimental import pallas as pl
from jax.experimental.pallas import tpu as pltpu

F32 = jnp.float32
BF16 = jnp.bfloat16

D_MODEL = 1024
DEPTH = 2
CHUNK = 64
EPS = 1e-6
NEG_INF = -1e30
GROUP_W = 256
HEAD_DIM = 64
POOL_WINDOWS = (2, 4, 8, 16)
POOL_HALO = 16
CA_HEADS = 4
CA_LEFT_CHUNKS = 8
CA_MAX_REL = 256
SA_HEADS = 4
IDX_HEADS = 8
IDX_DIM = 64
TOPK_MAX = 256
MLA_HEADS = 4
MLA_NOPE = 64
MLA_ROPE = 32
MLA_V = 64
ROPE_BASE = 10000.0
T5_BUCKETS = 32
T5_MAX_DIST = 128
D_FF = 2816
IN_WIDTHS = (256, 256, 256, 256, 256, 64, 64, 512, 64, 8, 256, 128, 32)
IN_OFFS = tuple(int(v) for v in np.cumsum((0,) + IN_WIDTHS))

LANES = 128
VMEM_LIMIT = 56 * 1024 * 1024

TM_PROJ = 512
TM_FFN = 512
TP_POOL = 512
TQ_CA = 256
CA_WIN = TQ_CA + CA_LEFT_CHUNKS * CHUNK
CA_NBLK = CA_WIN // TQ_CA
IWT_ROWS = 16
TQ_SA = 256
KB_SA = 256
COUNT_CHAINS = 2
SEARCH_FIRST_ROUND = 16
SEARCH_ROUND = 4
GUESS_SPREAD = 0.3
TQ_MLA = 256
MLA_HEADS_PER_PASS = 4
FF_CHUNK = 256

C_POOL = 0
C_CA = C_POOL + 256
C_SAQ = C_CA + 2 * GROUP_W
C_SAKV = C_SAQ + SA_HEADS * LANES
C_IQ = C_SAKV + LANES
C_IK = C_IQ + IDX_HEADS * IDX_DIM
C_CQ = C_IK + 2 * LANES
C_CKV = C_CQ + 256
C_KRF = C_CKV + LANES
C_KRS = C_KRF + LANES
C_END = C_KRS + LANES

INT_MIN = -2 ** 31
KEY_ALL = INT_MIN - int(np.array(-np.inf, np.float32).view(np.int32)) + 1


def _cparams(n_axes):
    return pltpu.CompilerParams(dimension_semantics=("arbitrary",) * n_axes,
                                vmem_limit_bytes=VMEM_LIMIT)


def _rms(x, g):
    return x * lax.rsqrt(jnp.mean(x * x, axis=-1, keepdims=True) + EPS) * g


def _dot(a, b):
    return jnp.dot(a, b, preferred_element_type=F32)


def _dot_t(a, b):
    return lax.dot_general(a, b, (((1,), (1,)), ((), ())), preferred_element_type=F32)


def _mod_kernel(c_ref, w_ref, b_ref, o_ref):
    c = c_ref[...]
    act = c * jax.nn.sigmoid(c)
    o_ref[0] = jnp.dot(act, w_ref[0], precision=lax.Precision.HIGHEST,
                       preferred_element_type=F32) + b_ref[0]


def _modulation(c, w_mod, b_mod):
    L, D, W = w_mod.shape
    B = c.shape[0]
    nj = W // D
    return pl.pallas_call(
        _mod_kernel,
        grid=(L, nj),
        in_specs=[pl.BlockSpec((B, D), lambda l, j: (0, 0)),
                  pl.BlockSpec((1, D, D), lambda l, j: (l, 0, j)),
                  pl.BlockSpec((1, 1, D), lambda l, j: (l, 0, j))],
        out_specs=pl.BlockSpec((1, B, D), lambda l, j: (l, 0, j)),
        out_shape=jax.ShapeDtypeStruct((L, B, W), F32),
        compiler_params=_cparams(2),
        name="modulation",
    )(c, w_mod, b_mod.reshape(L, 1, W))


def _inproj_kernel(x_ref, mod_ref, gmix_ref, w_ref, wt_ref, gcq_ref, gckv_ref, wq_ref, wqs_ref, wk_ref, wvt_ref,
                   cosq_ref, sinq_ref, cosk_ref, sink_ref,
                   pool_o, ca_o, saq_o, sakv_o, iq_o, ik_o, iwt_o, svt_o, cavt_o, mq_o, mk_o, mvt_o):
    sh1 = mod_ref[0, 0:1, :]
    sc1 = mod_ref[0, 1:2, :]
    h = (_rms(x_ref[...], gmix_ref[...]) * (1.0 + sc1) + sh1).astype(BF16)

    def seg(a, b):
        return _dot(h, w_ref[:, a:b])

    pool_o[...] = seg(C_POOL, C_CA)
    ca_o[...] = seg(C_CA, C_SAQ).astype(BF16)
    saq_o[...] = seg(C_SAQ, C_SAKV).astype(BF16)
    sakv_o[...] = seg(C_SAKV, C_IQ).astype(BF16)
    iq_o[...] = seg(C_IQ, C_IK).astype(BF16)
    ik_o[...] = seg(C_IK, C_CQ).astype(BF16)
    tr = _dot_t(wt_ref[...], h)
    iwt_o[...] = tr[0:IWT_ROWS] * ((IDX_HEADS ** -0.5) * (IDX_DIM ** -0.5))
    svt_o[...] = tr[IWT_ROWS:IWT_ROWS + HEAD_DIM].astype(BF16)
    cavt_o[...] = tr[IWT_ROWS + HEAD_DIM:].astype(BF16)

    qn = _rms(seg(C_CQ, C_CKV), gcq_ref[...]).astype(BF16)
    qf = _dot(qn, wq_ref[...])
    qs = _dot(qn, wqs_ref[...])
    cosq = jnp.concatenate([cosq_ref[...]] * MLA_HEADS, axis=1)
    sinq = jnp.concatenate([sinq_ref[...]] * MLA_HEADS, axis=1)
    mq_o[...] = (qf * cosq + qs * sinq).astype(BF16)

    kvn = _rms(seg(C_CKV, C_KRF), gckv_ref[...]).astype(BF16)
    kvf = _dot(kvn, wk_ref[...])
    krope = seg(C_KRF, C_KRS) * cosk_ref[...] + seg(C_KRS, C_END) * sink_ref[...]
    for hd in range(MLA_HEADS):
        mk_o[:, hd * LANES:(hd + 1) * LANES] = (kvf[:, hd * LANES:(hd + 1) * LANES] + krope).astype(BF16)
    mvt_o[...] = _dot_t(wvt_ref[...], kvn).astype(BF16)


def _inproj(x2, mod, gmix, w1, wt, gcq, gckv, wq, wqs, wk, wvt, rope_tabs, S):
    N, D = x2.shape
    TM = TM_PROJ
    nt = S // TM
    cosq, sinq, cosk, sink = rope_tabs

    def full(a):
        return pl.BlockSpec(a.shape, lambda i: (0,) * a.ndim)

    def tok(w):
        return pl.BlockSpec((TM, w), lambda i: (i, 0))

    tab = pl.BlockSpec((TM, LANES), lambda i: (i % nt, 0))
    def tokt(rows):
        return pl.BlockSpec((rows, TM), lambda i: (0, i))

    outs = [(C_CA - C_POOL, F32, True), (C_SAQ - C_CA, BF16, True), (C_SAKV - C_SAQ, BF16, True),
            (C_IQ - C_SAKV, BF16, True), (C_IK - C_IQ, BF16, True), (C_CQ - C_IK, BF16, True),
            (IWT_ROWS, F32, False), (HEAD_DIM, BF16, False), (GROUP_W, BF16, False),
            (MLA_HEADS * LANES, BF16, True), (MLA_HEADS * LANES, BF16, True), (GROUP_W, BF16, False)]
    return pl.pallas_call(
        _inproj_kernel,
        grid=(N // TM,),
        in_specs=[tok(D),
                  pl.BlockSpec((1, 6, D), lambda i: (i // nt, 0, 0)),
                  full(gmix), full(w1), full(wt), full(gcq), full(gckv), full(wq), full(wqs), full(wk), full(wvt),
                  tab, tab, tab, tab],
        out_specs=[tok(w) if tm else tokt(w) for w, _, tm in outs],
        out_shape=[jax.ShapeDtypeStruct((N, w) if tm else (w, N), dt) for w, dt, tm in outs],
        compiler_params=_cparams(1),
        name="inproj",
    )(x2, mod, gmix, w1, wt, gcq, gckv, wq, wqs, wk, wvt, cosq, sinq, cosk, sink)


def _pool_kernel(u_ref, halo_ref, w_ref, scale_ref, g_ref, o_ref, pad_scr):
    i = pl.program_id(1)
    TP = u_ref.shape[1]
    u = u_ref[0]
    pad_scr[0:POOL_HALO, :] = jnp.where(i > 0, halo_ref[0], 0.0)
    pad_scr[POOL_HALO:, :] = u

    def shifted(j):
        return pad_scr[POOL_HALO - j:POOL_HALO - j + TP, :]

    lane = lax.broadcasted_iota(jnp.int32, (TP, GROUP_W), 1)
    w2 = u + shifted(1)
    w4 = w2 + shifted(2) + shifted(3)
    w8 = w4
    for j in range(4, 8):
        w8 = w8 + shifted(j)
    w16 = w8
    for j in range(8, 16):
        w16 = w16 + shifted(j)
    win = jnp.where(lane < 64, w2, jnp.where(lane < 128, w4, jnp.where(lane < 192, w8, w16)))
    wlen = jnp.where(lane < 64, 2, jnp.where(lane < 128, 4, jnp.where(lane < 192, 8, 16)))
    t = i * TP + lax.broadcasted_iota(jnp.int32, (TP, GROUP_W), 0)
    cnt = jnp.minimum(t + 1, wlen).astype(F32)
    d = (win / cnt - u).astype(BF16)
    y = _dot(d, w_ref[...]) * scale_ref[...]
    o_ref[0] = _rms(y, g_ref[...]).astype(BF16)


def _pool(u, wbd, scale, g):
    B, S, W = u.shape
    TP = TP_POOL
    hb = TP // POOL_HALO
    return pl.pallas_call(
        _pool_kernel,
        grid=(B, S // TP),
        in_specs=[pl.BlockSpec((1, TP, W), lambda b, i: (b, i, 0)),
                  pl.BlockSpec((1, POOL_HALO, W), lambda b, i: (b, jnp.maximum(i * hb - 1, 0), 0)),
                  pl.BlockSpec((W, W), lambda b, i: (0, 0)),
                  pl.BlockSpec((1, W), lambda b, i: (0, 0)),
                  pl.BlockSpec((1, W), lambda b, i: (0, 0))],
        out_specs=pl.BlockSpec((1, TP, W), lambda b, i: (b, i, 0)),
        out_shape=jax.ShapeDtypeStruct((B, S, W), BF16),
        scratch_shapes=[pltpu.VMEM((POOL_HALO + TP, W), F32)],
        compiler_params=_cparams(2),
        name="pool_mixer",
    )(u, u, wbd, scale, g)


def _ca_kernel(q_ref, k_ref, vt_ref, bias_ref, g_ref, o_ref):
    i = pl.program_id(1)
    TQ = TQ_CA
    lane = lax.broadcasted_iota(jnp.int32, (TQ, LANES), 1)
    starts = []
    for j in range(CA_NBLK):
        kb = i - (CA_NBLK - 1) + j
        starts.append((kb >= 0, pl.multiple_of(jnp.maximum(kb, 0) * TQ, TQ)))
    scored = []
    for hd in range(CA_HEADS):
        cols = slice((hd // 2) * LANES, (hd // 2 + 1) * LANES)
        keep = (lane < HEAD_DIM) if hd % 2 == 0 else (lane >= HEAD_DIM)
        qh = jnp.where(keep, q_ref[0, :, cols].astype(F32), 0.0).astype(BF16)
        parts = []
        for j, (present, start) in enumerate(starts):
            s = _dot_t(k_ref[0, pl.ds(start, TQ), cols], qh) + bias_ref[hd, j * TQ:(j + 1) * TQ, :]
            parts.append(jnp.where(present, s, NEG_INF))
        m = parts[0].max(axis=0, keepdims=True)
        for s in parts[1:]:
            m = jnp.maximum(m, s.max(axis=0, keepdims=True))
        scored.append((parts, m))
    outs = []
    for hd, (parts, m) in enumerate(scored):
        l = jnp.zeros((1, TQ), F32)
        acc = jnp.zeros((HEAD_DIM, TQ), F32)
        for j, (_, start) in enumerate(starts):
            p = jnp.exp(parts[j] - m)
            l = l + p.sum(axis=0, keepdims=True)
            acc = acc + _dot(vt_ref[hd * HEAD_DIM:(hd + 1) * HEAD_DIM, pl.ds(start, TQ)], p.astype(BF16))
        outs.append(acc / l)
    o_ref[0] = _group_norm_t(jnp.concatenate(outs, axis=0), g_ref[...])


def _chunk_attention(caqk, cavt, bias, g):
    B, S, _ = caqk.shape
    W = GROUP_W
    TQ = TQ_CA
    return pl.pallas_call(
        _ca_kernel,
        grid=(B, S // TQ),
        in_specs=[pl.BlockSpec((1, TQ, W), lambda b, i: (b, i, 0)),
                  pl.BlockSpec((1, S, W), lambda b, i: (b, 0, 1)),
                  pl.BlockSpec((W, S), lambda b, i: (0, b)),
                  pl.BlockSpec(bias.shape, lambda b, i: (0, 0, 0)),
                  pl.BlockSpec((1, W), lambda b, i: (0, 0))],
        out_specs=pl.BlockSpec((1, TQ, W), lambda b, i: (b, i, 0)),
        out_shape=jax.ShapeDtypeStruct((B, S, W), BF16),
        compiler_params=_cparams(2),
        name="band_attention",
    )(caqk, caqk, cavt, bias, g)


def _score_key(score):
    b = lax.bitcast_convert_type(score, jnp.int32)
    return jnp.where(b < 0, jnp.int32(INT_MIN) - b, b)


def _sa_kernel(q_ref, kv_ref, vt_ref, iq_ref, ik_ref, iwt_ref, zq_ref, nbias_ref, g_ref, o_ref, key_scr):
    i = pl.program_id(1)
    TQ, KB = TQ_SA, KB_SA
    K = float(TOPK_MAX)
    nb = i + 1
    q0 = i * TQ
    cshift = CHUNK.bit_length() - 1
    kchunk = lax.broadcasted_iota(jnp.int32, (KB, TQ), 0) >> cshift
    qchunk = (q0 + lax.broadcasted_iota(jnp.int32, (1, TQ), 1)) >> cshift

    iwt = iwt_ref[...]

    def score_block(j, carry):
        smax, s1, s2 = carry
        k0 = pl.multiple_of(j * KB, KB)
        ik = ik_ref[0, pl.ds(k0, KB), :]
        ik2 = jnp.concatenate([ik[:, :LANES], ik[:, LANES:]], axis=0)
        sc = jnp.zeros((KB, TQ), F32)
        for p in range(IDX_HEADS // 2):
            logits = _dot_t(ik2, iq_ref[0, :, p * LANES:(p + 1) * LANES])
            sc = sc + iwt[2 * p:2 * p + 1, :] * jnp.maximum(logits[:KB], 0.0)
            sc = sc + iwt[2 * p + 1:2 * p + 2, :] * jnp.maximum(logits[KB:], 0.0)
        adm = kchunk <= qchunk - (k0 >> cshift)
        sc = jnp.where(adm, sc, -jnp.inf)
        key_scr[pl.ds(k0, KB), :] = _score_key(sc)
        smax = jnp.maximum(smax, sc.max(axis=0, keepdims=True))
        full = j < i
        s1 = jnp.where(full, s1 + sc.sum(axis=0, keepdims=True), s1)
        s2 = jnp.where(full, s2 + (sc * sc).sum(axis=0, keepdims=True), s2)
        return smax, s1, s2

    smax, s1, s2 = lax.fori_loop(
        0, (nb + 1) // 2, lambda j, c: score_block(2 * j + 1, score_block(2 * j, c)),
        (jnp.full((1, TQ), -jnp.inf, F32), jnp.zeros((1, TQ), F32), jnp.zeros((1, TQ), F32)))

    def count_ge(cand):
        def body(j, acc):
            blk = key_scr[pl.ds(pl.multiple_of(j * (2 * KB), 2 * KB), 2 * KB), :]
            ones = jnp.where(blk >= cand, 1.0, 0.0)
            return acc + ones.reshape(COUNT_CHAINS, -1, 8, TQ).sum(axis=1)
        acc = lax.fori_loop(0, (nb + 1) // 2, body, jnp.zeros((COUNT_CHAINS, 8, TQ), F32))
        return acc.sum(axis=0).sum(axis=0, keepdims=True)

    def search():
        def unkey(k):
            return lax.bitcast_convert_type(jnp.where(k < 0, jnp.int32(INT_MIN) - k, k), F32)

        def is_active(lo, hi, clo):
            return jnp.logical_and(clo > K, hi > lo + 1)

        def cond(st):
            _, lo, hi, clo, _ = st
            act = jnp.where(is_active(lo, hi, clo), 1.0, 0.0)
            return jnp.max(jnp.maximum(act[:, :LANES], act[:, LANES:])) > 0.0

        n_full = (i * KB).astype(F32)
        mean = s1 / n_full
        std = jnp.sqrt(jnp.maximum(s2 / n_full - mean * mean, 0.0))
        zq = jnp.max(zq_ref[...], axis=0, keepdims=True)
        guess_lo = _score_key(mean + (zq - GUESS_SPREAD) * std)
        guess_hi = _score_key(mean + (zq + GUESS_SPREAD) * std)

        def step(_, st):
            it, lo, hi, clo, chi = st
            active = is_active(lo, hi, clo)
            lf, hf = unkey(lo), unkey(hi)
            lc = jnp.log(clo)
            frac = jnp.clip((lc - math.log(K - 0.5)) / (lc - jnp.log(jnp.maximum(chi, 0.5))), 0.05, 0.95)
            cand = _score_key(lf + frac * (hf - lf))
            cand = jnp.where(it % 3 == 2, (lo >> 1) + (hi >> 1) + (lo & hi & 1), cand)
            cand = jnp.where(it == 0, guess_lo, cand)
            cand = jnp.where(it == 1, guess_hi, cand)
            cand = jnp.where(active, jnp.clip(cand, lo + 1, hi - 1), lo)
            cnt = count_ge(cand)
            up = jnp.logical_and(active, cnt >= K)
            down = jnp.logical_and(active, cnt < K)
            return (it + 1, jnp.where(up, cand, lo), jnp.where(down, cand, hi),
                    jnp.where(up, cnt, clo), jnp.where(down, cnt, chi))

        lo0 = jnp.full((1, TQ), KEY_ALL - 1, jnp.int32)
        hi0 = _score_key(smax) + 1
        clo0 = jnp.zeros((1, TQ), F32) + ((nb + 1) // 2 * (2 * KB)).astype(F32)
        st = (jnp.int32(0), lo0, hi0, clo0, jnp.zeros((1, TQ), F32))
        st = lax.fori_loop(0, SEARCH_FIRST_ROUND, step, st)
        st = lax.while_loop(cond, lambda s: lax.fori_loop(0, SEARCH_ROUND, step, s), st)
        return st[1], st[3]

    def no_search():
        return jnp.full((1, TQ), KEY_ALL, jnp.int32), jnp.full((1, TQ), K, F32)

    t, cnt_t = lax.cond(i > 0, search, no_search)
    t = jnp.maximum(t, KEY_ALL)

    @pl.when(jnp.max(cnt_t) > K)
    def _():
        allowed = K - count_ge(t + 1)
        r = lax.broadcasted_iota(jnp.int32, (KB, KB), 0)
        c = lax.broadcasted_iota(jnp.int32, (KB, KB), 1)
        earlier = jnp.where(c < r, 1.0, 0.0).astype(BF16)

        def body(j, seen):
            sl = pl.ds(pl.multiple_of(j * KB, KB), KB)
            blk = key_scr[sl, :]
            eq = jnp.where(blk == t, 1.0, 0.0)
            rank = _dot(earlier, eq.astype(BF16)) + seen
            demote = eq * jnp.where(rank >= allowed, 1.0, 0.0)
            key_scr[sl, :] = jnp.where(demote > 0.5, t - 1, blk)
            return seen + eq.sum(axis=0, keepdims=True)

        lax.fori_loop(0, nb, body, jnp.zeros((1, TQ), F32))

    def scores(j, bias_rows, present):
        k0 = pl.multiple_of(j * KB, KB)
        kblk = kv_ref[0, pl.ds(k0, KB), :]
        sel = key_scr[pl.ds(k0, KB), :] >= jnp.where(present, t, jnp.int32(2 ** 31 - 1))
        out = []
        for hd in range(SA_HEADS):
            s = _dot_t(kblk, q_ref[0, :, hd * LANES:(hd + 1) * LANES])
            if bias_rows is not None:
                s = s + nbias_ref[hd, bias_rows, :]
            s = jnp.where(sel, s, NEG_INF)
            out.append((s, s.max(axis=0, keepdims=True)))
        return tuple(out)

    def softmax_pv(j, s_all, st):
        vt = vt_ref[:, pl.ds(pl.multiple_of(j * KB, KB), KB)]
        out = []
        for hd in range(SA_HEADS):
            m, l, acc = st[hd]
            s, smax = s_all[hd]
            mn = jnp.maximum(m, smax)
            p = jnp.exp(s - mn)
            alpha = jnp.exp(m - mn)
            out.append((mn, alpha * l + p.sum(axis=0, keepdims=True), alpha * acc + _dot(vt, p.astype(BF16))))
        return tuple(out)

    def body(j, carry):
        s_cur, st = carry
        s_next = scores(j + 1, None, True)
        return s_next, softmax_pv(j, s_cur, st)

    st = tuple((jnp.full((1, TQ), NEG_INF, F32), jnp.zeros((1, TQ), F32), jnp.zeros((HEAD_DIM, TQ), F32))
               for _ in range(SA_HEADS))
    last_far = jnp.maximum(i - 2, 0)
    left = jnp.maximum(i - 1, 0)
    s_far, st = lax.fori_loop(0, last_far, body, (scores(0, None, i >= 2), st))
    s_left = scores(left, slice(0, KB), i >= 1)
    st = softmax_pv(last_far, s_far, st)
    s_diag = scores(i, slice(KB, 2 * KB), True)
    st = softmax_pv(left, s_left, st)
    st = softmax_pv(i, s_diag, st)
    y_t = jnp.concatenate([acc / l for _, l, acc in st], axis=0)
    o_ref[0] = _group_norm_t(y_t, g_ref[...])


def _sparse_attention(saq, sakv, svt, iq, ik, iwt, nbias, g):
    B, S, _ = saq.shape
    TQ = TQ_SA
    W = GROUP_W
    nt = S // TQ
    n_adm = (np.arange(S) // CHUNK + 1) * CHUNK
    zq = np.array([NormalDist().inv_cdf(1.0 - TOPK_MAX / n) if n > TOPK_MAX else 0.0 for n in n_adm], np.float32)
    zq = jnp.asarray(np.tile(zq[None, :], (8, 1)))
    return pl.pallas_call(
        _sa_kernel,
        grid=(B, nt),
        in_specs=[pl.BlockSpec((1, TQ, saq.shape[2]), lambda b, i: (b, i, 0)),
                  pl.BlockSpec((1, S, sakv.shape[2]), lambda b, i: (b, 0, 0)),
                  pl.BlockSpec((HEAD_DIM, S), lambda b, i: (0, b)),
                  pl.BlockSpec((1, TQ, iq.shape[2]), lambda b, i: (b, i, 0)),
                  pl.BlockSpec((1, S, ik.shape[2]), lambda b, i: (b, 0, 0)),
                  pl.BlockSpec((IWT_ROWS, TQ), lambda b, i: (0, b * nt + i)),
                  pl.BlockSpec((8, TQ), lambda b, i: (0, i)),
                  pl.BlockSpec(nbias.shape, lambda b, i: (0, 0, 0)),
                  pl.BlockSpec((1, W), lambda b, i: (0, 0))],
        out_specs=pl.BlockSpec((1, TQ, W), lambda b, i: (b, i, 0)),
        out_shape=jax.ShapeDtypeStruct((B, S, W), BF16),
        scratch_shapes=[pltpu.VMEM((S, TQ), jnp.int32)],
        compiler_params=_cparams(2),
        name="sparse_attention",
    )(saq, sakv, svt, iq, ik, iwt, zq, nbias, g)


def _group_norm_t(y_t, g):
    inv = lax.rsqrt(jnp.mean(y_t * y_t, axis=0, keepdims=True) + EPS)
    return ((y_t * inv).T * g).astype(BF16)


def _mla_kernel(q_ref, k_ref, vt_ref, g_ref, o_ref):
    i = pl.program_id(1)
    TQ = TQ_MLA
    cshift = CHUNK.bit_length() - 1
    kch = lax.broadcasted_iota(jnp.int32, (TQ, TQ), 0) >> cshift
    qch = lax.broadcasted_iota(jnp.int32, (TQ, TQ), 1) >> cshift

    def scores(j, heads, keep):
        k0 = pl.multiple_of(j * TQ, TQ)
        out = []
        for hd in heads:
            cols = slice(hd * LANES, (hd + 1) * LANES)
            s = _dot_t(k_ref[0, pl.ds(k0, TQ), cols], q_ref[0, :, cols])
            if keep is not None:
                s = jnp.where(keep, s, NEG_INF)
            out.append((s, s.max(axis=0, keepdims=True)))
        return tuple(out)

    def softmax_pv(j, heads, s_all, st):
        k0 = pl.multiple_of(j * TQ, TQ)
        out = []
        for n, hd in enumerate(heads):
            m, l, acc = st[n]
            s, smax = s_all[n]
            mn = jnp.maximum(m, smax)
            p = jnp.exp(s - mn)
            alpha = jnp.exp(m - mn)
            vt = vt_ref[hd * MLA_V:(hd + 1) * MLA_V, pl.ds(k0, TQ)]
            out.append((mn, alpha * l + p.sum(axis=0, keepdims=True),
                        alpha * acc + _dot(vt, p.astype(BF16))))
        return tuple(out)

    outs = []
    for g0 in range(0, MLA_HEADS, MLA_HEADS_PER_PASS):
        heads = tuple(range(g0, g0 + MLA_HEADS_PER_PASS))

        def body(j, carry, heads=heads):
            s_cur, st = carry
            s_next = scores(j + 1, heads, None)
            return s_next, softmax_pv(j, heads, s_cur, st)

        st = tuple((jnp.full((1, TQ), NEG_INF, F32), jnp.zeros((1, TQ), F32), jnp.zeros((MLA_V, TQ), F32))
                   for _ in heads)
        left = jnp.maximum(i - 1, 0)
        first_keep = kch >= jnp.where(i >= 1, 0, TQ)
        s_left, st = lax.fori_loop(0, left, body, (scores(0, heads, first_keep), st))
        s_diag = scores(i, heads, kch <= qch)
        st = softmax_pv(left, heads, s_left, st)
        outs += [acc / l for _, l, acc in softmax_pv(i, heads, s_diag, st)]
    o_ref[0] = _group_norm_t(jnp.concatenate(outs, axis=0), g_ref[...])


def _latent_attention(mq, mk, mvt, g):
    B, S, _ = mq.shape
    TQ = TQ_MLA
    W = GROUP_W
    return pl.pallas_call(
        _mla_kernel,
        grid=(B, S // TQ),
        in_specs=[pl.BlockSpec((1, TQ, mq.shape[2]), lambda b, i: (b, i, 0)),
                  pl.BlockSpec((1, S, mk.shape[2]), lambda b, i: (b, 0, 0)),
                  pl.BlockSpec((W, S), lambda b, i: (0, b)),
                  pl.BlockSpec((1, W), lambda b, i: (0, 0))],
        out_specs=pl.BlockSpec((1, TQ, W), lambda b, i: (b, i, 0)),
        out_shape=jax.ShapeDtypeStruct((B, S, W), BF16),
        compiler_params=_cparams(2),
        name="latent_attention",
    )(mq, mk, mvt, g)


def _ffn_kernel(ya_ref, yb_ref, yc_ref, yd_ref, x_ref, mod_ref, wout_ref, gffn_ref, w1_ref, w3_ref, w2_ref,
                gfin_ref, o_ref, acc_scr, *, final):
    gt1 = mod_ref[0, 2:3, :]
    sh2 = mod_ref[0, 3:4, :]
    sc2 = mod_ref[0, 4:5, :]
    gt2 = mod_ref[0, 5:6, :]
    attn = _dot(ya_ref[...], wout_ref[0:GROUP_W, :])
    for gi, y_ref in enumerate((yb_ref, yc_ref, yd_ref), start=1):
        attn = attn + _dot(y_ref[...], wout_ref[gi * GROUP_W:(gi + 1) * GROUP_W, :])
    x1 = x_ref[...] + gt1 * attn
    h = (_rms(x1, gffn_ref[...]) * (1.0 + sc2) + sh2).astype(BF16)
    for ci in range(D_FF // FF_CHUNK):
        cols = slice(ci * FF_CHUNK, (ci + 1) * FF_CHUNK)
        a = _dot(h, w1_ref[:, cols])
        gate = (a * jax.nn.sigmoid(a) * _dot(h, w3_ref[:, cols])).astype(BF16)
        part = _dot(gate, w2_ref[cols, :])
        if ci == 0:
            acc_scr[...] = part
        else:
            acc_scr[...] += part
    x2 = x1 + gt2 * acc_scr[...]
    o_ref[...] = _rms(x2, gfin_ref[...]) if final else x2


def _out_ffn(ys, x2, mod, wout, gffn, w1, w3, w2, gfin, S, final):
    N, D = x2.shape
    TM = TM_FFN
    nt = S // TM

    def full(a):
        return pl.BlockSpec(a.shape, lambda i: (0,) * a.ndim, pipeline_mode=pl.Buffered(1))

    def tok(w):
        return pl.BlockSpec((TM, w), lambda i: (i, 0))

    return pl.pallas_call(
        functools.partial(_ffn_kernel, final=final),
        grid=(N // TM,),
        in_specs=[tok(GROUP_W)] * 4 + [tok(D), pl.BlockSpec((1, 6, D), lambda i: (i // nt, 0, 0)),
                                       full(wout), full(gffn), full(w1), full(w3), full(w2), full(gfin)],
        out_specs=tok(D),
        out_shape=jax.ShapeDtypeStruct((N, D), F32),
        scratch_shapes=[pltpu.VMEM((TM, D), F32)],
        compiler_params=_cparams(1),
        name="out_ffn_final" if final else "out_ffn",
    )(*ys, x2, mod, wout, gffn, w1, w3, w2, gfin)


def _t5_bucket(rel):
    nb = T5_BUCKETS // 2
    max_exact = nb // 2
    ret = jnp.where(rel > 0, nb, 0)
    n = jnp.abs(rel)
    nf = jnp.maximum(n, 1).astype(jnp.float32)
    large = max_exact + (jnp.log(nf / max_exact) / math.log(T5_MAX_DIST / max_exact)
                         * (nb - max_exact)).astype(jnp.int32)
    large = jnp.minimum(large, nb - 1)
    return ret + jnp.where(n < max_exact, n, large)


def _rope_tables(S):
    half = MLA_ROPE // 2
    freqs = ROPE_BASE ** (-jnp.arange(half, dtype=F32) / half)
    ang = jnp.arange(S, dtype=jnp.int32).astype(F32)[:, None] * freqs[None, :]
    cos, sin = jnp.cos(ang), jnp.sin(ang)
    cos2 = jnp.concatenate([cos, cos], axis=1)
    sin2 = jnp.concatenate([-sin, sin], axis=1)
    zeros = jnp.zeros((S, LANES - MLA_NOPE - MLA_ROPE), F32)
    scale = (MLA_NOPE + MLA_ROPE) ** -0.5
    cosq = jnp.concatenate([jnp.full((S, MLA_NOPE), scale, F32), cos2 * scale, zeros], axis=1)
    sinq = jnp.concatenate([jnp.zeros((S, MLA_NOPE), F32), sin2 * scale, zeros], axis=1)
    cosk = jnp.concatenate([jnp.zeros((S, MLA_NOPE), F32), cos2, zeros], axis=1)
    sink = jnp.concatenate([jnp.zeros((S, MLA_NOPE), F32), sin2, zeros], axis=1)
    return cosq, sinq, cosk, sink


def _pack_in_weight(w):
    part = {n: w[:, IN_OFFS[k]:IN_OFFS[k + 1]] for k, n in enumerate(
        ('pool_u', 'ca_q', 'ca_k', 'ca_v', 'sa_q', 'sa_k', 'sa_v', 'idx_q', 'idx_k', 'idx_w',
         'mla_cq', 'mla_ckv', 'mla_kr'))}
    D = w.shape[0]
    z = lambda n: jnp.zeros((D, n), F32)
    qscale = HEAD_DIM ** -0.5
    saq = part['sa_q'].reshape(D, SA_HEADS, HEAD_DIM) * qscale
    saq = jnp.concatenate([saq, jnp.zeros_like(saq)], axis=2).reshape(D, SA_HEADS * LANES)
    kr = part['mla_kr']
    kr_swap = jnp.concatenate([kr[:, MLA_ROPE // 2:], kr[:, :MLA_ROPE // 2]], axis=1)
    pad_r = LANES - MLA_NOPE - MLA_ROPE
    cols = [part['pool_u'], part['ca_q'] * qscale, part['ca_k'], saq,
            part['sa_k'], part['sa_v'], part['idx_q'],
            part['idx_k'], z(IDX_DIM), z(IDX_DIM), part['idx_k'],
            part['mla_cq'], part['mla_ckv'],
            z(MLA_NOPE), kr, z(pad_r), z(MLA_NOPE), kr_swap, z(pad_r)]
    out = jnp.concatenate(cols, axis=1)
    assert out.shape[1] == C_END
    wt = jnp.concatenate([part['idx_w'].T, jnp.zeros((IWT_ROWS - IDX_HEADS, D), F32), part['sa_v'].T,
                          part['ca_v'].T], axis=0)
    return out.astype(BF16), wt.astype(BF16)


def _pack_mla_weights(w_uq, w_ukv):
    R = w_uq.shape[0]
    pad = jnp.zeros((R, MLA_HEADS, LANES - MLA_NOPE - MLA_ROPE), F32)
    rope_w = w_uq[:, :, MLA_NOPE:]
    rope_sw = jnp.concatenate([rope_w[:, :, MLA_ROPE // 2:], rope_w[:, :, :MLA_ROPE // 2]], axis=2)
    wq = jnp.concatenate([w_uq, pad], axis=2).reshape(R, MLA_HEADS * LANES)
    wqs = jnp.concatenate([jnp.zeros((R, MLA_HEADS, MLA_NOPE), F32), rope_sw, pad],
                          axis=2).reshape(R, MLA_HEADS * LANES)
    Rk = w_ukv.shape[0]
    wk = jnp.concatenate([w_ukv[:, :, :MLA_NOPE], jnp.zeros((Rk, MLA_HEADS, LANES - MLA_NOPE), F32)],
                         axis=2).reshape(Rk, MLA_HEADS * LANES)
    wvt = w_ukv[:, :, MLA_NOPE:].reshape(Rk, MLA_HEADS * MLA_V).T
    return wq.astype(BF16), wqs.astype(BF16), wk.astype(BF16), wvt.astype(BF16)


def _toeplitz(vec, rows, cols):
    L = vec.shape[-1]
    assert cols <= L - 1
    flat = jnp.tile(vec, (1, rows))[:, :rows * (L - 1)]
    return flat.reshape(vec.shape[0], rows, L - 1)[:, :, :cols]


def _signed_mod_range(L, hi):
    d = np.arange(L)
    return np.where(d <= hi, d, d - L)


def _band_bias(rel_table):
    L = CA_WIN + TQ_CA
    e = _signed_mod_range(L, TQ_CA - 1)
    ridx = np.clip(CA_LEFT_CHUNKS * CHUNK + e, -(CHUNK - 1), CA_MAX_REL) + (CHUNK - 1)
    bias = _toeplitz(rel_table[:, ridx].astype(F32), CA_WIN, TQ_CA)
    kc = np.arange(CA_WIN)[:, None] // CHUNK
    qc = np.arange(TQ_CA)[None, :] // CHUNK + CA_LEFT_CHUNKS
    valid = (kc <= qc) & (kc >= qc - CA_LEFT_CHUNKS)
    return jnp.where(valid[None], bias, NEG_INF)


def _t5_bias(t5_table):
    TQ = TQ_SA
    L = 3 * TQ
    e = _signed_mod_range(L, TQ - 1)
    rel = jnp.asarray(-e - TQ, jnp.int32)
    far = t5_table[_t5_bucket(jnp.int32(-(TQ + 1)))].astype(F32)
    vec = (t5_table[_t5_bucket(rel)].astype(F32) - far[None, :]).T
    return _toeplitz(vec, 2 * TQ, TQ)


def kernel(x, c, t5_table, w_mod, b_mod, g_mix, w_in, pool_w, pool_scale, ca_rel, mla_g_cq, mla_g_ckv,
           mla_w_uq, mla_w_ukv, g_group, w_out, g_ffn, ffn_w1, ffn_w3, ffn_w2, g_final):
    B, S, D = x.shape
    assert D == D_MODEL and S % TM_PROJ == 0 and S % TQ_SA == 0 and S >= 4 * TOPK_MAX
    N = B * S
    mod_all = _modulation(c, w_mod, b_mod)
    rope_tabs = _rope_tables(S)
    nbias = _t5_bias(t5_table)
    row = lambda v: v.reshape(1, -1).astype(F32)
    x2 = x.reshape(N, D)
    for l in range(DEPTH):
        mod = mod_all[l].reshape(B, 6, D)
        w1, wt = _pack_in_weight(w_in[l])
        wq, wqs, wk, wvt = _pack_mla_weights(mla_w_uq[l], mla_w_ukv[l])
        (pool_u, ca, saq, sakv, iq, ik, iwt, svt, cavt, mq, mk, mvt) = _inproj(
            x2, mod, row(g_mix[l]), w1, wt, row(mla_g_cq[l]), row(mla_g_ckv[l]), wq, wqs, wk, wvt, rope_tabs, S)
        gg = g_group[l].reshape(4, 1, GROUP_W).astype(F32)
        wbd = jax.scipy.linalg.block_diag(*[pool_w[l, gi] for gi in range(len(POOL_WINDOWS))]).astype(BF16)
        bsw = lambda a: a.reshape(B, S, a.shape[-1])
        y_a = _pool(bsw(pool_u), wbd, row(pool_scale[l]), gg[0])
        y_b = _chunk_attention(bsw(ca), cavt, _band_bias(ca_rel[l]), gg[1])
        y_c = _sparse_attention(bsw(saq), bsw(sakv), svt, bsw(iq), bsw(ik), iwt, nbias, gg[2])
        y_d = _latent_attention(bsw(mq), bsw(mk), mvt, gg[3])
        ys = [y.reshape(N, GROUP_W) for y in (y_a, y_b, y_c, y_d)]
        x2 = _out_ffn(ys, x2, mod, w_out[l].astype(BF16), row(g_ffn[l]), ffn_w1[l].astype(BF16),
                      ffn_w3[l].astype(BF16), ffn_w2[l].astype(BF16), row(g_final), S,
                      final=(l == DEPTH - 1))
    return x2.reshape(B, S, D)
```

```python
import functools
import math
from statistics import NormalDist

import jax
import jax.numpy as jnp
from jax import lax
import numpy as np
from jax.experimental import pallas as pl
from jax.experimental.pallas import tpu as pltpu

F32 = jnp.float32
BF16 = jnp.bfloat16

D_MODEL = 1024
DEPTH = 2
CHUNK = 64
EPS = 1e-6
NEG_INF = -1e30
GROUP_W = 256
HEAD_DIM = 64
POOL_WINDOWS = (2, 4, 8, 16)
POOL_HALO = 16
CA_HEADS = 4
CA_LEFT_CHUNKS = 8
CA_MAX_REL = 256
SA_HEADS = 4
IDX_HEADS = 8
IDX_DIM = 64
TOPK_MAX = 256
MLA_HEADS = 4
MLA_NOPE = 64
MLA_ROPE = 32
MLA_V = 64
ROPE_BASE = 10000.0
T5_BUCKETS = 32
T5_MAX_DIST = 128
D_FF = 2816
IN_WIDTHS = (256, 256, 256, 256, 256, 64, 64, 512, 64, 8, 256, 128, 32)
IN_OFFS = tuple(int(v) for v in np.cumsum((0,) + IN_WIDTHS))

LANES = 128
VMEM_LIMIT = 56 * 1024 * 1024

TM_PROJ = 512
TM_FFN = 512
TP_POOL = 512
TQ_CA = 256
CA_WIN = TQ_CA + CA_LEFT_CHUNKS * CHUNK
CA_NBLK = CA_WIN // TQ_CA
IWT_ROWS = 16
TQ_SA = 256
KB_SA = 256
COUNT_CHAINS = 2
SEARCH_FIRST_ROUND = 16
SEARCH_ROUND = 4
GUESS_SPREAD = 0.3
TQ_MLA = 256
FF_CHUNK = 256

C_POOL = 0
C_CA = C_POOL + 256
C_SAQ = C_CA + 2 * GROUP_W
C_SAKV = C_SAQ + SA_HEADS * LANES
C_IQ = C_SAKV + LANES
C_IK = C_IQ + IDX_HEADS * IDX_DIM
C_CQ = C_IK + 2 * LANES
C_CKV = C_CQ + 256
C_KRF = C_CKV + LANES
C_KRS = C_KRF + LANES
C_END = C_KRS + LANES

INT_MIN = -2 ** 31
KEY_ALL = INT_MIN - int(np.array(-np.inf, np.float32).view(np.int32)) + 1


def _cparams(n_axes):
    return pltpu.CompilerParams(dimension_semantics=("arbitrary",) * n_axes,
                                vmem_limit_bytes=VMEM_LIMIT)


def _rms(x, g):
    return x * lax.rsqrt(jnp.mean(x * x, axis=-1, keepdims=True) + EPS) * g


def _dot(a, b):
    return jnp.dot(a, b, preferred_element_type=F32)


def _dot_t(a, b):
    return lax.dot_general(a, b, (((1,), (1,)), ((), ())), preferred_element_type=F32)


def _mod_kernel(c_ref, w_ref, b_ref, o_ref):
    c = c_ref[...]
    act = c * jax.nn.sigmoid(c)
    o_ref[0] = jnp.dot(act, w_ref[0], precision=lax.Precision.HIGHEST,
                       preferred_element_type=F32) + b_ref[0]


def _modulation(c, w_mod, b_mod):
    L, D, W = w_mod.shape
    B = c.shape[0]
    nj = W // D
    return pl.pallas_call(
        _mod_kernel,
        grid=(L, nj),
        in_specs=[pl.BlockSpec((B, D), lambda l, j: (0, 0)),
                  pl.BlockSpec((1, D, D), lambda l, j: (l, 0, j)),
                  pl.BlockSpec((1, 1, D), lambda l, j: (l, 0, j))],
        out_specs=pl.BlockSpec((1, B, D), lambda l, j: (l, 0, j)),
        out_shape=jax.ShapeDtypeStruct((L, B, W), F32),
        compiler_params=_cparams(2),
        name="modulation",
    )(c, w_mod, b_mod.reshape(L, 1, W))


def _inproj_kernel(x_ref, mod_ref, gmix_ref, w_ref, wt_ref, gcq_ref, gckv_ref, wq_ref, wqs_ref, wk_ref, wvt_ref,
                   cosq_ref, sinq_ref, cosk_ref, sink_ref,
                   pool_o, ca_o, saq_o, sakv_o, iq_o, ik_o, iwt_o, svt_o, cavt_o, mq_o, mk_o, mvt_o):
    sh1 = mod_ref[0, 0:1, :]
    sc1 = mod_ref[0, 1:2, :]
    h = (_rms(x_ref[...], gmix_ref[...]) * (1.0 + sc1) + sh1).astype(BF16)

    def seg(a, b):
        return _dot(h, w_ref[:, a:b])

    pool_o[...] = seg(C_POOL, C_CA)
    ca_o[...] = seg(C_CA, C_SAQ).astype(BF16)
    saq_o[...] = seg(C_SAQ, C_SAKV).astype(BF16)
    sakv_o[...] = seg(C_SAKV, C_IQ).astype(BF16)
    iq_o[...] = seg(C_IQ, C_IK).astype(BF16)
    ik_o[...] = seg(C_IK, C_CQ).astype(BF16)
    tr = _dot_t(wt_ref[...], h)
    iwt_o[...] = tr[0:IWT_ROWS] * ((IDX_HEADS ** -0.5) * (IDX_DIM ** -0.5))
    svt_o[...] = tr[IWT_ROWS:IWT_ROWS + HEAD_DIM].astype(BF16)
    cavt_o[...] = tr[IWT_ROWS + HEAD_DIM:].astype(BF16)

    qn = _rms(seg(C_CQ, C_CKV), gcq_ref[...]).astype(BF16)
    qf = _dot(qn, wq_ref[...])
    qs = _dot(qn, wqs_ref[...])
    cosq = jnp.concatenate([cosq_ref[...]] * MLA_HEADS, axis=1)
    sinq = jnp.concatenate([sinq_ref[...]] * MLA_HEADS, axis=1)
    mq_o[...] = (qf * cosq + qs * sinq).astype(BF16)

    kvn = _rms(seg(C_CKV, C_KRF), gckv_ref[...]).astype(BF16)
    kvf = _dot(kvn, wk_ref[...])
    krope = seg(C_KRF, C_KRS) * cosk_ref[...] + seg(C_KRS, C_END) * sink_ref[...]
    for hd in range(MLA_HEADS):
        mk_o[:, hd * LANES:(hd + 1) * LANES] = (kvf[:, hd * LANES:(hd + 1) * LANES] + krope).astype(BF16)
    mvt_o[...] = _dot_t(wvt_ref[...], kvn).astype(BF16)


def _inproj(x2, mod, gmix, w1, wt, gcq, gckv, wq, wqs, wk, wvt, rope_tabs, S):
    N, D = x2.shape
    TM = TM_PROJ
    nt = S // TM
    cosq, sinq, cosk, sink = rope_tabs

    def full(a):
        return pl.BlockSpec(a.shape, lambda i: (0,) * a.ndim)

    def tok(w):
        return pl.BlockSpec((TM, w), lambda i: (i, 0))

    tab = pl.BlockSpec((TM, LANES), lambda i: (i % nt, 0))
    def tokt(rows):
        return pl.BlockSpec((rows, TM), lambda i: (0, i))

    outs = [(C_CA - C_POOL, F32, True), (C_SAQ - C_CA, BF16, True), (C_SAKV - C_SAQ, BF16, True),
            (C_IQ - C_SAKV, BF16, True), (C_IK - C_IQ, BF16, True), (C_CQ - C_IK, BF16, True),
            (IWT_ROWS, F32, False), (HEAD_DIM, BF16, False), (GROUP_W, BF16, False),
            (MLA_HEADS * LANES, BF16, True), (MLA_HEADS * LANES, BF16, True), (GROUP_W, BF16, False)]
    return pl.pallas_call(
        _inproj_kernel,
        grid=(N // TM,),
        in_specs=[tok(D),
                  pl.BlockSpec((1, 6, D), lambda i: (i // nt, 0, 0)),
                  full(gmix), full(w1), full(wt), full(gcq), full(gckv), full(wq), full(wqs), full(wk), full(wvt),
                  tab, tab, tab, tab],
        out_specs=[tok(w) if tm else tokt(w) for w, _, tm in outs],
        out_shape=[jax.ShapeDtypeStruct((N, w) if tm else (w, N), dt) for w, dt, tm in outs],
        compiler_params=_cparams(1),
        name="inproj",
    )(x2, mod, gmix, w1, wt, gcq, gckv, wq, wqs, wk, wvt, cosq, sinq, cosk, sink)


def _pool_kernel(u_ref, halo_ref, w_ref, scale_ref, g_ref, o_ref, pad_scr):
    i = pl.program_id(1)
    TP = u_ref.shape[1]
    u = u_ref[0]
    pad_scr[0:POOL_HALO, :] = jnp.where(i > 0, halo_ref[0], 0.0)
    pad_scr[POOL_HALO:, :] = u

    def shifted(j):
        return pad_scr[POOL_HALO - j:POOL_HALO - j + TP, :]

    lane = lax.broadcasted_iota(jnp.int32, (TP, GROUP_W), 1)
    w2 = u + shifted(1)
    w4 = w2 + shifted(2) + shifted(3)
    w8 = w4
    for j in range(4, 8):
        w8 = w8 + shifted(j)
    w16 = w8
    for j in range(8, 16):
        w16 = w16 + shifted(j)
    win = jnp.where(lane < 64, w2, jnp.where(lane < 128, w4, jnp.where(lane < 192, w8, w16)))
    wlen = jnp.where(lane < 64, 2, jnp.where(lane < 128, 4, jnp.where(lane < 192, 8, 16)))
    t = i * TP + lax.broadcasted_iota(jnp.int32, (TP, GROUP_W), 0)
    cnt = jnp.minimum(t + 1, wlen).astype(F32)
    d = (win / cnt - u).astype(BF16)
    y = _dot(d, w_ref[...]) * scale_ref[...]
    o_ref[0] = _rms(y, g_ref[...]).astype(BF16)


def _pool(u, wbd, scale, g):
    B, S, W = u.shape
    TP = TP_POOL
    hb = TP // POOL_HALO
    return pl.pallas_call(
        _pool_kernel,
        grid=(B, S // TP),
        in_specs=[pl.BlockSpec((1, TP, W), lambda b, i: (b, i, 0)),
                  pl.BlockSpec((1, POOL_HALO, W), lambda b, i: (b, jnp.maximum(i * hb - 1, 0), 0)),
                  pl.BlockSpec((W, W), lambda b, i: (0, 0)),
                  pl.BlockSpec((1, W), lambda b, i: (0, 0)),
                  pl.BlockSpec((1, W), lambda b, i: (0, 0))],
        out_specs=pl.BlockSpec((1, TP, W), lambda b, i: (b, i, 0)),
        out_shape=jax.ShapeDtypeStruct((B, S, W), BF16),
        scratch_shapes=[pltpu.VMEM((POOL_HALO + TP, W), F32)],
        compiler_params=_cparams(2),
        name="pool_mixer",
    )(u, u, wbd, scale, g)


def _ca_kernel(q_ref, k_ref, vt_ref, bias_ref, g_ref, o_ref):
    i = pl.program_id(1)
    TQ = TQ_CA
    lane = lax.broadcasted_iota(jnp.int32, (TQ, LANES), 1)
    starts = []
    for j in range(CA_NBLK):
        kb = i - (CA_NBLK - 1) + j
        starts.append((kb >= 0, pl.multiple_of(jnp.maximum(kb, 0) * TQ, TQ)))
    scored = []
    for hd in range(CA_HEADS):
        cols = slice((hd // 2) * LANES, (hd // 2 + 1) * LANES)
        keep = (lane < HEAD_DIM) if hd % 2 == 0 else (lane >= HEAD_DIM)
        qh = jnp.where(keep, q_ref[0, :, cols].astype(F32), 0.0).astype(BF16)
        parts = []
        for j, (present, start) in enumerate(starts):
            s = _dot_t(k_ref[0, pl.ds(start, TQ), cols], qh) + bias_ref[hd, j * TQ:(j + 1) * TQ, :]
            parts.append(jnp.where(present, s, NEG_INF))
        m = parts[0].max(axis=0, keepdims=True)
        for s in parts[1:]:
            m = jnp.maximum(m, s.max(axis=0, keepdims=True))
        scored.append((parts, m))
    outs = []
    for hd, (parts, m) in enumerate(scored):
        l = jnp.zeros((1, TQ), F32)
        acc = jnp.zeros((HEAD_DIM, TQ), F32)
        for j, (_, start) in enumerate(starts):
            p = jnp.exp(parts[j] - m)
            l = l + p.sum(axis=0, keepdims=True)
            acc = acc + _dot(vt_ref[hd * HEAD_DIM:(hd + 1) * HEAD_DIM, pl.ds(start, TQ)], p.astype(BF16))
        outs.append(acc / l)
    o_ref[0] = _group_norm_t(jnp.concatenate(outs, axis=0), g_ref[...])


def _chunk_attention(caqk, cavt, bias, g):
    B, S, _ = caqk.shape
    W = GROUP_W
    TQ = TQ_CA
    return pl.pallas_call(
        _ca_kernel,
        grid=(B, S // TQ),
        in_specs=[pl.BlockSpec((1, TQ, W), lambda b, i: (b, i, 0)),
                  pl.BlockSpec((1, S, W), lambda b, i: (b, 0, 1)),
                  pl.BlockSpec((W, S), lambda b, i: (0, b)),
                  pl.BlockSpec(bias.shape, lambda b, i: (0, 0, 0)),
                  pl.BlockSpec((1, W), lambda b, i: (0, 0))],
        out_specs=pl.BlockSpec((1, TQ, W), lambda b, i: (b, i, 0)),
        out_shape=jax.ShapeDtypeStruct((B, S, W), BF16),
        compiler_params=_cparams(2),
        name="band_attention",
    )(caqk, caqk, cavt, bias, g)


def _score_key(score):
    b = lax.bitcast_convert_type(score, jnp.int32)
    return jnp.where(b < 0, jnp.int32(INT_MIN) - b, b)


def _sa_kernel(q_ref, kv_ref, vt_ref, iq_ref, ik_ref, iwt_ref, zq_ref, nbias_ref, g_ref, o_ref,
               key_scr, s_scr, smax_scr):
    i = pl.program_id(1)
    TQ, KB = TQ_SA, KB_SA
    K = float(TOPK_MAX)
    nb = i + 1
    q0 = i * TQ
    cshift = CHUNK.bit_length() - 1
    kchunk = lax.broadcasted_iota(jnp.int32, (KB, TQ), 0) >> cshift
    qchunk = (q0 + lax.broadcasted_iota(jnp.int32, (1, TQ), 1)) >> cshift

    iwt = iwt_ref[...]

    def score_block(j, carry):
        smax, s1, s2 = carry
        k0 = pl.multiple_of(j * KB, KB)
        ik = ik_ref[0, pl.ds(k0, KB), :]
        ik2 = jnp.concatenate([ik[:, :LANES], ik[:, LANES:]], axis=0)
        sc = jnp.zeros((KB, TQ), F32)
        for p in range(IDX_HEADS // 2):
            logits = _dot_t(ik2, iq_ref[0, :, p * LANES:(p + 1) * LANES])
            sc = sc + iwt[2 * p:2 * p + 1, :] * jnp.maximum(logits[:KB], 0.0)
            sc = sc + iwt[2 * p + 1:2 * p + 2, :] * jnp.maximum(logits[KB:], 0.0)
        adm = kchunk <= qchunk - (k0 >> cshift)
        sc = jnp.where(adm, sc, -jnp.inf)
        key_scr[pl.ds(k0, KB), :] = _score_key(sc)
        smax = jnp.maximum(smax, sc.max(axis=0, keepdims=True))
        full = j < i
        s1 = jnp.where(full, s1 + sc.sum(axis=0, keepdims=True), s1)
        s2 = jnp.where(full, s2 + (sc * sc).sum(axis=0, keepdims=True), s2)
        return smax, s1, s2

    smax, s1, s2 = lax.fori_loop(
        0, (nb + 1) // 2, lambda j, c: score_block(2 * j + 1, score_block(2 * j, c)),
        (jnp.full((1, TQ), -jnp.inf, F32), jnp.zeros((1, TQ), F32), jnp.zeros((1, TQ), F32)))

    def count_ge(cand):
        def body(j, acc):
            blk = key_scr[pl.ds(pl.multiple_of(j * (2 * KB), 2 * KB), 2 * KB), :]
            ones = jnp.where(blk >= cand, 1.0, 0.0)
            return acc + ones.reshape(COUNT_CHAINS, -1, 8, TQ).sum(axis=1)
        acc = lax.fori_loop(0, (nb + 1) // 2, body, jnp.zeros((COUNT_CHAINS, 8, TQ), F32))
        return acc.sum(axis=0).sum(axis=0, keepdims=True)

    def search():
        def unkey(k):
            return lax.bitcast_convert_type(jnp.where(k < 0, jnp.int32(INT_MIN) - k, k), F32)

        def is_active(lo, hi, clo):
            return jnp.logical_and(clo > K, hi > lo + 1)

        def cond(st):
            _, lo, hi, clo, _ = st
            act = jnp.where(is_active(lo, hi, clo), 1.0, 0.0)
            return jnp.max(jnp.maximum(act[:, :LANES], act[:, LANES:])) > 0.0

        n_full = (i * KB).astype(F32)
        mean = s1 / n_full
        std = jnp.sqrt(jnp.maximum(s2 / n_full - mean * mean, 0.0))
        zq = jnp.max(zq_ref[...], axis=0, keepdims=True)
        guess_lo = _score_key(mean + (zq - GUESS_SPREAD) * std)
        guess_hi = _score_key(mean + (zq + GUESS_SPREAD) * std)

        def step(_, st):
            it, lo, hi, clo, chi = st
            active = is_active(lo, hi, clo)
            lf, hf = unkey(lo), unkey(hi)
            lc = jnp.log(clo)
            frac = jnp.clip((lc - math.log(K - 0.5)) / (lc - jnp.log(jnp.maximum(chi, 0.5))), 0.05, 0.95)
            cand = _score_key(lf + frac * (hf - lf))
            cand = jnp.where(it % 3 == 2, (lo >> 1) + (hi >> 1) + (lo & hi & 1), cand)
            cand = jnp.where(it == 0, guess_lo, cand)
            cand = jnp.where(it == 1, guess_hi, cand)
            cand = jnp.where(active, jnp.clip(cand, lo + 1, hi - 1), lo)
            cnt = count_ge(cand)
            up = jnp.logical_and(active, cnt >= K)
            down = jnp.logical_and(active, cnt < K)
            return (it + 1, jnp.where(up, cand, lo), jnp.where(down, cand, hi),
                    jnp.where(up, cnt, clo), jnp.where(down, cnt, chi))

        lo0 = jnp.full((1, TQ), KEY_ALL - 1, jnp.int32)
        hi0 = _score_key(smax) + 1
        clo0 = jnp.zeros((1, TQ), F32) + ((nb + 1) // 2 * (2 * KB)).astype(F32)
        st = (jnp.int32(0), lo0, hi0, clo0, jnp.zeros((1, TQ), F32))
        st = lax.fori_loop(0, SEARCH_FIRST_ROUND, step, st)
        st = lax.while_loop(cond, lambda s: lax.fori_loop(0, SEARCH_ROUND, step, s), st)
        return st[1], st[3]

    def no_search():
        return jnp.full((1, TQ), KEY_ALL, jnp.int32), jnp.full((1, TQ), K, F32)

    t, cnt_t = lax.cond(i > 0, search, no_search)
    t = jnp.maximum(t, KEY_ALL)

    @pl.when(jnp.max(cnt_t) > K)
    def _():
        allowed = K - count_ge(t + 1)
        r = lax.broadcasted_iota(jnp.int32, (KB, KB), 0)
        c = lax.broadcasted_iota(jnp.int32, (KB, KB), 1)
        earlier = jnp.where(c < r, 1.0, 0.0).astype(BF16)

        def body(j, seen):
            sl = pl.ds(pl.multiple_of(j * KB, KB), KB)
            blk = key_scr[sl, :]
            eq = jnp.where(blk == t, 1.0, 0.0)
            rank = _dot(earlier, eq.astype(BF16)) + seen
            demote = eq * jnp.where(rank >= allowed, 1.0, 0.0)
            key_scr[sl, :] = jnp.where(demote > 0.5, t - 1, blk)
            return seen + eq.sum(axis=0, keepdims=True)

        lax.fori_loop(0, nb, body, jnp.zeros((1, TQ), F32))

    def scores_to(slot, j, bias_rows, present):
        k0 = pl.multiple_of(j * KB, KB)
        kblk = kv_ref[0, pl.ds(k0, KB), :]
        sel = key_scr[pl.ds(k0, KB), :] >= jnp.where(present, t, jnp.int32(2 ** 31 - 1))
        for hd in range(SA_HEADS):
            s = _dot_t(kblk, q_ref[0, :, hd * LANES:(hd + 1) * LANES])
            if bias_rows is not None:
                s = s + nbias_ref[hd, bias_rows, :]
            s = jnp.where(sel, s, NEG_INF)
            s_scr[slot, hd] = s
            smax_scr[slot, hd] = jnp.broadcast_to(s.max(axis=0, keepdims=True), smax_scr.shape[2:])

    def softmax_pv(slot, j, st):
        vt = vt_ref[:, pl.ds(pl.multiple_of(j * KB, KB), KB)]
        out = []
        for hd in range(SA_HEADS):
            m, l, acc = st[hd]
            mn = jnp.maximum(m, smax_scr[slot, hd][0:1])
            p = jnp.exp(s_scr[slot, hd] - mn)
            alpha = jnp.exp(m - mn)
            out.append((mn, alpha * l + p.sum(axis=0, keepdims=True), alpha * acc + _dot(vt, p.astype(BF16))))
        return tuple(out)

    def body(j, st):
        slot = j % 2
        st = softmax_pv(slot, j, st)
        scores_to(1 - slot, j + 1, None, True)
        return st

    st = tuple((jnp.full((1, TQ), NEG_INF, F32), jnp.zeros((1, TQ), F32), jnp.zeros((HEAD_DIM, TQ), F32))
               for _ in range(SA_HEADS))
    last_far = jnp.maximum(i - 2, 0)
    left = jnp.maximum(i - 1, 0)
    scores_to(0, 0, None, i >= 2)
    st = lax.fori_loop(0, last_far, body, st)
    slot = last_far % 2
    scores_to(1 - slot, left, slice(0, KB), i >= 1)
    st = softmax_pv(slot, last_far, st)
    scores_to(slot, i, slice(KB, 2 * KB), True)
    st = softmax_pv(1 - slot, left, st)
    st = softmax_pv(slot, i, st)
    y_t = jnp.concatenate([acc / l for _, l, acc in st], axis=0)
    o_ref[0] = _group_norm_t(y_t, g_ref[...])


def _sparse_attention(saq, sakv, svt, iq, ik, iwt, nbias, g):
    B, S, _ = saq.shape
    TQ = TQ_SA
    W = GROUP_W
    nt = S // TQ
    n_adm = (np.arange(S) // CHUNK + 1) * CHUNK
    zq = np.array([NormalDist().inv_cdf(1.0 - TOPK_MAX / n) if n > TOPK_MAX else 0.0 for n in n_adm], np.float32)
    zq = jnp.asarray(np.tile(zq[None, :], (8, 1)))
    return pl.pallas_call(
        _sa_kernel,
        grid=(B, nt),
        in_specs=[pl.BlockSpec((1, TQ, saq.shape[2]), lambda b, i: (b, i, 0)),
                  pl.BlockSpec((1, S, sakv.shape[2]), lambda b, i: (b, 0, 0)),
                  pl.BlockSpec((HEAD_DIM, S), lambda b, i: (0, b)),
                  pl.BlockSpec((1, TQ, iq.shape[2]), lambda b, i: (b, i, 0)),
                  pl.BlockSpec((1, S, ik.shape[2]), lambda b, i: (b, 0, 0)),
                  pl.BlockSpec((IWT_ROWS, TQ), lambda b, i: (0, b * nt + i)),
                  pl.BlockSpec((8, TQ), lambda b, i: (0, i)),
                  pl.BlockSpec(nbias.shape, lambda b, i: (0, 0, 0)),
                  pl.BlockSpec((1, W), lambda b, i: (0, 0))],
        out_specs=pl.BlockSpec((1, TQ, W), lambda b, i: (b, i, 0)),
        out_shape=jax.ShapeDtypeStruct((B, S, W), BF16),
        scratch_shapes=[pltpu.VMEM((S, TQ), jnp.int32), pltpu.VMEM((2, SA_HEADS, KB_SA, TQ), F32),
                        pltpu.VMEM((2, SA_HEADS, 8, TQ), F32)],
        compiler_params=_cparams(2),
        name="sparse_attention",
    )(saq, sakv, svt, iq, ik, iwt, zq, nbias, g)


def _group_norm_t(y_t, g):
    inv = lax.rsqrt(jnp.mean(y_t * y_t, axis=0, keepdims=True) + EPS)
    return ((y_t * inv).T * g).astype(BF16)


def _mla_kernel(q_ref, k_ref, vt_ref, g_ref, o_ref, s_scr, smax_scr):
    i = pl.program_id(1)
    TQ = TQ_MLA
    cshift = CHUNK.bit_length() - 1
    kch = lax.broadcasted_iota(jnp.int32, (TQ, TQ), 0) >> cshift
    qch = lax.broadcasted_iota(jnp.int32, (TQ, TQ), 1) >> cshift

    def scores_to(slot, j, keep):
        k0 = pl.multiple_of(j * TQ, TQ)
        for hd in range(MLA_HEADS):
            cols = slice(hd * LANES, (hd + 1) * LANES)
            s = _dot_t(k_ref[0, pl.ds(k0, TQ), cols], q_ref[0, :, cols])
            if keep is not None:
                s = jnp.where(keep, s, NEG_INF)
            s_scr[slot, hd] = s
            smax_scr[slot, hd] = jnp.broadcast_to(s.max(axis=0, keepdims=True), smax_scr.shape[2:])

    def softmax_pv(slot, j, st):
        k0 = pl.multiple_of(j * TQ, TQ)
        out = []
        for hd in range(MLA_HEADS):
            m, l, acc = st[hd]
            mn = jnp.maximum(m, smax_scr[slot, hd][0:1])
            p = jnp.exp(s_scr[slot, hd] - mn)
            alpha = jnp.exp(m - mn)
            vt = vt_ref[hd * MLA_V:(hd + 1) * MLA_V, pl.ds(k0, TQ)]
            out.append((mn, alpha * l + p.sum(axis=0, keepdims=True),
                        alpha * acc + _dot(vt, p.astype(BF16))))
        return tuple(out)

    def body(j, st):
        slot = j % 2
        st = softmax_pv(slot, j, st)
        scores_to(1 - slot, j + 1, None)
        return st

    st = tuple((jnp.full((1, TQ), NEG_INF, F32), jnp.zeros((1, TQ), F32), jnp.zeros((MLA_V, TQ), F32))
               for _ in range(MLA_HEADS))
    left = jnp.maximum(i - 1, 0)
    scores_to(0, 0, kch >= jnp.where(i >= 1, 0, TQ))
    st = lax.fori_loop(0, left, body, st)
    slot = left % 2
    st = softmax_pv(slot, left, st)
    scores_to(1 - slot, i, kch <= qch)
    st = softmax_pv(1 - slot, i, st)
    o_ref[0] = _group_norm_t(jnp.concatenate([acc / l for _, l, acc in st], axis=0), g_ref[...])


def _latent_attention(mq, mk, mvt, g):
    B, S, _ = mq.shape
    TQ = TQ_MLA
    W = GROUP_W
    return pl.pallas_call(
        _mla_kernel,
        grid=(B, S // TQ),
        in_specs=[pl.BlockSpec((1, TQ, mq.shape[2]), lambda b, i: (b, i, 0)),
                  pl.BlockSpec((1, S, mk.shape[2]), lambda b, i: (b, 0, 0)),
                  pl.BlockSpec((W, S), lambda b, i: (0, b)),
                  pl.BlockSpec((1, W), lambda b, i: (0, 0))],
        out_specs=pl.BlockSpec((1, TQ, W), lambda b, i: (b, i, 0)),
        out_shape=jax.ShapeDtypeStruct((B, S, W), BF16),
        scratch_shapes=[pltpu.VMEM((2, MLA_HEADS, TQ, TQ), F32), pltpu.VMEM((2, MLA_HEADS, 8, TQ), F32)],
        compiler_params=_cparams(2),
        name="latent_attention",
    )(mq, mk, mvt, g)


def _ffn_kernel(ya_ref, yb_ref, yc_ref, yd_ref, x_ref, mod_ref, wout_ref, gffn_ref, w1_ref, w3_ref, w2_ref,
                gfin_ref, o_ref, acc_scr, *, final):
    gt1 = mod_ref[0, 2:3, :]
    sh2 = mod_ref[0, 3:4, :]
    sc2 = mod_ref[0, 4:5, :]
    gt2 = mod_ref[0, 5:6, :]
    attn = _dot(ya_ref[...], wout_ref[0:GROUP_W, :])
    for gi, y_ref in enumerate((yb_ref, yc_ref, yd_ref), start=1):
        attn = attn + _dot(y_ref[...], wout_ref[gi * GROUP_W:(gi + 1) * GROUP_W, :])
    x1 = x_ref[...] + gt1 * attn
    h = (_rms(x1, gffn_ref[...]) * (1.0 + sc2) + sh2).astype(BF16)
    for ci in range(D_FF // FF_CHUNK):
        cols = slice(ci * FF_CHUNK, (ci + 1) * FF_CHUNK)
        a = _dot(h, w1_ref[:, cols])
        gate = (a * jax.nn.sigmoid(a) * _dot(h, w3_ref[:, cols])).astype(BF16)
        part = _dot(gate, w2_ref[cols, :])
        if ci == 0:
            acc_scr[...] = part
        else:
            acc_scr[...] += part
    x2 = x1 + gt2 * acc_scr[...]
    o_ref[...] = _rms(x2, gfin_ref[...]) if final else x2


def _out_ffn(ys, x2, mod, wout, gffn, w1, w3, w2, gfin, S, final):
    N, D = x2.shape
    TM = TM_FFN
    nt = S // TM

    def full(a):
        return pl.BlockSpec(a.shape, lambda i: (0,) * a.ndim, pipeline_mode=pl.Buffered(1))

    def tok(w):
        return pl.BlockSpec((TM, w), lambda i: (i, 0))

    return pl.pallas_call(
        functools.partial(_ffn_kernel, final=final),
        grid=(N // TM,),
        in_specs=[tok(GROUP_W)] * 4 + [tok(D), pl.BlockSpec((1, 6, D), lambda i: (i // nt, 0, 0)),
                                       full(wout), full(gffn), full(w1), full(w3), full(w2), full(gfin)],
        out_specs=tok(D),
        out_shape=jax.ShapeDtypeStruct((N, D), F32),
        scratch_shapes=[pltpu.VMEM((TM, D), F32)],
        compiler_params=_cparams(1),
        name="out_ffn_final" if final else "out_ffn",
    )(*ys, x2, mod, wout, gffn, w1, w3, w2, gfin)


def _t5_bucket(rel):
    nb = T5_BUCKETS // 2
    max_exact = nb // 2
    ret = jnp.where(rel > 0, nb, 0)
    n = jnp.abs(rel)
    nf = jnp.maximum(n, 1).astype(jnp.float32)
    large = max_exact + (jnp.log(nf / max_exact) / math.log(T5_MAX_DIST / max_exact)
                         * (nb - max_exact)).astype(jnp.int32)
    large = jnp.minimum(large, nb - 1)
    return ret + jnp.where(n < max_exact, n, large)


def _rope_tables(S):
    half = MLA_ROPE // 2
    freqs = ROPE_BASE ** (-jnp.arange(half, dtype=F32) / half)
    ang = jnp.arange(S, dtype=jnp.int32).astype(F32)[:, None] * freqs[None, :]
    cos, sin = jnp.cos(ang), jnp.sin(ang)
    cos2 = jnp.concatenate([cos, cos], axis=1)
    sin2 = jnp.concatenate([-sin, sin], axis=1)
    zeros = jnp.zeros((S, LANES - MLA_NOPE - MLA_ROPE), F32)
    scale = (MLA_NOPE + MLA_ROPE) ** -0.5
    cosq = jnp.concatenate([jnp.full((S, MLA_NOPE), scale, F32), cos2 * scale, zeros], axis=1)
    sinq = jnp.concatenate([jnp.zeros((S, MLA_NOPE), F32), sin2 * scale, zeros], axis=1)
    cosk = jnp.concatenate([jnp.zeros((S, MLA_NOPE), F32), cos2, zeros], axis=1)
    sink = jnp.concatenate([jnp.zeros((S, MLA_NOPE), F32), sin2, zeros], axis=1)
    return cosq, sinq, cosk, sink


def _pack_in_weight(w):
    part = {n: w[:, IN_OFFS[k]:IN_OFFS[k + 1]] for k, n in enumerate(
        ('pool_u', 'ca_q', 'ca_k', 'ca_v', 'sa_q', 'sa_k', 'sa_v', 'idx_q', 'idx_k', 'idx_w',
         'mla_cq', 'mla_ckv', 'mla_kr'))}
    D = w.shape[0]
    z = lambda n: jnp.zeros((D, n), F32)
    qscale = HEAD_DIM ** -0.5
    saq = part['sa_q'].reshape(D, SA_HEADS, HEAD_DIM) * qscale
    saq = jnp.concatenate([saq, jnp.zeros_like(saq)], axis=2).reshape(D, SA_HEADS * LANES)
    kr = part['mla_kr']
    kr_swap = jnp.concatenate([kr[:, MLA_ROPE // 2:], kr[:, :MLA_ROPE // 2]], axis=1)
    pad_r = LANES - MLA_NOPE - MLA_ROPE
    cols = [part['pool_u'], part['ca_q'] * qscale, part['ca_k'], saq,
            part['sa_k'], part['sa_v'], part['idx_q'],
            part['idx_k'], z(IDX_DIM), z(IDX_DIM), part['idx_k'],
            part['mla_cq'], part['mla_ckv'],
            z(MLA_NOPE), kr, z(pad_r), z(MLA_NOPE), kr_swap, z(pad_r)]
    out = jnp.concatenate(cols, axis=1)
    assert out.shape[1] == C_END
    wt = jnp.concatenate([part['idx_w'].T, jnp.zeros((IWT_ROWS - IDX_HEADS, D), F32), part['sa_v'].T,
                          part['ca_v'].T], axis=0)
    return out.astype(BF16), wt.astype(BF16)


def _pack_mla_weights(w_uq, w_ukv):
    R = w_uq.shape[0]
    pad = jnp.zeros((R, MLA_HEADS, LANES - MLA_NOPE - MLA_ROPE), F32)
    rope_w = w_uq[:, :, MLA_NOPE:]
    rope_sw = jnp.concatenate([rope_w[:, :, MLA_ROPE // 2:], rope_w[:, :, :MLA_ROPE // 2]], axis=2)
    wq = jnp.concatenate([w_uq, pad], axis=2).reshape(R, MLA_HEADS * LANES)
    wqs = jnp.concatenate([jnp.zeros((R, MLA_HEADS, MLA_NOPE), F32), rope_sw, pad],
                          axis=2).reshape(R, MLA_HEADS * LANES)
    Rk = w_ukv.shape[0]
    wk = jnp.concatenate([w_ukv[:, :, :MLA_NOPE], jnp.zeros((Rk, MLA_HEADS, LANES - MLA_NOPE), F32)],
                         axis=2).reshape(Rk, MLA_HEADS * LANES)
    wvt = w_ukv[:, :, MLA_NOPE:].reshape(Rk, MLA_HEADS * MLA_V).T
    return wq.astype(BF16), wqs.astype(BF16), wk.astype(BF16), wvt.astype(BF16)


def _toeplitz(vec, rows, cols):
    L = vec.shape[-1]
    assert cols <= L - 1
    flat = jnp.tile(vec, (1, rows))[:, :rows * (L - 1)]
    return flat.reshape(vec.shape[0], rows, L - 1)[:, :, :cols]


def _signed_mod_range(L, hi):
    d = np.arange(L)
    return np.where(d <= hi, d, d - L)


def _band_bias(rel_table):
    L = CA_WIN + TQ_CA
    e = _signed_mod_range(L, TQ_CA - 1)
    ridx = np.clip(CA_LEFT_CHUNKS * CHUNK + e, -(CHUNK - 1), CA_MAX_REL) + (CHUNK - 1)
    bias = _toeplitz(rel_table[:, ridx].astype(F32), CA_WIN, TQ_CA)
    kc = np.arange(CA_WIN)[:, None] // CHUNK
    qc = np.arange(TQ_CA)[None, :] // CHUNK + CA_LEFT_CHUNKS
    valid = (kc <= qc) & (kc >= qc - CA_LEFT_CHUNKS)
    return jnp.where(valid[None], bias, NEG_INF)


def _t5_bias(t5_table):
    TQ = TQ_SA
    L = 3 * TQ
    e = _signed_mod_range(L, TQ - 1)
    rel = jnp.asarray(-e - TQ, jnp.int32)
    far = t5_table[_t5_bucket(jnp.int32(-(TQ + 1)))].astype(F32)
    vec = (t5_table[_t5_bucket(rel)].astype(F32) - far[None, :]).T
    return _toeplitz(vec, 2 * TQ, TQ)


def kernel(x, c, t5_table, w_mod, b_mod, g_mix, w_in, pool_w, pool_scale, ca_rel, mla_g_cq, mla_g_ckv,
           mla_w_uq, mla_w_ukv, g_group, w_out, g_ffn, ffn_w1, ffn_w3, ffn_w2, g_final):
    B, S, D = x.shape
    assert D == D_MODEL and S % TM_PROJ == 0 and S % TQ_SA == 0 and S >= 4 * TOPK_MAX
    N = B * S
    mod_all = _modulation(c, w_mod, b_mod)
    rope_tabs = _rope_tables(S)
    nbias = _t5_bias(t5_table)
    row = lambda v: v.reshape(1, -1).astype(F32)
    x2 = x.reshape(N, D)
    for l in range(DEPTH):
        mod = mod_all[l].reshape(B, 6, D)
        w1, wt = _pack_in_weight(w_in[l])
        wq, wqs, wk, wvt = _pack_mla_weights(mla_w_uq[l], mla_w_ukv[l])
        (pool_u, ca, saq, sakv, iq, ik, iwt, svt, cavt, mq, mk, mvt) = _inproj(
            x2, mod, row(g_mix[l]), w1, wt, row(mla_g_cq[l]), row(mla_g_ckv[l]), wq, wqs, wk, wvt, rope_tabs, S)
        gg = g_group[l].reshape(4, 1, GROUP_W).astype(F32)
        wbd = jax.scipy.linalg.block_diag(*[pool_w[l, gi] for gi in range(len(POOL_WINDOWS))]).astype(BF16)
        bsw = lambda a: a.reshape(B, S, a.shape[-1])
        y_a = _pool(bsw(pool_u), wbd, row(pool_scale[l]), gg[0])
        y_b = _chunk_attention(bsw(ca), cavt, _band_bias(ca_rel[l]), gg[1])
        y_c = _sparse_attention(bsw(saq), bsw(sakv), svt, bsw(iq), bsw(ik), iwt, nbias, gg[2])
        y_d = _latent_attention(bsw(mq), bsw(mk), mvt, gg[3])
        ys = [y.reshape(N, GROUP_W) for y in (y_a, y_b, y_c, y_d)]
        x2 = _out_ffn(ys, x2, mod, w_out[l].astype(BF16), row(g_ffn[l]), ffn_w1[l].astype(BF16),
                      ffn_w3[l].astype(BF16), ffn_w2[l].astype(BF16), row(g_final), S,
                      final=(l == DEPTH - 1))
    return x2.reshape(B, S, D)
```

```python
import functools
import math
from statistics import NormalDist

import jax
import jax.numpy as jnp
from jax import lax
import numpy as np
from jax.experimental import pallas as pl
from jax.experimental.pallas import tpu as pltpu

F32 = jnp.float32
BF16 = jnp.bfloat16

D_MODEL = 1024
DEPTH = 2
CHUNK = 64
EPS = 1e-6
NEG_INF = -1e30
GROUP_W = 256
HEAD_DIM = 64
POOL_WINDOWS = (2, 4, 8, 16)
POOL_HALO = 16
CA_HEADS = 4
CA_LEFT_CHUNKS = 8
CA_MAX_REL = 256
SA_HEADS = 4
IDX_HEADS = 8
IDX_DIM = 64
TOPK_MAX = 256
MLA_HEADS = 4
MLA_NOPE = 64
MLA_ROPE = 32
MLA_V = 64
ROPE_BASE = 10000.0
T5_BUCKETS = 32
T5_MAX_DIST = 128
D_FF = 2816
IN_WIDTHS = (256, 256, 256, 256, 256, 64, 64, 512, 64, 8, 256, 128, 32)
IN_OFFS = tuple(int(v) for v in np.cumsum((0,) + IN_WIDTHS))

LANES = 128
VMEM_LIMIT = 56 * 1024 * 1024

TM_PROJ = 512
TM_FFN = 512
TP_POOL = 512
TQ_CA = 256
CA_WIN = TQ_CA + CA_LEFT_CHUNKS * CHUNK
CA_NBLK = CA_WIN // TQ_CA
IWT_ROWS = 16
TQ_SA = 256
KB_SA = 256
COUNT_CHAINS = 2
SEARCH_FIRST_ROUND = 16
SEARCH_ROUND = 4
GUESS_SPREAD = 0.3
TQ_MLA = 256
FF_CHUNK = 256

C_POOL = 0
C_CA = C_POOL + 256
C_SAQ = C_CA + 2 * GROUP_W
C_SAKV = C_SAQ + SA_HEADS * LANES
C_IQ = C_SAKV + LANES
C_IK = C_IQ + IDX_HEADS * IDX_DIM
C_CQ = C_IK + 2 * LANES
C_CKV = C_CQ + 256
C_KRF = C_CKV + LANES
C_KRS = C_KRF + LANES
C_END = C_KRS + LANES

INT_MIN = -2 ** 31
KEY_ALL = INT_MIN - int(np.array(-np.inf, np.float32).view(np.int32)) + 1


def _cparams(n_axes):
    return pltpu.CompilerParams(dimension_semantics=("arbitrary",) * n_axes,
                                vmem_limit_bytes=VMEM_LIMIT)


def _rms(x, g):
    return x * lax.rsqrt(jnp.mean(x * x, axis=-1, keepdims=True) + EPS) * g


def _dot(a, b):
    return jnp.dot(a, b, preferred_element_type=F32)


def _dot_t(a, b):
    return lax.dot_general(a, b, (((1,), (1,)), ((), ())), preferred_element_type=F32)


def _mod_kernel(c_ref, w_ref, b_ref, o_ref):
    c = c_ref[...]
    act = c * jax.nn.sigmoid(c)
    o_ref[0] = jnp.dot(act, w_ref[0], precision=lax.Precision.HIGHEST,
                       preferred_element_type=F32) + b_ref[0]


def _modulation(c, w_mod, b_mod):
    L, D, W = w_mod.shape
    B = c.shape[0]
    nj = W // D
    return pl.pallas_call(
        _mod_kernel,
        grid=(L, nj),
        in_specs=[pl.BlockSpec((B, D), lambda l, j: (0, 0)),
                  pl.BlockSpec((1, D, D), lambda l, j: (l, 0, j)),
                  pl.BlockSpec((1, 1, D), lambda l, j: (l, 0, j))],
        out_specs=pl.BlockSpec((1, B, D), lambda l, j: (l, 0, j)),
        out_shape=jax.ShapeDtypeStruct((L, B, W), F32),
        compiler_params=_cparams(2),
        name="modulation",
    )(c, w_mod, b_mod.reshape(L, 1, W))


def _inproj_kernel(x_ref, mod_ref, gmix_ref, w_ref, wt_ref, gcq_ref, gckv_ref, wq_ref, wqs_ref, wk_ref, wvt_ref,
                   cosq_ref, sinq_ref, cosk_ref, sink_ref,
                   pool_o, ca_o, saq_o, sakv_o, iq_o, ik_o, iwt_o, svt_o, cavt_o, mq_o, mk_o, mvt_o):
    sh1 = mod_ref[0, 0:1, :]
    sc1 = mod_ref[0, 1:2, :]
    h = (_rms(x_ref[...], gmix_ref[...]) * (1.0 + sc1) + sh1).astype(BF16)

    def seg(a, b):
        return _dot(h, w_ref[:, a:b])

    pool_o[...] = seg(C_POOL, C_CA)
    ca_o[...] = seg(C_CA, C_SAQ).astype(BF16)
    saq_o[...] = seg(C_SAQ, C_SAKV).astype(BF16)
    sakv_o[...] = seg(C_SAKV, C_IQ).astype(BF16)
    iq_o[...] = seg(C_IQ, C_IK).astype(BF16)
    ik_o[...] = seg(C_IK, C_CQ).astype(BF16)
    tr = _dot_t(wt_ref[...], h)
    iwt_o[...] = tr[0:IWT_ROWS] * ((IDX_HEADS ** -0.5) * (IDX_DIM ** -0.5))
    svt_o[...] = tr[IWT_ROWS:IWT_ROWS + HEAD_DIM].astype(BF16)
    cavt_o[...] = tr[IWT_ROWS + HEAD_DIM:].astype(BF16)

    qn = _rms(seg(C_CQ, C_CKV), gcq_ref[...]).astype(BF16)
    qf = _dot(qn, wq_ref[...])
    qs = _dot(qn, wqs_ref[...])
    cosq = jnp.concatenate([cosq_ref[...]] * MLA_HEADS, axis=1)
    sinq = jnp.concatenate([sinq_ref[...]] * MLA_HEADS, axis=1)
    mq_o[...] = (qf * cosq + qs * sinq).astype(BF16)

    kvn = _rms(seg(C_CKV, C_KRF), gckv_ref[...]).astype(BF16)
    kvf = _dot(kvn, wk_ref[...])
    krope = seg(C_KRF, C_KRS) * cosk_ref[...] + seg(C_KRS, C_END) * sink_ref[...]
    for hd in range(MLA_HEADS):
        mk_o[:, hd * LANES:(hd + 1) * LANES] = (kvf[:, hd * LANES:(hd + 1) * LANES] + krope).astype(BF16)
    mvt_o[...] = _dot_t(wvt_ref[...], kvn).astype(BF16)


def _inproj(x2, mod, gmix, w1, wt, gcq, gckv, wq, wqs, wk, wvt, rope_tabs, S):
    N, D = x2.shape
    TM = TM_PROJ
    nt = S // TM
    cosq, sinq, cosk, sink = rope_tabs

    def full(a):
        return pl.BlockSpec(a.shape, lambda i: (0,) * a.ndim)

    def tok(w):
        return pl.BlockSpec((TM, w), lambda i: (i, 0))

    tab = pl.BlockSpec((TM, LANES), lambda i: (i % nt, 0))
    def tokt(rows):
        return pl.BlockSpec((rows, TM), lambda i: (0, i))

    outs = [(C_CA - C_POOL, F32, True), (C_SAQ - C_CA, BF16, True), (C_SAKV - C_SAQ, BF16, True),
            (C_IQ - C_SAKV, BF16, True), (C_IK - C_IQ, BF16, True), (C_CQ - C_IK, BF16, True),
            (IWT_ROWS, F32, False), (HEAD_DIM, BF16, False), (GROUP_W, BF16, False),
            (MLA_HEADS * LANES, BF16, True), (MLA_HEADS * LANES, BF16, True), (GROUP_W, BF16, False)]
    return pl.pallas_call(
        _inproj_kernel,
        grid=(N // TM,),
        in_specs=[tok(D),
                  pl.BlockSpec((1, 6, D), lambda i: (i // nt, 0, 0)),
                  full(gmix), full(w1), full(wt), full(gcq), full(gckv), full(wq), full(wqs), full(wk), full(wvt),
                  tab, tab, tab, tab],
        out_specs=[tok(w) if tm else tokt(w) for w, _, tm in outs],
        out_shape=[jax.ShapeDtypeStruct((N, w) if tm else (w, N), dt) for w, dt, tm in outs],
        compiler_params=_cparams(1),
        name="inproj",
    )(x2, mod, gmix, w1, wt, gcq, gckv, wq, wqs, wk, wvt, cosq, sinq, cosk, sink)


def _pool_kernel(u_ref, halo_ref, w_ref, scale_ref, g_ref, o_ref, pad_scr):
    i = pl.program_id(1)
    TP = u_ref.shape[1]
    u = u_ref[0]
    pad_scr[0:POOL_HALO, :] = jnp.where(i > 0, halo_ref[0], 0.0)
    pad_scr[POOL_HALO:, :] = u

    def shifted(j):
        return pad_scr[POOL_HALO - j:POOL_HALO - j + TP, :]

    lane = lax.broadcasted_iota(jnp.int32, (TP, GROUP_W), 1)
    w2 = u + shifted(1)
    w4 = w2 + shifted(2) + shifted(3)
    w8 = w4
    for j in range(4, 8):
        w8 = w8 + shifted(j)
    w16 = w8
    for j in range(8, 16):
        w16 = w16 + shifted(j)
    win = jnp.where(lane < 64, w2, jnp.where(lane < 128, w4, jnp.where(lane < 192, w8, w16)))
    wlen = jnp.where(lane < 64, 2, jnp.where(lane < 128, 4, jnp.where(lane < 192, 8, 16)))
    t = i * TP + lax.broadcasted_iota(jnp.int32, (TP, GROUP_W), 0)
    cnt = jnp.minimum(t + 1, wlen).astype(F32)
    d = (win / cnt - u).astype(BF16)
    y = _dot(d, w_ref[...]) * scale_ref[...]
    o_ref[0] = _rms(y, g_ref[...]).astype(BF16)


def _pool(u, wbd, scale, g):
    B, S, W = u.shape
    TP = TP_POOL
    hb = TP // POOL_HALO
    return pl.pallas_call(
        _pool_kernel,
        grid=(B, S // TP),
        in_specs=[pl.BlockSpec((1, TP, W), lambda b, i: (b, i, 0)),
                  pl.BlockSpec((1, POOL_HALO, W), lambda b, i: (b, jnp.maximum(i * hb - 1, 0), 0)),
                  pl.BlockSpec((W, W), lambda b, i: (0, 0)),
                  pl.BlockSpec((1, W), lambda b, i: (0, 0)),
                  pl.BlockSpec((1, W), lambda b, i: (0, 0))],
        out_specs=pl.BlockSpec((1, TP, W), lambda b, i: (b, i, 0)),
        out_shape=jax.ShapeDtypeStruct((B, S, W), BF16),
        scratch_shapes=[pltpu.VMEM((POOL_HALO + TP, W), F32)],
        compiler_params=_cparams(2),
        name="pool_mixer",
    )(u, u, wbd, scale, g)


def _ca_kernel(q_ref, k_ref, vt_ref, bias_ref, g_ref, o_ref):
    i = pl.program_id(1)
    TQ = TQ_CA
    lane = lax.broadcasted_iota(jnp.int32, (TQ, LANES), 1)
    starts = []
    for j in range(CA_NBLK):
        kb = i - (CA_NBLK - 1) + j
        starts.append((kb >= 0, pl.multiple_of(jnp.maximum(kb, 0) * TQ, TQ)))
    scored = []
    for hd in range(CA_HEADS):
        cols = slice((hd // 2) * LANES, (hd // 2 + 1) * LANES)
        keep = (lane < HEAD_DIM) if hd % 2 == 0 else (lane >= HEAD_DIM)
        qh = jnp.where(keep, q_ref[0, :, cols].astype(F32), 0.0).astype(BF16)
        parts = []
        for j, (present, start) in enumerate(starts):
            s = _dot_t(k_ref[0, pl.ds(start, TQ), cols], qh) + bias_ref[hd, j * TQ:(j + 1) * TQ, :]
            parts.append(jnp.where(present, s, NEG_INF))
        m = parts[0].max(axis=0, keepdims=True)
        for s in parts[1:]:
            m = jnp.maximum(m, s.max(axis=0, keepdims=True))
        scored.append((parts, m))
    outs = []
    for hd, (parts, m) in enumerate(scored):
        l = jnp.zeros((1, TQ), F32)
        acc = jnp.zeros((HEAD_DIM, TQ), F32)
        for j, (_, start) in enumerate(starts):
            p = jnp.exp(parts[j] - m)
            l = l + p.sum(axis=0, keepdims=True)
            acc = acc + _dot(vt_ref[hd * HEAD_DIM:(hd + 1) * HEAD_DIM, pl.ds(start, TQ)], p.astype(BF16))
        outs.append(acc / l)
    o_ref[0] = _group_norm_t(jnp.concatenate(outs, axis=0), g_ref[...])


def _chunk_attention(caqk, cavt, bias, g):
    B, S, _ = caqk.shape
    W = GROUP_W
    TQ = TQ_CA
    return pl.pallas_call(
        _ca_kernel,
        grid=(B, S // TQ),
        in_specs=[pl.BlockSpec((1, TQ, W), lambda b, i: (b, i, 0)),
                  pl.BlockSpec((1, S, W), lambda b, i: (b, 0, 1)),
                  pl.BlockSpec((W, S), lambda b, i: (0, b)),
                  pl.BlockSpec(bias.shape, lambda b, i: (0, 0, 0)),
                  pl.BlockSpec((1, W), lambda b, i: (0, 0))],
        out_specs=pl.BlockSpec((1, TQ, W), lambda b, i: (b, i, 0)),
        out_shape=jax.ShapeDtypeStruct((B, S, W), BF16),
        compiler_params=_cparams(2),
        name="band_attention",
    )(caqk, caqk, cavt, bias, g)


def _score_key(score):
    b = lax.bitcast_convert_type(score, jnp.int32)
    return jnp.where(b < 0, jnp.int32(INT_MIN) - b, b)


def _sa_kernel(q_ref, kv_ref, vt_ref, iq_ref, ik_ref, iwt_ref, zq_ref, nbias_ref, g_ref, o_ref,
               key_scr, s0_scr, s1_scr, smax0_scr, smax1_scr):
    s_scr, smax_scr = (s0_scr, s1_scr), (smax0_scr, smax1_scr)
    i = pl.program_id(1)
    TQ, KB = TQ_SA, KB_SA
    K = float(TOPK_MAX)
    nb = i + 1
    q0 = i * TQ
    cshift = CHUNK.bit_length() - 1
    kchunk = lax.broadcasted_iota(jnp.int32, (KB, TQ), 0) >> cshift
    qchunk = (q0 + lax.broadcasted_iota(jnp.int32, (1, TQ), 1)) >> cshift

    iwt = iwt_ref[...]

    def score_block(j, carry):
        smax, s1, s2 = carry
        k0 = pl.multiple_of(j * KB, KB)
        ik = ik_ref[0, pl.ds(k0, KB), :]
        ik2 = jnp.concatenate([ik[:, :LANES], ik[:, LANES:]], axis=0)
        sc = jnp.zeros((KB, TQ), F32)
        for p in range(IDX_HEADS // 2):
            logits = _dot_t(ik2, iq_ref[0, :, p * LANES:(p + 1) * LANES])
            sc = sc + iwt[2 * p:2 * p + 1, :] * jnp.maximum(logits[:KB], 0.0)
            sc = sc + iwt[2 * p + 1:2 * p + 2, :] * jnp.maximum(logits[KB:], 0.0)
        adm = kchunk <= qchunk - (k0 >> cshift)
        sc = jnp.where(adm, sc, -jnp.inf)
        key_scr[pl.ds(k0, KB), :] = _score_key(sc)
        smax = jnp.maximum(smax, sc.max(axis=0, keepdims=True))
        full = j < i
        s1 = jnp.where(full, s1 + sc.sum(axis=0, keepdims=True), s1)
        s2 = jnp.where(full, s2 + (sc * sc).sum(axis=0, keepdims=True), s2)
        return smax, s1, s2

    smax, s1, s2 = lax.fori_loop(
        0, (nb + 1) // 2, lambda j, c: score_block(2 * j + 1, score_block(2 * j, c)),
        (jnp.full((1, TQ), -jnp.inf, F32), jnp.zeros((1, TQ), F32), jnp.zeros((1, TQ), F32)))

    def count_ge(cand):
        def body(j, acc):
            blk = key_scr[pl.ds(pl.multiple_of(j * (2 * KB), 2 * KB), 2 * KB), :]
            ones = jnp.where(blk >= cand, 1.0, 0.0)
            return acc + ones.reshape(COUNT_CHAINS, -1, 8, TQ).sum(axis=1)
        acc = lax.fori_loop(0, (nb + 1) // 2, body, jnp.zeros((COUNT_CHAINS, 8, TQ), F32))
        return acc.sum(axis=0).sum(axis=0, keepdims=True)

    def search():
        def unkey(k):
            return lax.bitcast_convert_type(jnp.where(k < 0, jnp.int32(INT_MIN) - k, k), F32)

        def is_active(lo, hi, clo):
            return jnp.logical_and(clo > K, hi > lo + 1)

        def cond(st):
            _, lo, hi, clo, _ = st
            act = jnp.where(is_active(lo, hi, clo), 1.0, 0.0)
            return jnp.max(jnp.maximum(act[:, :LANES], act[:, LANES:])) > 0.0

        n_full = (i * KB).astype(F32)
        mean = s1 / n_full
        std = jnp.sqrt(jnp.maximum(s2 / n_full - mean * mean, 0.0))
        zq = jnp.max(zq_ref[...], axis=0, keepdims=True)
        guess_lo = _score_key(mean + (zq - GUESS_SPREAD) * std)
        guess_hi = _score_key(mean + (zq + GUESS_SPREAD) * std)

        def step(_, st):
            it, lo, hi, clo, chi = st
            active = is_active(lo, hi, clo)
            lf, hf = unkey(lo), unkey(hi)
            lc = jnp.log(clo)
            frac = jnp.clip((lc - math.log(K - 0.5)) / (lc - jnp.log(jnp.maximum(chi, 0.5))), 0.05, 0.95)
            cand = _score_key(lf + frac * (hf - lf))
            cand = jnp.where(it % 3 == 2, (lo >> 1) + (hi >> 1) + (lo & hi & 1), cand)
            cand = jnp.where(it == 0, guess_lo, cand)
            cand = jnp.where(it == 1, guess_hi, cand)
            cand = jnp.where(active, jnp.clip(cand, lo + 1, hi - 1), lo)
            cnt = count_ge(cand)
            up = jnp.logical_and(active, cnt >= K)
            down = jnp.logical_and(active, cnt < K)
            return (it + 1, jnp.where(up, cand, lo), jnp.where(down, cand, hi),
                    jnp.where(up, cnt, clo), jnp.where(down, cnt, chi))

        lo0 = jnp.full((1, TQ), KEY_ALL - 1, jnp.int32)
        hi0 = _score_key(smax) + 1
        clo0 = jnp.zeros((1, TQ), F32) + ((nb + 1) // 2 * (2 * KB)).astype(F32)
        st = (jnp.int32(0), lo0, hi0, clo0, jnp.zeros((1, TQ), F32))
        st = lax.fori_loop(0, SEARCH_FIRST_ROUND, step, st)
        st = lax.while_loop(cond, lambda s: lax.fori_loop(0, SEARCH_ROUND, step, s), st)
        return st[1], st[3]

    def no_search():
        return jnp.full((1, TQ), KEY_ALL, jnp.int32), jnp.full((1, TQ), K, F32)

    t, cnt_t = lax.cond(i > 0, search, no_search)
    t = jnp.maximum(t, KEY_ALL)

    @pl.when(jnp.max(cnt_t) > K)
    def _():
        allowed = K - count_ge(t + 1)
        r = lax.broadcasted_iota(jnp.int32, (KB, KB), 0)
        c = lax.broadcasted_iota(jnp.int32, (KB, KB), 1)
        earlier = jnp.where(c < r, 1.0, 0.0).astype(BF16)

        def body(j, seen):
            sl = pl.ds(pl.multiple_of(j * KB, KB), KB)
            blk = key_scr[sl, :]
            eq = jnp.where(blk == t, 1.0, 0.0)
            rank = _dot(earlier, eq.astype(BF16)) + seen
            demote = eq * jnp.where(rank >= allowed, 1.0, 0.0)
            key_scr[sl, :] = jnp.where(demote > 0.5, t - 1, blk)
            return seen + eq.sum(axis=0, keepdims=True)

        lax.fori_loop(0, nb, body, jnp.zeros((1, TQ), F32))

    def scores_to(slot, j, bias_rows, present):
        k0 = pl.multiple_of(j * KB, KB)
        kblk = kv_ref[0, pl.ds(k0, KB), :]
        sel = key_scr[pl.ds(k0, KB), :] >= (t if present is True else jnp.where(present, t, jnp.int32(2 ** 31 - 1)))
        for hd in range(SA_HEADS):
            s = _dot_t(kblk, q_ref[0, :, hd * LANES:(hd + 1) * LANES])
            if bias_rows is not None:
                s = s + nbias_ref[hd, bias_rows, :]
            s = jnp.where(sel, s, NEG_INF)
            s_scr[slot][hd] = s
            smax_scr[slot][hd] = jnp.broadcast_to(s.max(axis=0, keepdims=True), smax_scr[slot].shape[1:])

    def softmax_pv(slot, j, st):
        vt = vt_ref[:, pl.ds(pl.multiple_of(j * KB, KB), KB)]
        out = []
        for hd in range(SA_HEADS):
            m, l, acc = st[hd]
            mn = jnp.maximum(m, smax_scr[slot][hd][0:1])
            p = jnp.exp(s_scr[slot][hd] - mn)
            alpha = jnp.exp(m - mn)
            out.append((mn, alpha * l + p.sum(axis=0, keepdims=True), alpha * acc + _dot(vt, p.astype(BF16))))
        return tuple(out)

    st = tuple((jnp.full((1, TQ), NEG_INF, F32), jnp.zeros((1, TQ), F32), jnp.zeros((HEAD_DIM, TQ), F32))
               for _ in range(SA_HEADS))
    left = jnp.maximum(i - 1, 0)
    tails = [(left, lambda slot: scores_to(slot, left, slice(0, KB), i >= 1)),
             (i, lambda slot: scores_to(slot, i, slice(KB, 2 * KB), True))]
    st = _block_pipeline(left, i, lambda slot, j, present: scores_to(slot, j, None, present), tails, softmax_pv, st)
    y_t = jnp.concatenate([acc / l for _, l, acc in st], axis=0)
    o_ref[0] = _group_norm_t(y_t, g_ref[...])


def _sparse_attention(saq, sakv, svt, iq, ik, iwt, nbias, g):
    B, S, _ = saq.shape
    TQ = TQ_SA
    W = GROUP_W
    nt = S // TQ
    n_adm = (np.arange(S) // CHUNK + 1) * CHUNK
    zq = np.array([NormalDist().inv_cdf(1.0 - TOPK_MAX / n) if n > TOPK_MAX else 0.0 for n in n_adm], np.float32)
    zq = jnp.asarray(np.tile(zq[None, :], (8, 1)))
    return pl.pallas_call(
        _sa_kernel,
        grid=(B, nt),
        in_specs=[pl.BlockSpec((1, TQ, saq.shape[2]), lambda b, i: (b, i, 0)),
                  pl.BlockSpec((1, S, sakv.shape[2]), lambda b, i: (b, 0, 0)),
                  pl.BlockSpec((HEAD_DIM, S), lambda b, i: (0, b)),
                  pl.BlockSpec((1, TQ, iq.shape[2]), lambda b, i: (b, i, 0)),
                  pl.BlockSpec((1, S, ik.shape[2]), lambda b, i: (b, 0, 0)),
                  pl.BlockSpec((IWT_ROWS, TQ), lambda b, i: (0, b * nt + i)),
                  pl.BlockSpec((8, TQ), lambda b, i: (0, i)),
                  pl.BlockSpec(nbias.shape, lambda b, i: (0, 0, 0)),
                  pl.BlockSpec((1, W), lambda b, i: (0, 0))],
        out_specs=pl.BlockSpec((1, TQ, W), lambda b, i: (b, i, 0)),
        out_shape=jax.ShapeDtypeStruct((B, S, W), BF16),
        scratch_shapes=([pltpu.VMEM((S, TQ), jnp.int32)] + [pltpu.VMEM((SA_HEADS, KB_SA, TQ), F32)] * 2
                        + [pltpu.VMEM((SA_HEADS, 8, TQ), F32)] * 2),
        compiler_params=_cparams(2),
        name="sparse_attention",
    )(saq, sakv, svt, iq, ik, iwt, zq, nbias, g)


def _group_norm_t(y_t, g):
    inv = lax.rsqrt(jnp.mean(y_t * y_t, axis=0, keepdims=True) + EPS)
    return ((y_t * inv).T * g).astype(BF16)


def _block_pipeline(n_plain, last_blk, score_plain, tails, softmax, st):
    off = n_plain % 2
    n_loop = jnp.maximum((n_plain + off) // 2 - 1, 0)

    def blk(pos):
        return jnp.clip(pos - off, 0, last_blk)

    def body(pp, st):
        pos = 2 * pp
        score_plain(1, blk(pos + 1), True)
        st = softmax(0, blk(pos), st)
        score_plain(0, blk(pos + 2), True)
        return softmax(1, blk(pos + 1), st)

    score_plain(0, blk(0), jnp.logical_and(n_plain >= 1, off == 0))
    st = lax.fori_loop(0, n_loop, body, st)
    e0 = 2 * n_loop
    slot, pending = 0, blk(e0)
    steps = [(blk(e0 + 1), lambda s: score_plain(s, blk(e0 + 1), n_plain >= 1))] + list(tails)
    for nxt, score_fn in steps:
        score_fn(1 - slot)
        st = softmax(slot, pending, st)
        slot, pending = 1 - slot, nxt
    return softmax(slot, pending, st)


def _mla_kernel(q_ref, k_ref, vt_ref, g_ref, o_ref, s0_scr, s1_scr, smax0_scr, smax1_scr):
    i = pl.program_id(1)
    TQ = TQ_MLA
    cshift = CHUNK.bit_length() - 1
    kch = lax.broadcasted_iota(jnp.int32, (TQ, TQ), 0) >> cshift
    qch = lax.broadcasted_iota(jnp.int32, (TQ, TQ), 1) >> cshift
    s_scr, smax_scr = (s0_scr, s1_scr), (smax0_scr, smax1_scr)

    def scores_to(slot, j, keep):
        k0 = pl.multiple_of(j * TQ, TQ)
        for hd in range(MLA_HEADS):
            cols = slice(hd * LANES, (hd + 1) * LANES)
            s = _dot_t(k_ref[0, pl.ds(k0, TQ), cols], q_ref[0, :, cols])
            if keep is not None:
                s = jnp.where(keep, s, NEG_INF)
            s_scr[slot][hd] = s
            smax_scr[slot][hd] = jnp.broadcast_to(s.max(axis=0, keepdims=True), smax_scr[slot].shape[1:])

    def score_plain(slot, j, present):
        scores_to(slot, j, None if present is True else kch >= jnp.where(present, 0, TQ))

    def softmax_pv(slot, j, st):
        k0 = pl.multiple_of(j * TQ, TQ)
        out = []
        for hd in range(MLA_HEADS):
            m, l, acc = st[hd]
            mn = jnp.maximum(m, smax_scr[slot][hd][0:1])
            p = jnp.exp(s_scr[slot][hd] - mn)
            alpha = jnp.exp(m - mn)
            vt = vt_ref[hd * MLA_V:(hd + 1) * MLA_V, pl.ds(k0, TQ)]
            out.append((mn, alpha * l + p.sum(axis=0, keepdims=True),
                        alpha * acc + _dot(vt, p.astype(BF16))))
        return tuple(out)

    st = tuple((jnp.full((1, TQ), NEG_INF, F32), jnp.zeros((1, TQ), F32), jnp.zeros((MLA_V, TQ), F32))
               for _ in range(MLA_HEADS))
    st = _block_pipeline(i, i, score_plain, [(i, lambda slot: scores_to(slot, i, kch <= qch))], softmax_pv, st)
    o_ref[0] = _group_norm_t(jnp.concatenate([acc / l for _, l, acc in st], axis=0), g_ref[...])


def _latent_attention(mq, mk, mvt, g):
    B, S, _ = mq.shape
    TQ = TQ_MLA
    W = GROUP_W
    return pl.pallas_call(
        _mla_kernel,
        grid=(B, S // TQ),
        in_specs=[pl.BlockSpec((1, TQ, mq.shape[2]), lambda b, i: (b, i, 0)),
                  pl.BlockSpec((1, S, mk.shape[2]), lambda b, i: (b, 0, 0)),
                  pl.BlockSpec((W, S), lambda b, i: (0, b)),
                  pl.BlockSpec((1, W), lambda b, i: (0, 0))],
        out_specs=pl.BlockSpec((1, TQ, W), lambda b, i: (b, i, 0)),
        out_shape=jax.ShapeDtypeStruct((B, S, W), BF16),
        scratch_shapes=[pltpu.VMEM((MLA_HEADS, TQ, TQ), F32)] * 2 + [pltpu.VMEM((MLA_HEADS, 8, TQ), F32)] * 2,
        compiler_params=_cparams(2),
        name="latent_attention",
    )(mq, mk, mvt, g)


def _ffn_kernel(ya_ref, yb_ref, yc_ref, yd_ref, x_ref, mod_ref, wout_ref, gffn_ref, w1_ref, w3_ref, w2_ref,
                gfin_ref, o_ref, acc_scr, *, final):
    gt1 = mod_ref[0, 2:3, :]
    sh2 = mod_ref[0, 3:4, :]
    sc2 = mod_ref[0, 4:5, :]
    gt2 = mod_ref[0, 5:6, :]
    attn = _dot(ya_ref[...], wout_ref[0:GROUP_W, :])
    for gi, y_ref in enumerate((yb_ref, yc_ref, yd_ref), start=1):
        attn = attn + _dot(y_ref[...], wout_ref[gi * GROUP_W:(gi + 1) * GROUP_W, :])
    x1 = x_ref[...] + gt1 * attn
    h = (_rms(x1, gffn_ref[...]) * (1.0 + sc2) + sh2).astype(BF16)
    for ci in range(D_FF // FF_CHUNK):
        cols = slice(ci * FF_CHUNK, (ci + 1) * FF_CHUNK)
        a = _dot(h, w1_ref[:, cols])
        gate = (a * jax.nn.sigmoid(a) * _dot(h, w3_ref[:, cols])).astype(BF16)
        part = _dot(gate, w2_ref[cols, :])
        if ci == 0:
            acc_scr[...] = part
        else:
            acc_scr[...] += part
    x2 = x1 + gt2 * acc_scr[...]
    o_ref[...] = _rms(x2, gfin_ref[...]) if final else x2


def _out_ffn(ys, x2, mod, wout, gffn, w1, w3, w2, gfin, S, final):
    N, D = x2.shape
    TM = TM_FFN
    nt = S // TM

    def full(a):
        return pl.BlockSpec(a.shape, lambda i: (0,) * a.ndim, pipeline_mode=pl.Buffered(1))

    def tok(w):
        return pl.BlockSpec((TM, w), lambda i: (i, 0))

    return pl.pallas_call(
        functools.partial(_ffn_kernel, final=final),
        grid=(N // TM,),
        in_specs=[tok(GROUP_W)] * 4 + [tok(D), pl.BlockSpec((1, 6, D), lambda i: (i // nt, 0, 0)),
                                       full(wout), full(gffn), full(w1), full(w3), full(w2), full(gfin)],
        out_specs=tok(D),
        out_shape=jax.ShapeDtypeStruct((N, D), F32),
        scratch_shapes=[pltpu.VMEM((TM, D), F32)],
        compiler_params=_cparams(1),
        name="out_ffn_final" if final else "out_ffn",
    )(*ys, x2, mod, wout, gffn, w1, w3, w2, gfin)


def _t5_bucket(rel):
    nb = T5_BUCKETS // 2
    max_exact = nb // 2
    ret = jnp.where(rel > 0, nb, 0)
    n = jnp.abs(rel)
    nf = jnp.maximum(n, 1).astype(jnp.float32)
    large = max_exact + (jnp.log(nf / max_exact) / math.log(T5_MAX_DIST / max_exact)
                         * (nb - max_exact)).astype(jnp.int32)
    large = jnp.minimum(large, nb - 1)
    return ret + jnp.where(n < max_exact, n, large)


def _rope_tables(S):
    half = MLA_ROPE // 2
    freqs = ROPE_BASE ** (-jnp.arange(half, dtype=F32) / half)
    ang = jnp.arange(S, dtype=jnp.int32).astype(F32)[:, None] * freqs[None, :]
    cos, sin = jnp.cos(ang), jnp.sin(ang)
    cos2 = jnp.concatenate([cos, cos], axis=1)
    sin2 = jnp.concatenate([-sin, sin], axis=1)
    zeros = jnp.zeros((S, LANES - MLA_NOPE - MLA_ROPE), F32)
    scale = (MLA_NOPE + MLA_ROPE) ** -0.5
    cosq = jnp.concatenate([jnp.full((S, MLA_NOPE), scale, F32), cos2 * scale, zeros], axis=1)
    sinq = jnp.concatenate([jnp.zeros((S, MLA_NOPE), F32), sin2 * scale, zeros], axis=1)
    cosk = jnp.concatenate([jnp.zeros((S, MLA_NOPE), F32), cos2, zeros], axis=1)
    sink = jnp.concatenate([jnp.zeros((S, MLA_NOPE), F32), sin2, zeros], axis=1)
    return cosq, sinq, cosk, sink


def _pack_in_weight(w):
    part = {n: w[:, IN_OFFS[k]:IN_OFFS[k + 1]] for k, n in enumerate(
        ('pool_u', 'ca_q', 'ca_k', 'ca_v', 'sa_q', 'sa_k', 'sa_v', 'idx_q', 'idx_k', 'idx_w',
         'mla_cq', 'mla_ckv', 'mla_kr'))}
    D = w.shape[0]
    z = lambda n: jnp.zeros((D, n), F32)
    qscale = HEAD_DIM ** -0.5
    saq = part['sa_q'].reshape(D, SA_HEADS, HEAD_DIM) * qscale
    saq = jnp.concatenate([saq, jnp.zeros_like(saq)], axis=2).reshape(D, SA_HEADS * LANES)
    kr = part['mla_kr']
    kr_swap = jnp.concatenate([kr[:, MLA_ROPE // 2:], kr[:, :MLA_ROPE // 2]], axis=1)
    pad_r = LANES - MLA_NOPE - MLA_ROPE
    cols = [part['pool_u'], part['ca_q'] * qscale, part['ca_k'], saq,
            part['sa_k'], part['sa_v'], part['idx_q'],
            part['idx_k'], z(IDX_DIM), z(IDX_DIM), part['idx_k'],
            part['mla_cq'], part['mla_ckv'],
            z(MLA_NOPE), kr, z(pad_r), z(MLA_NOPE), kr_swap, z(pad_r)]
    out = jnp.concatenate(cols, axis=1)
    assert out.shape[1] == C_END
    wt = jnp.concatenate([part['idx_w'].T, jnp.zeros((IWT_ROWS - IDX_HEADS, D), F32), part['sa_v'].T,
                          part['ca_v'].T], axis=0)
    return out.astype(BF16), wt.astype(BF16)


def _pack_mla_weights(w_uq, w_ukv):
    R = w_uq.shape[0]
    pad = jnp.zeros((R, MLA_HEADS, LANES - MLA_NOPE - MLA_ROPE), F32)
    rope_w = w_uq[:, :, MLA_NOPE:]
    rope_sw = jnp.concatenate([rope_w[:, :, MLA_ROPE // 2:], rope_w[:, :, :MLA_ROPE // 2]], axis=2)
    wq = jnp.concatenate([w_uq, pad], axis=2).reshape(R, MLA_HEADS * LANES)
    wqs = jnp.concatenate([jnp.zeros((R, MLA_HEADS, MLA_NOPE), F32), rope_sw, pad],
                          axis=2).reshape(R, MLA_HEADS * LANES)
    Rk = w_ukv.shape[0]
    wk = jnp.concatenate([w_ukv[:, :, :MLA_NOPE], jnp.zeros((Rk, MLA_HEADS, LANES - MLA_NOPE), F32)],
                         axis=2).reshape(Rk, MLA_HEADS * LANES)
    wvt = w_ukv[:, :, MLA_NOPE:].reshape(Rk, MLA_HEADS * MLA_V).T
    return wq.astype(BF16), wqs.astype(BF16), wk.astype(BF16), wvt.astype(BF16)


def _toeplitz(vec, rows, cols):
    L = vec.shape[-1]
    assert cols <= L - 1
    flat = jnp.tile(vec, (1, rows))[:, :rows * (L - 1)]
    return flat.reshape(vec.shape[0], rows, L - 1)[:, :, :cols]


def _signed_mod_range(L, hi):
    d = np.arange(L)
    return np.where(d <= hi, d, d - L)


def _band_bias(rel_table):
    L = CA_WIN + TQ_CA
    e = _signed_mod_range(L, TQ_CA - 1)
    ridx = np.clip(CA_LEFT_CHUNKS * CHUNK + e, -(CHUNK - 1), CA_MAX_REL) + (CHUNK - 1)
    bias = _toeplitz(rel_table[:, ridx].astype(F32), CA_WIN, TQ_CA)
    kc = np.arange(CA_WIN)[:, None] // CHUNK
    qc = np.arange(TQ_CA)[None, :] // CHUNK + CA_LEFT_CHUNKS
    valid = (kc <= qc) & (kc >= qc - CA_LEFT_CHUNKS)
    return jnp.where(valid[None], bias, NEG_INF)


def _t5_bias(t5_table):
    TQ = TQ_SA
    L = 3 * TQ
    e = _signed_mod_range(L, TQ - 1)
    rel = jnp.asarray(-e - TQ, jnp.int32)
    far = t5_table[_t5_bucket(jnp.int32(-(TQ + 1)))].astype(F32)
    vec = (t5_table[_t5_bucket(rel)].astype(F32) - far[None, :]).T
    return _toeplitz(vec, 2 * TQ, TQ)


def kernel(x, c, t5_table, w_mod, b_mod, g_mix, w_in, pool_w, pool_scale, ca_rel, mla_g_cq, mla_g_ckv,
           mla_w_uq, mla_w_ukv, g_group, w_out, g_ffn, ffn_w1, ffn_w3, ffn_w2, g_final):
    B, S, D = x.shape
    assert D == D_MODEL and S % TM_PROJ == 0 and S % TQ_SA == 0 and S >= 4 * TOPK_MAX
    N = B * S
    mod_all = _modulation(c, w_mod, b_mod)
    rope_tabs = _rope_tables(S)
    nbias = _t5_bias(t5_table)
    row = lambda v: v.reshape(1, -1).astype(F32)
    x2 = x.reshape(N, D)
    for l in range(DEPTH):
        mod = mod_all[l].reshape(B, 6, D)
        w1, wt = _pack_in_weight(w_in[l])
        wq, wqs, wk, wvt = _pack_mla_weights(mla_w_uq[l], mla_w_ukv[l])
        (pool_u, ca, saq, sakv, iq, ik, iwt, svt, cavt, mq, mk, mvt) = _inproj(
            x2, mod, row(g_mix[l]), w1, wt, row(mla_g_cq[l]), row(mla_g_ckv[l]), wq, wqs, wk, wvt, rope_tabs, S)
        gg = g_group[l].reshape(4, 1, GROUP_W).astype(F32)
        wbd = jax.scipy.linalg.block_diag(*[pool_w[l, gi] for gi in range(len(POOL_WINDOWS))]).astype(BF16)
        bsw = lambda a: a.reshape(B, S, a.shape[-1])
        y_a = _pool(bsw(pool_u), wbd, row(pool_scale[l]), gg[0])
        y_b = _chunk_attention(bsw(ca), cavt, _band_bias(ca_rel[l]), gg[1])
        y_c = _sparse_attention(bsw(saq), bsw(sakv), svt, bsw(iq), bsw(ik), iwt, nbias, gg[2])
        y_d = _latent_attention(bsw(mq), bsw(mk), mvt, gg[3])
        ys = [y.reshape(N, GROUP_W) for y in (y_a, y_b, y_c, y_d)]
        x2 = _out_ffn(ys, x2, mod, w_out[l].astype(BF16), row(g_ffn[l]), ffn_w1[l].astype(BF16),
                      ffn_w3[l].astype(BF16), ffn_w2[l].astype(BF16), row(g_final), S,
                      final=(l == DEPTH - 1))
    return x2.reshape(B, S, D)
```

```python
import functools
import math
from statistics import NormalDist

import jax
import jax.numpy as jnp
from jax import lax
import numpy as np
from jax.experimental import pallas as pl
from jax.experimental.pallas import tpu as pltpu

F32 = jnp.float32
BF16 = jnp.bfloat16

D_MODEL = 1024
DEPTH = 2
CHUNK = 64
EPS = 1e-6
NEG_INF = -1e30
GROUP_W = 256
HEAD_DIM = 64
POOL_WINDOWS = (2, 4, 8, 16)
POOL_HALO = 16
CA_HEADS = 4
CA_LEFT_CHUNKS = 8
CA_MAX_REL = 256
SA_HEADS = 4
IDX_HEADS = 8
IDX_DIM = 64
TOPK_MAX = 256
MLA_HEADS = 4
MLA_NOPE = 64
MLA_ROPE = 32
MLA_V = 64
ROPE_BASE = 10000.0
T5_BUCKETS = 32
T5_MAX_DIST = 128
D_FF = 2816
IN_WIDTHS = (256, 256, 256, 256, 256, 64, 64, 512, 64, 8, 256, 128, 32)
IN_OFFS = tuple(int(v) for v in np.cumsum((0,) + IN_WIDTHS))

LANES = 128
VMEM_LIMIT = 56 * 1024 * 1024

TM_PROJ = 512
TM_FFN = 512
TP_POOL = 512
TQ_CA = 256
CA_WIN = TQ_CA + CA_LEFT_CHUNKS * CHUNK
CA_NBLK = CA_WIN // TQ_CA
IWT_ROWS = 16
TQ_SA = 256
KB_SA = 256
COUNT_CHAINS = 2
SEARCH_FIRST_ROUND = 16
SEARCH_ROUND = 4
GUESS_SPREAD = 0.3
PEEL_FIRST = 14
PEEL_EVERY = 4
TQ_MLA = 256
FF_CHUNK = 256

C_POOL = 0
C_CA = C_POOL + 256
C_SAQ = C_CA + 2 * GROUP_W
C_SAKV = C_SAQ + SA_HEADS * LANES
C_IQ = C_SAKV + LANES
C_IK = C_IQ + IDX_HEADS * IDX_DIM
C_CQ = C_IK + 2 * LANES
C_CKV = C_CQ + 256
C_KRF = C_CKV + LANES
C_KRS = C_KRF + LANES
C_END = C_KRS + LANES

INT_MIN = -2 ** 31
KEY_ALL = INT_MIN - int(np.array(-np.inf, np.float32).view(np.int32)) + 1


def _cparams(n_axes):
    return pltpu.CompilerParams(dimension_semantics=("arbitrary",) * n_axes,
                                vmem_limit_bytes=VMEM_LIMIT)


def _rms(x, g):
    return x * lax.rsqrt(jnp.mean(x * x, axis=-1, keepdims=True) + EPS) * g


def _dot(a, b):
    return jnp.dot(a, b, preferred_element_type=F32)


def _dot_t(a, b):
    return lax.dot_general(a, b, (((1,), (1,)), ((), ())), preferred_element_type=F32)


def _mod_kernel(c_ref, w_ref, b_ref, o_ref):
    c = c_ref[...]
    act = c * jax.nn.sigmoid(c)
    o_ref[0] = jnp.dot(act, w_ref[0], precision=lax.Precision.HIGHEST,
                       preferred_element_type=F32) + b_ref[0]


def _modulation(c, w_mod, b_mod):
    L, D, W = w_mod.shape
    B = c.shape[0]
    nj = W // D
    return pl.pallas_call(
        _mod_kernel,
        grid=(L, nj),
        in_specs=[pl.BlockSpec((B, D), lambda l, j: (0, 0)),
                  pl.BlockSpec((1, D, D), lambda l, j: (l, 0, j)),
                  pl.BlockSpec((1, 1, D), lambda l, j: (l, 0, j))],
        out_specs=pl.BlockSpec((1, B, D), lambda l, j: (l, 0, j)),
        out_shape=jax.ShapeDtypeStruct((L, B, W), F32),
        compiler_params=_cparams(2),
        name="modulation",
    )(c, w_mod, b_mod.reshape(L, 1, W))


def _inproj_kernel(x_ref, mod_ref, gmix_ref, w_ref, wt_ref, gcq_ref, gckv_ref, wq_ref, wqs_ref, wk_ref, wvt_ref,
                   cosq_ref, sinq_ref, cosk_ref, sink_ref,
                   pool_o, ca_o, saq_o, sakv_o, iq_o, ik_o, iwt_o, svt_o, cavt_o, mq_o, mk_o, mvt_o):
    sh1 = mod_ref[0, 0:1, :]
    sc1 = mod_ref[0, 1:2, :]
    h = (_rms(x_ref[...], gmix_ref[...]) * (1.0 + sc1) + sh1).astype(BF16)

    def seg(a, b):
        return _dot(h, w_ref[:, a:b])

    pool_o[...] = seg(C_POOL, C_CA)
    ca_o[...] = seg(C_CA, C_SAQ).astype(BF16)
    saq_o[...] = seg(C_SAQ, C_SAKV).astype(BF16)
    sakv_o[...] = seg(C_SAKV, C_IQ).astype(BF16)
    iq_o[...] = seg(C_IQ, C_IK).astype(BF16)
    ik_o[...] = seg(C_IK, C_CQ).astype(BF16)
    tr = _dot_t(wt_ref[...], h)
    iwt_o[...] = tr[0:IWT_ROWS] * ((IDX_HEADS ** -0.5) * (IDX_DIM ** -0.5))
    svt_o[...] = tr[IWT_ROWS:IWT_ROWS + HEAD_DIM].astype(BF16)
    cavt_o[...] = tr[IWT_ROWS + HEAD_DIM:].astype(BF16)

    qn = _rms(seg(C_CQ, C_CKV), gcq_ref[...]).astype(BF16)
    qf = _dot(qn, wq_ref[...])
    qs = _dot(qn, wqs_ref[...])
    cosq = jnp.concatenate([cosq_ref[...]] * MLA_HEADS, axis=1)
    sinq = jnp.concatenate([sinq_ref[...]] * MLA_HEADS, axis=1)
    mq_o[...] = (qf * cosq + qs * sinq).astype(BF16)

    kvn = _rms(seg(C_CKV, C_KRF), gckv_ref[...]).astype(BF16)
    kvf = _dot(kvn, wk_ref[...])
    krope = seg(C_KRF, C_KRS) * cosk_ref[...] + seg(C_KRS, C_END) * sink_ref[...]
    for hd in range(MLA_HEADS):
        mk_o[:, hd * LANES:(hd + 1) * LANES] = (kvf[:, hd * LANES:(hd + 1) * LANES] + krope).astype(BF16)
    mvt_o[...] = _dot_t(wvt_ref[...], kvn).astype(BF16)


def _inproj(x2, mod, gmix, w1, wt, gcq, gckv, wq, wqs, wk, wvt, rope_tabs, S):
    N, D = x2.shape
    TM = TM_PROJ
    nt = S // TM
    cosq, sinq, cosk, sink = rope_tabs

    def full(a):
        return pl.BlockSpec(a.shape, lambda i: (0,) * a.ndim)

    def tok(w):
        return pl.BlockSpec((TM, w), lambda i: (i, 0))

    tab = pl.BlockSpec((TM, LANES), lambda i: (i % nt, 0))
    def tokt(rows):
        return pl.BlockSpec((rows, TM), lambda i: (0, i))

    outs = [(C_CA - C_POOL, F32, True), (C_SAQ - C_CA, BF16, True), (C_SAKV - C_SAQ, BF16, True),
            (C_IQ - C_SAKV, BF16, True), (C_IK - C_IQ, BF16, True), (C_CQ - C_IK, BF16, True),
            (IWT_ROWS, F32, False), (HEAD_DIM, BF16, False), (GROUP_W, BF16, False),
            (MLA_HEADS * LANES, BF16, True), (MLA_HEADS * LANES, BF16, True), (GROUP_W, BF16, False)]
    return pl.pallas_call(
        _inproj_kernel,
        grid=(N // TM,),
        in_specs=[tok(D),
                  pl.BlockSpec((1, 6, D), lambda i: (i // nt, 0, 0)),
                  full(gmix), full(w1), full(wt), full(gcq), full(gckv), full(wq), full(wqs), full(wk), full(wvt),
                  tab, tab, tab, tab],
        out_specs=[tok(w) if tm else tokt(w) for w, _, tm in outs],
        out_shape=[jax.ShapeDtypeStruct((N, w) if tm else (w, N), dt) for w, dt, tm in outs],
        compiler_params=_cparams(1),
        name="inproj",
    )(x2, mod, gmix, w1, wt, gcq, gckv, wq, wqs, wk, wvt, cosq, sinq, cosk, sink)


def _pool_kernel(u_ref, halo_ref, w_ref, scale_ref, g_ref, o_ref, pad_scr):
    i = pl.program_id(1)
    TP = u_ref.shape[1]
    u = u_ref[0]
    pad_scr[0:POOL_HALO, :] = jnp.where(i > 0, halo_ref[0], 0.0)
    pad_scr[POOL_HALO:, :] = u

    def shifted(j):
        return pad_scr[POOL_HALO - j:POOL_HALO - j + TP, :]

    lane = lax.broadcasted_iota(jnp.int32, (TP, GROUP_W), 1)
    w2 = u + shifted(1)
    w4 = w2 + shifted(2) + shifted(3)
    w8 = w4
    for j in range(4, 8):
        w8 = w8 + shifted(j)
    w16 = w8
    for j in range(8, 16):
        w16 = w16 + shifted(j)
    win = jnp.where(lane < 64, w2, jnp.where(lane < 128, w4, jnp.where(lane < 192, w8, w16)))
    wlen = jnp.where(lane < 64, 2, jnp.where(lane < 128, 4, jnp.where(lane < 192, 8, 16)))
    t = i * TP + lax.broadcasted_iota(jnp.int32, (TP, GROUP_W), 0)
    cnt = jnp.minimum(t + 1, wlen).astype(F32)
    d = (win / cnt - u).astype(BF16)
    y = _dot(d, w_ref[...]) * scale_ref[...]
    o_ref[0] = _rms(y, g_ref[...]).astype(BF16)


def _pool(u, wbd, scale, g):
    B, S, W = u.shape
    TP = TP_POOL
    hb = TP // POOL_HALO
    return pl.pallas_call(
        _pool_kernel,
        grid=(B, S // TP),
        in_specs=[pl.BlockSpec((1, TP, W), lambda b, i: (b, i, 0)),
                  pl.BlockSpec((1, POOL_HALO, W), lambda b, i: (b, jnp.maximum(i * hb - 1, 0), 0)),
                  pl.BlockSpec((W, W), lambda b, i: (0, 0)),
                  pl.BlockSpec((1, W), lambda b, i: (0, 0)),
                  pl.BlockSpec((1, W), lambda b, i: (0, 0))],
        out_specs=pl.BlockSpec((1, TP, W), lambda b, i: (b, i, 0)),
        out_shape=jax.ShapeDtypeStruct((B, S, W), BF16),
        scratch_shapes=[pltpu.VMEM((POOL_HALO + TP, W), F32)],
        compiler_params=_cparams(2),
        name="pool_mixer",
    )(u, u, wbd, scale, g)


def _ca_kernel(q_ref, k_ref, vt_ref, bias_ref, g_ref, o_ref):
    i = pl.program_id(1)
    TQ = TQ_CA
    lane = lax.broadcasted_iota(jnp.int32, (TQ, LANES), 1)
    starts = []
    for j in range(CA_NBLK):
        kb = i - (CA_NBLK - 1) + j
        starts.append((kb >= 0, pl.multiple_of(jnp.maximum(kb, 0) * TQ, TQ)))
    scored = []
    for hd in range(CA_HEADS):
        cols = slice((hd // 2) * LANES, (hd // 2 + 1) * LANES)
        keep = (lane < HEAD_DIM) if hd % 2 == 0 else (lane >= HEAD_DIM)
        qh = jnp.where(keep, q_ref[0, :, cols].astype(F32), 0.0).astype(BF16)
        parts = []
        for j, (present, start) in enumerate(starts):
            s = _dot_t(k_ref[0, pl.ds(start, TQ), cols], qh) + bias_ref[hd, j * TQ:(j + 1) * TQ, :]
            parts.append(jnp.where(present, s, NEG_INF))
        m = parts[0].max(axis=0, keepdims=True)
        for s in parts[1:]:
            m = jnp.maximum(m, s.max(axis=0, keepdims=True))
        scored.append((parts, m))
    outs = []
    for hd, (parts, m) in enumerate(scored):
        l = jnp.zeros((1, TQ), F32)
        acc = jnp.zeros((HEAD_DIM, TQ), F32)
        for j, (_, start) in enumerate(starts):
            p = jnp.exp(parts[j] - m)
            l = l + p.sum(axis=0, keepdims=True)
            acc = acc + _dot(vt_ref[hd * HEAD_DIM:(hd + 1) * HEAD_DIM, pl.ds(start, TQ)], p.astype(BF16))
        outs.append(acc / l)
    o_ref[0] = _group_norm_t(jnp.concatenate(outs, axis=0), g_ref[...])


def _chunk_attention(caqk, cavt, bias, g):
    B, S, _ = caqk.shape
    W = GROUP_W
    TQ = TQ_CA
    return pl.pallas_call(
        _ca_kernel,
        grid=(B, S // TQ),
        in_specs=[pl.BlockSpec((1, TQ, W), lambda b, i: (b, i, 0)),
                  pl.BlockSpec((1, S, W), lambda b, i: (b, 0, 1)),
                  pl.BlockSpec((W, S), lambda b, i: (0, b)),
                  pl.BlockSpec(bias.shape, lambda b, i: (0, 0, 0)),
                  pl.BlockSpec((1, W), lambda b, i: (0, 0))],
        out_specs=pl.BlockSpec((1, TQ, W), lambda b, i: (b, i, 0)),
        out_shape=jax.ShapeDtypeStruct((B, S, W), BF16),
        compiler_params=_cparams(2),
        name="band_attention",
    )(caqk, caqk, cavt, bias, g)


def _score_key(score):
    b = lax.bitcast_convert_type(score, jnp.int32)
    return jnp.where(b < 0, jnp.int32(INT_MIN) - b, b)


def _sa_kernel(q_ref, kv_ref, vt_ref, iq_ref, ik_ref, iwt_ref, zq_ref, nbias_ref, g_ref, o_ref,
               key_scr, s0_scr, s1_scr, smax0_scr, smax1_scr):
    s_scr, smax_scr = (s0_scr, s1_scr), (smax0_scr, smax1_scr)
    i = pl.program_id(1)
    TQ, KB = TQ_SA, KB_SA
    K = float(TOPK_MAX)
    nb = i + 1
    q0 = i * TQ
    cshift = CHUNK.bit_length() - 1
    kchunk = lax.broadcasted_iota(jnp.int32, (KB, TQ), 0) >> cshift
    qchunk = (q0 + lax.broadcasted_iota(jnp.int32, (1, TQ), 1)) >> cshift

    iwt = iwt_ref[...]

    def score_block(j, carry):
        smax, s1, s2 = carry
        k0 = pl.multiple_of(j * KB, KB)
        ik = ik_ref[0, pl.ds(k0, KB), :]
        ik2 = jnp.concatenate([ik[:, :LANES], ik[:, LANES:]], axis=0)
        sc = jnp.zeros((KB, TQ), F32)
        for p in range(IDX_HEADS // 2):
            logits = _dot_t(ik2, iq_ref[0, :, p * LANES:(p + 1) * LANES])
            sc = sc + iwt[2 * p:2 * p + 1, :] * jnp.maximum(logits[:KB], 0.0)
            sc = sc + iwt[2 * p + 1:2 * p + 2, :] * jnp.maximum(logits[KB:], 0.0)
        adm = kchunk <= qchunk - (k0 >> cshift)
        sc = jnp.where(adm, sc, -jnp.inf)
        key_scr[pl.ds(k0, KB), :] = _score_key(sc)
        smax = jnp.maximum(smax, sc.max(axis=0, keepdims=True))
        full = j < i
        s1 = jnp.where(full, s1 + sc.sum(axis=0, keepdims=True), s1)
        s2 = jnp.where(full, s2 + (sc * sc).sum(axis=0, keepdims=True), s2)
        return smax, s1, s2

    smax, s1, s2 = lax.fori_loop(
        0, (nb + 1) // 2, lambda j, c: score_block(2 * j + 1, score_block(2 * j, c)),
        (jnp.full((1, TQ), -jnp.inf, F32), jnp.zeros((1, TQ), F32), jnp.zeros((1, TQ), F32)))

    def count_ge(cand):
        def body(j, acc):
            blk = key_scr[pl.ds(pl.multiple_of(j * (2 * KB), 2 * KB), 2 * KB), :]
            ones = jnp.where(blk >= cand, 1.0, 0.0)
            return acc + ones.reshape(COUNT_CHAINS, -1, 8, TQ).sum(axis=1)
        acc = lax.fori_loop(0, (nb + 1) // 2, body, jnp.zeros((COUNT_CHAINS, 8, TQ), F32))
        return acc.sum(axis=0).sum(axis=0, keepdims=True)

    def max_below(bound):
        def body(j, acc):
            blk = key_scr[pl.ds(pl.multiple_of(j * (2 * KB), 2 * KB), 2 * KB), :]
            below = jnp.where(blk < bound, blk, jnp.int32(INT_MIN))
            return jnp.maximum(acc, below.reshape(COUNT_CHAINS, -1, 8, TQ).max(axis=1))
        acc = lax.fori_loop(0, (nb + 1) // 2, body, jnp.full((COUNT_CHAINS, 8, TQ), INT_MIN, jnp.int32))
        return acc.max(axis=0).max(axis=0, keepdims=True)

    def search():
        def unkey(k):
            return lax.bitcast_convert_type(jnp.where(k < 0, jnp.int32(INT_MIN) - k, k), F32)

        def is_active(lo, hi, clo):
            return jnp.logical_and(clo > K, hi > lo + 1)

        def cond(st):
            _, lo, hi, clo, _ = st
            act = jnp.where(is_active(lo, hi, clo), 1.0, 0.0)
            return jnp.max(jnp.maximum(act[:, :LANES], act[:, LANES:])) > 0.0

        n_full = (i * KB).astype(F32)
        mean = s1 / n_full
        std = jnp.sqrt(jnp.maximum(s2 / n_full - mean * mean, 0.0))
        zq = jnp.max(zq_ref[...], axis=0, keepdims=True)
        guess_lo = _score_key(mean + (zq - GUESS_SPREAD) * std)
        guess_hi = _score_key(mean + (zq + GUESS_SPREAD) * std)

        def step(_, st):
            it, lo, hi, clo, chi = st
            active = is_active(lo, hi, clo)
            lf, hf = unkey(lo), unkey(hi)
            lc = jnp.log(clo)
            frac = jnp.clip((lc - math.log(K - 0.5)) / (lc - jnp.log(jnp.maximum(chi, 0.5))), 0.05, 0.95)
            cand = _score_key(lf + frac * (hf - lf))
            cand = jnp.where(it % 3 == 2, (lo >> 1) + (hi >> 1) + (lo & hi & 1), cand)
            cand = jnp.where(it == 0, guess_lo, cand)
            cand = jnp.where(it == 1, guess_hi, cand)
            cand = jnp.clip(cand, lo + 1, hi - 1)
            peel = jnp.logical_and(it >= PEEL_FIRST, (it - PEEL_FIRST) % PEEL_EVERY == 0)
            cand = lax.cond(peel, lambda: max_below(hi), lambda: cand)
            cand = jnp.where(active, cand, lo)
            cnt = count_ge(cand)
            up = jnp.logical_and(active, cnt >= K)
            down = jnp.logical_and(active, cnt < K)
            hi = jnp.where(down, cand, jnp.where(jnp.logical_and(up, peel), cand + 1, hi))
            return (it + 1, jnp.where(up, cand, lo), hi, jnp.where(up, cnt, clo), jnp.where(down, cnt, chi))

        lo0 = jnp.full((1, TQ), KEY_ALL - 1, jnp.int32)
        hi0 = _score_key(smax) + 1
        clo0 = jnp.zeros((1, TQ), F32) + ((nb + 1) // 2 * (2 * KB)).astype(F32)
        st = (jnp.int32(0), lo0, hi0, clo0, jnp.zeros((1, TQ), F32))
        st = lax.fori_loop(0, SEARCH_FIRST_ROUND, step, st)
        st = lax.while_loop(cond, lambda s: lax.fori_loop(0, SEARCH_ROUND, step, s), st)
        return st[1], st[3]

    def no_search():
        return jnp.full((1, TQ), KEY_ALL, jnp.int32), jnp.full((1, TQ), K, F32)

    t, cnt_t = lax.cond(i > 0, search, no_search)
    t = jnp.maximum(t, KEY_ALL)

    @pl.when(jnp.max(cnt_t) > K)
    def _():
        allowed = K - count_ge(t + 1)
        r = lax.broadcasted_iota(jnp.int32, (KB, KB), 0)
        c = lax.broadcasted_iota(jnp.int32, (KB, KB), 1)
        earlier = jnp.where(c < r, 1.0, 0.0).astype(BF16)

        def body(j, seen):
            sl = pl.ds(pl.multiple_of(j * KB, KB), KB)
            blk = key_scr[sl, :]
            eq = jnp.where(blk == t, 1.0, 0.0)
            rank = _dot(earlier, eq.astype(BF16)) + seen
            demote = eq * jnp.where(rank >= allowed, 1.0, 0.0)
            key_scr[sl, :] = jnp.where(demote > 0.5, t - 1, blk)
            return seen + eq.sum(axis=0, keepdims=True)

        lax.fori_loop(0, nb, body, jnp.zeros((1, TQ), F32))

    def scores_to(slot, j, bias_rows, present):
        k0 = pl.multiple_of(j * KB, KB)
        kblk = kv_ref[0, pl.ds(k0, KB), :]
        sel = key_scr[pl.ds(k0, KB), :] >= (t if present is True else jnp.where(present, t, jnp.int32(2 ** 31 - 1)))
        for hd in range(SA_HEADS):
            s = _dot_t(kblk, q_ref[0, :, hd * LANES:(hd + 1) * LANES])
            if bias_rows is not None:
                s = s + nbias_ref[hd, bias_rows, :]
            s = jnp.where(sel, s, NEG_INF)
            s_scr[slot][hd] = s
            smax_scr[slot][hd] = jnp.broadcast_to(s.max(axis=0, keepdims=True), smax_scr[slot].shape[1:])

    def softmax_pv(slot, j, st):
        vt = vt_ref[:, pl.ds(pl.multiple_of(j * KB, KB), KB)]
        out = []
        for hd in range(SA_HEADS):
            m, l, acc = st[hd]
            mn = jnp.maximum(m, smax_scr[slot][hd][0:1])
            p = jnp.exp(s_scr[slot][hd] - mn)
            alpha = jnp.exp(m - mn)
            out.append((mn, alpha * l + p.sum(axis=0, keepdims=True), alpha * acc + _dot(vt, p.astype(BF16))))
        return tuple(out)

    st = tuple((jnp.full((1, TQ), NEG_INF, F32), jnp.zeros((1, TQ), F32), jnp.zeros((HEAD_DIM, TQ), F32))
               for _ in range(SA_HEADS))
    left = jnp.maximum(i - 1, 0)
    tails = [(left, lambda slot: scores_to(slot, left, slice(0, KB), i >= 1)),
             (i, lambda slot: scores_to(slot, i, slice(KB, 2 * KB), True))]
    st = _block_pipeline(left, i, lambda slot, j, present: scores_to(slot, j, None, present), tails, softmax_pv, st)
    y_t = jnp.concatenate([acc / l for _, l, acc in st], axis=0)
    o_ref[0] = _group_norm_t(y_t, g_ref[...])


def _sparse_attention(saq, sakv, svt, iq, ik, iwt, nbias, g):
    B, S, _ = saq.shape
    TQ = TQ_SA
    W = GROUP_W
    nt = S // TQ
    n_adm = (np.arange(S) // CHUNK + 1) * CHUNK
    zq = np.array([NormalDist().inv_cdf(1.0 - TOPK_MAX / n) if n > TOPK_MAX else 0.0 for n in n_adm], np.float32)
    zq = jnp.asarray(np.tile(zq[None, :], (8, 1)))
    return pl.pallas_call(
        _sa_kernel,
        grid=(B, nt),
        in_specs=[pl.BlockSpec((1, TQ, saq.shape[2]), lambda b, i: (b, i, 0)),
                  pl.BlockSpec((1, S, sakv.shape[2]), lambda b, i: (b, 0, 0)),
                  pl.BlockSpec((HEAD_DIM, S), lambda b, i: (0, b)),
                  pl.BlockSpec((1, TQ, iq.shape[2]), lambda b, i: (b, i, 0)),
                  pl.BlockSpec((1, S, ik.shape[2]), lambda b, i: (b, 0, 0)),
                  pl.BlockSpec((IWT_ROWS, TQ), lambda b, i: (0, b * nt + i)),
                  pl.BlockSpec((8, TQ), lambda b, i: (0, i)),
                  pl.BlockSpec(nbias.shape, lambda b, i: (0, 0, 0)),
                  pl.BlockSpec((1, W), lambda b, i: (0, 0))],
        out_specs=pl.BlockSpec((1, TQ, W), lambda b, i: (b, i, 0)),
        out_shape=jax.ShapeDtypeStruct((B, S, W), BF16),
        scratch_shapes=([pltpu.VMEM((S, TQ), jnp.int32)] + [pltpu.VMEM((SA_HEADS, KB_SA, TQ), F32)] * 2
                        + [pltpu.VMEM((SA_HEADS, 8, TQ), F32)] * 2),
        compiler_params=_cparams(2),
        name="sparse_attention",
    )(saq, sakv, svt, iq, ik, iwt, zq, nbias, g)


def _group_norm_t(y_t, g):
    inv = lax.rsqrt(jnp.mean(y_t * y_t, axis=0, keepdims=True) + EPS)
    return ((y_t * inv).T * g).astype(BF16)


def _block_pipeline(n_plain, last_blk, score_plain, tails, softmax, st):
    off = n_plain % 2
    n_loop = jnp.maximum((n_plain + off) // 2 - 1, 0)

    def blk(pos):
        return jnp.clip(pos - off, 0, last_blk)

    def body(pp, st):
        pos = 2 * pp
        score_plain(1, blk(pos + 1), True)
        st = softmax(0, blk(pos), st)
        score_plain(0, blk(pos + 2), True)
        return softmax(1, blk(pos + 1), st)

    score_plain(0, blk(0), jnp.logical_and(n_plain >= 1, off == 0))
    st = lax.fori_loop(0, n_loop, body, st)
    e0 = 2 * n_loop
    slot, pending = 0, blk(e0)
    steps = [(blk(e0 + 1), lambda s: score_plain(s, blk(e0 + 1), n_plain >= 1))] + list(tails)
    for nxt, score_fn in steps:
        score_fn(1 - slot)
        st = softmax(slot, pending, st)
        slot, pending = 1 - slot, nxt
    return softmax(slot, pending, st)


def _mla_kernel(q_ref, k_ref, vt_ref, g_ref, o_ref, s0_scr, s1_scr, smax0_scr, smax1_scr):
    i = pl.program_id(1)
    TQ = TQ_MLA
    cshift = CHUNK.bit_length() - 1
    kch = lax.broadcasted_iota(jnp.int32, (TQ, TQ), 0) >> cshift
    qch = lax.broadcasted_iota(jnp.int32, (TQ, TQ), 1) >> cshift
    s_scr, smax_scr = (s0_scr, s1_scr), (smax0_scr, smax1_scr)

    def scores_to(slot, j, keep):
        k0 = pl.multiple_of(j * TQ, TQ)
        for hd in range(MLA_HEADS):
            cols = slice(hd * LANES, (hd + 1) * LANES)
            s = _dot_t(k_ref[0, pl.ds(k0, TQ), cols], q_ref[0, :, cols])
            if keep is not None:
                s = jnp.where(keep, s, NEG_INF)
            s_scr[slot][hd] = s
            smax_scr[slot][hd] = jnp.broadcast_to(s.max(axis=0, keepdims=True), smax_scr[slot].shape[1:])

    def score_plain(slot, j, present):
        scores_to(slot, j, None if present is True else kch >= jnp.where(present, 0, TQ))

    def softmax_pv(slot, j, st):
        k0 = pl.multiple_of(j * TQ, TQ)
        out = []
        for hd in range(MLA_HEADS):
            m, l, acc = st[hd]
            mn = jnp.maximum(m, smax_scr[slot][hd][0:1])
            p = jnp.exp(s_scr[slot][hd] - mn)
            alpha = jnp.exp(m - mn)
            vt = vt_ref[hd * MLA_V:(hd + 1) * MLA_V, pl.ds(k0, TQ)]
            out.append((mn, alpha * l + p.sum(axis=0, keepdims=True),
                        alpha * acc + _dot(vt, p.astype(BF16))))
        return tuple(out)

    st = tuple((jnp.full((1, TQ), NEG_INF, F32), jnp.zeros((1, TQ), F32), jnp.zeros((MLA_V, TQ), F32))
               for _ in range(MLA_HEADS))
    st = _block_pipeline(i, i, score_plain, [(i, lambda slot: scores_to(slot, i, kch <= qch))], softmax_pv, st)
    o_ref[0] = _group_norm_t(jnp.concatenate([acc / l for _, l, acc in st], axis=0), g_ref[...])


def _latent_attention(mq, mk, mvt, g):
    B, S, _ = mq.shape
    TQ = TQ_MLA
    W = GROUP_W
    return pl.pallas_call(
        _mla_kernel,
        grid=(B, S // TQ),
        in_specs=[pl.BlockSpec((1, TQ, mq.shape[2]), lambda b, i: (b, i, 0)),
                  pl.BlockSpec((1, S, mk.shape[2]), lambda b, i: (b, 0, 0)),
                  pl.BlockSpec((W, S), lambda b, i: (0, b)),
                  pl.BlockSpec((1, W), lambda b, i: (0, 0))],
        out_specs=pl.BlockSpec((1, TQ, W), lambda b, i: (b, i, 0)),
        out_shape=jax.ShapeDtypeStruct((B, S, W), BF16),
        scratch_shapes=[pltpu.VMEM((MLA_HEADS, TQ, TQ), F32)] * 2 + [pltpu.VMEM((MLA_HEADS, 8, TQ), F32)] * 2,
        compiler_params=_cparams(2),
        name="latent_attention",
    )(mq, mk, mvt, g)


def _ffn_kernel(ya_ref, yb_ref, yc_ref, yd_ref, x_ref, mod_ref, wout_ref, gffn_ref, w1_ref, w3_ref, w2_ref,
                gfin_ref, o_ref, acc_scr, *, final):
    gt1 = mod_ref[0, 2:3, :]
    sh2 = mod_ref[0, 3:4, :]
    sc2 = mod_ref[0, 4:5, :]
    gt2 = mod_ref[0, 5:6, :]
    attn = _dot(ya_ref[...], wout_ref[0:GROUP_W, :])
    for gi, y_ref in enumerate((yb_ref, yc_ref, yd_ref), start=1):
        attn = attn + _dot(y_ref[...], wout_ref[gi * GROUP_W:(gi + 1) * GROUP_W, :])
    x1 = x_ref[...] + gt1 * attn
    h = (_rms(x1, gffn_ref[...]) * (1.0 + sc2) + sh2).astype(BF16)
    for ci in range(D_FF // FF_CHUNK):
        cols = slice(ci * FF_CHUNK, (ci + 1) * FF_CHUNK)
        a = _dot(h, w1_ref[:, cols])
        gate = (a * jax.nn.sigmoid(a) * _dot(h, w3_ref[:, cols])).astype(BF16)
        part = _dot(gate, w2_ref[cols, :])
        if ci == 0:
            acc_scr[...] = part
        else:
            acc_scr[...] += part
    x2 = x1 + gt2 * acc_scr[...]
    o_ref[...] = _rms(x2, gfin_ref[...]) if final else x2


def _out_ffn(ys, x2, mod, wout, gffn, w1, w3, w2, gfin, S, final):
    N, D = x2.shape
    TM = TM_FFN
    nt = S // TM

    def full(a):
        return pl.BlockSpec(a.shape, lambda i: (0,) * a.ndim, pipeline_mode=pl.Buffered(1))

    def tok(w):
        return pl.BlockSpec((TM, w), lambda i: (i, 0))

    return pl.pallas_call(
        functools.partial(_ffn_kernel, final=final),
        grid=(N // TM,),
        in_specs=[tok(GROUP_W)] * 4 + [tok(D), pl.BlockSpec((1, 6, D), lambda i: (i // nt, 0, 0)),
                                       full(wout), full(gffn), full(w1), full(w3), full(w2), full(gfin)],
        out_specs=tok(D),
        out_shape=jax.ShapeDtypeStruct((N, D), F32),
        scratch_shapes=[pltpu.VMEM((TM, D), F32)],
        compiler_params=_cparams(1),
        name="out_ffn_final" if final else "out_ffn",
    )(*ys, x2, mod, wout, gffn, w1, w3, w2, gfin)


def _t5_bucket(rel):
    nb = T5_BUCKETS // 2
    max_exact = nb // 2
    ret = jnp.where(rel > 0, nb, 0)
    n = jnp.abs(rel)
    nf = jnp.maximum(n, 1).astype(jnp.float32)
    large = max_exact + (jnp.log(nf / max_exact) / math.log(T5_MAX_DIST / max_exact)
                         * (nb - max_exact)).astype(jnp.int32)
    large = jnp.minimum(large, nb - 1)
    return ret + jnp.where(n < max_exact, n, large)


def _rope_tables(S):
    half = MLA_ROPE // 2
    freqs = ROPE_BASE ** (-jnp.arange(half, dtype=F32) / half)
    ang = jnp.arange(S, dtype=jnp.int32).astype(F32)[:, None] * freqs[None, :]
    cos, sin = jnp.cos(ang), jnp.sin(ang)
    cos2 = jnp.concatenate([cos, cos], axis=1)
    sin2 = jnp.concatenate([-sin, sin], axis=1)
    zeros = jnp.zeros((S, LANES - MLA_NOPE - MLA_ROPE), F32)
    scale = (MLA_NOPE + MLA_ROPE) ** -0.5
    cosq = jnp.concatenate([jnp.full((S, MLA_NOPE), scale, F32), cos2 * scale, zeros], axis=1)
    sinq = jnp.concatenate([jnp.zeros((S, MLA_NOPE), F32), sin2 * scale, zeros], axis=1)
    cosk = jnp.concatenate([jnp.zeros((S, MLA_NOPE), F32), cos2, zeros], axis=1)
    sink = jnp.concatenate([jnp.zeros((S, MLA_NOPE), F32), sin2, zeros], axis=1)
    return cosq, sinq, cosk, sink


def _pack_in_weight(w):
    part = {n: w[:, IN_OFFS[k]:IN_OFFS[k + 1]] for k, n in enumerate(
        ('pool_u', 'ca_q', 'ca_k', 'ca_v', 'sa_q', 'sa_k', 'sa_v', 'idx_q', 'idx_k', 'idx_w',
         'mla_cq', 'mla_ckv', 'mla_kr'))}
    D = w.shape[0]
    z = lambda n: jnp.zeros((D, n), F32)
    qscale = HEAD_DIM ** -0.5
    saq = part['sa_q'].reshape(D, SA_HEADS, HEAD_DIM) * qscale
    saq = jnp.concatenate([saq, jnp.zeros_like(saq)], axis=2).reshape(D, SA_HEADS * LANES)
    kr = part['mla_kr']
    kr_swap = jnp.concatenate([kr[:, MLA_ROPE // 2:], kr[:, :MLA_ROPE // 2]], axis=1)
    pad_r = LANES - MLA_NOPE - MLA_ROPE
    cols = [part['pool_u'], part['ca_q'] * qscale, part['ca_k'], saq,
            part['sa_k'], part['sa_v'], part['idx_q'],
            part['idx_k'], z(IDX_DIM), z(IDX_DIM), part['idx_k'],
            part['mla_cq'], part['mla_ckv'],
            z(MLA_NOPE), kr, z(pad_r), z(MLA_NOPE), kr_swap, z(pad_r)]
    out = jnp.concatenate(cols, axis=1)
    assert out.shape[1] == C_END
    wt = jnp.concatenate([part['idx_w'].T, jnp.zeros((IWT_ROWS - IDX_HEADS, D), F32), part['sa_v'].T,
                          part['ca_v'].T], axis=0)
    return out.astype(BF16), wt.astype(BF16)


def _pack_mla_weights(w_uq, w_ukv):
    R = w_uq.shape[0]
    pad = jnp.zeros((R, MLA_HEADS, LANES - MLA_NOPE - MLA_ROPE), F32)
    rope_w = w_uq[:, :, MLA_NOPE:]
    rope_sw = jnp.concatenate([rope_w[:, :, MLA_ROPE // 2:], rope_w[:, :, :MLA_ROPE // 2]], axis=2)
    wq = jnp.concatenate([w_uq, pad], axis=2).reshape(R, MLA_HEADS * LANES)
    wqs = jnp.concatenate([jnp.zeros((R, MLA_HEADS, MLA_NOPE), F32), rope_sw, pad],
                          axis=2).reshape(R, MLA_HEADS * LANES)
    Rk = w_ukv.shape[0]
    wk = jnp.concatenate([w_ukv[:, :, :MLA_NOPE], jnp.zeros((Rk, MLA_HEADS, LANES - MLA_NOPE), F32)],
                         axis=2).reshape(Rk, MLA_HEADS * LANES)
    wvt = w_ukv[:, :, MLA_NOPE:].reshape(Rk, MLA_HEADS * MLA_V).T
    return wq.astype(BF16), wqs.astype(BF16), wk.astype(BF16), wvt.astype(BF16)


def _toeplitz(vec, rows, cols):
    L = vec.shape[-1]
    assert cols <= L - 1
    flat = jnp.tile(vec, (1, rows))[:, :rows * (L - 1)]
    return flat.reshape(vec.shape[0], rows, L - 1)[:, :, :cols]


def _signed_mod_range(L, hi):
    d = np.arange(L)
    return np.where(d <= hi, d, d - L)


def _band_bias(rel_table):
    L = CA_WIN + TQ_CA
    e = _signed_mod_range(L, TQ_CA - 1)
    ridx = np.clip(CA_LEFT_CHUNKS * CHUNK + e, -(CHUNK - 1), CA_MAX_REL) + (CHUNK - 1)
    bias = _toeplitz(rel_table[:, ridx].astype(F32), CA_WIN, TQ_CA)
    kc = np.arange(CA_WIN)[:, None] // CHUNK
    qc = np.arange(TQ_CA)[None, :] // CHUNK + CA_LEFT_CHUNKS
    valid = (kc <= qc) & (kc >= qc - CA_LEFT_CHUNKS)
    return jnp.where(valid[None], bias, NEG_INF)


def _t5_bias(t5_table):
    TQ = TQ_SA
    L = 3 * TQ
    e = _signed_mod_range(L, TQ - 1)
    rel = jnp.asarray(-e - TQ, jnp.int32)
    far = t5_table[_t5_bucket(jnp.int32(-(TQ + 1)))].astype(F32)
    vec = (t5_table[_t5_bucket(rel)].astype(F32) - far[None, :]).T
    return _toeplitz(vec, 2 * TQ, TQ)


def kernel(x, c, t5_table, w_mod, b_mod, g_mix, w_in, pool_w, pool_scale, ca_rel, mla_g_cq, mla_g_ckv,
           mla_w_uq, mla_w_ukv, g_group, w_out, g_ffn, ffn_w1, ffn_w3, ffn_w2, g_final):
    B, S, D = x.shape
    assert D == D_MODEL and S % TM_PROJ == 0 and S % TQ_SA == 0 and S >= 4 * TOPK_MAX
    N = B * S
    mod_all = _modulation(c, w_mod, b_mod)
    rope_tabs = _rope_tables(S)
    nbias = _t5_bias(t5_table)
    row = lambda v: v.reshape(1, -1).astype(F32)
    x2 = x.reshape(N, D)
    for l in range(DEPTH):
        mod = mod_all[l].reshape(B, 6, D)
        w1, wt = _pack_in_weight(w_in[l])
        wq, wqs, wk, wvt = _pack_mla_weights(mla_w_uq[l], mla_w_ukv[l])
        (pool_u, ca, saq, sakv, iq, ik, iwt, svt, cavt, mq, mk, mvt) = _inproj(
            x2, mod, row(g_mix[l]), w1, wt, row(mla_g_cq[l]), row(mla_g_ckv[l]), wq, wqs, wk, wvt, rope_tabs, S)
        gg = g_group[l].reshape(4, 1, GROUP_W).astype(F32)
        wbd = jax.scipy.linalg.block_diag(*[pool_w[l, gi] for gi in range(len(POOL_WINDOWS))]).astype(BF16)
        bsw = lambda a: a.reshape(B, S, a.shape[-1])
        y_a = _pool(bsw(pool_u), wbd, row(pool_scale[l]), gg[0])
        y_b = _chunk_attention(bsw(ca), cavt, _band_bias(ca_rel[l]), gg[1])
        y_c = _sparse_attention(bsw(saq), bsw(sakv), svt, bsw(iq), bsw(ik), iwt, nbias, gg[2])
        y_d = _latent_attention(bsw(mq), bsw(mk), mvt, gg[3])
        ys = [y.reshape(N, GROUP_W) for y in (y_a, y_b, y_c, y_d)]
        x2 = _out_ffn(ys, x2, mod, w_out[l].astype(BF16), row(g_ffn[l]), ffn_w1[l].astype(BF16),
                      ffn_w3[l].astype(BF16), ffn_w2[l].astype(BF16), row(g_final), S,
                      final=(l == DEPTH - 1))
    return x2.reshape(B, S, D)
```

```python
import functools
import math
from statistics import NormalDist

import jax
import jax.numpy as jnp
from jax import lax
import numpy as np
from jax.experimental import pallas as pl
from jax.experimental.pallas import tpu as pltpu

F32 = jnp.float32
BF16 = jnp.bfloat16

D_MODEL = 1024
DEPTH = 2
CHUNK = 64
EPS = 1e-6
NEG_INF = -1e30
GROUP_W = 256
HEAD_DIM = 64
POOL_WINDOWS = (2, 4, 8, 16)
POOL_HALO = 16
CA_HEADS = 4
CA_LEFT_CHUNKS = 8
CA_MAX_REL = 256
SA_HEADS = 4
IDX_HEADS = 8
IDX_DIM = 64
TOPK_MAX = 256
MLA_HEADS = 4
MLA_NOPE = 64
MLA_ROPE = 32
MLA_V = 64
ROPE_BASE = 10000.0
T5_BUCKETS = 32
T5_MAX_DIST = 128
D_FF = 2816
IN_WIDTHS = (256, 256, 256, 256, 256, 64, 64, 512, 64, 8, 256, 128, 32)
IN_OFFS = tuple(int(v) for v in np.cumsum((0,) + IN_WIDTHS))

LANES = 128
VMEM_LIMIT = 56 * 1024 * 1024

TM_PROJ = 512
TM_FFN = 512
TP_POOL = 512
TQ_CA = 256
CA_WIN = TQ_CA + CA_LEFT_CHUNKS * CHUNK
CA_NBLK = CA_WIN // TQ_CA
IWT_ROWS = 16
TQ_SA = 256
KB_SA = 256
COUNT_CHAINS = 2
SEARCH_FIRST_ROUND = 16
SEARCH_ROUND = 4
GUESS_SPREAD = 0.3
TQ_MLA = 256
FF_CHUNK = 256

C_POOL = 0
C_CA = C_POOL + 256
C_SAQ = C_CA + 2 * GROUP_W
C_SAKV = C_SAQ + SA_HEADS * LANES
C_IQ = C_SAKV + LANES
C_IK = C_IQ + IDX_HEADS * IDX_DIM
C_CQ = C_IK + 2 * LANES
C_CKV = C_CQ + 256
C_KRF = C_CKV + LANES
C_KRS = C_KRF + LANES
C_END = C_KRS + LANES

INT_MIN = -2 ** 31
KEY_ALL = INT_MIN - int(np.array(-np.inf, np.float32).view(np.int32)) + 1


def _cparams(n_axes):
    return pltpu.CompilerParams(dimension_semantics=("arbitrary",) * n_axes,
                                vmem_limit_bytes=VMEM_LIMIT)


def _rms(x, g):
    return x * lax.rsqrt(jnp.mean(x * x, axis=-1, keepdims=True) + EPS) * g


def _dot(a, b):
    return jnp.dot(a, b, preferred_element_type=F32)


def _dot_t(a, b):
    return lax.dot_general(a, b, (((1,), (1,)), ((), ())), preferred_element_type=F32)


def _mod_kernel(c_ref, w_ref, b_ref, o_ref):
    c = c_ref[...]
    act = c * jax.nn.sigmoid(c)
    o_ref[0] = jnp.dot(act, w_ref[0], precision=lax.Precision.HIGHEST,
                       preferred_element_type=F32) + b_ref[0]


def _modulation(c, w_mod, b_mod):
    L, D, W = w_mod.shape
    B = c.shape[0]
    nj = W // D
    return pl.pallas_call(
        _mod_kernel,
        grid=(L, nj),
        in_specs=[pl.BlockSpec((B, D), lambda l, j: (0, 0)),
                  pl.BlockSpec((1, D, D), lambda l, j: (l, 0, j)),
                  pl.BlockSpec((1, 1, D), lambda l, j: (l, 0, j))],
        out_specs=pl.BlockSpec((1, B, D), lambda l, j: (l, 0, j)),
        out_shape=jax.ShapeDtypeStruct((L, B, W), F32),
        compiler_params=_cparams(2),
        name="modulation",
    )(c, w_mod, b_mod.reshape(L, 1, W))


def _inproj_kernel(x_ref, mod_ref, gmix_ref, w_ref, wt_ref, gcq_ref, gckv_ref, wq_ref, wqs_ref, wk_ref, wvt_ref,
                   cosq_ref, sinq_ref, cosk_ref, sink_ref,
                   pool_o, ca_o, saq_o, sakv_o, iq_o, ik_o, iwt_o, svt_o, cavt_o, mq_o, mk_o, mvt_o):
    sh1 = mod_ref[0, 0:1, :]
    sc1 = mod_ref[0, 1:2, :]
    h = (_rms(x_ref[...], gmix_ref[...]) * (1.0 + sc1) + sh1).astype(BF16)

    def seg(a, b):
        return _dot(h, w_ref[:, a:b])

    pool_o[...] = seg(C_POOL, C_CA)
    ca_o[...] = seg(C_CA, C_SAQ).astype(BF16)
    saq_o[...] = seg(C_SAQ, C_SAKV).astype(BF16)
    sakv_o[...] = seg(C_SAKV, C_IQ).astype(BF16)
    iq_o[...] = seg(C_IQ, C_IK).astype(BF16)
    ik_o[...] = seg(C_IK, C_CQ).astype(BF16)
    tr = _dot_t(wt_ref[...], h)
    iwt_o[...] = tr[0:IWT_ROWS] * ((IDX_HEADS ** -0.5) * (IDX_DIM ** -0.5))
    svt_o[...] = tr[IWT_ROWS:IWT_ROWS + HEAD_DIM].astype(BF16)
    cavt_o[...] = tr[IWT_ROWS + HEAD_DIM:].astype(BF16)

    qn = _rms(seg(C_CQ, C_CKV), gcq_ref[...]).astype(BF16)
    qf = _dot(qn, wq_ref[...])
    qs = _dot(qn, wqs_ref[...])
    cosq = jnp.concatenate([cosq_ref[...]] * MLA_HEADS, axis=1)
    sinq = jnp.concatenate([sinq_ref[...]] * MLA_HEADS, axis=1)
    mq_o[...] = (qf * cosq + qs * sinq).astype(BF16)

    kvn = _rms(seg(C_CKV, C_KRF), gckv_ref[...]).astype(BF16)
    kvf = _dot(kvn, wk_ref[...])
    krope = seg(C_KRF, C_KRS) * cosk_ref[...] + seg(C_KRS, C_END) * sink_ref[...]
    for hd in range(MLA_HEADS):
        mk_o[:, hd * LANES:(hd + 1) * LANES] = (kvf[:, hd * LANES:(hd + 1) * LANES] + krope).astype(BF16)
    mvt_o[...] = _dot_t(wvt_ref[...], kvn).astype(BF16)


def _inproj(x2, mod, gmix, w1, wt, gcq, gckv, wq, wqs, wk, wvt, rope_tabs, S):
    N, D = x2.shape
    TM = TM_PROJ
    nt = S // TM
    cosq, sinq, cosk, sink = rope_tabs

    def full(a):
        return pl.BlockSpec(a.shape, lambda i: (0,) * a.ndim)

    def tok(w):
        return pl.BlockSpec((TM, w), lambda i: (i, 0))

    tab = pl.BlockSpec((TM, LANES), lambda i: (i % nt, 0))
    def tokt(rows):
        return pl.BlockSpec((rows, TM), lambda i: (0, i))

    outs = [(C_CA - C_POOL, F32, True), (C_SAQ - C_CA, BF16, True), (C_SAKV - C_SAQ, BF16, True),
            (C_IQ - C_SAKV, BF16, True), (C_IK - C_IQ, BF16, True), (C_CQ - C_IK, BF16, True),
            (IWT_ROWS, F32, False), (HEAD_DIM, BF16, False), (GROUP_W, BF16, False),
            (MLA_HEADS * LANES, BF16, True), (MLA_HEADS * LANES, BF16, True), (GROUP_W, BF16, False)]
    return pl.pallas_call(
        _inproj_kernel,
        grid=(N // TM,),
        in_specs=[tok(D),
                  pl.BlockSpec((1, 6, D), lambda i: (i // nt, 0, 0)),
                  full(gmix), full(w1), full(wt), full(gcq), full(gckv), full(wq), full(wqs), full(wk), full(wvt),
                  tab, tab, tab, tab],
        out_specs=[tok(w) if tm else tokt(w) for w, _, tm in outs],
        out_shape=[jax.ShapeDtypeStruct((N, w) if tm else (w, N), dt) for w, dt, tm in outs],
        compiler_params=_cparams(1),
        name="inproj",
    )(x2, mod, gmix, w1, wt, gcq, gckv, wq, wqs, wk, wvt, cosq, sinq, cosk, sink)


def _pool_kernel(u_ref, halo_ref, w_ref, scale_ref, g_ref, o_ref, pad_scr):
    i = pl.program_id(1)
    TP = u_ref.shape[1]
    u = u_ref[0]
    pad_scr[0:POOL_HALO, :] = jnp.where(i > 0, halo_ref[0], 0.0)
    pad_scr[POOL_HALO:, :] = u

    def shifted(j):
        return pad_scr[POOL_HALO - j:POOL_HALO - j + TP, :]

    lane = lax.broadcasted_iota(jnp.int32, (TP, GROUP_W), 1)
    w2 = u + shifted(1)
    w4 = w2 + shifted(2) + shifted(3)
    w8 = w4
    for j in range(4, 8):
        w8 = w8 + shifted(j)
    w16 = w8
    for j in range(8, 16):
        w16 = w16 + shifted(j)
    win = jnp.where(lane < 64, w2, jnp.where(lane < 128, w4, jnp.where(lane < 192, w8, w16)))
    wlen = jnp.where(lane < 64, 2, jnp.where(lane < 128, 4, jnp.where(lane < 192, 8, 16)))
    t = i * TP + lax.broadcasted_iota(jnp.int32, (TP, GROUP_W), 0)
    cnt = jnp.minimum(t + 1, wlen).astype(F32)
    d = (win / cnt - u).astype(BF16)
    y = _dot(d, w_ref[...]) * scale_ref[...]
    o_ref[0] = _rms(y, g_ref[...]).astype(BF16)


def _pool(u, wbd, scale, g):
    B, S, W = u.shape
    TP = TP_POOL
    hb = TP // POOL_HALO
    return pl.pallas_call(
        _pool_kernel,
        grid=(B, S // TP),
        in_specs=[pl.BlockSpec((1, TP, W), lambda b, i: (b, i, 0)),
                  pl.BlockSpec((1, POOL_HALO, W), lambda b, i: (b, jnp.maximum(i * hb - 1, 0), 0)),
                  pl.BlockSpec((W, W), lambda b, i: (0, 0)),
                  pl.BlockSpec((1, W), lambda b, i: (0, 0)),
                  pl.BlockSpec((1, W), lambda b, i: (0, 0))],
        out_specs=pl.BlockSpec((1, TP, W), lambda b, i: (b, i, 0)),
        out_shape=jax.ShapeDtypeStruct((B, S, W), BF16),
        scratch_shapes=[pltpu.VMEM((POOL_HALO + TP, W), F32)],
        compiler_params=_cparams(2),
        name="pool_mixer",
    )(u, u, wbd, scale, g)


def _ca_kernel(q_ref, k_ref, vt_ref, bias_ref, g_ref, o_ref):
    i = pl.program_id(1)
    TQ = TQ_CA
    lane = lax.broadcasted_iota(jnp.int32, (TQ, LANES), 1)
    starts = []
    for j in range(CA_NBLK):
        kb = i - (CA_NBLK - 1) + j
        starts.append((kb >= 0, pl.multiple_of(jnp.maximum(kb, 0) * TQ, TQ)))
    scored = []
    for hd in range(CA_HEADS):
        cols = slice((hd // 2) * LANES, (hd // 2 + 1) * LANES)
        keep = (lane < HEAD_DIM) if hd % 2 == 0 else (lane >= HEAD_DIM)
        qh = jnp.where(keep, q_ref[0, :, cols].astype(F32), 0.0).astype(BF16)
        parts = []
        for j, (present, start) in enumerate(starts):
            s = _dot_t(k_ref[0, pl.ds(start, TQ), cols], qh) + bias_ref[hd, j * TQ:(j + 1) * TQ, :]
            parts.append(jnp.where(present, s, NEG_INF))
        m = parts[0].max(axis=0, keepdims=True)
        for s in parts[1:]:
            m = jnp.maximum(m, s.max(axis=0, keepdims=True))
        scored.append((parts, m))
    outs = []
    for hd, (parts, m) in enumerate(scored):
        l = jnp.zeros((1, TQ), F32)
        acc = jnp.zeros((HEAD_DIM, TQ), F32)
        for j, (_, start) in enumerate(starts):
            p = jnp.exp(parts[j] - m)
            l = l + p.sum(axis=0, keepdims=True)
            acc = acc + _dot(vt_ref[hd * HEAD_DIM:(hd + 1) * HEAD_DIM, pl.ds(start, TQ)], p.astype(BF16))
        outs.append(acc / l)
    o_ref[0] = _group_norm_t(jnp.concatenate(outs, axis=0), g_ref[...])


def _chunk_attention(caqk, cavt, bias, g):
    B, S, _ = caqk.shape
    W = GROUP_W
    TQ = TQ_CA
    return pl.pallas_call(
        _ca_kernel,
        grid=(B, S // TQ),
        in_specs=[pl.BlockSpec((1, TQ, W), lambda b, i: (b, i, 0)),
                  pl.BlockSpec((1, S, W), lambda b, i: (b, 0, 1)),
                  pl.BlockSpec((W, S), lambda b, i: (0, b)),
                  pl.BlockSpec(bias.shape, lambda b, i: (0, 0, 0)),
                  pl.BlockSpec((1, W), lambda b, i: (0, 0))],
        out_specs=pl.BlockSpec((1, TQ, W), lambda b, i: (b, i, 0)),
        out_shape=jax.ShapeDtypeStruct((B, S, W), BF16),
        compiler_params=_cparams(2),
        name="band_attention",
    )(caqk, caqk, cavt, bias, g)


def _score_key(score):
    b = lax.bitcast_convert_type(score, jnp.int32)
    return jnp.where(b < 0, jnp.int32(INT_MIN) - b, b)


def _sa_kernel(q_ref, kv_ref, vt_ref, iq_ref, ik_ref, iwt_ref, zq_ref, nbias_ref, g_ref, o_ref,
               key_scr, s0_scr, s1_scr, smax0_scr, smax1_scr):
    s_scr, smax_scr = (s0_scr, s1_scr), (smax0_scr, smax1_scr)
    i = pl.program_id(1)
    TQ, KB = TQ_SA, KB_SA
    K = float(TOPK_MAX)
    nb = i + 1
    q0 = i * TQ
    cshift = CHUNK.bit_length() - 1
    kchunk = lax.broadcasted_iota(jnp.int32, (KB, TQ), 0) >> cshift
    qchunk = (q0 + lax.broadcasted_iota(jnp.int32, (1, TQ), 1)) >> cshift

    iwt = iwt_ref[...]

    def score_block(j, carry):
        smax, s1, s2 = carry
        k0 = pl.multiple_of(j * KB, KB)
        ik = ik_ref[0, pl.ds(k0, KB), :]
        ik2 = jnp.concatenate([ik[:, :LANES], ik[:, LANES:]], axis=0)
        sc = jnp.zeros((KB, TQ), F32)
        for p in range(IDX_HEADS // 2):
            logits = _dot_t(ik2, iq_ref[0, :, p * LANES:(p + 1) * LANES])
            sc = sc + iwt[2 * p:2 * p + 1, :] * jnp.maximum(logits[:KB], 0.0)
            sc = sc + iwt[2 * p + 1:2 * p + 2, :] * jnp.maximum(logits[KB:], 0.0)
        adm = kchunk <= qchunk - (k0 >> cshift)
        sc = jnp.where(adm, sc, -jnp.inf)
        key_scr[pl.ds(k0, KB), :] = _score_key(sc)
        smax = jnp.maximum(smax, sc.max(axis=0, keepdims=True))
        full = j < i
        s1 = jnp.where(full, s1 + sc.sum(axis=0, keepdims=True), s1)
        s2 = jnp.where(full, s2 + (sc * sc).sum(axis=0, keepdims=True), s2)
        return smax, s1, s2

    smax, s1, s2 = lax.fori_loop(
        0, (nb + 1) // 2, lambda j, c: score_block(2 * j + 1, score_block(2 * j, c)),
        (jnp.full((1, TQ), -jnp.inf, F32), jnp.zeros((1, TQ), F32), jnp.zeros((1, TQ), F32)))

    def count_ge(cand):
        def body(j, acc):
            blk = key_scr[pl.ds(pl.multiple_of(j * (2 * KB), 2 * KB), 2 * KB), :]
            ones = jnp.where(blk >= cand, 1.0, 0.0)
            return acc + ones.reshape(COUNT_CHAINS, -1, 8, TQ).sum(axis=1)
        acc = lax.fori_loop(0, (nb + 1) // 2, body, jnp.zeros((COUNT_CHAINS, 8, TQ), F32))
        return acc.sum(axis=0).sum(axis=0, keepdims=True)

    def max_below(bound):
        def body(j, acc):
            blk = key_scr[pl.ds(pl.multiple_of(j * (2 * KB), 2 * KB), 2 * KB), :]
            below = jnp.where(blk < bound, blk, jnp.int32(INT_MIN))
            return jnp.maximum(acc, below.reshape(COUNT_CHAINS, -1, 8, TQ).max(axis=1))
        acc = lax.fori_loop(0, (nb + 1) // 2, body, jnp.full((COUNT_CHAINS, 8, TQ), INT_MIN, jnp.int32))
        return acc.max(axis=0).max(axis=0, keepdims=True)

    def search():
        def unkey(k):
            return lax.bitcast_convert_type(jnp.where(k < 0, jnp.int32(INT_MIN) - k, k), F32)

        def is_active(lo, hi, clo):
            return jnp.logical_and(clo > K, hi > lo + 1)

        def cond(st):
            _, lo, hi, clo, _ = st
            act = jnp.where(is_active(lo, hi, clo), 1.0, 0.0)
            return jnp.max(jnp.maximum(act[:, :LANES], act[:, LANES:])) > 0.0

        n_full = (i * KB).astype(F32)
        mean = s1 / n_full
        std = jnp.sqrt(jnp.maximum(s2 / n_full - mean * mean, 0.0))
        zq = jnp.max(zq_ref[...], axis=0, keepdims=True)
        guess_lo = _score_key(mean + (zq - GUESS_SPREAD) * std)
        guess_hi = _score_key(mean + (zq + GUESS_SPREAD) * std)

        def step(st, peel):
            it, lo, hi, clo, chi = st
            active = is_active(lo, hi, clo)
            if peel:
                cand = max_below(hi)
            else:
                lf, hf = unkey(lo), unkey(hi)
                lc = jnp.log(clo)
                frac = jnp.clip((lc - math.log(K - 0.5)) / (lc - jnp.log(jnp.maximum(chi, 0.5))), 0.05, 0.95)
                cand = _score_key(lf + frac * (hf - lf))
                cand = jnp.where(it % 3 == 2, (lo >> 1) + (hi >> 1) + (lo & hi & 1), cand)
                cand = jnp.where(it == 0, guess_lo, cand)
                cand = jnp.where(it == 1, guess_hi, cand)
                cand = jnp.clip(cand, lo + 1, hi - 1)
            cand = jnp.where(active, cand, lo)
            cnt = count_ge(cand)
            up = jnp.logical_and(active, cnt >= K)
            down = jnp.logical_and(active, cnt < K)
            hi = jnp.where(down, cand, jnp.where(up, cand + 1, hi) if peel else hi)
            return (it + 1, jnp.where(up, cand, lo), hi, jnp.where(up, cnt, clo), jnp.where(down, cnt, chi))

        def steps(n, st):
            return lax.fori_loop(0, n, lambda _, s: step(s, False), st)

        lo0 = jnp.full((1, TQ), KEY_ALL - 1, jnp.int32)
        hi0 = _score_key(smax) + 1
        clo0 = jnp.zeros((1, TQ), F32) + ((nb + 1) // 2 * (2 * KB)).astype(F32)
        st = (jnp.int32(0), lo0, hi0, clo0, jnp.zeros((1, TQ), F32))
        st = steps(SEARCH_FIRST_ROUND - 1, st)
        st = step(st, True)
        st = lax.while_loop(cond, lambda s: step(steps(SEARCH_ROUND - 1, s), True), st)
        return st[1], st[3]

    def no_search():
        return jnp.full((1, TQ), KEY_ALL, jnp.int32), jnp.full((1, TQ), K, F32)

    t, cnt_t = lax.cond(i > 0, search, no_search)
    t = jnp.maximum(t, KEY_ALL)

    @pl.when(jnp.max(cnt_t) > K)
    def _():
        allowed = K - count_ge(t + 1)
        r = lax.broadcasted_iota(jnp.int32, (KB, KB), 0)
        c = lax.broadcasted_iota(jnp.int32, (KB, KB), 1)
        earlier = jnp.where(c < r, 1.0, 0.0).astype(BF16)

        def body(jj, seen):
            sls = [pl.ds(pl.multiple_of((2 * jj + u) * KB, KB), KB) for u in range(2)]
            blks = [key_scr[sl, :] for sl in sls]
            eqs = [jnp.where(blk == t, 1.0, 0.0) for blk in blks]
            seens = [seen, seen + eqs[0].sum(axis=0, keepdims=True)]
            ranks = [_dot(earlier, eq.astype(BF16)) + sn for eq, sn in zip(eqs, seens)]
            for sl, blk, eq, rank in zip(sls, blks, eqs, ranks):
                demote = eq * jnp.where(rank >= allowed, 1.0, 0.0)
                key_scr[sl, :] = jnp.where(demote > 0.5, t - 1, blk)
            return seens[1] + eqs[1].sum(axis=0, keepdims=True)

        lax.fori_loop(0, (nb + 1) // 2, body, jnp.zeros((1, TQ), F32))

    def scores_to(slot, j, bias_rows, present):
        k0 = pl.multiple_of(j * KB, KB)
        kblk = kv_ref[0, pl.ds(k0, KB), :]
        sel = key_scr[pl.ds(k0, KB), :] >= (t if present is True else jnp.where(present, t, jnp.int32(2 ** 31 - 1)))
        for hd in range(SA_HEADS):
            s = _dot_t(kblk, q_ref[0, :, hd * LANES:(hd + 1) * LANES])
            if bias_rows is not None:
                s = s + nbias_ref[hd, bias_rows, :]
            s = jnp.where(sel, s, NEG_INF)
            s_scr[slot][hd] = s
            smax_scr[slot][hd] = jnp.broadcast_to(s.max(axis=0, keepdims=True), smax_scr[slot].shape[1:])

    def softmax_pv(slot, j, st):
        vt = vt_ref[:, pl.ds(pl.multiple_of(j * KB, KB), KB)]
        out = []
        for hd in range(SA_HEADS):
            m, l, acc = st[hd]
            mn = jnp.maximum(m, smax_scr[slot][hd][0:1])
            p = jnp.exp(s_scr[slot][hd] - mn)
            alpha = jnp.exp(m - mn)
            out.append((mn, alpha * l + p.sum(axis=0, keepdims=True), alpha * acc + _dot(vt, p.astype(BF16))))
        return tuple(out)

    st = tuple((jnp.full((1, TQ), NEG_INF, F32), jnp.zeros((1, TQ), F32), jnp.zeros((HEAD_DIM, TQ), F32))
               for _ in range(SA_HEADS))
    left = jnp.maximum(i - 1, 0)
    tails = [(left, lambda slot: scores_to(slot, left, slice(0, KB), i >= 1)),
             (i, lambda slot: scores_to(slot, i, slice(KB, 2 * KB), True))]
    st = _block_pipeline(left, i, lambda slot, j, present: scores_to(slot, j, None, present), tails, softmax_pv, st)
    y_t = jnp.concatenate([acc / l for _, l, acc in st], axis=0)
    o_ref[0] = _group_norm_t(y_t, g_ref[...])


def _sparse_attention(saq, sakv, svt, iq, ik, iwt, nbias, g):
    B, S, _ = saq.shape
    TQ = TQ_SA
    W = GROUP_W
    nt = S // TQ
    n_adm = (np.arange(S) // CHUNK + 1) * CHUNK
    zq = np.array([NormalDist().inv_cdf(1.0 - TOPK_MAX / n) if n > TOPK_MAX else 0.0 for n in n_adm], np.float32)
    zq = jnp.asarray(np.tile(zq[None, :], (8, 1)))
    return pl.pallas_call(
        _sa_kernel,
        grid=(B, nt),
        in_specs=[pl.BlockSpec((1, TQ, saq.shape[2]), lambda b, i: (b, i, 0)),
                  pl.BlockSpec((1, S, sakv.shape[2]), lambda b, i: (b, 0, 0)),
                  pl.BlockSpec((HEAD_DIM, S), lambda b, i: (0, b)),
                  pl.BlockSpec((1, TQ, iq.shape[2]), lambda b, i: (b, i, 0)),
                  pl.BlockSpec((1, S, ik.shape[2]), lambda b, i: (b, 0, 0)),
                  pl.BlockSpec((IWT_ROWS, TQ), lambda b, i: (0, b * nt + i)),
                  pl.BlockSpec((8, TQ), lambda b, i: (0, i)),
                  pl.BlockSpec(nbias.shape, lambda b, i: (0, 0, 0)),
                  pl.BlockSpec((1, W), lambda b, i: (0, 0))],
        out_specs=pl.BlockSpec((1, TQ, W), lambda b, i: (b, i, 0)),
        out_shape=jax.ShapeDtypeStruct((B, S, W), BF16),
        scratch_shapes=([pltpu.VMEM((S, TQ), jnp.int32)] + [pltpu.VMEM((SA_HEADS, KB_SA, TQ), F32)] * 2
                        + [pltpu.VMEM((SA_HEADS, 8, TQ), F32)] * 2),
        compiler_params=_cparams(2),
        name="sparse_attention",
    )(saq, sakv, svt, iq, ik, iwt, zq, nbias, g)


def _group_norm_t(y_t, g):
    inv = lax.rsqrt(jnp.mean(y_t * y_t, axis=0, keepdims=True) + EPS)
    return ((y_t * inv).T * g).astype(BF16)


def _block_pipeline(n_plain, last_blk, score_plain, tails, softmax, st):
    off = n_plain % 2
    n_loop = jnp.maximum((n_plain + off) // 2 - 1, 0)

    def blk(pos):
        return jnp.clip(pos - off, 0, last_blk)

    def body(pp, st):
        pos = 2 * pp
        score_plain(1, blk(pos + 1), True)
        st = softmax(0, blk(pos), st)
        score_plain(0, blk(pos + 2), True)
        return softmax(1, blk(pos + 1), st)

    score_plain(0, blk(0), jnp.logical_and(n_plain >= 1, off == 0))
    st = lax.fori_loop(0, n_loop, body, st)
    e0 = 2 * n_loop
    slot, pending = 0, blk(e0)
    steps = [(blk(e0 + 1), lambda s: score_plain(s, blk(e0 + 1), n_plain >= 1))] + list(tails)
    for nxt, score_fn in steps:
        score_fn(1 - slot)
        st = softmax(slot, pending, st)
        slot, pending = 1 - slot, nxt
    return softmax(slot, pending, st)


def _mla_kernel(q_ref, k_ref, vt_ref, g_ref, o_ref, s0_scr, s1_scr, smax0_scr, smax1_scr):
    i = pl.program_id(1)
    TQ = TQ_MLA
    cshift = CHUNK.bit_length() - 1
    kch = lax.broadcasted_iota(jnp.int32, (TQ, TQ), 0) >> cshift
    qch = lax.broadcasted_iota(jnp.int32, (TQ, TQ), 1) >> cshift
    s_scr, smax_scr = (s0_scr, s1_scr), (smax0_scr, smax1_scr)

    def scores_to(slot, j, keep):
        k0 = pl.multiple_of(j * TQ, TQ)
        for hd in range(MLA_HEADS):
            cols = slice(hd * LANES, (hd + 1) * LANES)
            s = _dot_t(k_ref[0, pl.ds(k0, TQ), cols], q_ref[0, :, cols])
            if keep is not None:
                s = jnp.where(keep, s, NEG_INF)
            s_scr[slot][hd] = s
            smax_scr[slot][hd] = jnp.broadcast_to(s.max(axis=0, keepdims=True), smax_scr[slot].shape[1:])

    def score_plain(slot, j, present):
        scores_to(slot, j, None if present is True else kch >= jnp.where(present, 0, TQ))

    def softmax_pv(slot, j, st):
        k0 = pl.multiple_of(j * TQ, TQ)
        out = []
        for hd in range(MLA_HEADS):
            m, l, acc = st[hd]
            mn = jnp.maximum(m, smax_scr[slot][hd][0:1])
            p = jnp.exp(s_scr[slot][hd] - mn)
            alpha = jnp.exp(m - mn)
            vt = vt_ref[hd * MLA_V:(hd + 1) * MLA_V, pl.ds(k0, TQ)]
            out.append((mn, alpha * l + p.sum(axis=0, keepdims=True),
                        alpha * acc + _dot(vt, p.astype(BF16))))
        return tuple(out)

    st = tuple((jnp.full((1, TQ), NEG_INF, F32), jnp.zeros((1, TQ), F32), jnp.zeros((MLA_V, TQ), F32))
               for _ in range(MLA_HEADS))
    st = _block_pipeline(i, i, score_plain, [(i, lambda slot: scores_to(slot, i, kch <= qch))], softmax_pv, st)
    o_ref[0] = _group_norm_t(jnp.concatenate([acc / l for _, l, acc in st], axis=0), g_ref[...])


def _latent_attention(mq, mk, mvt, g):
    B, S, _ = mq.shape
    TQ = TQ_MLA
    W = GROUP_W
    return pl.pallas_call(
        _mla_kernel,
        grid=(B, S // TQ),
        in_specs=[pl.BlockSpec((1, TQ, mq.shape[2]), lambda b, i: (b, i, 0)),
                  pl.BlockSpec((1, S, mk.shape[2]), lambda b, i: (b, 0, 0)),
                  pl.BlockSpec((W, S), lambda b, i: (0, b)),
                  pl.BlockSpec((1, W), lambda b, i: (0, 0))],
        out_specs=pl.BlockSpec((1, TQ, W), lambda b, i: (b, i, 0)),
        out_shape=jax.ShapeDtypeStruct((B, S, W), BF16),
        scratch_shapes=[pltpu.VMEM((MLA_HEADS, TQ, TQ), F32)] * 2 + [pltpu.VMEM((MLA_HEADS, 8, TQ), F32)] * 2,
        compiler_params=_cparams(2),
        name="latent_attention",
    )(mq, mk, mvt, g)


def _ffn_kernel(ya_ref, yb_ref, yc_ref, yd_ref, x_ref, mod_ref, wout_ref, gffn_ref, w1_ref, w3_ref, w2_ref,
                gfin_ref, o_ref, acc_scr, *, final):
    gt1 = mod_ref[0, 2:3, :]
    sh2 = mod_ref[0, 3:4, :]
    sc2 = mod_ref[0, 4:5, :]
    gt2 = mod_ref[0, 5:6, :]
    attn = _dot(ya_ref[...], wout_ref[0:GROUP_W, :])
    for gi, y_ref in enumerate((yb_ref, yc_ref, yd_ref), start=1):
        attn = attn + _dot(y_ref[...], wout_ref[gi * GROUP_W:(gi + 1) * GROUP_W, :])
    x1 = x_ref[...] + gt1 * attn
    h = (_rms(x1, gffn_ref[...]) * (1.0 + sc2) + sh2).astype(BF16)
    for ci in range(D_FF // FF_CHUNK):
        cols = slice(ci * FF_CHUNK, (ci + 1) * FF_CHUNK)
        a = _dot(h, w1_ref[:, cols])
        gate = (a * jax.nn.sigmoid(a) * _dot(h, w3_ref[:, cols])).astype(BF16)
        part = _dot(gate, w2_ref[cols, :])
        if ci == 0:
            acc_scr[...] = part
        else:
            acc_scr[...] += part
    x2 = x1 + gt2 * acc_scr[...]
    o_ref[...] = _rms(x2, gfin_ref[...]) if final else x2


def _out_ffn(ys, x2, mod, wout, gffn, w1, w3, w2, gfin, S, final):
    N, D = x2.shape
    TM = TM_FFN
    nt = S // TM

    def full(a):
        return pl.BlockSpec(a.shape, lambda i: (0,) * a.ndim, pipeline_mode=pl.Buffered(1))

    def tok(w):
        return pl.BlockSpec((TM, w), lambda i: (i, 0))

    return pl.pallas_call(
        functools.partial(_ffn_kernel, final=final),
        grid=(N // TM,),
        in_specs=[tok(GROUP_W)] * 4 + [tok(D), pl.BlockSpec((1, 6, D), lambda i: (i // nt, 0, 0)),
                                       full(wout), full(gffn), full(w1), full(w3), full(w2), full(gfin)],
        out_specs=tok(D),
        out_shape=jax.ShapeDtypeStruct((N, D), F32),
        scratch_shapes=[pltpu.VMEM((TM, D), F32)],
        compiler_params=_cparams(1),
        name="out_ffn_final" if final else "out_ffn",
    )(*ys, x2, mod, wout, gffn, w1, w3, w2, gfin)


def _t5_bucket(rel):
    nb = T5_BUCKETS // 2
    max_exact = nb // 2
    ret = jnp.where(rel > 0, nb, 0)
    n = jnp.abs(rel)
    nf = jnp.maximum(n, 1).astype(jnp.float32)
    large = max_exact + (jnp.log(nf / max_exact) / math.log(T5_MAX_DIST / max_exact)
                         * (nb - max_exact)).astype(jnp.int32)
    large = jnp.minimum(large, nb - 1)
    return ret + jnp.where(n < max_exact, n, large)


def _rope_tables(S):
    half = MLA_ROPE // 2
    freqs = ROPE_BASE ** (-jnp.arange(half, dtype=F32) / half)
    ang = jnp.arange(S, dtype=jnp.int32).astype(F32)[:, None] * freqs[None, :]
    cos, sin = jnp.cos(ang), jnp.sin(ang)
    cos2 = jnp.concatenate([cos, cos], axis=1)
    sin2 = jnp.concatenate([-sin, sin], axis=1)
    zeros = jnp.zeros((S, LANES - MLA_NOPE - MLA_ROPE), F32)
    scale = (MLA_NOPE + MLA_ROPE) ** -0.5
    cosq = jnp.concatenate([jnp.full((S, MLA_NOPE), scale, F32), cos2 * scale, zeros], axis=1)
    sinq = jnp.concatenate([jnp.zeros((S, MLA_NOPE), F32), sin2 * scale, zeros], axis=1)
    cosk = jnp.concatenate([jnp.zeros((S, MLA_NOPE), F32), cos2, zeros], axis=1)
    sink = jnp.concatenate([jnp.zeros((S, MLA_NOPE), F32), sin2, zeros], axis=1)
    return cosq, sinq, cosk, sink


def _pack_in_weight(w):
    part = {n: w[:, IN_OFFS[k]:IN_OFFS[k + 1]] for k, n in enumerate(
        ('pool_u', 'ca_q', 'ca_k', 'ca_v', 'sa_q', 'sa_k', 'sa_v', 'idx_q', 'idx_k', 'idx_w',
         'mla_cq', 'mla_ckv', 'mla_kr'))}
    D = w.shape[0]
    z = lambda n: jnp.zeros((D, n), F32)
    qscale = HEAD_DIM ** -0.5
    saq = part['sa_q'].reshape(D, SA_HEADS, HEAD_DIM) * qscale
    saq = jnp.concatenate([saq, jnp.zeros_like(saq)], axis=2).reshape(D, SA_HEADS * LANES)
    kr = part['mla_kr']
    kr_swap = jnp.concatenate([kr[:, MLA_ROPE // 2:], kr[:, :MLA_ROPE // 2]], axis=1)
    pad_r = LANES - MLA_NOPE - MLA_ROPE
    cols = [part['pool_u'], part['ca_q'] * qscale, part['ca_k'], saq,
            part['sa_k'], part['sa_v'], part['idx_q'],
            part['idx_k'], z(IDX_DIM), z(IDX_DIM), part['idx_k'],
            part['mla_cq'], part['mla_ckv'],
            z(MLA_NOPE), kr, z(pad_r), z(MLA_NOPE), kr_swap, z(pad_r)]
    out = jnp.concatenate(cols, axis=1)
    assert out.shape[1] == C_END
    wt = jnp.concatenate([part['idx_w'].T, jnp.zeros((IWT_ROWS - IDX_HEADS, D), F32), part['sa_v'].T,
                          part['ca_v'].T], axis=0)
    return out.astype(BF16), wt.astype(BF16)


def _pack_mla_weights(w_uq, w_ukv):
    R = w_uq.shape[0]
    pad = jnp.zeros((R, MLA_HEADS, LANES - MLA_NOPE - MLA_ROPE), F32)
    rope_w = w_uq[:, :, MLA_NOPE:]
    rope_sw = jnp.concatenate([rope_w[:, :, MLA_ROPE // 2:], rope_w[:, :, :MLA_ROPE // 2]], axis=2)
    wq = jnp.concatenate([w_uq, pad], axis=2).reshape(R, MLA_HEADS * LANES)
    wqs = jnp.concatenate([jnp.zeros((R, MLA_HEADS, MLA_NOPE), F32), rope_sw, pad],
                          axis=2).reshape(R, MLA_HEADS * LANES)
    Rk = w_ukv.shape[0]
    wk = jnp.concatenate([w_ukv[:, :, :MLA_NOPE], jnp.zeros((Rk, MLA_HEADS, LANES - MLA_NOPE), F32)],
                         axis=2).reshape(Rk, MLA_HEADS * LANES)
    wvt = w_ukv[:, :, MLA_NOPE:].reshape(Rk, MLA_HEADS * MLA_V).T
    return wq.astype(BF16), wqs.astype(BF16), wk.astype(BF16), wvt.astype(BF16)


def _toeplitz(vec, rows, cols):
    L = vec.shape[-1]
    assert cols <= L - 1
    flat = jnp.tile(vec, (1, rows))[:, :rows * (L - 1)]
    return flat.reshape(vec.shape[0], rows, L - 1)[:, :, :cols]


def _signed_mod_range(L, hi):
    d = np.arange(L)
    return np.where(d <= hi, d, d - L)


def _band_bias(rel_table):
    L = CA_WIN + TQ_CA
    e = _signed_mod_range(L, TQ_CA - 1)
    ridx = np.clip(CA_LEFT_CHUNKS * CHUNK + e, -(CHUNK - 1), CA_MAX_REL) + (CHUNK - 1)
    bias = _toeplitz(rel_table[:, ridx].astype(F32), CA_WIN, TQ_CA)
    kc = np.arange(CA_WIN)[:, None] // CHUNK
    qc = np.arange(TQ_CA)[None, :] // CHUNK + CA_LEFT_CHUNKS
    valid = (kc <= qc) & (kc >= qc - CA_LEFT_CHUNKS)
    return jnp.where(valid[None], bias, NEG_INF)


def _t5_bias(t5_table):
    TQ = TQ_SA
    L = 3 * TQ
    e = _signed_mod_range(L, TQ - 1)
    rel = jnp.asarray(-e - TQ, jnp.int32)
    far = t5_table[_t5_bucket(jnp.int32(-(TQ + 1)))].astype(F32)
    vec = (t5_table[_t5_bucket(rel)].astype(F32) - far[None, :]).T
    return _toeplitz(vec, 2 * TQ, TQ)


def kernel(x, c, t5_table, w_mod, b_mod, g_mix, w_in, pool_w, pool_scale, ca_rel, mla_g_cq, mla_g_ckv,
           mla_w_uq, mla_w_ukv, g_group, w_out, g_ffn, ffn_w1, ffn_w3, ffn_w2, g_final):
    B, S, D = x.shape
    assert D == D_MODEL and S % TM_PROJ == 0 and S % TQ_SA == 0 and S >= 4 * TOPK_MAX
    N = B * S
    mod_all = _modulation(c, w_mod, b_mod)
    rope_tabs = _rope_tables(S)
    nbias = _t5_bias(t5_table)
    row = lambda v: v.reshape(1, -1).astype(F32)
    x2 = x.reshape(N, D)
    for l in range(DEPTH):
        mod = mod_all[l].reshape(B, 6, D)
        w1, wt = _pack_in_weight(w_in[l])
        wq, wqs, wk, wvt = _pack_mla_weights(mla_w_uq[l], mla_w_ukv[l])
        (pool_u, ca, saq, sakv, iq, ik, iwt, svt, cavt, mq, mk, mvt) = _inproj(
            x2, mod, row(g_mix[l]), w1, wt, row(mla_g_cq[l]), row(mla_g_ckv[l]), wq, wqs, wk, wvt, rope_tabs, S)
        gg = g_group[l].reshape(4, 1, GROUP_W).astype(F32)
        wbd = jax.scipy.linalg.block_diag(*[pool_w[l, gi] for gi in range(len(POOL_WINDOWS))]).astype(BF16)
        bsw = lambda a: a.reshape(B, S, a.shape[-1])
        y_a = _pool(bsw(pool_u), wbd, row(pool_scale[l]), gg[0])
        y_b = _chunk_attention(bsw(ca), cavt, _band_bias(ca_rel[l]), gg[1])
        y_c = _sparse_attention(bsw(saq), bsw(sakv), svt, bsw(iq), bsw(ik), iwt, nbias, gg[2])
        y_d = _latent_attention(bsw(mq), bsw(mk), mvt, gg[3])
        ys = [y.reshape(N, GROUP_W) for y in (y_a, y_b, y_c, y_d)]
        x2 = _out_ffn(ys, x2, mod, w_out[l].astype(BF16), row(g_ffn[l]), ffn_w1[l].astype(BF16),
                      ffn_w3[l].astype(BF16), ffn_w2[l].astype(BF16), row(g_final), S,
                      final=(l == DEPTH - 1))
    return x2.reshape(B, S, D)
```

```python
import functools
import math
from statistics import NormalDist

import jax
import jax.numpy as jnp
from jax import lax
import numpy as np
from jax.experimental import pallas as pl
from jax.experimental.pallas import tpu as pltpu

F32 = jnp.float32
BF16 = jnp.bfloat16

D_MODEL = 1024
DEPTH = 2
CHUNK = 64
EPS = 1e-6
NEG_INF = -1e30
GROUP_W = 256
HEAD_DIM = 64
POOL_WINDOWS = (2, 4, 8, 16)
POOL_HALO = 16
CA_HEADS = 4
CA_LEFT_CHUNKS = 8
CA_MAX_REL = 256
SA_HEADS = 4
IDX_HEADS = 8
IDX_DIM = 64
TOPK_MAX = 256
MLA_HEADS = 4
MLA_NOPE = 64
MLA_ROPE = 32
MLA_V = 64
ROPE_BASE = 10000.0
T5_BUCKETS = 32
T5_MAX_DIST = 128
D_FF = 2816
IN_WIDTHS = (256, 256, 256, 256, 256, 64, 64, 512, 64, 8, 256, 128, 32)
IN_OFFS = tuple(int(v) for v in np.cumsum((0,) + IN_WIDTHS))

LANES = 128
VMEM_LIMIT = 56 * 1024 * 1024

TM_PROJ = 512
TM_FFN = 512
TP_POOL = 512
TQ_CA = 256
CA_WIN = TQ_CA + CA_LEFT_CHUNKS * CHUNK
CA_NBLK = CA_WIN // TQ_CA
IWT_ROWS = 16
TQ_SA = 256
KB_SA = 256
COUNT_CHAINS = 2
SEARCH_FIRST_ROUND = 16
SEARCH_ROUND = 4
GUESS_SPREAD = 0.3
TQ_MLA = 256
FF_CHUNK = 256

C_POOL = 0
C_CA = C_POOL + 256
C_SAQ = C_CA + 2 * GROUP_W
C_SAKV = C_SAQ + SA_HEADS * LANES
C_IQ = C_SAKV + LANES
C_IK = C_IQ + IDX_HEADS * IDX_DIM
C_CQ = C_IK + 2 * LANES
C_CKV = C_CQ + 256
C_KRF = C_CKV + LANES
C_KRS = C_KRF + LANES
C_END = C_KRS + LANES

INT_MIN = -2 ** 31
KEY_ALL = INT_MIN - int(np.array(-np.inf, np.float32).view(np.int32)) + 1


def _cparams(n_axes):
    return pltpu.CompilerParams(dimension_semantics=("arbitrary",) * n_axes,
                                vmem_limit_bytes=VMEM_LIMIT)


def _rms(x, g):
    return x * lax.rsqrt(jnp.mean(x * x, axis=-1, keepdims=True) + EPS) * g


def _dot(a, b):
    return jnp.dot(a, b, preferred_element_type=F32)


def _dot_t(a, b):
    return lax.dot_general(a, b, (((1,), (1,)), ((), ())), preferred_element_type=F32)


def _mod_kernel(c_ref, w_ref, b_ref, o_ref):
    c = c_ref[...]
    act = c * jax.nn.sigmoid(c)
    o_ref[0] = jnp.dot(act, w_ref[0], precision=lax.Precision.HIGHEST,
                       preferred_element_type=F32) + b_ref[0]


def _modulation(c, w_mod, b_mod):
    L, D, W = w_mod.shape
    B = c.shape[0]
    nj = W // D
    return pl.pallas_call(
        _mod_kernel,
        grid=(L, nj),
        in_specs=[pl.BlockSpec((B, D), lambda l, j: (0, 0)),
                  pl.BlockSpec((1, D, D), lambda l, j: (l, 0, j)),
                  pl.BlockSpec((1, 1, D), lambda l, j: (l, 0, j))],
        out_specs=pl.BlockSpec((1, B, D), lambda l, j: (l, 0, j)),
        out_shape=jax.ShapeDtypeStruct((L, B, W), F32),
        compiler_params=_cparams(2),
        name="modulation",
    )(c, w_mod, b_mod.reshape(L, 1, W))


def _inproj_kernel(x_ref, mod_ref, gmix_ref, w_ref, wt_ref, gcq_ref, gckv_ref, wq_ref, wqs_ref, wk_ref, wvt_ref,
                   cosq_ref, sinq_ref, cosk_ref, sink_ref,
                   pool_o, ca_o, saq_o, sakv_o, iq_o, ik_o, iwt_o, svt_o, cavt_o, mq_o, mk_o, mvt_o):
    sh1 = mod_ref[0, 0:1, :]
    sc1 = mod_ref[0, 1:2, :]
    h = (_rms(x_ref[...], gmix_ref[...]) * (1.0 + sc1) + sh1).astype(BF16)

    def seg(a, b):
        return _dot(h, w_ref[:, a:b])

    pool_o[...] = seg(C_POOL, C_CA)
    ca_o[...] = seg(C_CA, C_SAQ).astype(BF16)
    saq_o[...] = seg(C_SAQ, C_SAKV).astype(BF16)
    sakv_o[...] = seg(C_SAKV, C_IQ).astype(BF16)
    iq_o[...] = seg(C_IQ, C_IK).astype(BF16)
    ik_o[...] = seg(C_IK, C_CQ).astype(BF16)
    tr = _dot_t(wt_ref[...], h)
    iwt_o[...] = tr[0:IWT_ROWS] * ((IDX_HEADS ** -0.5) * (IDX_DIM ** -0.5))
    svt_o[...] = tr[IWT_ROWS:IWT_ROWS + HEAD_DIM].astype(BF16)
    cavt_o[...] = tr[IWT_ROWS + HEAD_DIM:].astype(BF16)

    qn = _rms(seg(C_CQ, C_CKV), gcq_ref[...]).astype(BF16)
    qf = _dot(qn, wq_ref[...])
    qs = _dot(qn, wqs_ref[...])
    cosq = jnp.concatenate([cosq_ref[...]] * MLA_HEADS, axis=1)
    sinq = jnp.concatenate([sinq_ref[...]] * MLA_HEADS, axis=1)
    mq_o[...] = (qf * cosq + qs * sinq).astype(BF16)

    kvn = _rms(seg(C_CKV, C_KRF), gckv_ref[...]).astype(BF16)
    kvf = _dot(kvn, wk_ref[...])
    krope = seg(C_KRF, C_KRS) * cosk_ref[...] + seg(C_KRS, C_END) * sink_ref[...]
    for hd in range(MLA_HEADS):
        mk_o[:, hd * LANES:(hd + 1) * LANES] = (kvf[:, hd * LANES:(hd + 1) * LANES] + krope).astype(BF16)
    mvt_o[...] = _dot_t(wvt_ref[...], kvn).astype(BF16)


def _inproj(x2, mod, gmix, w1, wt, gcq, gckv, wq, wqs, wk, wvt, rope_tabs, S):
    N, D = x2.shape
    TM = TM_PROJ
    nt = S // TM
    cosq, sinq, cosk, sink = rope_tabs

    def full(a):
        return pl.BlockSpec(a.shape, lambda i: (0,) * a.ndim)

    def tok(w):
        return pl.BlockSpec((TM, w), lambda i: (i, 0))

    tab = pl.BlockSpec((TM, LANES), lambda i: (i % nt, 0))
    def tokt(rows):
        return pl.BlockSpec((rows, TM), lambda i: (0, i))

    outs = [(C_CA - C_POOL, F32, True), (C_SAQ - C_CA, BF16, True), (C_SAKV - C_SAQ, BF16, True),
            (C_IQ - C_SAKV, BF16, True), (C_IK - C_IQ, BF16, True), (C_CQ - C_IK, BF16, True),
            (IWT_ROWS, F32, False), (HEAD_DIM, BF16, False), (GROUP_W, BF16, False),
            (MLA_HEADS * LANES, BF16, True), (MLA_HEADS * LANES, BF16, True), (GROUP_W, BF16, False)]
    return pl.pallas_call(
        _inproj_kernel,
        grid=(N // TM,),
        in_specs=[tok(D),
                  pl.BlockSpec((1, 6, D), lambda i: (i // nt, 0, 0)),
                  full(gmix), full(w1), full(wt), full(gcq), full(gckv), full(wq), full(wqs), full(wk), full(wvt),
                  tab, tab, tab, tab],
        out_specs=[tok(w) if tm else tokt(w) for w, _, tm in outs],
        out_shape=[jax.ShapeDtypeStruct((N, w) if tm else (w, N), dt) for w, dt, tm in outs],
        compiler_params=_cparams(1),
        name="inproj",
    )(x2, mod, gmix, w1, wt, gcq, gckv, wq, wqs, wk, wvt, cosq, sinq, cosk, sink)


def _pool_kernel(u_ref, halo_ref, w_ref, scale_ref, g_ref, o_ref, pad_scr):
    i = pl.program_id(1)
    TP = u_ref.shape[1]
    u = u_ref[0]
    pad_scr[0:POOL_HALO, :] = jnp.where(i > 0, halo_ref[0], 0.0)
    pad_scr[POOL_HALO:, :] = u

    def shifted(j):
        return pad_scr[POOL_HALO - j:POOL_HALO - j + TP, :]

    lane = lax.broadcasted_iota(jnp.int32, (TP, GROUP_W), 1)
    w2 = u + shifted(1)
    w4 = w2 + shifted(2) + shifted(3)
    w8 = w4
    for j in range(4, 8):
        w8 = w8 + shifted(j)
    w16 = w8
    for j in range(8, 16):
        w16 = w16 + shifted(j)
    win = jnp.where(lane < 64, w2, jnp.where(lane < 128, w4, jnp.where(lane < 192, w8, w16)))
    wlen = jnp.where(lane < 64, 2, jnp.where(lane < 128, 4, jnp.where(lane < 192, 8, 16)))
    t = i * TP + lax.broadcasted_iota(jnp.int32, (TP, GROUP_W), 0)
    cnt = jnp.minimum(t + 1, wlen).astype(F32)
    d = (win / cnt - u).astype(BF16)
    y = _dot(d, w_ref[...]) * scale_ref[...]
    o_ref[0] = _rms(y, g_ref[...]).astype(BF16)


def _pool(u, wbd, scale, g):
    B, S, W = u.shape
    TP = TP_POOL
    hb = TP // POOL_HALO
    return pl.pallas_call(
        _pool_kernel,
        grid=(B, S // TP),
        in_specs=[pl.BlockSpec((1, TP, W), lambda b, i: (b, i, 0)),
                  pl.BlockSpec((1, POOL_HALO, W), lambda b, i: (b, jnp.maximum(i * hb - 1, 0), 0)),
                  pl.BlockSpec((W, W), lambda b, i: (0, 0)),
                  pl.BlockSpec((1, W), lambda b, i: (0, 0)),
                  pl.BlockSpec((1, W), lambda b, i: (0, 0))],
        out_specs=pl.BlockSpec((1, TP, W), lambda b, i: (b, i, 0)),
        out_shape=jax.ShapeDtypeStruct((B, S, W), BF16),
        scratch_shapes=[pltpu.VMEM((POOL_HALO + TP, W), F32)],
        compiler_params=_cparams(2),
        name="pool_mixer",
    )(u, u, wbd, scale, g)


def _ca_kernel(q_ref, k_ref, vt_ref, bias_ref, g_ref, o_ref):
    i = pl.program_id(1)
    TQ = TQ_CA
    lane = lax.broadcasted_iota(jnp.int32, (TQ, LANES), 1)
    starts = []
    for j in range(CA_NBLK):
        kb = i - (CA_NBLK - 1) + j
        starts.append((kb >= 0, pl.multiple_of(jnp.maximum(kb, 0) * TQ, TQ)))
    scored = []
    for hd in range(CA_HEADS):
        cols = slice((hd // 2) * LANES, (hd // 2 + 1) * LANES)
        keep = (lane < HEAD_DIM) if hd % 2 == 0 else (lane >= HEAD_DIM)
        qh = jnp.where(keep, q_ref[0, :, cols].astype(F32), 0.0).astype(BF16)
        parts = []
        for j, (present, start) in enumerate(starts):
            s = _dot_t(k_ref[0, pl.ds(start, TQ), cols], qh) + bias_ref[hd, j * TQ:(j + 1) * TQ, :]
            parts.append(jnp.where(present, s, NEG_INF))
        m = parts[0].max(axis=0, keepdims=True)
        for s in parts[1:]:
            m = jnp.maximum(m, s.max(axis=0, keepdims=True))
        scored.append((parts, m))
    outs = []
    for hd, (parts, m) in enumerate(scored):
        l = jnp.zeros((1, TQ), F32)
        acc = jnp.zeros((HEAD_DIM, TQ), F32)
        for j, (_, start) in enumerate(starts):
            p = jnp.exp(parts[j] - m)
            l = l + p.sum(axis=0, keepdims=True)
            acc = acc + _dot(vt_ref[hd * HEAD_DIM:(hd + 1) * HEAD_DIM, pl.ds(start, TQ)], p.astype(BF16))
        outs.append(acc / l)
    o_ref[0] = _group_norm_t(jnp.concatenate(outs, axis=0), g_ref[...])


def _chunk_attention(caqk, cavt, bias, g):
    B, S, _ = caqk.shape
    W = GROUP_W
    TQ = TQ_CA
    return pl.pallas_call(
        _ca_kernel,
        grid=(B, S // TQ),
        in_specs=[pl.BlockSpec((1, TQ, W), lambda b, i: (b, i, 0)),
                  pl.BlockSpec((1, S, W), lambda b, i: (b, 0, 1)),
                  pl.BlockSpec((W, S), lambda b, i: (0, b)),
                  pl.BlockSpec(bias.shape, lambda b, i: (0, 0, 0)),
                  pl.BlockSpec((1, W), lambda b, i: (0, 0))],
        out_specs=pl.BlockSpec((1, TQ, W), lambda b, i: (b, i, 0)),
        out_shape=jax.ShapeDtypeStruct((B, S, W), BF16),
        compiler_params=_cparams(2),
        name="band_attention",
    )(caqk, caqk, cavt, bias, g)


def _score_key(score):
    b = lax.bitcast_convert_type(score, jnp.int32)
    return jnp.where(b < 0, jnp.int32(INT_MIN) - b, b)


def _sa_kernel(q_ref, kv_ref, vt_ref, iq_ref, ik_ref, iwt_ref, zq_ref, nbias_ref, g_ref, o_ref,
               key_scr, s0_scr, s1_scr, smax0_scr, smax1_scr):
    s_scr, smax_scr = (s0_scr, s1_scr), (smax0_scr, smax1_scr)
    i = pl.program_id(1)
    TQ, KB = TQ_SA, KB_SA
    K = float(TOPK_MAX)
    nb = i + 1
    q0 = i * TQ
    cshift = CHUNK.bit_length() - 1
    kchunk = lax.broadcasted_iota(jnp.int32, (KB, TQ), 0) >> cshift
    qchunk = (q0 + lax.broadcasted_iota(jnp.int32, (1, TQ), 1)) >> cshift

    iwt = iwt_ref[...]

    def score_block(j, carry, tail, moments):
        smax, s1, s2 = carry
        k0 = pl.multiple_of(j * KB, KB)
        ik = ik_ref[0, pl.ds(k0, KB), :]
        ik2 = jnp.concatenate([ik[:, :LANES], ik[:, LANES:]], axis=0)
        sc = jnp.zeros((KB, TQ), F32)
        for p in range(IDX_HEADS // 2):
            logits = _dot_t(ik2, iq_ref[0, :, p * LANES:(p + 1) * LANES])
            sc = sc + iwt[2 * p:2 * p + 1, :] * jnp.maximum(logits[:KB], 0.0)
            sc = sc + iwt[2 * p + 1:2 * p + 2, :] * jnp.maximum(logits[KB:], 0.0)
        if tail:
            sc = jnp.where(kchunk <= qchunk - (k0 >> cshift), sc, -jnp.inf)
        key_scr[pl.ds(k0, KB), :] = _score_key(sc)
        smax = jnp.maximum(smax, sc.max(axis=0, keepdims=True))
        if moments:
            s1 = s1 + sc.sum(axis=0, keepdims=True)
            s2 = s2 + (sc * sc).sum(axis=0, keepdims=True)
        return smax, s1, s2

    n_pairs = (nb + 1) // 2
    carry = lax.fori_loop(
        0, n_pairs - 1, lambda j, c: score_block(2 * j + 1, score_block(2 * j, c, False, True), False, False),
        (jnp.full((1, TQ), -jnp.inf, F32), jnp.zeros((1, TQ), F32), jnp.zeros((1, TQ), F32)))
    last = 2 * (n_pairs - 1)
    smax, s1, s2 = score_block(last + 1, score_block(last, carry, True, False), True, False)
    n_moments = ((n_pairs - 1) * KB).astype(F32)

    def count_ge(cand):
        def body(j, acc):
            blk = key_scr[pl.ds(pl.multiple_of(j * (2 * KB), 2 * KB), 2 * KB), :]
            ones = jnp.where(blk >= cand, 1.0, 0.0)
            return acc + ones.reshape(COUNT_CHAINS, -1, 8, TQ).sum(axis=1)
        acc = lax.fori_loop(0, (nb + 1) // 2, body, jnp.zeros((COUNT_CHAINS, 8, TQ), F32))
        return acc.sum(axis=0).sum(axis=0, keepdims=True)

    def max_below(bound):
        def body(j, acc):
            blk = key_scr[pl.ds(pl.multiple_of(j * (2 * KB), 2 * KB), 2 * KB), :]
            below = jnp.where(blk < bound, blk, jnp.int32(INT_MIN))
            return jnp.maximum(acc, below.reshape(COUNT_CHAINS, -1, 8, TQ).max(axis=1))
        acc = lax.fori_loop(0, (nb + 1) // 2, body, jnp.full((COUNT_CHAINS, 8, TQ), INT_MIN, jnp.int32))
        return acc.max(axis=0).max(axis=0, keepdims=True)

    def search():
        def unkey(k):
            return lax.bitcast_convert_type(jnp.where(k < 0, jnp.int32(INT_MIN) - k, k), F32)

        def is_active(lo, hi, clo):
            return jnp.logical_and(clo > K, hi > lo + 1)

        def cond(st):
            _, lo, hi, clo, _ = st
            act = jnp.where(is_active(lo, hi, clo), 1.0, 0.0)
            return jnp.max(jnp.maximum(act[:, :LANES], act[:, LANES:])) > 0.0

        mean = s1 / n_moments
        std = jnp.sqrt(jnp.maximum(s2 / n_moments - mean * mean, 0.0))
        zq = jnp.max(zq_ref[...], axis=0, keepdims=True)
        guess_lo = _score_key(mean + (zq - GUESS_SPREAD) * std)
        guess_hi = _score_key(mean + (zq + GUESS_SPREAD) * std)

        def step(st, peel):
            it, lo, hi, clo, chi = st
            active = is_active(lo, hi, clo)
            if peel:
                cand = max_below(hi)
            else:
                lf, hf = unkey(lo), unkey(hi)
                lc = jnp.log(clo)
                frac = jnp.clip((lc - math.log(K - 0.5)) / (lc - jnp.log(jnp.maximum(chi, 0.5))), 0.05, 0.95)
                cand = _score_key(lf + frac * (hf - lf))
                cand = jnp.where(it % 3 == 2, (lo >> 1) + (hi >> 1) + (lo & hi & 1), cand)
                cand = jnp.where(it == 0, guess_lo, cand)
                cand = jnp.where(it == 1, guess_hi, cand)
                cand = jnp.clip(cand, lo + 1, hi - 1)
            cand = jnp.where(active, cand, lo)
            cnt = count_ge(cand)
            up = jnp.logical_and(active, cnt >= K)
            down = jnp.logical_and(active, cnt < K)
            hi = jnp.where(down, cand, jnp.where(up, cand + 1, hi) if peel else hi)
            return (it + 1, jnp.where(up, cand, lo), hi, jnp.where(up, cnt, clo), jnp.where(down, cnt, chi))

        def steps(n, st):
            return lax.fori_loop(0, n, lambda _, s: step(s, False), st)

        lo0 = jnp.full((1, TQ), KEY_ALL - 1, jnp.int32)
        hi0 = _score_key(smax) + 1
        clo0 = jnp.zeros((1, TQ), F32) + ((nb + 1) // 2 * (2 * KB)).astype(F32)
        st = (jnp.int32(0), lo0, hi0, clo0, jnp.zeros((1, TQ), F32))
        st = steps(SEARCH_FIRST_ROUND - 1, st)
        st = step(st, True)
        st = lax.while_loop(cond, lambda s: step(steps(SEARCH_ROUND - 1, s), True), st)
        return st[1], st[3]

    def no_search():
        return jnp.full((1, TQ), KEY_ALL, jnp.int32), jnp.full((1, TQ), K, F32)

    t, cnt_t = lax.cond(i > 0, search, no_search)
    t = jnp.maximum(t, KEY_ALL)

    @pl.when(jnp.max(cnt_t) > K)
    def _():
        allowed = K - count_ge(t + 1)
        r = lax.broadcasted_iota(jnp.int32, (KB, KB), 0)
        c = lax.broadcasted_iota(jnp.int32, (KB, KB), 1)
        earlier = jnp.where(c < r, 1.0, 0.0).astype(BF16)

        def body(jj, seen):
            sls = [pl.ds(pl.multiple_of((2 * jj + u) * KB, KB), KB) for u in range(2)]
            blks = [key_scr[sl, :] for sl in sls]
            eqs = [jnp.where(blk == t, 1.0, 0.0) for blk in blks]
            seens = [seen, seen + eqs[0].sum(axis=0, keepdims=True)]
            ranks = [_dot(earlier, eq.astype(BF16)) + sn for eq, sn in zip(eqs, seens)]
            for sl, blk, eq, rank in zip(sls, blks, eqs, ranks):
                demote = eq * jnp.where(rank >= allowed, 1.0, 0.0)
                key_scr[sl, :] = jnp.where(demote > 0.5, t - 1, blk)
            return seens[1] + eqs[1].sum(axis=0, keepdims=True)

        lax.fori_loop(0, (nb + 1) // 2, body, jnp.zeros((1, TQ), F32))

    def scores_to(slot, j, bias_rows, present):
        k0 = pl.multiple_of(j * KB, KB)
        kblk = kv_ref[0, pl.ds(k0, KB), :]
        sel = key_scr[pl.ds(k0, KB), :] >= (t if present is True else jnp.where(present, t, jnp.int32(2 ** 31 - 1)))
        for hd in range(SA_HEADS):
            s = _dot_t(kblk, q_ref[0, :, hd * LANES:(hd + 1) * LANES])
            if bias_rows is not None:
                s = s + nbias_ref[hd, bias_rows, :]
            s = jnp.where(sel, s, NEG_INF)
            s_scr[slot][hd] = s
            smax_scr[slot][hd] = jnp.broadcast_to(s.max(axis=0, keepdims=True), smax_scr[slot].shape[1:])

    def softmax_pv(slot, j, st):
        vt = vt_ref[:, pl.ds(pl.multiple_of(j * KB, KB), KB)]
        out = []
        for hd in range(SA_HEADS):
            m, l, acc = st[hd]
            mn = jnp.maximum(m, smax_scr[slot][hd][0:1])
            p = jnp.exp(s_scr[slot][hd] - mn)
            alpha = jnp.exp(m - mn)
            out.append((mn, alpha * l + p.sum(axis=0, keepdims=True), alpha * acc + _dot(vt, p.astype(BF16))))
        return tuple(out)

    st = tuple((jnp.full((1, TQ), NEG_INF, F32), jnp.zeros((1, TQ), F32), jnp.zeros((HEAD_DIM, TQ), F32))
               for _ in range(SA_HEADS))
    left = jnp.maximum(i - 1, 0)
    tails = [(left, lambda slot: scores_to(slot, left, slice(0, KB), i >= 1)),
             (i, lambda slot: scores_to(slot, i, slice(KB, 2 * KB), True))]
    st = _block_pipeline(left, i, lambda slot, j, present: scores_to(slot, j, None, present), tails, softmax_pv, st)
    y_t = jnp.concatenate([acc / l for _, l, acc in st], axis=0)
    o_ref[0] = _group_norm_t(y_t, g_ref[...])


def _sparse_attention(saq, sakv, svt, iq, ik, iwt, nbias, g):
    B, S, _ = saq.shape
    TQ = TQ_SA
    W = GROUP_W
    nt = S // TQ
    n_adm = (np.arange(S) // CHUNK + 1) * CHUNK
    zq = np.array([NormalDist().inv_cdf(1.0 - TOPK_MAX / n) if n > TOPK_MAX else 0.0 for n in n_adm], np.float32)
    zq = jnp.asarray(np.tile(zq[None, :], (8, 1)))
    return pl.pallas_call(
        _sa_kernel,
        grid=(B, nt),
        in_specs=[pl.BlockSpec((1, TQ, saq.shape[2]), lambda b, i: (b, i, 0)),
                  pl.BlockSpec((1, S, sakv.shape[2]), lambda b, i: (b, 0, 0)),
                  pl.BlockSpec((HEAD_DIM, S), lambda b, i: (0, b)),
                  pl.BlockSpec((1, TQ, iq.shape[2]), lambda b, i: (b, i, 0)),
                  pl.BlockSpec((1, S, ik.shape[2]), lambda b, i: (b, 0, 0)),
                  pl.BlockSpec((IWT_ROWS, TQ), lambda b, i: (0, b * nt + i)),
                  pl.BlockSpec((8, TQ), lambda b, i: (0, i)),
                  pl.BlockSpec(nbias.shape, lambda b, i: (0, 0, 0)),
                  pl.BlockSpec((1, W), lambda b, i: (0, 0))],
        out_specs=pl.BlockSpec((1, TQ, W), lambda b, i: (b, i, 0)),
        out_shape=jax.ShapeDtypeStruct((B, S, W), BF16),
        scratch_shapes=([pltpu.VMEM((S, TQ), jnp.int32)] + [pltpu.VMEM((SA_HEADS, KB_SA, TQ), F32)] * 2
                        + [pltpu.VMEM((SA_HEADS, 8, TQ), F32)] * 2),
        compiler_params=_cparams(2),
        name="sparse_attention",
    )(saq, sakv, svt, iq, ik, iwt, zq, nbias, g)


def _group_norm_t(y_t, g):
    inv = lax.rsqrt(jnp.mean(y_t * y_t, axis=0, keepdims=True) + EPS)
    return ((y_t * inv).T * g).astype(BF16)


def _block_pipeline(n_plain, last_blk, score_plain, tails, softmax, st):
    off = n_plain % 2
    n_loop = jnp.maximum((n_plain + off) // 2 - 1, 0)

    def blk(pos):
        return jnp.clip(pos - off, 0, last_blk)

    def body(pp, st):
        pos = 2 * pp
        score_plain(1, blk(pos + 1), True)
        st = softmax(0, blk(pos), st)
        score_plain(0, blk(pos + 2), True)
        return softmax(1, blk(pos + 1), st)

    score_plain(0, blk(0), jnp.logical_and(n_plain >= 1, off == 0))
    st = lax.fori_loop(0, n_loop, body, st)
    e0 = 2 * n_loop
    slot, pending = 0, blk(e0)
    steps = [(blk(e0 + 1), lambda s: score_plain(s, blk(e0 + 1), n_plain >= 1))] + list(tails)
    for nxt, score_fn in steps:
        score_fn(1 - slot)
        st = softmax(slot, pending, st)
        slot, pending = 1 - slot, nxt
    return softmax(slot, pending, st)


def _mla_kernel(q_ref, k_ref, vt_ref, g_ref, o_ref, s0_scr, s1_scr, smax0_scr, smax1_scr):
    i = pl.program_id(1)
    TQ = TQ_MLA
    cshift = CHUNK.bit_length() - 1
    kch = lax.broadcasted_iota(jnp.int32, (TQ, TQ), 0) >> cshift
    qch = lax.broadcasted_iota(jnp.int32, (TQ, TQ), 1) >> cshift
    s_scr, smax_scr = (s0_scr, s1_scr), (smax0_scr, smax1_scr)

    def scores_to(slot, j, keep):
        k0 = pl.multiple_of(j * TQ, TQ)
        for hd in range(MLA_HEADS):
            cols = slice(hd * LANES, (hd + 1) * LANES)
            s = _dot_t(k_ref[0, pl.ds(k0, TQ), cols], q_ref[0, :, cols])
            if keep is not None:
                s = jnp.where(keep, s, NEG_INF)
            s_scr[slot][hd] = s
            smax_scr[slot][hd] = jnp.broadcast_to(s.max(axis=0, keepdims=True), smax_scr[slot].shape[1:])

    def score_plain(slot, j, present):
        scores_to(slot, j, None if present is True else kch >= jnp.where(present, 0, TQ))

    def softmax_pv(slot, j, st):
        k0 = pl.multiple_of(j * TQ, TQ)
        out = []
        for hd in range(MLA_HEADS):
            m, l, acc = st[hd]
            mn = jnp.maximum(m, smax_scr[slot][hd][0:1])
            p = jnp.exp(s_scr[slot][hd] - mn)
            alpha = jnp.exp(m - mn)
            vt = vt_ref[hd * MLA_V:(hd + 1) * MLA_V, pl.ds(k0, TQ)]
            out.append((mn, alpha * l + p.sum(axis=0, keepdims=True),
                        alpha * acc + _dot(vt, p.astype(BF16))))
        return tuple(out)

    st = tuple((jnp.full((1, TQ), NEG_INF, F32), jnp.zeros((1, TQ), F32), jnp.zeros((MLA_V, TQ), F32))
               for _ in range(MLA_HEADS))
    st = _block_pipeline(i, i, score_plain, [(i, lambda slot: scores_to(slot, i, kch <= qch))], softmax_pv, st)
    o_ref[0] = _group_norm_t(jnp.concatenate([acc / l for _, l, acc in st], axis=0), g_ref[...])


def _latent_attention(mq, mk, mvt, g):
    B, S, _ = mq.shape
    TQ = TQ_MLA
    W = GROUP_W
    return pl.pallas_call(
        _mla_kernel,
        grid=(B, S // TQ),
        in_specs=[pl.BlockSpec((1, TQ, mq.shape[2]), lambda b, i: (b, i, 0)),
                  pl.BlockSpec((1, S, mk.shape[2]), lambda b, i: (b, 0, 0)),
                  pl.BlockSpec((W, S), lambda b, i: (0, b)),
                  pl.BlockSpec((1, W), lambda b, i: (0, 0))],
        out_specs=pl.BlockSpec((1, TQ, W), lambda b, i: (b, i, 0)),
        out_shape=jax.ShapeDtypeStruct((B, S, W), BF16),
        scratch_shapes=[pltpu.VMEM((MLA_HEADS, TQ, TQ), F32)] * 2 + [pltpu.VMEM((MLA_HEADS, 8, TQ), F32)] * 2,
        compiler_params=_cparams(2),
        name="latent_attention",
    )(mq, mk, mvt, g)


def _ffn_kernel(ya_ref, yb_ref, yc_ref, yd_ref, x_ref, mod_ref, wout_ref, gffn_ref, w1_ref, w3_ref, w2_ref,
                gfin_ref, o_ref, acc_scr, *, final):
    gt1 = mod_ref[0, 2:3, :]
    sh2 = mod_ref[0, 3:4, :]
    sc2 = mod_ref[0, 4:5, :]
    gt2 = mod_ref[0, 5:6, :]
    attn = _dot(ya_ref[...], wout_ref[0:GROUP_W, :])
    for gi, y_ref in enumerate((yb_ref, yc_ref, yd_ref), start=1):
        attn = attn + _dot(y_ref[...], wout_ref[gi * GROUP_W:(gi + 1) * GROUP_W, :])
    x1 = x_ref[...] + gt1 * attn
    h = (_rms(x1, gffn_ref[...]) * (1.0 + sc2) + sh2).astype(BF16)
    for ci in range(D_FF // FF_CHUNK):
        cols = slice(ci * FF_CHUNK, (ci + 1) * FF_CHUNK)
        a = _dot(h, w1_ref[:, cols])
        gate = (a * jax.nn.sigmoid(a) * _dot(h, w3_ref[:, cols])).astype(BF16)
        part = _dot(gate, w2_ref[cols, :])
        if ci == 0:
            acc_scr[...] = part
        else:
            acc_scr[...] += part
    x2 = x1 + gt2 * acc_scr[...]
    o_ref[...] = _rms(x2, gfin_ref[...]) if final else x2


def _out_ffn(ys, x2, mod, wout, gffn, w1, w3, w2, gfin, S, final):
    N, D = x2.shape
    TM = TM_FFN
    nt = S // TM

    def full(a):
        return pl.BlockSpec(a.shape, lambda i: (0,) * a.ndim, pipeline_mode=pl.Buffered(1))

    def tok(w):
        return pl.BlockSpec((TM, w), lambda i: (i, 0))

    return pl.pallas_call(
        functools.partial(_ffn_kernel, final=final),
        grid=(N // TM,),
        in_specs=[tok(GROUP_W)] * 4 + [tok(D), pl.BlockSpec((1, 6, D), lambda i: (i // nt, 0, 0)),
                                       full(wout), full(gffn), full(w1), full(w3), full(w2), full(gfin)],
        out_specs=tok(D),
        out_shape=jax.ShapeDtypeStruct((N, D), F32),
        scratch_shapes=[pltpu.VMEM((TM, D), F32)],
        compiler_params=_cparams(1),
        name="out_ffn_final" if final else "out_ffn",
    )(*ys, x2, mod, wout, gffn, w1, w3, w2, gfin)


def _t5_bucket(rel):
    nb = T5_BUCKETS // 2
    max_exact = nb // 2
    ret = jnp.where(rel > 0, nb, 0)
    n = jnp.abs(rel)
    nf = jnp.maximum(n, 1).astype(jnp.float32)
    large = max_exact + (jnp.log(nf / max_exact) / math.log(T5_MAX_DIST / max_exact)
                         * (nb - max_exact)).astype(jnp.int32)
    large = jnp.minimum(large, nb - 1)
    return ret + jnp.where(n < max_exact, n, large)


def _rope_tables(S):
    half = MLA_ROPE // 2
    freqs = ROPE_BASE ** (-jnp.arange(half, dtype=F32) / half)
    ang = jnp.arange(S, dtype=jnp.int32).astype(F32)[:, None] * freqs[None, :]
    cos, sin = jnp.cos(ang), jnp.sin(ang)
    cos2 = jnp.concatenate([cos, cos], axis=1)
    sin2 = jnp.concatenate([-sin, sin], axis=1)
    zeros = jnp.zeros((S, LANES - MLA_NOPE - MLA_ROPE), F32)
    scale = (MLA_NOPE + MLA_ROPE) ** -0.5
    cosq = jnp.concatenate([jnp.full((S, MLA_NOPE), scale, F32), cos2 * scale, zeros], axis=1)
    sinq = jnp.concatenate([jnp.zeros((S, MLA_NOPE), F32), sin2 * scale, zeros], axis=1)
    cosk = jnp.concatenate([jnp.zeros((S, MLA_NOPE), F32), cos2, zeros], axis=1)
    sink = jnp.concatenate([jnp.zeros((S, MLA_NOPE), F32), sin2, zeros], axis=1)
    return cosq, sinq, cosk, sink


def _pack_in_weight(w):
    part = {n: w[:, IN_OFFS[k]:IN_OFFS[k + 1]] for k, n in enumerate(
        ('pool_u', 'ca_q', 'ca_k', 'ca_v', 'sa_q', 'sa_k', 'sa_v', 'idx_q', 'idx_k', 'idx_w',
         'mla_cq', 'mla_ckv', 'mla_kr'))}
    D = w.shape[0]
    z = lambda n: jnp.zeros((D, n), F32)
    qscale = HEAD_DIM ** -0.5
    saq = part['sa_q'].reshape(D, SA_HEADS, HEAD_DIM) * qscale
    saq = jnp.concatenate([saq, jnp.zeros_like(saq)], axis=2).reshape(D, SA_HEADS * LANES)
    kr = part['mla_kr']
    kr_swap = jnp.concatenate([kr[:, MLA_ROPE // 2:], kr[:, :MLA_ROPE // 2]], axis=1)
    pad_r = LANES - MLA_NOPE - MLA_ROPE
    cols = [part['pool_u'], part['ca_q'] * qscale, part['ca_k'], saq,
            part['sa_k'], part['sa_v'], part['idx_q'],
            part['idx_k'], z(IDX_DIM), z(IDX_DIM), part['idx_k'],
            part['mla_cq'], part['mla_ckv'],
            z(MLA_NOPE), kr, z(pad_r), z(MLA_NOPE), kr_swap, z(pad_r)]
    out = jnp.concatenate(cols, axis=1)
    assert out.shape[1] == C_END
    wt = jnp.concatenate([part['idx_w'].T, jnp.zeros((IWT_ROWS - IDX_HEADS, D), F32), part['sa_v'].T,
                          part['ca_v'].T], axis=0)
    return out.astype(BF16), wt.astype(BF16)


def _pack_mla_weights(w_uq, w_ukv):
    R = w_uq.shape[0]
    pad = jnp.zeros((R, MLA_HEADS, LANES - MLA_NOPE - MLA_ROPE), F32)
    rope_w = w_uq[:, :, MLA_NOPE:]
    rope_sw = jnp.concatenate([rope_w[:, :, MLA_ROPE // 2:], rope_w[:, :, :MLA_ROPE // 2]], axis=2)
    wq = jnp.concatenate([w_uq, pad], axis=2).reshape(R, MLA_HEADS * LANES)
    wqs = jnp.concatenate([jnp.zeros((R, MLA_HEADS, MLA_NOPE), F32), rope_sw, pad],
                          axis=2).reshape(R, MLA_HEADS * LANES)
    Rk = w_ukv.shape[0]
    wk = jnp.concatenate([w_ukv[:, :, :MLA_NOPE], jnp.zeros((Rk, MLA_HEADS, LANES - MLA_NOPE), F32)],
                         axis=2).reshape(Rk, MLA_HEADS * LANES)
    wvt = w_ukv[:, :, MLA_NOPE:].reshape(Rk, MLA_HEADS * MLA_V).T
    return wq.astype(BF16), wqs.astype(BF16), wk.astype(BF16), wvt.astype(BF16)


def _toeplitz(vec, rows, cols):
    L = vec.shape[-1]
    assert cols <= L - 1
    flat = jnp.tile(vec, (1, rows))[:, :rows * (L - 1)]
    return flat.reshape(vec.shape[0], rows, L - 1)[:, :, :cols]


def _signed_mod_range(L, hi):
    d = np.arange(L)
    return np.where(d <= hi, d, d - L)


def _band_bias(rel_table):
    L = CA_WIN + TQ_CA
    e = _signed_mod_range(L, TQ_CA - 1)
    ridx = np.clip(CA_LEFT_CHUNKS * CHUNK + e, -(CHUNK - 1), CA_MAX_REL) + (CHUNK - 1)
    bias = _toeplitz(rel_table[:, ridx].astype(F32), CA_WIN, TQ_CA)
    kc = np.arange(CA_WIN)[:, None] // CHUNK
    qc = np.arange(TQ_CA)[None, :] // CHUNK + CA_LEFT_CHUNKS
    valid = (kc <= qc) & (kc >= qc - CA_LEFT_CHUNKS)
    return jnp.where(valid[None], bias, NEG_INF)


def _t5_bias(t5_table):
    TQ = TQ_SA
    L = 3 * TQ
    e = _signed_mod_range(L, TQ - 1)
    rel = jnp.asarray(-e - TQ, jnp.int32)
    far = t5_table[_t5_bucket(jnp.int32(-(TQ + 1)))].astype(F32)
    vec = (t5_table[_t5_bucket(rel)].astype(F32) - far[None, :]).T
    return _toeplitz(vec, 2 * TQ, TQ)


def kernel(x, c, t5_table, w_mod, b_mod, g_mix, w_in, pool_w, pool_scale, ca_rel, mla_g_cq, mla_g_ckv,
           mla_w_uq, mla_w_ukv, g_group, w_out, g_ffn, ffn_w1, ffn_w3, ffn_w2, g_final):
    B, S, D = x.shape
    assert D == D_MODEL and S % TM_PROJ == 0 and S % TQ_SA == 0 and S >= 4 * TOPK_MAX
    N = B * S
    mod_all = _modulation(c, w_mod, b_mod)
    rope_tabs = _rope_tables(S)
    nbias = _t5_bias(t5_table)
    row = lambda v: v.reshape(1, -1).astype(F32)
    x2 = x.reshape(N, D)
    for l in range(DEPTH):
        mod = mod_all[l].reshape(B, 6, D)
        w1, wt = _pack_in_weight(w_in[l])
        wq, wqs, wk, wvt = _pack_mla_weights(mla_w_uq[l], mla_w_ukv[l])
        (pool_u, ca, saq, sakv, iq, ik, iwt, svt, cavt, mq, mk, mvt) = _inproj(
            x2, mod, row(g_mix[l]), w1, wt, row(mla_g_cq[l]), row(mla_g_ckv[l]), wq, wqs, wk, wvt, rope_tabs, S)
        gg = g_group[l].reshape(4, 1, GROUP_W).astype(F32)
        wbd = jax.scipy.linalg.block_diag(*[pool_w[l, gi] for gi in range(len(POOL_WINDOWS))]).astype(BF16)
        bsw = lambda a: a.reshape(B, S, a.shape[-1])
        y_a = _pool(bsw(pool_u), wbd, row(pool_scale[l]), gg[0])
        y_b = _chunk_attention(bsw(ca), cavt, _band_bias(ca_rel[l]), gg[1])
        y_c = _sparse_attention(bsw(saq), bsw(sakv), svt, bsw(iq), bsw(ik), iwt, nbias, gg[2])
        y_d = _latent_attention(bsw(mq), bsw(mk), mvt, gg[3])
        ys = [y.reshape(N, GROUP_W) for y in (y_a, y_b, y_c, y_d)]
        x2 = _out_ffn(ys, x2, mod, w_out[l].astype(BF16), row(g_ffn[l]), ffn_w1[l].astype(BF16),
                      ffn_w3[l].astype(BF16), ffn_w2[l].astype(BF16), row(g_final), S,
                      final=(l == DEPTH - 1))
    return x2.reshape(B, S, D)
```

```python
import functools
import math
from statistics import NormalDist

import jax
import jax.numpy as jnp
from jax import lax
import numpy as np
from jax.experimental import pallas as pl
from jax.experimental.pallas import tpu as pltpu

F32 = jnp.float32
BF16 = jnp.bfloat16

D_MODEL = 1024
DEPTH = 2
CHUNK = 64
EPS = 1e-6
NEG_INF = -1e30
GROUP_W = 256
HEAD_DIM = 64
POOL_WINDOWS = (2, 4, 8, 16)
POOL_HALO = 16
CA_HEADS = 4
CA_LEFT_CHUNKS = 8
CA_MAX_REL = 256
SA_HEADS = 4
IDX_HEADS = 8
IDX_DIM = 64
TOPK_MAX = 256
MLA_HEADS = 4
MLA_NOPE = 64
MLA_ROPE = 32
MLA_V = 64
ROPE_BASE = 10000.0
T5_BUCKETS = 32
T5_MAX_DIST = 128
D_FF = 2816
IN_WIDTHS = (256, 256, 256, 256, 256, 64, 64, 512, 64, 8, 256, 128, 32)
IN_OFFS = tuple(int(v) for v in np.cumsum((0,) + IN_WIDTHS))

LANES = 128
VMEM_LIMIT = 56 * 1024 * 1024

TM_PROJ = 512
TM_FFN = 512
TP_POOL = 512
TQ_CA = 256
CA_WIN = TQ_CA + CA_LEFT_CHUNKS * CHUNK
CA_NBLK = CA_WIN // TQ_CA
IWT_ROWS = 16
TQ_SA = 256
KB_SA = 256
COUNT_CHAINS = 2
SEARCH_FIRST_ROUND = 16
SEARCH_ROUND = 4
GUESS_SPREAD = 0.3
TQ_MLA = 256
FF_CHUNK = 256

C_POOL = 0
C_CA = C_POOL + 256
C_SAQ = C_CA + 2 * GROUP_W
C_SAKV = C_SAQ + SA_HEADS * LANES
C_IQ = C_SAKV + LANES
C_IK = C_IQ + IDX_HEADS * IDX_DIM
C_CQ = C_IK + 2 * LANES
C_CKV = C_CQ + 256
C_KRF = C_CKV + LANES
C_KRS = C_KRF + LANES
C_END = C_KRS + LANES

INT_MIN = -2 ** 31
KEY_ALL = INT_MIN - int(np.array(-np.inf, np.float32).view(np.int32)) + 1


def _cparams(n_axes):
    return pltpu.CompilerParams(dimension_semantics=("arbitrary",) * n_axes,
                                vmem_limit_bytes=VMEM_LIMIT)


def _rms(x, g):
    return x * lax.rsqrt(jnp.mean(x * x, axis=-1, keepdims=True) + EPS) * g


def _dot(a, b):
    return jnp.dot(a, b, preferred_element_type=F32)


def _dot_t(a, b):
    return lax.dot_general(a, b, (((1,), (1,)), ((), ())), preferred_element_type=F32)


def _mod_kernel(c_ref, w_ref, b_ref, o_ref):
    c = c_ref[...]
    act = c * jax.nn.sigmoid(c)
    o_ref[0] = jnp.dot(act, w_ref[0], precision=lax.Precision.HIGHEST,
                       preferred_element_type=F32) + b_ref[0]


def _modulation(c, w_mod, b_mod):
    L, D, W = w_mod.shape
    B = c.shape[0]
    nj = W // D
    return pl.pallas_call(
        _mod_kernel,
        grid=(L, nj),
        in_specs=[pl.BlockSpec((B, D), lambda l, j: (0, 0)),
                  pl.BlockSpec((1, D, D), lambda l, j: (l, 0, j)),
                  pl.BlockSpec((1, 1, D), lambda l, j: (l, 0, j))],
        out_specs=pl.BlockSpec((1, B, D), lambda l, j: (l, 0, j)),
        out_shape=jax.ShapeDtypeStruct((L, B, W), F32),
        compiler_params=_cparams(2),
        name="modulation",
    )(c, w_mod, b_mod.reshape(L, 1, W))


def _inproj_kernel(x_ref, mod_ref, gmix_ref, w_ref, wt_ref, gcq_ref, gckv_ref, wq_ref, wqs_ref, wk_ref, wvt_ref,
                   cosq_ref, sinq_ref, cosk_ref, sink_ref,
                   pool_o, ca_o, saq_o, sakv_o, iq_o, ik_o, iwt_o, svt_o, cavt_o, mq_o, mk_o, mvt_o):
    sh1 = mod_ref[0, 0:1, :]
    sc1 = mod_ref[0, 1:2, :]
    h = (_rms(x_ref[...], gmix_ref[...]) * (1.0 + sc1) + sh1).astype(BF16)

    def seg(a, b):
        return _dot(h, w_ref[:, a:b])

    qn = _rms(seg(C_CQ, C_CKV), gcq_ref[...]).astype(BF16)
    kvn = _rms(seg(C_CKV, C_KRF), gckv_ref[...]).astype(BF16)
    krope = seg(C_KRF, C_KRS) * cosk_ref[...] + seg(C_KRS, C_END) * sink_ref[...]

    pool_o[...] = seg(C_POOL, C_CA)
    ca_o[...] = seg(C_CA, C_SAQ).astype(BF16)
    qf = _dot(qn, wq_ref[...])
    qs = _dot(qn, wqs_ref[...])
    saq_o[...] = seg(C_SAQ, C_SAKV).astype(BF16)
    cosq = jnp.concatenate([cosq_ref[...]] * MLA_HEADS, axis=1)
    sinq = jnp.concatenate([sinq_ref[...]] * MLA_HEADS, axis=1)
    mq_o[...] = (qf * cosq + qs * sinq).astype(BF16)
    kvf = _dot(kvn, wk_ref[...])
    sakv_o[...] = seg(C_SAKV, C_IQ).astype(BF16)
    iq_o[...] = seg(C_IQ, C_IK).astype(BF16)
    for hd in range(MLA_HEADS):
        mk_o[:, hd * LANES:(hd + 1) * LANES] = (kvf[:, hd * LANES:(hd + 1) * LANES] + krope).astype(BF16)
    mvt_o[...] = _dot_t(wvt_ref[...], kvn).astype(BF16)
    ik_o[...] = seg(C_IK, C_CQ).astype(BF16)
    tr = _dot_t(wt_ref[...], h)
    iwt_o[...] = tr[0:IWT_ROWS] * ((IDX_HEADS ** -0.5) * (IDX_DIM ** -0.5))
    svt_o[...] = tr[IWT_ROWS:IWT_ROWS + HEAD_DIM].astype(BF16)
    cavt_o[...] = tr[IWT_ROWS + HEAD_DIM:].astype(BF16)


def _inproj(x2, mod, gmix, w1, wt, gcq, gckv, wq, wqs, wk, wvt, rope_tabs, S):
    N, D = x2.shape
    TM = TM_PROJ
    nt = S // TM
    cosq, sinq, cosk, sink = rope_tabs

    def full(a):
        return pl.BlockSpec(a.shape, lambda i: (0,) * a.ndim)

    def tok(w):
        return pl.BlockSpec((TM, w), lambda i: (i, 0))

    tab = pl.BlockSpec((TM, LANES), lambda i: (i % nt, 0))
    def tokt(rows):
        return pl.BlockSpec((rows, TM), lambda i: (0, i))

    outs = [(C_CA - C_POOL, F32, True), (C_SAQ - C_CA, BF16, True), (C_SAKV - C_SAQ, BF16, True),
            (C_IQ - C_SAKV, BF16, True), (C_IK - C_IQ, BF16, True), (C_CQ - C_IK, BF16, True),
            (IWT_ROWS, F32, False), (HEAD_DIM, BF16, False), (GROUP_W, BF16, False),
            (MLA_HEADS * LANES, BF16, True), (MLA_HEADS * LANES, BF16, True), (GROUP_W, BF16, False)]
    return pl.pallas_call(
        _inproj_kernel,
        grid=(N // TM,),
        in_specs=[tok(D),
                  pl.BlockSpec((1, 6, D), lambda i: (i // nt, 0, 0)),
                  full(gmix), full(w1), full(wt), full(gcq), full(gckv), full(wq), full(wqs), full(wk), full(wvt),
                  tab, tab, tab, tab],
        out_specs=[tok(w) if tm else tokt(w) for w, _, tm in outs],
        out_shape=[jax.ShapeDtypeStruct((N, w) if tm else (w, N), dt) for w, dt, tm in outs],
        compiler_params=_cparams(1),
        name="inproj",
    )(x2, mod, gmix, w1, wt, gcq, gckv, wq, wqs, wk, wvt, cosq, sinq, cosk, sink)


def _pool_kernel(u_ref, halo_ref, w_ref, scale_ref, g_ref, o_ref, pad_scr):
    i = pl.program_id(1)
    TP = u_ref.shape[1]
    u = u_ref[0]
    pad_scr[0:POOL_HALO, :] = jnp.where(i > 0, halo_ref[0], 0.0)
    pad_scr[POOL_HALO:, :] = u

    def shifted(j):
        return pad_scr[POOL_HALO - j:POOL_HALO - j + TP, :]

    lane = lax.broadcasted_iota(jnp.int32, (TP, GROUP_W), 1)
    w2 = u + shifted(1)
    w4 = w2 + shifted(2) + shifted(3)
    w8 = w4
    for j in range(4, 8):
        w8 = w8 + shifted(j)
    w16 = w8
    for j in range(8, 16):
        w16 = w16 + shifted(j)
    win = jnp.where(lane < 64, w2, jnp.where(lane < 128, w4, jnp.where(lane < 192, w8, w16)))
    wlen = jnp.where(lane < 64, 2, jnp.where(lane < 128, 4, jnp.where(lane < 192, 8, 16)))
    t = i * TP + lax.broadcasted_iota(jnp.int32, (TP, GROUP_W), 0)
    cnt = jnp.minimum(t + 1, wlen).astype(F32)
    d = (win / cnt - u).astype(BF16)
    y = _dot(d, w_ref[...]) * scale_ref[...]
    o_ref[0] = _rms(y, g_ref[...]).astype(BF16)


def _pool(u, wbd, scale, g):
    B, S, W = u.shape
    TP = TP_POOL
    hb = TP // POOL_HALO
    return pl.pallas_call(
        _pool_kernel,
        grid=(B, S // TP),
        in_specs=[pl.BlockSpec((1, TP, W), lambda b, i: (b, i, 0)),
                  pl.BlockSpec((1, POOL_HALO, W), lambda b, i: (b, jnp.maximum(i * hb - 1, 0), 0)),
                  pl.BlockSpec((W, W), lambda b, i: (0, 0)),
                  pl.BlockSpec((1, W), lambda b, i: (0, 0)),
                  pl.BlockSpec((1, W), lambda b, i: (0, 0))],
        out_specs=pl.BlockSpec((1, TP, W), lambda b, i: (b, i, 0)),
        out_shape=jax.ShapeDtypeStruct((B, S, W), BF16),
        scratch_shapes=[pltpu.VMEM((POOL_HALO + TP, W), F32)],
        compiler_params=_cparams(2),
        name="pool_mixer",
    )(u, u, wbd, scale, g)


def _ca_kernel(q_ref, k_ref, vt_ref, bias_ref, g_ref, o_ref):
    i = pl.program_id(1)
    TQ = TQ_CA
    lane = lax.broadcasted_iota(jnp.int32, (TQ, LANES), 1)
    starts = []
    for j in range(CA_NBLK):
        kb = i - (CA_NBLK - 1) + j
        starts.append((kb >= 0, pl.multiple_of(jnp.maximum(kb, 0) * TQ, TQ)))
    scored = []
    for hd in range(CA_HEADS):
        cols = slice((hd // 2) * LANES, (hd // 2 + 1) * LANES)
        keep = (lane < HEAD_DIM) if hd % 2 == 0 else (lane >= HEAD_DIM)
        qh = jnp.where(keep, q_ref[0, :, cols].astype(F32), 0.0).astype(BF16)
        parts = []
        for j, (present, start) in enumerate(starts):
            s = _dot_t(k_ref[0, pl.ds(start, TQ), cols], qh) + bias_ref[hd, j * TQ:(j + 1) * TQ, :]
            parts.append(jnp.where(present, s, NEG_INF))
        m = parts[0].max(axis=0, keepdims=True)
        for s in parts[1:]:
            m = jnp.maximum(m, s.max(axis=0, keepdims=True))
        scored.append((parts, m))
    outs = []
    for hd, (parts, m) in enumerate(scored):
        l = jnp.zeros((1, TQ), F32)
        acc = jnp.zeros((HEAD_DIM, TQ), F32)
        for j, (_, start) in enumerate(starts):
            p = jnp.exp(parts[j] - m)
            l = l + p.sum(axis=0, keepdims=True)
            acc = acc + _dot(vt_ref[hd * HEAD_DIM:(hd + 1) * HEAD_DIM, pl.ds(start, TQ)], p.astype(BF16))
        outs.append(acc / l)
    o_ref[0] = _group_norm_t(jnp.concatenate(outs, axis=0), g_ref[...])


def _chunk_attention(caqk, cavt, bias, g):
    B, S, _ = caqk.shape
    W = GROUP_W
    TQ = TQ_CA
    return pl.pallas_call(
        _ca_kernel,
        grid=(B, S // TQ),
        in_specs=[pl.BlockSpec((1, TQ, W), lambda b, i: (b, i, 0)),
                  pl.BlockSpec((1, S, W), lambda b, i: (b, 0, 1)),
                  pl.BlockSpec((W, S), lambda b, i: (0, b)),
                  pl.BlockSpec(bias.shape, lambda b, i: (0, 0, 0)),
                  pl.BlockSpec((1, W), lambda b, i: (0, 0))],
        out_specs=pl.BlockSpec((1, TQ, W), lambda b, i: (b, i, 0)),
        out_shape=jax.ShapeDtypeStruct((B, S, W), BF16),
        compiler_params=_cparams(2),
        name="band_attention",
    )(caqk, caqk, cavt, bias, g)


def _score_key(score):
    b = lax.bitcast_convert_type(score, jnp.int32)
    return jnp.where(b < 0, jnp.int32(INT_MIN) - b, b)


def _sa_kernel(q_ref, kv_ref, vt_ref, iq_ref, ik_ref, iwt_ref, zq_ref, nbias_ref, g_ref, o_ref,
               key_scr, s0_scr, s1_scr, smax0_scr, smax1_scr):
    s_scr, smax_scr = (s0_scr, s1_scr), (smax0_scr, smax1_scr)
    i = pl.program_id(1)
    TQ, KB = TQ_SA, KB_SA
    K = float(TOPK_MAX)
    nb = i + 1
    q0 = i * TQ
    cshift = CHUNK.bit_length() - 1
    kchunk = lax.broadcasted_iota(jnp.int32, (KB, TQ), 0) >> cshift
    qchunk = (q0 + lax.broadcasted_iota(jnp.int32, (1, TQ), 1)) >> cshift

    iwt = iwt_ref[...]

    def score_block(j, carry, tail, moments):
        smax, s1, s2 = carry
        k0 = pl.multiple_of(j * KB, KB)
        ik = ik_ref[0, pl.ds(k0, KB), :]
        ik2 = jnp.concatenate([ik[:, :LANES], ik[:, LANES:]], axis=0)
        sc = jnp.zeros((KB, TQ), F32)
        for p in range(IDX_HEADS // 2):
            logits = _dot_t(ik2, iq_ref[0, :, p * LANES:(p + 1) * LANES])
            sc = sc + iwt[2 * p:2 * p + 1, :] * jnp.maximum(logits[:KB], 0.0)
            sc = sc + iwt[2 * p + 1:2 * p + 2, :] * jnp.maximum(logits[KB:], 0.0)
        if tail:
            sc = jnp.where(kchunk <= qchunk - (k0 >> cshift), sc, -jnp.inf)
        key_scr[pl.ds(k0, KB), :] = _score_key(sc)
        smax = jnp.maximum(smax, sc.max(axis=0, keepdims=True))
        if moments:
            s1 = s1 + sc.sum(axis=0, keepdims=True)
            s2 = s2 + (sc * sc).sum(axis=0, keepdims=True)
        return smax, s1, s2

    n_pairs = (nb + 1) // 2
    carry = lax.fori_loop(
        0, n_pairs - 1, lambda j, c: score_block(2 * j + 1, score_block(2 * j, c, False, True), False, False),
        (jnp.full((1, TQ), -jnp.inf, F32), jnp.zeros((1, TQ), F32), jnp.zeros((1, TQ), F32)))
    last = 2 * (n_pairs - 1)
    smax, s1, s2 = score_block(last + 1, score_block(last, carry, True, False), True, False)
    n_moments = ((n_pairs - 1) * KB).astype(F32)

    def count_ge(cand):
        def body(j, acc):
            blk = key_scr[pl.ds(pl.multiple_of(j * (2 * KB), 2 * KB), 2 * KB), :]
            ones = jnp.where(blk >= cand, 1.0, 0.0)
            return acc + ones.reshape(COUNT_CHAINS, -1, 8, TQ).sum(axis=1)
        acc = lax.fori_loop(0, (nb + 1) // 2, body, jnp.zeros((COUNT_CHAINS, 8, TQ), F32))
        return acc.sum(axis=0).sum(axis=0, keepdims=True)

    def max_below(bound):
        def body(j, acc):
            blk = key_scr[pl.ds(pl.multiple_of(j * (2 * KB), 2 * KB), 2 * KB), :]
            below = jnp.where(blk < bound, blk, jnp.int32(INT_MIN))
            return jnp.maximum(acc, below.reshape(COUNT_CHAINS, -1, 8, TQ).max(axis=1))
        acc = lax.fori_loop(0, (nb + 1) // 2, body, jnp.full((COUNT_CHAINS, 8, TQ), INT_MIN, jnp.int32))
        return acc.max(axis=0).max(axis=0, keepdims=True)

    def search():
        def unkey(k):
            return lax.bitcast_convert_type(jnp.where(k < 0, jnp.int32(INT_MIN) - k, k), F32)

        def is_active(lo, hi, clo):
            return jnp.logical_and(clo > K, hi > lo + 1)

        def cond(st):
            _, lo, hi, clo, _ = st
            act = jnp.where(is_active(lo, hi, clo), 1.0, 0.0)
            return jnp.max(jnp.maximum(act[:, :LANES], act[:, LANES:])) > 0.0

        mean = s1 / n_moments
        std = jnp.sqrt(jnp.maximum(s2 / n_moments - mean * mean, 0.0))
        zq = jnp.max(zq_ref[...], axis=0, keepdims=True)
        guess_lo = _score_key(mean + (zq - GUESS_SPREAD) * std)
        guess_hi = _score_key(mean + (zq + GUESS_SPREAD) * std)

        def step(st, peel):
            it, lo, hi, clo, chi = st
            active = is_active(lo, hi, clo)
            if peel:
                cand = max_below(hi)
            else:
                lf, hf = unkey(lo), unkey(hi)
                lc = jnp.log(clo)
                frac = jnp.clip((lc - math.log(K - 0.5)) / (lc - jnp.log(jnp.maximum(chi, 0.5))), 0.05, 0.95)
                cand = _score_key(lf + frac * (hf - lf))
                cand = jnp.where(it % 3 == 2, (lo >> 1) + (hi >> 1) + (lo & hi & 1), cand)
                cand = jnp.where(it == 0, guess_lo, cand)
                cand = jnp.where(it == 1, guess_hi, cand)
                cand = jnp.clip(cand, lo + 1, hi - 1)
            cand = jnp.where(active, cand, lo)
            cnt = count_ge(cand)
            up = jnp.logical_and(active, cnt >= K)
            down = jnp.logical_and(active, cnt < K)
            hi = jnp.where(down, cand, jnp.where(up, cand + 1, hi) if peel else hi)
            return (it + 1, jnp.where(up, cand, lo), hi, jnp.where(up, cnt, clo), jnp.where(down, cnt, chi))

        def steps(n, st):
            return lax.fori_loop(0, n, lambda _, s: step(s, False), st)

        lo0 = jnp.full((1, TQ), KEY_ALL - 1, jnp.int32)
        hi0 = _score_key(smax) + 1
        clo0 = jnp.zeros((1, TQ), F32) + ((nb + 1) // 2 * (2 * KB)).astype(F32)
        st = (jnp.int32(0), lo0, hi0, clo0, jnp.zeros((1, TQ), F32))
        st = steps(SEARCH_FIRST_ROUND - 1, st)
        st = step(st, True)
        st = lax.while_loop(cond, lambda s: step(steps(SEARCH_ROUND - 1, s), True), st)
        return st[1], st[3]

    def no_search():
        return jnp.full((1, TQ), KEY_ALL, jnp.int32), jnp.full((1, TQ), K, F32)

    t, cnt_t = lax.cond(i > 0, search, no_search)
    t = jnp.maximum(t, KEY_ALL)

    @pl.when(jnp.max(cnt_t) > K)
    def _():
        allowed = K - count_ge(t + 1)
        r = lax.broadcasted_iota(jnp.int32, (KB, KB), 0)
        c = lax.broadcasted_iota(jnp.int32, (KB, KB), 1)
        earlier = jnp.where(c < r, 1.0, 0.0).astype(BF16)

        def body(jj, seen):
            sls = [pl.ds(pl.multiple_of((2 * jj + u) * KB, KB), KB) for u in range(2)]
            blks = [key_scr[sl, :] for sl in sls]
            eqs = [jnp.where(blk == t, 1.0, 0.0) for blk in blks]
            seens = [seen, seen + eqs[0].sum(axis=0, keepdims=True)]
            ranks = [_dot(earlier, eq.astype(BF16)) + sn for eq, sn in zip(eqs, seens)]
            for sl, blk, eq, rank in zip(sls, blks, eqs, ranks):
                demote = eq * jnp.where(rank >= allowed, 1.0, 0.0)
                key_scr[sl, :] = jnp.where(demote > 0.5, t - 1, blk)
            return seens[1] + eqs[1].sum(axis=0, keepdims=True)

        lax.fori_loop(0, (nb + 1) // 2, body, jnp.zeros((1, TQ), F32))

    def scores_to(slot, j, bias_rows, present):
        k0 = pl.multiple_of(j * KB, KB)
        kblk = kv_ref[0, pl.ds(k0, KB), :]
        sel = key_scr[pl.ds(k0, KB), :] >= (t if present is True else jnp.where(present, t, jnp.int32(2 ** 31 - 1)))
        for hd in range(SA_HEADS):
            s = _dot_t(kblk, q_ref[0, :, hd * LANES:(hd + 1) * LANES])
            if bias_rows is not None:
                s = s + nbias_ref[hd, bias_rows, :]
            s = jnp.where(sel, s, NEG_INF)
            s_scr[slot][hd] = s
            smax_scr[slot][hd] = jnp.broadcast_to(s.max(axis=0, keepdims=True), smax_scr[slot].shape[1:])

    def softmax_pv(slot, j, st):
        vt = vt_ref[:, pl.ds(pl.multiple_of(j * KB, KB), KB)]
        out = []
        for hd in range(SA_HEADS):
            m, l, acc = st[hd]
            mn = jnp.maximum(m, smax_scr[slot][hd][0:1])
            p = jnp.exp(s_scr[slot][hd] - mn)
            alpha = jnp.exp(m - mn)
            out.append((mn, alpha * l + p.sum(axis=0, keepdims=True), alpha * acc + _dot(vt, p.astype(BF16))))
        return tuple(out)

    st = tuple((jnp.full((1, TQ), NEG_INF, F32), jnp.zeros((1, TQ), F32), jnp.zeros((HEAD_DIM, TQ), F32))
               for _ in range(SA_HEADS))
    left = jnp.maximum(i - 1, 0)
    tails = [(left, lambda slot: scores_to(slot, left, slice(0, KB), i >= 1)),
             (i, lambda slot: scores_to(slot, i, slice(KB, 2 * KB), True))]
    st = _block_pipeline(left, i, lambda slot, j, present: scores_to(slot, j, None, present), tails, softmax_pv, st)
    y_t = jnp.concatenate([acc / l for _, l, acc in st], axis=0)
    o_ref[0] = _group_norm_t(y_t, g_ref[...])


def _sparse_attention(saq, sakv, svt, iq, ik, iwt, nbias, g):
    B, S, _ = saq.shape
    TQ = TQ_SA
    W = GROUP_W
    nt = S // TQ
    n_adm = (np.arange(S) // CHUNK + 1) * CHUNK
    zq = np.array([NormalDist().inv_cdf(1.0 - TOPK_MAX / n) if n > TOPK_MAX else 0.0 for n in n_adm], np.float32)
    zq = jnp.asarray(np.tile(zq[None, :], (8, 1)))
    return pl.pallas_call(
        _sa_kernel,
        grid=(B, nt),
        in_specs=[pl.BlockSpec((1, TQ, saq.shape[2]), lambda b, i: (b, i, 0)),
                  pl.BlockSpec((1, S, sakv.shape[2]), lambda b, i: (b, 0, 0)),
                  pl.BlockSpec((HEAD_DIM, S), lambda b, i: (0, b)),
                  pl.BlockSpec((1, TQ, iq.shape[2]), lambda b, i: (b, i, 0)),
                  pl.BlockSpec((1, S, ik.shape[2]), lambda b, i: (b, 0, 0)),
                  pl.BlockSpec((IWT_ROWS, TQ), lambda b, i: (0, b * nt + i)),
                  pl.BlockSpec((8, TQ), lambda b, i: (0, i)),
                  pl.BlockSpec(nbias.shape, lambda b, i: (0, 0, 0)),
                  pl.BlockSpec((1, W), lambda b, i: (0, 0))],
        out_specs=pl.BlockSpec((1, TQ, W), lambda b, i: (b, i, 0)),
        out_shape=jax.ShapeDtypeStruct((B, S, W), BF16),
        scratch_shapes=([pltpu.VMEM((S, TQ), jnp.int32)] + [pltpu.VMEM((SA_HEADS, KB_SA, TQ), F32)] * 2
                        + [pltpu.VMEM((SA_HEADS, 8, TQ), F32)] * 2),
        compiler_params=_cparams(2),
        name="sparse_attention",
    )(saq, sakv, svt, iq, ik, iwt, zq, nbias, g)


def _group_norm_t(y_t, g):
    inv = lax.rsqrt(jnp.mean(y_t * y_t, axis=0, keepdims=True) + EPS)
    return ((y_t * inv).T * g).astype(BF16)


def _block_pipeline(n_plain, last_blk, score_plain, tails, softmax, st):
    off = n_plain % 2
    n_loop = jnp.maximum((n_plain + off) // 2 - 1, 0)

    def blk(pos):
        return jnp.clip(pos - off, 0, last_blk)

    def body(pp, st):
        pos = 2 * pp
        score_plain(1, blk(pos + 1), True)
        st = softmax(0, blk(pos), st)
        score_plain(0, blk(pos + 2), True)
        return softmax(1, blk(pos + 1), st)

    score_plain(0, blk(0), jnp.logical_and(n_plain >= 1, off == 0))
    st = lax.fori_loop(0, n_loop, body, st)
    e0 = 2 * n_loop
    slot, pending = 0, blk(e0)
    steps = [(blk(e0 + 1), lambda s: score_plain(s, blk(e0 + 1), n_plain >= 1))] + list(tails)
    for nxt, score_fn in steps:
        score_fn(1 - slot)
        st = softmax(slot, pending, st)
        slot, pending = 1 - slot, nxt
    return softmax(slot, pending, st)


def _mla_kernel(q_ref, k_ref, vt_ref, g_ref, o_ref, s0_scr, s1_scr, smax0_scr, smax1_scr):
    i = pl.program_id(1)
    TQ = TQ_MLA
    cshift = CHUNK.bit_length() - 1
    kch = lax.broadcasted_iota(jnp.int32, (TQ, TQ), 0) >> cshift
    qch = lax.broadcasted_iota(jnp.int32, (TQ, TQ), 1) >> cshift
    s_scr, smax_scr = (s0_scr, s1_scr), (smax0_scr, smax1_scr)

    def scores_to(slot, j, keep):
        k0 = pl.multiple_of(j * TQ, TQ)
        for hd in range(MLA_HEADS):
            cols = slice(hd * LANES, (hd + 1) * LANES)
            s = _dot_t(k_ref[0, pl.ds(k0, TQ), cols], q_ref[0, :, cols])
            if keep is not None:
                s = jnp.where(keep, s, NEG_INF)
            s_scr[slot][hd] = s
            smax_scr[slot][hd] = jnp.broadcast_to(s.max(axis=0, keepdims=True), smax_scr[slot].shape[1:])

    def score_plain(slot, j, present):
        scores_to(slot, j, None if present is True else kch >= jnp.where(present, 0, TQ))

    def softmax_pv(slot, j, st):
        k0 = pl.multiple_of(j * TQ, TQ)
        out = []
        for hd in range(MLA_HEADS):
            m, l, acc = st[hd]
            mn = jnp.maximum(m, smax_scr[slot][hd][0:1])
            p = jnp.exp(s_scr[slot][hd] - mn)
            alpha = jnp.exp(m - mn)
            vt = vt_ref[hd * MLA_V:(hd + 1) * MLA_V, pl.ds(k0, TQ)]
            out.append((mn, alpha * l + p.sum(axis=0, keepdims=True),
                        alpha * acc + _dot(vt, p.astype(BF16))))
        return tuple(out)

    st = tuple((jnp.full((1, TQ), NEG_INF, F32), jnp.zeros((1, TQ), F32), jnp.zeros((MLA_V, TQ), F32))
               for _ in range(MLA_HEADS))
    st = _block_pipeline(i, i, score_plain, [(i, lambda slot: scores_to(slot, i, kch <= qch))], softmax_pv, st)
    o_ref[0] = _group_norm_t(jnp.concatenate([acc / l for _, l, acc in st], axis=0), g_ref[...])


def _latent_attention(mq, mk, mvt, g):
    B, S, _ = mq.shape
    TQ = TQ_MLA
    W = GROUP_W
    return pl.pallas_call(
        _mla_kernel,
        grid=(B, S // TQ),
        in_specs=[pl.BlockSpec((1, TQ, mq.shape[2]), lambda b, i: (b, i, 0)),
                  pl.BlockSpec((1, S, mk.shape[2]), lambda b, i: (b, 0, 0)),
                  pl.BlockSpec((W, S), lambda b, i: (0, b)),
                  pl.BlockSpec((1, W), lambda b, i: (0, 0))],
        out_specs=pl.BlockSpec((1, TQ, W), lambda b, i: (b, i, 0)),
        out_shape=jax.ShapeDtypeStruct((B, S, W), BF16),
        scratch_shapes=[pltpu.VMEM((MLA_HEADS, TQ, TQ), F32)] * 2 + [pltpu.VMEM((MLA_HEADS, 8, TQ), F32)] * 2,
        compiler_params=_cparams(2),
        name="latent_attention",
    )(mq, mk, mvt, g)


def _ffn_kernel(ya_ref, yb_ref, yc_ref, yd_ref, x_ref, mod_ref, wout_ref, gffn_ref, w1_ref, w3_ref, w2_ref,
                gfin_ref, o_ref, acc_scr, *, final):
    gt1 = mod_ref[0, 2:3, :]
    sh2 = mod_ref[0, 3:4, :]
    sc2 = mod_ref[0, 4:5, :]
    gt2 = mod_ref[0, 5:6, :]
    attn = _dot(ya_ref[...], wout_ref[0:GROUP_W, :])
    for gi, y_ref in enumerate((yb_ref, yc_ref, yd_ref), start=1):
        attn = attn + _dot(y_ref[...], wout_ref[gi * GROUP_W:(gi + 1) * GROUP_W, :])
    x1 = x_ref[...] + gt1 * attn
    h = (_rms(x1, gffn_ref[...]) * (1.0 + sc2) + sh2).astype(BF16)
    for ci in range(D_FF // FF_CHUNK):
        cols = slice(ci * FF_CHUNK, (ci + 1) * FF_CHUNK)
        a = _dot(h, w1_ref[:, cols])
        gate = (a * jax.nn.sigmoid(a) * _dot(h, w3_ref[:, cols])).astype(BF16)
        part = _dot(gate, w2_ref[cols, :])
        if ci == 0:
            acc_scr[...] = part
        else:
            acc_scr[...] += part
    x2 = x1 + gt2 * acc_scr[...]
    o_ref[...] = _rms(x2, gfin_ref[...]) if final else x2


def _out_ffn(ys, x2, mod, wout, gffn, w1, w3, w2, gfin, S, final):
    N, D = x2.shape
    TM = TM_FFN
    nt = S // TM

    def full(a):
        return pl.BlockSpec(a.shape, lambda i: (0,) * a.ndim, pipeline_mode=pl.Buffered(1))

    def tok(w):
        return pl.BlockSpec((TM, w), lambda i: (i, 0))

    return pl.pallas_call(
        functools.partial(_ffn_kernel, final=final),
        grid=(N // TM,),
        in_specs=[tok(GROUP_W)] * 4 + [tok(D), pl.BlockSpec((1, 6, D), lambda i: (i // nt, 0, 0)),
                                       full(wout), full(gffn), full(w1), full(w3), full(w2), full(gfin)],
        out_specs=tok(D),
        out_shape=jax.ShapeDtypeStruct((N, D), F32),
        scratch_shapes=[pltpu.VMEM((TM, D), F32)],
        compiler_params=_cparams(1),
        name="out_ffn_final" if final else "out_ffn",
    )(*ys, x2, mod, wout, gffn, w1, w3, w2, gfin)


def _t5_bucket(rel):
    nb = T5_BUCKETS // 2
    max_exact = nb // 2
    ret = jnp.where(rel > 0, nb, 0)
    n = jnp.abs(rel)
    nf = jnp.maximum(n, 1).astype(jnp.float32)
    large = max_exact + (jnp.log(nf / max_exact) / math.log(T5_MAX_DIST / max_exact)
                         * (nb - max_exact)).astype(jnp.int32)
    large = jnp.minimum(large, nb - 1)
    return ret + jnp.where(n < max_exact, n, large)


def _rope_tables(S):
    half = MLA_ROPE // 2
    freqs = ROPE_BASE ** (-jnp.arange(half, dtype=F32) / half)
    ang = jnp.arange(S, dtype=jnp.int32).astype(F32)[:, None] * freqs[None, :]
    cos, sin = jnp.cos(ang), jnp.sin(ang)
    cos2 = jnp.concatenate([cos, cos], axis=1)
    sin2 = jnp.concatenate([-sin, sin], axis=1)
    zeros = jnp.zeros((S, LANES - MLA_NOPE - MLA_ROPE), F32)
    scale = (MLA_NOPE + MLA_ROPE) ** -0.5
    cosq = jnp.concatenate([jnp.full((S, MLA_NOPE), scale, F32), cos2 * scale, zeros], axis=1)
    sinq = jnp.concatenate([jnp.zeros((S, MLA_NOPE), F32), sin2 * scale, zeros], axis=1)
    cosk = jnp.concatenate([jnp.zeros((S, MLA_NOPE), F32), cos2, zeros], axis=1)
    sink = jnp.concatenate([jnp.zeros((S, MLA_NOPE), F32), sin2, zeros], axis=1)
    return cosq, sinq, cosk, sink


def _pack_in_weight(w):
    part = {n: w[:, IN_OFFS[k]:IN_OFFS[k + 1]] for k, n in enumerate(
        ('pool_u', 'ca_q', 'ca_k', 'ca_v', 'sa_q', 'sa_k', 'sa_v', 'idx_q', 'idx_k', 'idx_w',
         'mla_cq', 'mla_ckv', 'mla_kr'))}
    D = w.shape[0]
    z = lambda n: jnp.zeros((D, n), F32)
    qscale = HEAD_DIM ** -0.5
    saq = part['sa_q'].reshape(D, SA_HEADS, HEAD_DIM) * qscale
    saq = jnp.concatenate([saq, jnp.zeros_like(saq)], axis=2).reshape(D, SA_HEADS * LANES)
    kr = part['mla_kr']
    kr_swap = jnp.concatenate([kr[:, MLA_ROPE // 2:], kr[:, :MLA_ROPE // 2]], axis=1)
    pad_r = LANES - MLA_NOPE - MLA_ROPE
    cols = [part['pool_u'], part['ca_q'] * qscale, part['ca_k'], saq,
            part['sa_k'], part['sa_v'], part['idx_q'],
            part['idx_k'], z(IDX_DIM), z(IDX_DIM), part['idx_k'],
            part['mla_cq'], part['mla_ckv'],
            z(MLA_NOPE), kr, z(pad_r), z(MLA_NOPE), kr_swap, z(pad_r)]
    out = jnp.concatenate(cols, axis=1)
    assert out.shape[1] == C_END
    wt = jnp.concatenate([part['idx_w'].T, jnp.zeros((IWT_ROWS - IDX_HEADS, D), F32), part['sa_v'].T,
                          part['ca_v'].T], axis=0)
    return out.astype(BF16), wt.astype(BF16)


def _pack_mla_weights(w_uq, w_ukv):
    R = w_uq.shape[0]
    pad = jnp.zeros((R, MLA_HEADS, LANES - MLA_NOPE - MLA_ROPE), F32)
    rope_w = w_uq[:, :, MLA_NOPE:]
    rope_sw = jnp.concatenate([rope_w[:, :, MLA_ROPE // 2:], rope_w[:, :, :MLA_ROPE // 2]], axis=2)
    wq = jnp.concatenate([w_uq, pad], axis=2).reshape(R, MLA_HEADS * LANES)
    wqs = jnp.concatenate([jnp.zeros((R, MLA_HEADS, MLA_NOPE), F32), rope_sw, pad],
                          axis=2).reshape(R, MLA_HEADS * LANES)
    Rk = w_ukv.shape[0]
    wk = jnp.concatenate([w_ukv[:, :, :MLA_NOPE], jnp.zeros((Rk, MLA_HEADS, LANES - MLA_NOPE), F32)],
                         axis=2).reshape(Rk, MLA_HEADS * LANES)
    wvt = w_ukv[:, :, MLA_NOPE:].reshape(Rk, MLA_HEADS * MLA_V).T
    return wq.astype(BF16), wqs.astype(BF16), wk.astype(BF16), wvt.astype(BF16)


def _toeplitz(vec, rows, cols):
    L = vec.shape[-1]
    assert cols <= L - 1
    flat = jnp.tile(vec, (1, rows))[:, :rows * (L - 1)]
    return flat.reshape(vec.shape[0], rows, L - 1)[:, :, :cols]


def _signed_mod_range(L, hi):
    d = np.arange(L)
    return np.where(d <= hi, d, d - L)


def _band_bias(rel_table):
    L = CA_WIN + TQ_CA
    e = _signed_mod_range(L, TQ_CA - 1)
    ridx = np.clip(CA_LEFT_CHUNKS * CHUNK + e, -(CHUNK - 1), CA_MAX_REL) + (CHUNK - 1)
    bias = _toeplitz(rel_table[:, ridx].astype(F32), CA_WIN, TQ_CA)
    kc = np.arange(CA_WIN)[:, None] // CHUNK
    qc = np.arange(TQ_CA)[None, :] // CHUNK + CA_LEFT_CHUNKS
    valid = (kc <= qc) & (kc >= qc - CA_LEFT_CHUNKS)
    return jnp.where(valid[None], bias, NEG_INF)


def _t5_bias(t5_table):
    TQ = TQ_SA
    L = 3 * TQ
    e = _signed_mod_range(L, TQ - 1)
    rel = jnp.asarray(-e - TQ, jnp.int32)
    far = t5_table[_t5_bucket(jnp.int32(-(TQ + 1)))].astype(F32)
    vec = (t5_table[_t5_bucket(rel)].astype(F32) - far[None, :]).T
    return _toeplitz(vec, 2 * TQ, TQ)


def kernel(x, c, t5_table, w_mod, b_mod, g_mix, w_in, pool_w, pool_scale, ca_rel, mla_g_cq, mla_g_ckv,
           mla_w_uq, mla_w_ukv, g_group, w_out, g_ffn, ffn_w1, ffn_w3, ffn_w2, g_final):
    B, S, D = x.shape
    assert D == D_MODEL and S % TM_PROJ == 0 and S % TQ_SA == 0 and S >= 4 * TOPK_MAX
    N = B * S
    mod_all = _modulation(c, w_mod, b_mod)
    rope_tabs = _rope_tables(S)
    nbias = _t5_bias(t5_table)
    row = lambda v: v.reshape(1, -1).astype(F32)
    x2 = x.reshape(N, D)
    for l in range(DEPTH):
        mod = mod_all[l].reshape(B, 6, D)
        w1, wt = _pack_in_weight(w_in[l])
        wq, wqs, wk, wvt = _pack_mla_weights(mla_w_uq[l], mla_w_ukv[l])
        (pool_u, ca, saq, sakv, iq, ik, iwt, svt, cavt, mq, mk, mvt) = _inproj(
            x2, mod, row(g_mix[l]), w1, wt, row(mla_g_cq[l]), row(mla_g_ckv[l]), wq, wqs, wk, wvt, rope_tabs, S)
        gg = g_group[l].reshape(4, 1, GROUP_W).astype(F32)
        wbd = jax.scipy.linalg.block_diag(*[pool_w[l, gi] for gi in range(len(POOL_WINDOWS))]).astype(BF16)
        bsw = lambda a: a.reshape(B, S, a.shape[-1])
        y_a = _pool(bsw(pool_u), wbd, row(pool_scale[l]), gg[0])
        y_b = _chunk_attention(bsw(ca), cavt, _band_bias(ca_rel[l]), gg[1])
        y_c = _sparse_attention(bsw(saq), bsw(sakv), svt, bsw(iq), bsw(ik), iwt, nbias, gg[2])
        y_d = _latent_attention(bsw(mq), bsw(mk), mvt, gg[3])
        ys = [y.reshape(N, GROUP_W) for y in (y_a, y_b, y_c, y_d)]
        x2 = _out_ffn(ys, x2, mod, w_out[l].astype(BF16), row(g_ffn[l]), ffn_w1[l].astype(BF16),
                      ffn_w3[l].astype(BF16), ffn_w2[l].astype(BF16), row(g_final), S,
                      final=(l == DEPTH - 1))
    return x2.reshape(B, S, D)
```

```python
import functools
import math
from statistics import NormalDist

import jax
import jax.numpy as jnp
from jax import lax
import numpy as np
from jax.experimental import pallas as pl
from jax.experimental.pallas import tpu as pltpu

F32 = jnp.float32
BF16 = jnp.bfloat16

D_MODEL = 1024
DEPTH = 2
CHUNK = 64
EPS = 1e-6
NEG_INF = -1e30
GROUP_W = 256
HEAD_DIM = 64
POOL_WINDOWS = (2, 4, 8, 16)
POOL_HALO = 16
CA_HEADS = 4
CA_LEFT_CHUNKS = 8
CA_MAX_REL = 256
SA_HEADS = 4
IDX_HEADS = 8
IDX_DIM = 64
TOPK_MAX = 256
MLA_HEADS = 4
MLA_NOPE = 64
MLA_ROPE = 32
MLA_V = 64
ROPE_BASE = 10000.0
T5_BUCKETS = 32
T5_MAX_DIST = 128
D_FF = 2816
IN_WIDTHS = (256, 256, 256, 256, 256, 64, 64, 512, 64, 8, 256, 128, 32)
IN_OFFS = tuple(int(v) for v in np.cumsum((0,) + IN_WIDTHS))

LANES = 128
VMEM_LIMIT = 56 * 1024 * 1024

TM_PROJ = 512
TM_FFN = 512
TP_POOL = 512
TQ_CA = 256
CA_WIN = TQ_CA + CA_LEFT_CHUNKS * CHUNK
CA_NBLK = CA_WIN // TQ_CA
IWT_ROWS = 16
TQ_SA = 256
KB_SA = 256
COUNT_CHAINS = 2
SEARCH_FIRST_ROUND = 16
SEARCH_ROUND = 4
GUESS_SPREAD = 0.3
TQ_MLA = 256
FF_CHUNK = 256

C_POOL = 0
C_CA = C_POOL + 256
C_SAQ = C_CA + 2 * GROUP_W
C_SAKV = C_SAQ + SA_HEADS * LANES
C_IQ = C_SAKV + LANES
C_IK = C_IQ + IDX_HEADS * IDX_DIM
C_CQ = C_IK + 2 * LANES
C_CKV = C_CQ + 256
C_KRF = C_CKV + LANES
C_KRS = C_KRF + LANES
C_END = C_KRS + LANES

INT_MIN = -2 ** 31
KEY_ALL = INT_MIN - int(np.array(-np.inf, np.float32).view(np.int32)) + 1


def _cparams(n_axes):
    return pltpu.CompilerParams(dimension_semantics=("arbitrary",) * n_axes,
                                vmem_limit_bytes=VMEM_LIMIT)


def _rms(x, g):
    return x * lax.rsqrt(jnp.mean(x * x, axis=-1, keepdims=True) + EPS) * g


def _dot(a, b):
    return jnp.dot(a, b, preferred_element_type=F32)


def _dot_t(a, b):
    return lax.dot_general(a, b, (((1,), (1,)), ((), ())), preferred_element_type=F32)


def _mod_kernel(c_ref, w_ref, b_ref, o_ref):
    c = c_ref[...]
    act = c * jax.nn.sigmoid(c)
    o_ref[0] = jnp.dot(act, w_ref[0], precision=lax.Precision.HIGHEST,
                       preferred_element_type=F32) + b_ref[0]


def _modulation(c, w_mod, b_mod):
    L, D, W = w_mod.shape
    B = c.shape[0]
    nj = W // D
    return pl.pallas_call(
        _mod_kernel,
        grid=(L, nj),
        in_specs=[pl.BlockSpec((B, D), lambda l, j: (0, 0)),
                  pl.BlockSpec((1, D, D), lambda l, j: (l, 0, j)),
                  pl.BlockSpec((1, 1, D), lambda l, j: (l, 0, j))],
        out_specs=pl.BlockSpec((1, B, D), lambda l, j: (l, 0, j)),
        out_shape=jax.ShapeDtypeStruct((L, B, W), F32),
        compiler_params=_cparams(2),
        name="modulation",
    )(c, w_mod, b_mod.reshape(L, 1, W))


def _inproj_kernel(x_ref, mod_ref, gmix_ref, w_ref, wt_ref, gcq_ref, gckv_ref, wq_ref, wqs_ref, wk_ref, wvt_ref,
                   cosq_ref, sinq_ref, cosk_ref, sink_ref,
                   pool_o, ca_o, saq_o, sakv_o, iq_o, ik_o, iwt_o, svt_o, cavt_o, mq_o, mk_o, mvt_o):
    sh1 = mod_ref[0, 0:1, :]
    sc1 = mod_ref[0, 1:2, :]
    h = (_rms(x_ref[...], gmix_ref[...]) * (1.0 + sc1) + sh1).astype(BF16)

    def seg(a, b):
        return _dot(h, w_ref[:, a:b])

    qn = _rms(seg(C_CQ, C_CKV), gcq_ref[...]).astype(BF16)
    kvn = _rms(seg(C_CKV, C_KRF), gckv_ref[...]).astype(BF16)
    krope = seg(C_KRF, C_KRS) * cosk_ref[...] + seg(C_KRS, C_END) * sink_ref[...]

    pool_o[...] = seg(C_POOL, C_CA)
    ca_o[...] = seg(C_CA, C_SAQ).astype(BF16)
    qf = _dot(qn, wq_ref[...])
    qs = _dot(qn, wqs_ref[...])
    saq_o[...] = seg(C_SAQ, C_SAKV).astype(BF16)
    cosq = jnp.concatenate([cosq_ref[...]] * MLA_HEADS, axis=1)
    sinq = jnp.concatenate([sinq_ref[...]] * MLA_HEADS, axis=1)
    mq_o[...] = (qf * cosq + qs * sinq).astype(BF16)
    kvf = _dot(kvn, wk_ref[...])
    sakv_o[...] = seg(C_SAKV, C_IQ).astype(BF16)
    iq_o[...] = seg(C_IQ, C_IK).astype(BF16)
    for hd in range(MLA_HEADS):
        mk_o[:, hd * LANES:(hd + 1) * LANES] = (kvf[:, hd * LANES:(hd + 1) * LANES] + krope).astype(BF16)
    mvt_o[...] = _dot_t(wvt_ref[...], kvn).astype(BF16)
    ik_o[...] = seg(C_IK, C_CQ).astype(BF16)
    tr = _dot_t(wt_ref[...], h)
    iwt_o[...] = tr[0:IWT_ROWS] * ((IDX_HEADS ** -0.5) * (IDX_DIM ** -0.5))
    svt_o[...] = tr[IWT_ROWS:IWT_ROWS + HEAD_DIM].astype(BF16)
    cavt_o[...] = tr[IWT_ROWS + HEAD_DIM:].astype(BF16)


def _inproj(x2, mod, gmix, w1, wt, gcq, gckv, wq, wqs, wk, wvt, rope_tabs, S):
    N, D = x2.shape
    TM = TM_PROJ
    nt = S // TM
    cosq, sinq, cosk, sink = rope_tabs

    def full(a):
        return pl.BlockSpec(a.shape, lambda i: (0,) * a.ndim)

    def tok(w):
        return pl.BlockSpec((TM, w), lambda i: (i, 0))

    tab = pl.BlockSpec((TM, LANES), lambda i: (i % nt, 0))
    def tokt(rows):
        return pl.BlockSpec((rows, TM), lambda i: (0, i))

    outs = [(C_CA - C_POOL, F32, True), (C_SAQ - C_CA, BF16, True), (C_SAKV - C_SAQ, BF16, True),
            (C_IQ - C_SAKV, BF16, True), (C_IK - C_IQ, BF16, True), (C_CQ - C_IK, BF16, True),
            (IWT_ROWS, F32, False), (HEAD_DIM, BF16, False), (GROUP_W, BF16, False),
            (MLA_HEADS * LANES, BF16, True), (MLA_HEADS * LANES, BF16, True), (GROUP_W, BF16, False)]
    return pl.pallas_call(
        _inproj_kernel,
        grid=(N // TM,),
        in_specs=[tok(D),
                  pl.BlockSpec((1, 6, D), lambda i: (i // nt, 0, 0)),
                  full(gmix), full(w1), full(wt), full(gcq), full(gckv), full(wq), full(wqs), full(wk), full(wvt),
                  tab, tab, tab, tab],
        out_specs=[tok(w) if tm else tokt(w) for w, _, tm in outs],
        out_shape=[jax.ShapeDtypeStruct((N, w) if tm else (w, N), dt) for w, dt, tm in outs],
        compiler_params=_cparams(1),
        name="inproj",
    )(x2, mod, gmix, w1, wt, gcq, gckv, wq, wqs, wk, wvt, cosq, sinq, cosk, sink)


def _pool_kernel(u_ref, halo_ref, w_ref, scale_ref, g_ref, o_ref, pad_scr):
    i = pl.program_id(1)
    TP = u_ref.shape[1]
    u = u_ref[0]
    pad_scr[0:POOL_HALO, :] = jnp.where(i > 0, halo_ref[0], 0.0)
    pad_scr[POOL_HALO:, :] = u

    def shifted(j):
        return pad_scr[POOL_HALO - j:POOL_HALO - j + TP, :]

    lane = lax.broadcasted_iota(jnp.int32, (TP, GROUP_W), 1)
    w2 = u + shifted(1)
    w4 = w2 + shifted(2) + shifted(3)
    w8 = w4
    for j in range(4, 8):
        w8 = w8 + shifted(j)
    w16 = w8
    for j in range(8, 16):
        w16 = w16 + shifted(j)
    win = jnp.where(lane < 64, w2, jnp.where(lane < 128, w4, jnp.where(lane < 192, w8, w16)))
    wlen = jnp.where(lane < 64, 2, jnp.where(lane < 128, 4, jnp.where(lane < 192, 8, 16)))
    t = i * TP + lax.broadcasted_iota(jnp.int32, (TP, GROUP_W), 0)
    cnt = jnp.minimum(t + 1, wlen).astype(F32)
    d = (win / cnt - u).astype(BF16)
    y = _dot(d, w_ref[...]) * scale_ref[...]
    o_ref[0] = _rms(y, g_ref[...]).astype(BF16)


def _pool(u, wbd, scale, g):
    B, S, W = u.shape
    TP = TP_POOL
    hb = TP // POOL_HALO
    return pl.pallas_call(
        _pool_kernel,
        grid=(B, S // TP),
        in_specs=[pl.BlockSpec((1, TP, W), lambda b, i: (b, i, 0)),
                  pl.BlockSpec((1, POOL_HALO, W), lambda b, i: (b, jnp.maximum(i * hb - 1, 0), 0)),
                  pl.BlockSpec((W, W), lambda b, i: (0, 0)),
                  pl.BlockSpec((1, W), lambda b, i: (0, 0)),
                  pl.BlockSpec((1, W), lambda b, i: (0, 0))],
        out_specs=pl.BlockSpec((1, TP, W), lambda b, i: (b, i, 0)),
        out_shape=jax.ShapeDtypeStruct((B, S, W), BF16),
        scratch_shapes=[pltpu.VMEM((POOL_HALO + TP, W), F32)],
        compiler_params=_cparams(2),
        name="pool_mixer",
    )(u, u, wbd, scale, g)


def _ca_kernel(q_ref, k_ref, vt_ref, bias_ref, g_ref, o_ref):
    i = pl.program_id(1)
    TQ = TQ_CA
    lane = lax.broadcasted_iota(jnp.int32, (TQ, LANES), 1)
    starts = []
    for j in range(CA_NBLK):
        kb = i - (CA_NBLK - 1) + j
        starts.append((kb >= 0, pl.multiple_of(jnp.maximum(kb, 0) * TQ, TQ)))
    scored = []
    for hd in range(CA_HEADS):
        cols = slice((hd // 2) * LANES, (hd // 2 + 1) * LANES)
        keep = (lane < HEAD_DIM) if hd % 2 == 0 else (lane >= HEAD_DIM)
        qh = jnp.where(keep, q_ref[0, :, cols].astype(F32), 0.0).astype(BF16)
        parts = []
        for j, (present, start) in enumerate(starts):
            s = _dot_t(k_ref[0, pl.ds(start, TQ), cols], qh) + bias_ref[hd, j * TQ:(j + 1) * TQ, :]
            parts.append(jnp.where(present, s, NEG_INF))
        m = parts[0].max(axis=0, keepdims=True)
        for s in parts[1:]:
            m = jnp.maximum(m, s.max(axis=0, keepdims=True))
        scored.append((parts, m))
    outs = []
    for hd, (parts, m) in enumerate(scored):
        l = jnp.zeros((1, TQ), F32)
        acc = jnp.zeros((HEAD_DIM, TQ), F32)
        for j, (_, start) in enumerate(starts):
            p = jnp.exp(parts[j] - m)
            l = l + p.sum(axis=0, keepdims=True)
            acc = acc + _dot(vt_ref[hd * HEAD_DIM:(hd + 1) * HEAD_DIM, pl.ds(start, TQ)], p.astype(BF16))
        outs.append(acc / l)
    o_ref[0] = _group_norm_t(jnp.concatenate(outs, axis=0), g_ref[...])


def _chunk_attention(caqk, cavt, bias, g):
    B, S, _ = caqk.shape
    W = GROUP_W
    TQ = TQ_CA
    return pl.pallas_call(
        _ca_kernel,
        grid=(B, S // TQ),
        in_specs=[pl.BlockSpec((1, TQ, W), lambda b, i: (b, i, 0)),
                  pl.BlockSpec((1, S, W), lambda b, i: (b, 0, 1)),
                  pl.BlockSpec((W, S), lambda b, i: (0, b)),
                  pl.BlockSpec(bias.shape, lambda b, i: (0, 0, 0)),
                  pl.BlockSpec((1, W), lambda b, i: (0, 0))],
        out_specs=pl.BlockSpec((1, TQ, W), lambda b, i: (b, i, 0)),
        out_shape=jax.ShapeDtypeStruct((B, S, W), BF16),
        compiler_params=_cparams(2),
        name="band_attention",
    )(caqk, caqk, cavt, bias, g)


def _score_key(score):
    b = lax.bitcast_convert_type(score, jnp.int32)
    return jnp.where(b < 0, jnp.int32(INT_MIN) - b, b)


def _sa_kernel(q_ref, kv_ref, vt_ref, iq_ref, ik_ref, iwt_ref, zq_ref, nbias_ref, g_ref, o_ref,
               key_scr, s0_scr, s1_scr, smax0_scr, smax1_scr):
    s_scr, smax_scr = (s0_scr, s1_scr), (smax0_scr, smax1_scr)
    i = pl.program_id(1)
    TQ, KB = TQ_SA, KB_SA
    K = float(TOPK_MAX)
    nb = i + 1
    q0 = i * TQ
    cshift = CHUNK.bit_length() - 1
    kchunk = lax.broadcasted_iota(jnp.int32, (KB, TQ), 0) >> cshift
    qchunk = (q0 + lax.broadcasted_iota(jnp.int32, (1, TQ), 1)) >> cshift

    iwt = iwt_ref[...]

    def score_block(j, carry, tail, moments):
        smax, s1, s2 = carry
        k0 = pl.multiple_of(j * KB, KB)
        ik = ik_ref[0, pl.ds(k0, KB), :]
        ik2 = jnp.concatenate([ik[:, :LANES], ik[:, LANES:]], axis=0)
        sc = jnp.zeros((KB, TQ), F32)
        for p in range(IDX_HEADS // 2):
            logits = _dot_t(ik2, iq_ref[0, :, p * LANES:(p + 1) * LANES])
            sc = sc + iwt[2 * p:2 * p + 1, :] * jnp.maximum(logits[:KB], 0.0)
            sc = sc + iwt[2 * p + 1:2 * p + 2, :] * jnp.maximum(logits[KB:], 0.0)
        if tail:
            sc = jnp.where(kchunk <= qchunk - (k0 >> cshift), sc, -jnp.inf)
        key_scr[pl.ds(k0, KB), :] = _score_key(sc)
        smax = jnp.maximum(smax, sc.max(axis=0, keepdims=True))
        if moments:
            s1 = s1 + sc.sum(axis=0, keepdims=True)
            s2 = s2 + (sc * sc).sum(axis=0, keepdims=True)
        return smax, s1, s2

    n_pairs = (nb + 1) // 2
    carry = lax.fori_loop(
        0, n_pairs - 1, lambda j, c: score_block(2 * j + 1, score_block(2 * j, c, False, True), False, False),
        (jnp.full((1, TQ), -jnp.inf, F32), jnp.zeros((1, TQ), F32), jnp.zeros((1, TQ), F32)))
    last = 2 * (n_pairs - 1)
    carry = score_block(last, carry, True, False)

    def pad_block(c):
        key_scr[pl.ds(pl.multiple_of((last + 1) * KB, KB), KB), :] = jnp.full((KB, TQ), KEY_ALL - 1, jnp.int32)
        return c

    smax, s1, s2 = lax.cond(nb % 2 == 1, pad_block, lambda c: score_block(last + 1, c, True, False), carry)
    n_moments = ((n_pairs - 1) * KB).astype(F32)

    def count_ge(cand):
        def body(j, acc):
            blk = key_scr[pl.ds(pl.multiple_of(j * (2 * KB), 2 * KB), 2 * KB), :]
            ones = jnp.where(blk >= cand, 1.0, 0.0)
            return acc + ones.reshape(COUNT_CHAINS, -1, 8, TQ).sum(axis=1)
        acc = lax.fori_loop(0, (nb + 1) // 2, body, jnp.zeros((COUNT_CHAINS, 8, TQ), F32))
        return acc.sum(axis=0).sum(axis=0, keepdims=True)

    def max_below(bound):
        def body(j, acc):
            blk = key_scr[pl.ds(pl.multiple_of(j * (2 * KB), 2 * KB), 2 * KB), :]
            below = jnp.where(blk < bound, blk, jnp.int32(INT_MIN))
            return jnp.maximum(acc, below.reshape(COUNT_CHAINS, -1, 8, TQ).max(axis=1))
        acc = lax.fori_loop(0, (nb + 1) // 2, body, jnp.full((COUNT_CHAINS, 8, TQ), INT_MIN, jnp.int32))
        return acc.max(axis=0).max(axis=0, keepdims=True)

    def search():
        def unkey(k):
            return lax.bitcast_convert_type(jnp.where(k < 0, jnp.int32(INT_MIN) - k, k), F32)

        def is_active(lo, hi, clo):
            return jnp.logical_and(clo > K, hi > lo + 1)

        def cond(st):
            _, lo, hi, clo, _ = st
            act = jnp.where(is_active(lo, hi, clo), 1.0, 0.0)
            return jnp.max(jnp.maximum(act[:, :LANES], act[:, LANES:])) > 0.0

        mean = s1 / n_moments
        std = jnp.sqrt(jnp.maximum(s2 / n_moments - mean * mean, 0.0))
        zq = jnp.max(zq_ref[...], axis=0, keepdims=True)
        guess_lo = _score_key(mean + (zq - GUESS_SPREAD) * std)
        guess_hi = _score_key(mean + (zq + GUESS_SPREAD) * std)

        def step(st, peel):
            it, lo, hi, clo, chi = st
            active = is_active(lo, hi, clo)
            if peel:
                cand = max_below(hi)
            else:
                lf, hf = unkey(lo), unkey(hi)
                lc = jnp.log(clo)
                frac = jnp.clip((lc - math.log(K - 0.5)) / (lc - jnp.log(jnp.maximum(chi, 0.5))), 0.05, 0.95)
                cand = _score_key(lf + frac * (hf - lf))
                cand = jnp.where(it % 3 == 2, (lo >> 1) + (hi >> 1) + (lo & hi & 1), cand)
                cand = jnp.where(it == 0, guess_lo, cand)
                cand = jnp.where(it == 1, guess_hi, cand)
                cand = jnp.clip(cand, lo + 1, hi - 1)
            cand = jnp.where(active, cand, lo)
            cnt = count_ge(cand)
            up = jnp.logical_and(active, cnt >= K)
            down = jnp.logical_and(active, cnt < K)
            hi = jnp.where(down, cand, jnp.where(up, cand + 1, hi) if peel else hi)
            return (it + 1, jnp.where(up, cand, lo), hi, jnp.where(up, cnt, clo), jnp.where(down, cnt, chi))

        def steps(n, st):
            return lax.fori_loop(0, n, lambda _, s: step(s, False), st)

        lo0 = jnp.full((1, TQ), KEY_ALL - 1, jnp.int32)
        hi0 = _score_key(smax) + 1
        clo0 = jnp.zeros((1, TQ), F32) + ((nb + 1) // 2 * (2 * KB)).astype(F32)
        st = (jnp.int32(0), lo0, hi0, clo0, jnp.zeros((1, TQ), F32))
        st = steps(SEARCH_FIRST_ROUND - 1, st)
        st = step(st, True)
        st = lax.while_loop(cond, lambda s: step(steps(SEARCH_ROUND - 1, s), True), st)
        return st[1], st[3]

    def no_search():
        return jnp.full((1, TQ), KEY_ALL, jnp.int32), jnp.full((1, TQ), K, F32)

    t, cnt_t = lax.cond(i > 0, search, no_search)
    t = jnp.maximum(t, KEY_ALL)

    @pl.when(jnp.max(cnt_t) > K)
    def _():
        allowed = K - count_ge(t + 1)
        r = lax.broadcasted_iota(jnp.int32, (KB, KB), 0)
        c = lax.broadcasted_iota(jnp.int32, (KB, KB), 1)
        earlier = jnp.where(c < r, 1.0, 0.0).astype(BF16)

        def body(jj, seen):
            sls = [pl.ds(pl.multiple_of((2 * jj + u) * KB, KB), KB) for u in range(2)]
            blks = [key_scr[sl, :] for sl in sls]
            eqs = [jnp.where(blk == t, 1.0, 0.0) for blk in blks]
            seens = [seen, seen + eqs[0].sum(axis=0, keepdims=True)]
            ranks = [_dot(earlier, eq.astype(BF16)) + sn for eq, sn in zip(eqs, seens)]
            for sl, blk, eq, rank in zip(sls, blks, eqs, ranks):
                demote = eq * jnp.where(rank >= allowed, 1.0, 0.0)
                key_scr[sl, :] = jnp.where(demote > 0.5, t - 1, blk)
            return seens[1] + eqs[1].sum(axis=0, keepdims=True)

        lax.fori_loop(0, (nb + 1) // 2, body, jnp.zeros((1, TQ), F32))

    def scores_to(slot, j, bias_rows, present):
        k0 = pl.multiple_of(j * KB, KB)
        kblk = kv_ref[0, pl.ds(k0, KB), :]
        sel = key_scr[pl.ds(k0, KB), :] >= (t if present is True else jnp.where(present, t, jnp.int32(2 ** 31 - 1)))
        for hd in range(SA_HEADS):
            s = _dot_t(kblk, q_ref[0, :, hd * LANES:(hd + 1) * LANES])
            if bias_rows is not None:
                s = s + nbias_ref[hd, bias_rows, :]
            s = jnp.where(sel, s, NEG_INF)
            s_scr[slot][hd] = s
            smax_scr[slot][hd] = jnp.broadcast_to(s.max(axis=0, keepdims=True), smax_scr[slot].shape[1:])

    def softmax_pv(slot, j, st):
        vt = vt_ref[:, pl.ds(pl.multiple_of(j * KB, KB), KB)]
        out = []
        for hd in range(SA_HEADS):
            m, l, acc = st[hd]
            mn = jnp.maximum(m, smax_scr[slot][hd][0:1])
            p = jnp.exp(s_scr[slot][hd] - mn)
            alpha = jnp.exp(m - mn)
            out.append((mn, alpha * l + p.sum(axis=0, keepdims=True), alpha * acc + _dot(vt, p.astype(BF16))))
        return tuple(out)

    st = tuple((jnp.full((1, TQ), NEG_INF, F32), jnp.zeros((1, TQ), F32), jnp.zeros((HEAD_DIM, TQ), F32))
               for _ in range(SA_HEADS))
    left = jnp.maximum(i - 1, 0)
    tails = [(left, lambda slot: scores_to(slot, left, slice(0, KB), i >= 1)),
             (i, lambda slot: scores_to(slot, i, slice(KB, 2 * KB), True))]
    st = _block_pipeline(left, i, lambda slot, j, present: scores_to(slot, j, None, present), tails, softmax_pv, st)
    y_t = jnp.concatenate([acc / l for _, l, acc in st], axis=0)
    o_ref[0] = _group_norm_t(y_t, g_ref[...])


def _sparse_attention(saq, sakv, svt, iq, ik, iwt, nbias, g):
    B, S, _ = saq.shape
    TQ = TQ_SA
    W = GROUP_W
    nt = S // TQ
    n_adm = (np.arange(S) // CHUNK + 1) * CHUNK
    zq = np.array([NormalDist().inv_cdf(1.0 - TOPK_MAX / n) if n > TOPK_MAX else 0.0 for n in n_adm], np.float32)
    zq = jnp.asarray(np.tile(zq[None, :], (8, 1)))
    return pl.pallas_call(
        _sa_kernel,
        grid=(B, nt),
        in_specs=[pl.BlockSpec((1, TQ, saq.shape[2]), lambda b, i: (b, i, 0)),
                  pl.BlockSpec((1, S, sakv.shape[2]), lambda b, i: (b, 0, 0)),
                  pl.BlockSpec((HEAD_DIM, S), lambda b, i: (0, b)),
                  pl.BlockSpec((1, TQ, iq.shape[2]), lambda b, i: (b, i, 0)),
                  pl.BlockSpec((1, S, ik.shape[2]), lambda b, i: (b, 0, 0)),
                  pl.BlockSpec((IWT_ROWS, TQ), lambda b, i: (0, b * nt + i)),
                  pl.BlockSpec((8, TQ), lambda b, i: (0, i)),
                  pl.BlockSpec(nbias.shape, lambda b, i: (0, 0, 0)),
                  pl.BlockSpec((1, W), lambda b, i: (0, 0))],
        out_specs=pl.BlockSpec((1, TQ, W), lambda b, i: (b, i, 0)),
        out_shape=jax.ShapeDtypeStruct((B, S, W), BF16),
        scratch_shapes=([pltpu.VMEM((S, TQ), jnp.int32)] + [pltpu.VMEM((SA_HEADS, KB_SA, TQ), F32)] * 2
                        + [pltpu.VMEM((SA_HEADS, 8, TQ), F32)] * 2),
        compiler_params=_cparams(2),
        name="sparse_attention",
    )(saq, sakv, svt, iq, ik, iwt, zq, nbias, g)


def _group_norm_t(y_t, g):
    inv = lax.rsqrt(jnp.mean(y_t * y_t, axis=0, keepdims=True) + EPS)
    return ((y_t * inv).T * g).astype(BF16)


def _block_pipeline(n_plain, last_blk, score_plain, tails, softmax, st):
    off = n_plain % 2
    n_loop = jnp.maximum((n_plain + off) // 2 - 1, 0)

    def blk(pos):
        return jnp.clip(pos - off, 0, last_blk)

    def body(pp, st):
        pos = 2 * pp
        score_plain(1, blk(pos + 1), True)
        st = softmax(0, blk(pos), st)
        score_plain(0, blk(pos + 2), True)
        return softmax(1, blk(pos + 1), st)

    score_plain(0, blk(0), jnp.logical_and(n_plain >= 1, off == 0))
    st = lax.fori_loop(0, n_loop, body, st)
    e0 = 2 * n_loop
    slot, pending = 0, blk(e0)
    steps = [(blk(e0 + 1), lambda s: score_plain(s, blk(e0 + 1), n_plain >= 1))] + list(tails)
    for nxt, score_fn in steps:
        score_fn(1 - slot)
        st = softmax(slot, pending, st)
        slot, pending = 1 - slot, nxt
    return softmax(slot, pending, st)


def _mla_kernel(q_ref, k_ref, vt_ref, g_ref, o_ref, s0_scr, s1_scr, smax0_scr, smax1_scr):
    i = pl.program_id(1)
    TQ = TQ_MLA
    cshift = CHUNK.bit_length() - 1
    kch = lax.broadcasted_iota(jnp.int32, (TQ, TQ), 0) >> cshift
    qch = lax.broadcasted_iota(jnp.int32, (TQ, TQ), 1) >> cshift
    s_scr, smax_scr = (s0_scr, s1_scr), (smax0_scr, smax1_scr)

    def scores_to(slot, j, keep):
        k0 = pl.multiple_of(j * TQ, TQ)
        for hd in range(MLA_HEADS):
            cols = slice(hd * LANES, (hd + 1) * LANES)
            s = _dot_t(k_ref[0, pl.ds(k0, TQ), cols], q_ref[0, :, cols])
            if keep is not None:
                s = jnp.where(keep, s, NEG_INF)
            s_scr[slot][hd] = s
            smax_scr[slot][hd] = jnp.broadcast_to(s.max(axis=0, keepdims=True), smax_scr[slot].shape[1:])

    def score_plain(slot, j, present):
        scores_to(slot, j, None if present is True else kch >= jnp.where(present, 0, TQ))

    def softmax_pv(slot, j, st):
        k0 = pl.multiple_of(j * TQ, TQ)
        out = []
        for hd in range(MLA_HEADS):
            m, l, acc = st[hd]
            mn = jnp.maximum(m, smax_scr[slot][hd][0:1])
            p = jnp.exp(s_scr[slot][hd] - mn)
            alpha = jnp.exp(m - mn)
            vt = vt_ref[hd * MLA_V:(hd + 1) * MLA_V, pl.ds(k0, TQ)]
            out.append((mn, alpha * l + p.sum(axis=0, keepdims=True),
                        alpha * acc + _dot(vt, p.astype(BF16))))
        return tuple(out)

    st = tuple((jnp.full((1, TQ), NEG_INF, F32), jnp.zeros((1, TQ), F32), jnp.zeros((MLA_V, TQ), F32))
               for _ in range(MLA_HEADS))
    st = _block_pipeline(i, i, score_plain, [(i, lambda slot: scores_to(slot, i, kch <= qch))], softmax_pv, st)
    o_ref[0] = _group_norm_t(jnp.concatenate([acc / l for _, l, acc in st], axis=0), g_ref[...])


def _latent_attention(mq, mk, mvt, g):
    B, S, _ = mq.shape
    TQ = TQ_MLA
    W = GROUP_W
    return pl.pallas_call(
        _mla_kernel,
        grid=(B, S // TQ),
        in_specs=[pl.BlockSpec((1, TQ, mq.shape[2]), lambda b, i: (b, i, 0)),
                  pl.BlockSpec((1, S, mk.shape[2]), lambda b, i: (b, 0, 0)),
                  pl.BlockSpec((W, S), lambda b, i: (0, b)),
                  pl.BlockSpec((1, W), lambda b, i: (0, 0))],
        out_specs=pl.BlockSpec((1, TQ, W), lambda b, i: (b, i, 0)),
        out_shape=jax.ShapeDtypeStruct((B, S, W), BF16),
        scratch_shapes=[pltpu.VMEM((MLA_HEADS, TQ, TQ), F32)] * 2 + [pltpu.VMEM((MLA_HEADS, 8, TQ), F32)] * 2,
        compiler_params=_cparams(2),
        name="latent_attention",
    )(mq, mk, mvt, g)


def _ffn_kernel(ya_ref, yb_ref, yc_ref, yd_ref, x_ref, mod_ref, wout_ref, gffn_ref, w1_ref, w3_ref, w2_ref,
                gfin_ref, o_ref, acc_scr, *, final):
    gt1 = mod_ref[0, 2:3, :]
    sh2 = mod_ref[0, 3:4, :]
    sc2 = mod_ref[0, 4:5, :]
    gt2 = mod_ref[0, 5:6, :]
    attn = _dot(ya_ref[...], wout_ref[0:GROUP_W, :])
    for gi, y_ref in enumerate((yb_ref, yc_ref, yd_ref), start=1):
        attn = attn + _dot(y_ref[...], wout_ref[gi * GROUP_W:(gi + 1) * GROUP_W, :])
    x1 = x_ref[...] + gt1 * attn
    h = (_rms(x1, gffn_ref[...]) * (1.0 + sc2) + sh2).astype(BF16)
    for ci in range(D_FF // FF_CHUNK):
        cols = slice(ci * FF_CHUNK, (ci + 1) * FF_CHUNK)
        a = _dot(h, w1_ref[:, cols])
        gate = (a * jax.nn.sigmoid(a) * _dot(h, w3_ref[:, cols])).astype(BF16)
        part = _dot(gate, w2_ref[cols, :])
        if ci == 0:
            acc_scr[...] = part
        else:
            acc_scr[...] += part
    x2 = x1 + gt2 * acc_scr[...]
    o_ref[...] = _rms(x2, gfin_ref[...]) if final else x2


def _out_ffn(ys, x2, mod, wout, gffn, w1, w3, w2, gfin, S, final):
    N, D = x2.shape
    TM = TM_FFN
    nt = S // TM

    def full(a):
        return pl.BlockSpec(a.shape, lambda i: (0,) * a.ndim, pipeline_mode=pl.Buffered(1))

    def tok(w):
        return pl.BlockSpec((TM, w), lambda i: (i, 0))

    return pl.pallas_call(
        functools.partial(_ffn_kernel, final=final),
        grid=(N // TM,),
        in_specs=[tok(GROUP_W)] * 4 + [tok(D), pl.BlockSpec((1, 6, D), lambda i: (i // nt, 0, 0)),
                                       full(wout), full(gffn), full(w1), full(w3), full(w2), full(gfin)],
        out_specs=tok(D),
        out_shape=jax.ShapeDtypeStruct((N, D), F32),
        scratch_shapes=[pltpu.VMEM((TM, D), F32)],
        compiler_params=_cparams(1),
        name="out_ffn_final" if final else "out_ffn",
    )(*ys, x2, mod, wout, gffn, w1, w3, w2, gfin)


def _t5_bucket(rel):
    nb = T5_BUCKETS // 2
    max_exact = nb // 2
    ret = jnp.where(rel > 0, nb, 0)
    n = jnp.abs(rel)
    nf = jnp.maximum(n, 1).astype(jnp.float32)
    large = max_exact + (jnp.log(nf / max_exact) / math.log(T5_MAX_DIST / max_exact)
                         * (nb - max_exact)).astype(jnp.int32)
    large = jnp.minimum(large, nb - 1)
    return ret + jnp.where(n < max_exact, n, large)


def _rope_tables(S):
    half = MLA_ROPE // 2
    freqs = ROPE_BASE ** (-jnp.arange(half, dtype=F32) / half)
    ang = jnp.arange(S, dtype=jnp.int32).astype(F32)[:, None] * freqs[None, :]
    cos, sin = jnp.cos(ang), jnp.sin(ang)
    cos2 = jnp.concatenate([cos, cos], axis=1)
    sin2 = jnp.concatenate([-sin, sin], axis=1)
    zeros = jnp.zeros((S, LANES - MLA_NOPE - MLA_ROPE), F32)
    scale = (MLA_NOPE + MLA_ROPE) ** -0.5
    cosq = jnp.concatenate([jnp.full((S, MLA_NOPE), scale, F32), cos2 * scale, zeros], axis=1)
    sinq = jnp.concatenate([jnp.zeros((S, MLA_NOPE), F32), sin2 * scale, zeros], axis=1)
    cosk = jnp.concatenate([jnp.zeros((S, MLA_NOPE), F32), cos2, zeros], axis=1)
    sink = jnp.concatenate([jnp.zeros((S, MLA_NOPE), F32), sin2, zeros], axis=1)
    return cosq, sinq, cosk, sink


def _pack_in_weight(w):
    part = {n: w[:, IN_OFFS[k]:IN_OFFS[k + 1]] for k, n in enumerate(
        ('pool_u', 'ca_q', 'ca_k', 'ca_v', 'sa_q', 'sa_k', 'sa_v', 'idx_q', 'idx_k', 'idx_w',
         'mla_cq', 'mla_ckv', 'mla_kr'))}
    D = w.shape[0]
    z = lambda n: jnp.zeros((D, n), F32)
    qscale = HEAD_DIM ** -0.5
    saq = part['sa_q'].reshape(D, SA_HEADS, HEAD_DIM) * qscale
    saq = jnp.concatenate([saq, jnp.zeros_like(saq)], axis=2).reshape(D, SA_HEADS * LANES)
    kr = part['mla_kr']
    kr_swap = jnp.concatenate([kr[:, MLA_ROPE // 2:], kr[:, :MLA_ROPE // 2]], axis=1)
    pad_r = LANES - MLA_NOPE - MLA_ROPE
    cols = [part['pool_u'], part['ca_q'] * qscale, part['ca_k'], saq,
            part['sa_k'], part['sa_v'], part['idx_q'],
            part['idx_k'], z(IDX_DIM), z(IDX_DIM), part['idx_k'],
            part['mla_cq'], part['mla_ckv'],
            z(MLA_NOPE), kr, z(pad_r), z(MLA_NOPE), kr_swap, z(pad_r)]
    out = jnp.concatenate(cols, axis=1)
    assert out.shape[1] == C_END
    wt = jnp.concatenate([part['idx_w'].T, jnp.zeros((IWT_ROWS - IDX_HEADS, D), F32), part['sa_v'].T,
                          part['ca_v'].T], axis=0)
    return out.astype(BF16), wt.astype(BF16)


def _pack_mla_weights(w_uq, w_ukv):
    R = w_uq.shape[0]
    pad = jnp.zeros((R, MLA_HEADS, LANES - MLA_NOPE - MLA_ROPE), F32)
    rope_w = w_uq[:, :, MLA_NOPE:]
    rope_sw = jnp.concatenate([rope_w[:, :, MLA_ROPE // 2:], rope_w[:, :, :MLA_ROPE // 2]], axis=2)
    wq = jnp.concatenate([w_uq, pad], axis=2).reshape(R, MLA_HEADS * LANES)
    wqs = jnp.concatenate([jnp.zeros((R, MLA_HEADS, MLA_NOPE), F32), rope_sw, pad],
                          axis=2).reshape(R, MLA_HEADS * LANES)
    Rk = w_ukv.shape[0]
    wk = jnp.concatenate([w_ukv[:, :, :MLA_NOPE], jnp.zeros((Rk, MLA_HEADS, LANES - MLA_NOPE), F32)],
                         axis=2).reshape(Rk, MLA_HEADS * LANES)
    wvt = w_ukv[:, :, MLA_NOPE:].reshape(Rk, MLA_HEADS * MLA_V).T
    return wq.astype(BF16), wqs.astype(BF16), wk.astype(BF16), wvt.astype(BF16)


def _toeplitz(vec, rows, cols):
    L = vec.shape[-1]
    assert cols <= L - 1
    flat = jnp.tile(vec, (1, rows))[:, :rows * (L - 1)]
    return flat.reshape(vec.shape[0], rows, L - 1)[:, :, :cols]


def _signed_mod_range(L, hi):
    d = np.arange(L)
    return np.where(d <= hi, d, d - L)


def _band_bias(rel_table):
    L = CA_WIN + TQ_CA
    e = _signed_mod_range(L, TQ_CA - 1)
    ridx = np.clip(CA_LEFT_CHUNKS * CHUNK + e, -(CHUNK - 1), CA_MAX_REL) + (CHUNK - 1)
    bias = _toeplitz(rel_table[:, ridx].astype(F32), CA_WIN, TQ_CA)
    kc = np.arange(CA_WIN)[:, None] // CHUNK
    qc = np.arange(TQ_CA)[None, :] // CHUNK + CA_LEFT_CHUNKS
    valid = (kc <= qc) & (kc >= qc - CA_LEFT_CHUNKS)
    return jnp.where(valid[None], bias, NEG_INF)


def _t5_bias(t5_table):
    TQ = TQ_SA
    L = 3 * TQ
    e = _signed_mod_range(L, TQ - 1)
    rel = jnp.asarray(-e - TQ, jnp.int32)
    far = t5_table[_t5_bucket(jnp.int32(-(TQ + 1)))].astype(F32)
    vec = (t5_table[_t5_bucket(rel)].astype(F32) - far[None, :]).T
    return _toeplitz(vec, 2 * TQ, TQ)


def kernel(x, c, t5_table, w_mod, b_mod, g_mix, w_in, pool_w, pool_scale, ca_rel, mla_g_cq, mla_g_ckv,
           mla_w_uq, mla_w_ukv, g_group, w_out, g_ffn, ffn_w1, ffn_w3, ffn_w2, g_final):
    B, S, D = x.shape
    assert D == D_MODEL and S % TM_PROJ == 0 and S % TQ_SA == 0 and S >= 4 * TOPK_MAX
    N = B * S
    mod_all = _modulation(c, w_mod, b_mod)
    rope_tabs = _rope_tables(S)
    nbias = _t5_bias(t5_table)
    row = lambda v: v.reshape(1, -1).astype(F32)
    x2 = x.reshape(N, D)
    for l in range(DEPTH):
        mod = mod_all[l].reshape(B, 6, D)
        w1, wt = _pack_in_weight(w_in[l])
        wq, wqs, wk, wvt = _pack_mla_weights(mla_w_uq[l], mla_w_ukv[l])
        (pool_u, ca, saq, sakv, iq, ik, iwt, svt, cavt, mq, mk, mvt) = _inproj(
            x2, mod, row(g_mix[l]), w1, wt, row(mla_g_cq[l]), row(mla_g_ckv[l]), wq, wqs, wk, wvt, rope_tabs, S)
        gg = g_group[l].reshape(4, 1, GROUP_W).astype(F32)
        wbd = jax.scipy.linalg.block_diag(*[pool_w[l, gi] for gi in range(len(POOL_WINDOWS))]).astype(BF16)
        bsw = lambda a: a.reshape(B, S, a.shape[-1])
        y_a = _pool(bsw(pool_u), wbd, row(pool_scale[l]), gg[0])
        y_b = _chunk_attention(bsw(ca), cavt, _band_bias(ca_rel[l]), gg[1])
        y_c = _sparse_attention(bsw(saq), bsw(sakv), svt, bsw(iq), bsw(ik), iwt, nbias, gg[2])
        y_d = _latent_attention(bsw(mq), bsw(mk), mvt, gg[3])
        ys = [y.reshape(N, GROUP_W) for y in (y_a, y_b, y_c, y_d)]
        x2 = _out_ffn(ys, x2, mod, w_out[l].astype(BF16), row(g_ffn[l]), ffn_w1[l].astype(BF16),
                      ffn_w3[l].astype(BF16), ffn_w2[l].astype(BF16), row(g_final), S,
                      final=(l == DEPTH - 1))
    return x2.reshape(B, S, D)
```

```python
import functools
import math
from statistics import NormalDist

import jax
import jax.numpy as jnp
from jax import lax
import numpy as np
from jax.experimental import pallas as pl
from jax.experimental.pallas import tpu as pltpu

F32 = jnp.float32
BF16 = jnp.bfloat16

D_MODEL = 1024
DEPTH = 2
CHUNK = 64
EPS = 1e-6
NEG_INF = -1e30
GROUP_W = 256
HEAD_DIM = 64
POOL_WINDOWS = (2, 4, 8, 16)
POOL_HALO = 16
CA_HEADS = 4
CA_LEFT_CHUNKS = 8
CA_MAX_REL = 256
SA_HEADS = 4
IDX_HEADS = 8
IDX_DIM = 64
TOPK_MAX = 256
MLA_HEADS = 4
MLA_NOPE = 64
MLA_ROPE = 32
MLA_V = 64
ROPE_BASE = 10000.0
T5_BUCKETS = 32
T5_MAX_DIST = 128
D_FF = 2816
IN_WIDTHS = (256, 256, 256, 256, 256, 64, 64, 512, 64, 8, 256, 128, 32)
IN_OFFS = tuple(int(v) for v in np.cumsum((0,) + IN_WIDTHS))

LANES = 128
VMEM_LIMIT = 56 * 1024 * 1024

TM_PROJ = 512
TM_FFN = 512
TP_POOL = 512
TQ_CA = 256
CA_WIN = TQ_CA + CA_LEFT_CHUNKS * CHUNK
CA_NBLK = CA_WIN // TQ_CA
IWT_ROWS = 16
TQ_SA = 256
KB_SA = 256
COUNT_CHAINS = 2
SEARCH_FIRST_ROUND = 16
SEARCH_ROUND = 4
GUESS_SPREAD = 0.3
TQ_MLA = 512
KB_MLA = 256
FF_CHUNK = 256

C_POOL = 0
C_CA = C_POOL + 256
C_SAQ = C_CA + 2 * GROUP_W
C_SAKV = C_SAQ + SA_HEADS * LANES
C_IQ = C_SAKV + LANES
C_IK = C_IQ + IDX_HEADS * IDX_DIM
C_CQ = C_IK + 2 * LANES
C_CKV = C_CQ + 256
C_KRF = C_CKV + LANES
C_KRS = C_KRF + LANES
C_END = C_KRS + LANES

INT_MIN = -2 ** 31
KEY_ALL = INT_MIN - int(np.array(-np.inf, np.float32).view(np.int32)) + 1


def _cparams(n_axes):
    return pltpu.CompilerParams(dimension_semantics=("arbitrary",) * n_axes,
                                vmem_limit_bytes=VMEM_LIMIT)


def _rms(x, g):
    return x * lax.rsqrt(jnp.mean(x * x, axis=-1, keepdims=True) + EPS) * g


def _dot(a, b):
    return jnp.dot(a, b, preferred_element_type=F32)


def _dot_t(a, b):
    return lax.dot_general(a, b, (((1,), (1,)), ((), ())), preferred_element_type=F32)


def _mod_kernel(c_ref, w_ref, b_ref, o_ref):
    c = c_ref[...]
    act = c * jax.nn.sigmoid(c)
    o_ref[0] = jnp.dot(act, w_ref[0], precision=lax.Precision.HIGHEST,
                       preferred_element_type=F32) + b_ref[0]


def _modulation(c, w_mod, b_mod):
    L, D, W = w_mod.shape
    B = c.shape[0]
    nj = W // D
    return pl.pallas_call(
        _mod_kernel,
        grid=(L, nj),
        in_specs=[pl.BlockSpec((B, D), lambda l, j: (0, 0)),
                  pl.BlockSpec((1, D, D), lambda l, j: (l, 0, j)),
                  pl.BlockSpec((1, 1, D), lambda l, j: (l, 0, j))],
        out_specs=pl.BlockSpec((1, B, D), lambda l, j: (l, 0, j)),
        out_shape=jax.ShapeDtypeStruct((L, B, W), F32),
        compiler_params=_cparams(2),
        name="modulation",
    )(c, w_mod, b_mod.reshape(L, 1, W))


def _inproj_kernel(x_ref, mod_ref, gmix_ref, w_ref, wt_ref, gcq_ref, gckv_ref, wq_ref, wqs_ref, wk_ref, wvt_ref,
                   cosq_ref, sinq_ref, cosk_ref, sink_ref,
                   pool_o, ca_o, saq_o, sakv_o, iq_o, ik_o, iwt_o, svt_o, cavt_o, mq_o, mk_o, mvt_o):
    sh1 = mod_ref[0, 0:1, :]
    sc1 = mod_ref[0, 1:2, :]
    h = (_rms(x_ref[...], gmix_ref[...]) * (1.0 + sc1) + sh1).astype(BF16)

    def seg(a, b):
        return _dot(h, w_ref[:, a:b])

    qn = _rms(seg(C_CQ, C_CKV), gcq_ref[...]).astype(BF16)
    kvn = _rms(seg(C_CKV, C_KRF), gckv_ref[...]).astype(BF16)
    krope = seg(C_KRF, C_KRS) * cosk_ref[...] + seg(C_KRS, C_END) * sink_ref[...]

    pool_o[...] = seg(C_POOL, C_CA)
    ca_o[...] = seg(C_CA, C_SAQ).astype(BF16)
    qf = _dot(qn, wq_ref[...])
    qs = _dot(qn, wqs_ref[...])
    saq_o[...] = seg(C_SAQ, C_SAKV).astype(BF16)
    cosq = jnp.concatenate([cosq_ref[...]] * MLA_HEADS, axis=1)
    sinq = jnp.concatenate([sinq_ref[...]] * MLA_HEADS, axis=1)
    mq_o[...] = (qf * cosq + qs * sinq).astype(BF16)
    kvf = _dot(kvn, wk_ref[...])
    sakv_o[...] = seg(C_SAKV, C_IQ).astype(BF16)
    iq_o[...] = seg(C_IQ, C_IK).astype(BF16)
    for hd in range(MLA_HEADS):
        mk_o[:, hd * LANES:(hd + 1) * LANES] = (kvf[:, hd * LANES:(hd + 1) * LANES] + krope).astype(BF16)
    mvt_o[...] = _dot_t(wvt_ref[...], kvn).astype(BF16)
    ik_o[...] = seg(C_IK, C_CQ).astype(BF16)
    tr = _dot_t(wt_ref[...], h)
    iwt_o[...] = tr[0:IWT_ROWS] * ((IDX_HEADS ** -0.5) * (IDX_DIM ** -0.5))
    svt_o[...] = tr[IWT_ROWS:IWT_ROWS + HEAD_DIM].astype(BF16)
    cavt_o[...] = tr[IWT_ROWS + HEAD_DIM:].astype(BF16)


def _inproj(x2, mod, gmix, w1, wt, gcq, gckv, wq, wqs, wk, wvt, rope_tabs, S):
    N, D = x2.shape
    TM = TM_PROJ
    nt = S // TM
    cosq, sinq, cosk, sink = rope_tabs

    def full(a):
        return pl.BlockSpec(a.shape, lambda i: (0,) * a.ndim)

    def tok(w):
        return pl.BlockSpec((TM, w), lambda i: (i, 0))

    tab = pl.BlockSpec((TM, LANES), lambda i: (i % nt, 0))
    def tokt(rows):
        return pl.BlockSpec((rows, TM), lambda i: (0, i))

    outs = [(C_CA - C_POOL, F32, True), (C_SAQ - C_CA, BF16, True), (C_SAKV - C_SAQ, BF16, True),
            (C_IQ - C_SAKV, BF16, True), (C_IK - C_IQ, BF16, True), (C_CQ - C_IK, BF16, True),
            (IWT_ROWS, F32, False), (HEAD_DIM, BF16, False), (GROUP_W, BF16, False),
            (MLA_HEADS * LANES, BF16, True), (MLA_HEADS * LANES, BF16, True), (GROUP_W, BF16, False)]
    return pl.pallas_call(
        _inproj_kernel,
        grid=(N // TM,),
        in_specs=[tok(D),
                  pl.BlockSpec((1, 6, D), lambda i: (i // nt, 0, 0)),
                  full(gmix), full(w1), full(wt), full(gcq), full(gckv), full(wq), full(wqs), full(wk), full(wvt),
                  tab, tab, tab, tab],
        out_specs=[tok(w) if tm else tokt(w) for w, _, tm in outs],
        out_shape=[jax.ShapeDtypeStruct((N, w) if tm else (w, N), dt) for w, dt, tm in outs],
        compiler_params=_cparams(1),
        name="inproj",
    )(x2, mod, gmix, w1, wt, gcq, gckv, wq, wqs, wk, wvt, cosq, sinq, cosk, sink)


def _pool_kernel(u_ref, halo_ref, w_ref, scale_ref, g_ref, o_ref, pad_scr):
    i = pl.program_id(1)
    TP = u_ref.shape[1]
    u = u_ref[0]
    pad_scr[0:POOL_HALO, :] = jnp.where(i > 0, halo_ref[0], 0.0)
    pad_scr[POOL_HALO:, :] = u

    def shifted(j):
        return pad_scr[POOL_HALO - j:POOL_HALO - j + TP, :]

    lane = lax.broadcasted_iota(jnp.int32, (TP, GROUP_W), 1)
    w2 = u + shifted(1)
    w4 = w2 + shifted(2) + shifted(3)
    w8 = w4
    for j in range(4, 8):
        w8 = w8 + shifted(j)
    w16 = w8
    for j in range(8, 16):
        w16 = w16 + shifted(j)
    win = jnp.where(lane < 64, w2, jnp.where(lane < 128, w4, jnp.where(lane < 192, w8, w16)))
    wlen = jnp.where(lane < 64, 2, jnp.where(lane < 128, 4, jnp.where(lane < 192, 8, 16)))
    t = i * TP + lax.broadcasted_iota(jnp.int32, (TP, GROUP_W), 0)
    cnt = jnp.minimum(t + 1, wlen).astype(F32)
    d = (win / cnt - u).astype(BF16)
    y = _dot(d, w_ref[...]) * scale_ref[...]
    o_ref[0] = _rms(y, g_ref[...]).astype(BF16)


def _pool(u, wbd, scale, g):
    B, S, W = u.shape
    TP = TP_POOL
    hb = TP // POOL_HALO
    return pl.pallas_call(
        _pool_kernel,
        grid=(B, S // TP),
        in_specs=[pl.BlockSpec((1, TP, W), lambda b, i: (b, i, 0)),
                  pl.BlockSpec((1, POOL_HALO, W), lambda b, i: (b, jnp.maximum(i * hb - 1, 0), 0)),
                  pl.BlockSpec((W, W), lambda b, i: (0, 0)),
                  pl.BlockSpec((1, W), lambda b, i: (0, 0)),
                  pl.BlockSpec((1, W), lambda b, i: (0, 0))],
        out_specs=pl.BlockSpec((1, TP, W), lambda b, i: (b, i, 0)),
        out_shape=jax.ShapeDtypeStruct((B, S, W), BF16),
        scratch_shapes=[pltpu.VMEM((POOL_HALO + TP, W), F32)],
        compiler_params=_cparams(2),
        name="pool_mixer",
    )(u, u, wbd, scale, g)


def _ca_kernel(q_ref, k_ref, vt_ref, bias_ref, g_ref, o_ref):
    i = pl.program_id(1)
    TQ = TQ_CA
    lane = lax.broadcasted_iota(jnp.int32, (TQ, LANES), 1)
    starts = []
    for j in range(CA_NBLK):
        kb = i - (CA_NBLK - 1) + j
        starts.append((kb >= 0, pl.multiple_of(jnp.maximum(kb, 0) * TQ, TQ)))
    scored = []
    for hd in range(CA_HEADS):
        cols = slice((hd // 2) * LANES, (hd // 2 + 1) * LANES)
        keep = (lane < HEAD_DIM) if hd % 2 == 0 else (lane >= HEAD_DIM)
        qh = jnp.where(keep, q_ref[0, :, cols].astype(F32), 0.0).astype(BF16)
        parts = []
        for j, (present, start) in enumerate(starts):
            s = _dot_t(k_ref[0, pl.ds(start, TQ), cols], qh) + bias_ref[hd, j * TQ:(j + 1) * TQ, :]
            parts.append(jnp.where(present, s, NEG_INF))
        m = parts[0].max(axis=0, keepdims=True)
        for s in parts[1:]:
            m = jnp.maximum(m, s.max(axis=0, keepdims=True))
        scored.append((parts, m))
    outs = []
    for hd, (parts, m) in enumerate(scored):
        l = jnp.zeros((1, TQ), F32)
        acc = jnp.zeros((HEAD_DIM, TQ), F32)
        for j, (_, start) in enumerate(starts):
            p = jnp.exp(parts[j] - m)
            l = l + p.sum(axis=0, keepdims=True)
            acc = acc + _dot(vt_ref[hd * HEAD_DIM:(hd + 1) * HEAD_DIM, pl.ds(start, TQ)], p.astype(BF16))
        outs.append(acc / l)
    o_ref[0] = _group_norm_t(jnp.concatenate(outs, axis=0), g_ref[...])


def _chunk_attention(caqk, cavt, bias, g):
    B, S, _ = caqk.shape
    W = GROUP_W
    TQ = TQ_CA
    return pl.pallas_call(
        _ca_kernel,
        grid=(B, S // TQ),
        in_specs=[pl.BlockSpec((1, TQ, W), lambda b, i: (b, i, 0)),
                  pl.BlockSpec((1, S, W), lambda b, i: (b, 0, 1)),
                  pl.BlockSpec((W, S), lambda b, i: (0, b)),
                  pl.BlockSpec(bias.shape, lambda b, i: (0, 0, 0)),
                  pl.BlockSpec((1, W), lambda b, i: (0, 0))],
        out_specs=pl.BlockSpec((1, TQ, W), lambda b, i: (b, i, 0)),
        out_shape=jax.ShapeDtypeStruct((B, S, W), BF16),
        compiler_params=_cparams(2),
        name="band_attention",
    )(caqk, caqk, cavt, bias, g)


def _score_key(score):
    b = lax.bitcast_convert_type(score, jnp.int32)
    return jnp.where(b < 0, jnp.int32(INT_MIN) - b, b)


def _sa_kernel(q_ref, kv_ref, vt_ref, iq_ref, ik_ref, iwt_ref, zq_ref, nbias_ref, g_ref, o_ref,
               key_scr, s0_scr, s1_scr, smax0_scr, smax1_scr):
    s_scr, smax_scr = (s0_scr, s1_scr), (smax0_scr, smax1_scr)
    i = pl.program_id(1)
    TQ, KB = TQ_SA, KB_SA
    K = float(TOPK_MAX)
    nb = i + 1
    q0 = i * TQ
    cshift = CHUNK.bit_length() - 1
    kchunk = lax.broadcasted_iota(jnp.int32, (KB, TQ), 0) >> cshift
    qchunk = (q0 + lax.broadcasted_iota(jnp.int32, (1, TQ), 1)) >> cshift

    iwt = iwt_ref[...]

    def score_block(j, carry, tail, moments):
        smax, s1, s2 = carry
        k0 = pl.multiple_of(j * KB, KB)
        ik = ik_ref[0, pl.ds(k0, KB), :]
        ik2 = jnp.concatenate([ik[:, :LANES], ik[:, LANES:]], axis=0)
        sc = jnp.zeros((KB, TQ), F32)
        for p in range(IDX_HEADS // 2):
            logits = _dot_t(ik2, iq_ref[0, :, p * LANES:(p + 1) * LANES])
            sc = sc + iwt[2 * p:2 * p + 1, :] * jnp.maximum(logits[:KB], 0.0)
            sc = sc + iwt[2 * p + 1:2 * p + 2, :] * jnp.maximum(logits[KB:], 0.0)
        if tail:
            sc = jnp.where(kchunk <= qchunk - (k0 >> cshift), sc, -jnp.inf)
        key_scr[pl.ds(k0, KB), :] = _score_key(sc)
        smax = jnp.maximum(smax, sc.max(axis=0, keepdims=True))
        if moments:
            s1 = s1 + sc.sum(axis=0, keepdims=True)
            s2 = s2 + (sc * sc).sum(axis=0, keepdims=True)
        return smax, s1, s2

    n_pairs = (nb + 1) // 2
    carry = lax.fori_loop(
        0, n_pairs - 1, lambda j, c: score_block(2 * j + 1, score_block(2 * j, c, False, True), False, False),
        (jnp.full((1, TQ), -jnp.inf, F32), jnp.zeros((1, TQ), F32), jnp.zeros((1, TQ), F32)))
    last = 2 * (n_pairs - 1)
    carry = score_block(last, carry, True, False)

    def pad_block(c):
        key_scr[pl.ds(pl.multiple_of((last + 1) * KB, KB), KB), :] = jnp.full((KB, TQ), KEY_ALL - 1, jnp.int32)
        return c

    smax, s1, s2 = lax.cond(nb % 2 == 1, pad_block, lambda c: score_block(last + 1, c, True, False), carry)
    n_moments = ((n_pairs - 1) * KB).astype(F32)

    def count_ge(cand):
        def body(j, acc):
            blk = key_scr[pl.ds(pl.multiple_of(j * (2 * KB), 2 * KB), 2 * KB), :]
            ones = jnp.where(blk >= cand, 1.0, 0.0)
            return acc + ones.reshape(COUNT_CHAINS, -1, 8, TQ).sum(axis=1)
        acc = lax.fori_loop(0, (nb + 1) // 2, body, jnp.zeros((COUNT_CHAINS, 8, TQ), F32))
        return acc.sum(axis=0).sum(axis=0, keepdims=True)

    def max_below(bound):
        def body(j, acc):
            blk = key_scr[pl.ds(pl.multiple_of(j * (2 * KB), 2 * KB), 2 * KB), :]
            below = jnp.where(blk < bound, blk, jnp.int32(INT_MIN))
            return jnp.maximum(acc, below.reshape(COUNT_CHAINS, -1, 8, TQ).max(axis=1))
        acc = lax.fori_loop(0, (nb + 1) // 2, body, jnp.full((COUNT_CHAINS, 8, TQ), INT_MIN, jnp.int32))
        return acc.max(axis=0).max(axis=0, keepdims=True)

    def search():
        def unkey(k):
            return lax.bitcast_convert_type(jnp.where(k < 0, jnp.int32(INT_MIN) - k, k), F32)

        def is_active(lo, hi, clo):
            return jnp.logical_and(clo > K, hi > lo + 1)

        def cond(st):
            _, lo, hi, clo, _ = st
            act = jnp.where(is_active(lo, hi, clo), 1.0, 0.0)
            return jnp.max(jnp.maximum(act[:, :LANES], act[:, LANES:])) > 0.0

        mean = s1 / n_moments
        std = jnp.sqrt(jnp.maximum(s2 / n_moments - mean * mean, 0.0))
        zq = jnp.max(zq_ref[...], axis=0, keepdims=True)
        guess_lo = _score_key(mean + (zq - GUESS_SPREAD) * std)
        guess_hi = _score_key(mean + (zq + GUESS_SPREAD) * std)

        def step(st, peel):
            it, lo, hi, clo, chi = st
            active = is_active(lo, hi, clo)
            if peel:
                cand = max_below(hi)
            else:
                lf, hf = unkey(lo), unkey(hi)
                lc = jnp.log(clo)
                frac = jnp.clip((lc - math.log(K - 0.5)) / (lc - jnp.log(jnp.maximum(chi, 0.5))), 0.05, 0.95)
                cand = _score_key(lf + frac * (hf - lf))
                cand = jnp.where(it % 3 == 2, (lo >> 1) + (hi >> 1) + (lo & hi & 1), cand)
                cand = jnp.where(it == 0, guess_lo, cand)
                cand = jnp.where(it == 1, guess_hi, cand)
                cand = jnp.clip(cand, lo + 1, hi - 1)
            cand = jnp.where(active, cand, lo)
            cnt = count_ge(cand)
            up = jnp.logical_and(active, cnt >= K)
            down = jnp.logical_and(active, cnt < K)
            hi = jnp.where(down, cand, jnp.where(up, cand + 1, hi) if peel else hi)
            return (it + 1, jnp.where(up, cand, lo), hi, jnp.where(up, cnt, clo), jnp.where(down, cnt, chi))

        def steps(n, st):
            return lax.fori_loop(0, n, lambda _, s: step(s, False), st)

        lo0 = jnp.full((1, TQ), KEY_ALL - 1, jnp.int32)
        hi0 = _score_key(smax) + 1
        clo0 = jnp.zeros((1, TQ), F32) + ((nb + 1) // 2 * (2 * KB)).astype(F32)
        st = (jnp.int32(0), lo0, hi0, clo0, jnp.zeros((1, TQ), F32))
        st = steps(SEARCH_FIRST_ROUND - 1, st)
        st = step(st, True)
        st = lax.while_loop(cond, lambda s: step(steps(SEARCH_ROUND - 1, s), True), st)
        return st[1], st[3]

    def no_search():
        return jnp.full((1, TQ), KEY_ALL, jnp.int32), jnp.full((1, TQ), K, F32)

    t, cnt_t = lax.cond(i > 0, search, no_search)
    t = jnp.maximum(t, KEY_ALL)

    @pl.when(jnp.max(cnt_t) > K)
    def _():
        allowed = K - count_ge(t + 1)
        r = lax.broadcasted_iota(jnp.int32, (KB, KB), 0)
        c = lax.broadcasted_iota(jnp.int32, (KB, KB), 1)
        earlier = jnp.where(c < r, 1.0, 0.0).astype(BF16)

        def body(jj, seen):
            sls = [pl.ds(pl.multiple_of((2 * jj + u) * KB, KB), KB) for u in range(2)]
            blks = [key_scr[sl, :] for sl in sls]
            eqs = [jnp.where(blk == t, 1.0, 0.0) for blk in blks]
            seens = [seen, seen + eqs[0].sum(axis=0, keepdims=True)]
            ranks = [_dot(earlier, eq.astype(BF16)) + sn for eq, sn in zip(eqs, seens)]
            for sl, blk, eq, rank in zip(sls, blks, eqs, ranks):
                demote = eq * jnp.where(rank >= allowed, 1.0, 0.0)
                key_scr[sl, :] = jnp.where(demote > 0.5, t - 1, blk)
            return seens[1] + eqs[1].sum(axis=0, keepdims=True)

        lax.fori_loop(0, (nb + 1) // 2, body, jnp.zeros((1, TQ), F32))

    def scores_to(slot, j, bias_rows, present):
        k0 = pl.multiple_of(j * KB, KB)
        kblk = kv_ref[0, pl.ds(k0, KB), :]
        sel = key_scr[pl.ds(k0, KB), :] >= (t if present is True else jnp.where(present, t, jnp.int32(2 ** 31 - 1)))
        for hd in range(SA_HEADS):
            s = _dot_t(kblk, q_ref[0, :, hd * LANES:(hd + 1) * LANES])
            if bias_rows is not None:
                s = s + nbias_ref[hd, bias_rows, :]
            s = jnp.where(sel, s, NEG_INF)
            s_scr[slot][hd] = s
            smax_scr[slot][hd] = jnp.broadcast_to(s.max(axis=0, keepdims=True), smax_scr[slot].shape[1:])

    def softmax_pv(slot, j, st):
        vt = vt_ref[:, pl.ds(pl.multiple_of(j * KB, KB), KB)]
        out = []
        for hd in range(SA_HEADS):
            m, l, acc = st[hd]
            mn = jnp.maximum(m, smax_scr[slot][hd][0:1])
            p = jnp.exp(s_scr[slot][hd] - mn)
            alpha = jnp.exp(m - mn)
            out.append((mn, alpha * l + p.sum(axis=0, keepdims=True), alpha * acc + _dot(vt, p.astype(BF16))))
        return tuple(out)

    st = tuple((jnp.full((1, TQ), NEG_INF, F32), jnp.zeros((1, TQ), F32), jnp.zeros((HEAD_DIM, TQ), F32))
               for _ in range(SA_HEADS))
    left = jnp.maximum(i - 1, 0)
    tails = [(left, lambda slot: scores_to(slot, left, slice(0, KB), i >= 1)),
             (i, lambda slot: scores_to(slot, i, slice(KB, 2 * KB), True))]
    st = _block_pipeline(left, i, lambda slot, j, present: scores_to(slot, j, None, present), tails, softmax_pv, st)
    y_t = jnp.concatenate([acc / l for _, l, acc in st], axis=0)
    o_ref[0] = _group_norm_t(y_t, g_ref[...])


def _sparse_attention(saq, sakv, svt, iq, ik, iwt, nbias, g):
    B, S, _ = saq.shape
    TQ = TQ_SA
    W = GROUP_W
    nt = S // TQ
    n_adm = (np.arange(S) // CHUNK + 1) * CHUNK
    zq = np.array([NormalDist().inv_cdf(1.0 - TOPK_MAX / n) if n > TOPK_MAX else 0.0 for n in n_adm], np.float32)
    zq = jnp.asarray(np.tile(zq[None, :], (8, 1)))
    return pl.pallas_call(
        _sa_kernel,
        grid=(B, nt),
        in_specs=[pl.BlockSpec((1, TQ, saq.shape[2]), lambda b, i: (b, i, 0)),
                  pl.BlockSpec((1, S, sakv.shape[2]), lambda b, i: (b, 0, 0)),
                  pl.BlockSpec((HEAD_DIM, S), lambda b, i: (0, b)),
                  pl.BlockSpec((1, TQ, iq.shape[2]), lambda b, i: (b, i, 0)),
                  pl.BlockSpec((1, S, ik.shape[2]), lambda b, i: (b, 0, 0)),
                  pl.BlockSpec((IWT_ROWS, TQ), lambda b, i: (0, b * nt + i)),
                  pl.BlockSpec((8, TQ), lambda b, i: (0, i)),
                  pl.BlockSpec(nbias.shape, lambda b, i: (0, 0, 0)),
                  pl.BlockSpec((1, W), lambda b, i: (0, 0))],
        out_specs=pl.BlockSpec((1, TQ, W), lambda b, i: (b, i, 0)),
        out_shape=jax.ShapeDtypeStruct((B, S, W), BF16),
        scratch_shapes=([pltpu.VMEM((S, TQ), jnp.int32)] + [pltpu.VMEM((SA_HEADS, KB_SA, TQ), F32)] * 2
                        + [pltpu.VMEM((SA_HEADS, 8, TQ), F32)] * 2),
        compiler_params=_cparams(2),
        name="sparse_attention",
    )(saq, sakv, svt, iq, ik, iwt, zq, nbias, g)


def _group_norm_t(y_t, g):
    inv = lax.rsqrt(jnp.mean(y_t * y_t, axis=0, keepdims=True) + EPS)
    return ((y_t * inv).T * g).astype(BF16)


def _block_pipeline(n_plain, last_blk, score_plain, tails, softmax, st):
    off = n_plain % 2
    n_loop = jnp.maximum((n_plain + off) // 2 - 1, 0)

    def blk(pos):
        return jnp.clip(pos - off, 0, last_blk)

    def body(pp, st):
        pos = 2 * pp
        score_plain(1, blk(pos + 1), True)
        st = softmax(0, blk(pos), st)
        score_plain(0, blk(pos + 2), True)
        return softmax(1, blk(pos + 1), st)

    score_plain(0, blk(0), jnp.logical_and(n_plain >= 1, off == 0))
    st = lax.fori_loop(0, n_loop, body, st)
    e0 = 2 * n_loop
    slot, pending = 0, blk(e0)
    steps = [(blk(e0 + 1), lambda s: score_plain(s, blk(e0 + 1), n_plain >= 1))] + list(tails)
    for nxt, score_fn in steps:
        score_fn(1 - slot)
        st = softmax(slot, pending, st)
        slot, pending = 1 - slot, nxt
    return softmax(slot, pending, st)


def _mla_kernel(q_ref, k_ref, vt_ref, g_ref, o_ref, s0_scr, s1_scr, smax0_scr, smax1_scr):
    i = pl.program_id(1)
    TQ, KB = TQ_MLA, KB_MLA
    R = TQ // KB
    cshift = CHUNK.bit_length() - 1
    kch = lax.broadcasted_iota(jnp.int32, (KB, TQ), 0) >> cshift
    qch = lax.broadcasted_iota(jnp.int32, (KB, TQ), 1) >> cshift
    s_scr, smax_scr = (s0_scr, s1_scr), (smax0_scr, smax1_scr)

    def scores_to(slot, j, keep):
        k0 = pl.multiple_of(j * KB, KB)
        for hd in range(MLA_HEADS):
            cols = slice(hd * LANES, (hd + 1) * LANES)
            s = _dot_t(k_ref[0, pl.ds(k0, KB), cols], q_ref[0, :, cols])
            if keep is not None:
                s = jnp.where(keep, s, NEG_INF)
            s_scr[slot][hd] = s
            smax_scr[slot][hd] = jnp.broadcast_to(s.max(axis=0, keepdims=True), smax_scr[slot].shape[1:])

    def score_plain(slot, j, present):
        scores_to(slot, j, None if present is True else kch >= jnp.where(present, 0, TQ))

    def softmax_pv(slot, j, st):
        k0 = pl.multiple_of(j * KB, KB)
        out = []
        for hd in range(MLA_HEADS):
            m, l, acc = st[hd]
            mn = jnp.maximum(m, smax_scr[slot][hd][0:1])
            p = jnp.exp(s_scr[slot][hd] - mn)
            alpha = jnp.exp(m - mn)
            vt = vt_ref[hd * MLA_V:(hd + 1) * MLA_V, pl.ds(k0, KB)]
            out.append((mn, alpha * l + p.sum(axis=0, keepdims=True),
                        alpha * acc + _dot(vt, p.astype(BF16))))
        return tuple(out)

    st = tuple((jnp.full((1, TQ), NEG_INF, F32), jnp.zeros((1, TQ), F32), jnp.zeros((MLA_V, TQ), F32))
               for _ in range(MLA_HEADS))
    tails = [(R * i + d, functools.partial(
        lambda slot, d: scores_to(slot, R * i + d, kch + d * (KB // CHUNK) <= qch), d=d)) for d in range(R)]
    st = _block_pipeline(R * i, R * i + R - 1, score_plain, tails, softmax_pv, st)
    o_ref[0] = _group_norm_t(jnp.concatenate([acc / l for _, l, acc in st], axis=0), g_ref[...])


def _latent_attention(mq, mk, mvt, g):
    B, S, _ = mq.shape
    TQ = TQ_MLA
    W = GROUP_W
    return pl.pallas_call(
        _mla_kernel,
        grid=(B, S // TQ),
        in_specs=[pl.BlockSpec((1, TQ, mq.shape[2]), lambda b, i: (b, i, 0)),
                  pl.BlockSpec((1, S, mk.shape[2]), lambda b, i: (b, 0, 0)),
                  pl.BlockSpec((W, S), lambda b, i: (0, b)),
                  pl.BlockSpec((1, W), lambda b, i: (0, 0))],
        out_specs=pl.BlockSpec((1, TQ, W), lambda b, i: (b, i, 0)),
        out_shape=jax.ShapeDtypeStruct((B, S, W), BF16),
        scratch_shapes=[pltpu.VMEM((MLA_HEADS, KB_MLA, TQ), F32)] * 2 + [pltpu.VMEM((MLA_HEADS, 8, TQ), F32)] * 2,
        compiler_params=_cparams(2),
        name="latent_attention",
    )(mq, mk, mvt, g)


def _ffn_kernel(ya_ref, yb_ref, yc_ref, yd_ref, x_ref, mod_ref, wout_ref, gffn_ref, w1_ref, w3_ref, w2_ref,
                gfin_ref, o_ref, acc_scr, *, final):
    gt1 = mod_ref[0, 2:3, :]
    sh2 = mod_ref[0, 3:4, :]
    sc2 = mod_ref[0, 4:5, :]
    gt2 = mod_ref[0, 5:6, :]
    attn = _dot(ya_ref[...], wout_ref[0:GROUP_W, :])
    for gi, y_ref in enumerate((yb_ref, yc_ref, yd_ref), start=1):
        attn = attn + _dot(y_ref[...], wout_ref[gi * GROUP_W:(gi + 1) * GROUP_W, :])
    x1 = x_ref[...] + gt1 * attn
    h = (_rms(x1, gffn_ref[...]) * (1.0 + sc2) + sh2).astype(BF16)
    for ci in range(D_FF // FF_CHUNK):
        cols = slice(ci * FF_CHUNK, (ci + 1) * FF_CHUNK)
        a = _dot(h, w1_ref[:, cols])
        gate = (a * jax.nn.sigmoid(a) * _dot(h, w3_ref[:, cols])).astype(BF16)
        part = _dot(gate, w2_ref[cols, :])
        if ci == 0:
            acc_scr[...] = part
        else:
            acc_scr[...] += part
    x2 = x1 + gt2 * acc_scr[...]
    o_ref[...] = _rms(x2, gfin_ref[...]) if final else x2


def _out_ffn(ys, x2, mod, wout, gffn, w1, w3, w2, gfin, S, final):
    N, D = x2.shape
    TM = TM_FFN
    nt = S // TM

    def full(a):
        return pl.BlockSpec(a.shape, lambda i: (0,) * a.ndim, pipeline_mode=pl.Buffered(1))

    def tok(w):
        return pl.BlockSpec((TM, w), lambda i: (i, 0))

    return pl.pallas_call(
        functools.partial(_ffn_kernel, final=final),
        grid=(N // TM,),
        in_specs=[tok(GROUP_W)] * 4 + [tok(D), pl.BlockSpec((1, 6, D), lambda i: (i // nt, 0, 0)),
                                       full(wout), full(gffn), full(w1), full(w3), full(w2), full(gfin)],
        out_specs=tok(D),
        out_shape=jax.ShapeDtypeStruct((N, D), F32),
        scratch_shapes=[pltpu.VMEM((TM, D), F32)],
        compiler_params=_cparams(1),
        name="out_ffn_final" if final else "out_ffn",
    )(*ys, x2, mod, wout, gffn, w1, w3, w2, gfin)


def _t5_bucket(rel):
    nb = T5_BUCKETS // 2
    max_exact = nb // 2
    ret = jnp.where(rel > 0, nb, 0)
    n = jnp.abs(rel)
    nf = jnp.maximum(n, 1).astype(jnp.float32)
    large = max_exact + (jnp.log(nf / max_exact) / math.log(T5_MAX_DIST / max_exact)
                         * (nb - max_exact)).astype(jnp.int32)
    large = jnp.minimum(large, nb - 1)
    return ret + jnp.where(n < max_exact, n, large)


def _rope_tables(S):
    half = MLA_ROPE // 2
    freqs = ROPE_BASE ** (-jnp.arange(half, dtype=F32) / half)
    ang = jnp.arange(S, dtype=jnp.int32).astype(F32)[:, None] * freqs[None, :]
    cos, sin = jnp.cos(ang), jnp.sin(ang)
    cos2 = jnp.concatenate([cos, cos], axis=1)
    sin2 = jnp.concatenate([-sin, sin], axis=1)
    zeros = jnp.zeros((S, LANES - MLA_NOPE - MLA_ROPE), F32)
    scale = (MLA_NOPE + MLA_ROPE) ** -0.5
    cosq = jnp.concatenate([jnp.full((S, MLA_NOPE), scale, F32), cos2 * scale, zeros], axis=1)
    sinq = jnp.concatenate([jnp.zeros((S, MLA_NOPE), F32), sin2 * scale, zeros], axis=1)
    cosk = jnp.concatenate([jnp.zeros((S, MLA_NOPE), F32), cos2, zeros], axis=1)
    sink = jnp.concatenate([jnp.zeros((S, MLA_NOPE), F32), sin2, zeros], axis=1)
    return cosq, sinq, cosk, sink


def _pack_in_weight(w):
    part = {n: w[:, IN_OFFS[k]:IN_OFFS[k + 1]] for k, n in enumerate(
        ('pool_u', 'ca_q', 'ca_k', 'ca_v', 'sa_q', 'sa_k', 'sa_v', 'idx_q', 'idx_k', 'idx_w',
         'mla_cq', 'mla_ckv', 'mla_kr'))}
    D = w.shape[0]
    z = lambda n: jnp.zeros((D, n), F32)
    qscale = HEAD_DIM ** -0.5
    saq = part['sa_q'].reshape(D, SA_HEADS, HEAD_DIM) * qscale
    saq = jnp.concatenate([saq, jnp.zeros_like(saq)], axis=2).reshape(D, SA_HEADS * LANES)
    kr = part['mla_kr']
    kr_swap = jnp.concatenate([kr[:, MLA_ROPE // 2:], kr[:, :MLA_ROPE // 2]], axis=1)
    pad_r = LANES - MLA_NOPE - MLA_ROPE
    cols = [part['pool_u'], part['ca_q'] * qscale, part['ca_k'], saq,
            part['sa_k'], part['sa_v'], part['idx_q'],
            part['idx_k'], z(IDX_DIM), z(IDX_DIM), part['idx_k'],
            part['mla_cq'], part['mla_ckv'],
            z(MLA_NOPE), kr, z(pad_r), z(MLA_NOPE), kr_swap, z(pad_r)]
    out = jnp.concatenate(cols, axis=1)
    assert out.shape[1] == C_END
    wt = jnp.concatenate([part['idx_w'].T, jnp.zeros((IWT_ROWS - IDX_HEADS, D), F32), part['sa_v'].T,
                          part['ca_v'].T], axis=0)
    return out.astype(BF16), wt.astype(BF16)


def _pack_mla_weights(w_uq, w_ukv):
    R = w_uq.shape[0]
    pad = jnp.zeros((R, MLA_HEADS, LANES - MLA_NOPE - MLA_ROPE), F32)
    rope_w = w_uq[:, :, MLA_NOPE:]
    rope_sw = jnp.concatenate([rope_w[:, :, MLA_ROPE // 2:], rope_w[:, :, :MLA_ROPE // 2]], axis=2)
    wq = jnp.concatenate([w_uq, pad], axis=2).reshape(R, MLA_HEADS * LANES)
    wqs = jnp.concatenate([jnp.zeros((R, MLA_HEADS, MLA_NOPE), F32), rope_sw, pad],
                          axis=2).reshape(R, MLA_HEADS * LANES)
    Rk = w_ukv.shape[0]
    wk = jnp.concatenate([w_ukv[:, :, :MLA_NOPE], jnp.zeros((Rk, MLA_HEADS, LANES - MLA_NOPE), F32)],
                         axis=2).reshape(Rk, MLA_HEADS * LANES)
    wvt = w_ukv[:, :, MLA_NOPE:].reshape(Rk, MLA_HEADS * MLA_V).T
    return wq.astype(BF16), wqs.astype(BF16), wk.astype(BF16), wvt.astype(BF16)


def _toeplitz(vec, rows, cols):
    L = vec.shape[-1]
    assert cols <= L - 1
    flat = jnp.tile(vec, (1, rows))[:, :rows * (L - 1)]
    return flat.reshape(vec.shape[0], rows, L - 1)[:, :, :cols]


def _signed_mod_range(L, hi):
    d = np.arange(L)
    return np.where(d <= hi, d, d - L)


def _band_bias(rel_table):
    L = CA_WIN + TQ_CA
    e = _signed_mod_range(L, TQ_CA - 1)
    ridx = np.clip(CA_LEFT_CHUNKS * CHUNK + e, -(CHUNK - 1), CA_MAX_REL) + (CHUNK - 1)
    bias = _toeplitz(rel_table[:, ridx].astype(F32), CA_WIN, TQ_CA)
    kc = np.arange(CA_WIN)[:, None] // CHUNK
    qc = np.arange(TQ_CA)[None, :] // CHUNK + CA_LEFT_CHUNKS
    valid = (kc <= qc) & (kc >= qc - CA_LEFT_CHUNKS)
    return jnp.where(valid[None], bias, NEG_INF)


def _t5_bias(t5_table):
    TQ = TQ_SA
    L = 3 * TQ
    e = _signed_mod_range(L, TQ - 1)
    rel = jnp.asarray(-e - TQ, jnp.int32)
    far = t5_table[_t5_bucket(jnp.int32(-(TQ + 1)))].astype(F32)
    vec = (t5_table[_t5_bucket(rel)].astype(F32) - far[None, :]).T
    return _toeplitz(vec, 2 * TQ, TQ)


def kernel(x, c, t5_table, w_mod, b_mod, g_mix, w_in, pool_w, pool_scale, ca_rel, mla_g_cq, mla_g_ckv,
           mla_w_uq, mla_w_ukv, g_group, w_out, g_ffn, ffn_w1, ffn_w3, ffn_w2, g_final):
    B, S, D = x.shape
    assert D == D_MODEL and S % TM_PROJ == 0 and S % TQ_SA == 0 and S >= 4 * TOPK_MAX
    N = B * S
    mod_all = _modulation(c, w_mod, b_mod)
    rope_tabs = _rope_tables(S)
    nbias = _t5_bias(t5_table)
    row = lambda v: v.reshape(1, -1).astype(F32)
    x2 = x.reshape(N, D)
    for l in range(DEPTH):
        mod = mod_all[l].reshape(B, 6, D)
        w1, wt = _pack_in_weight(w_in[l])
        wq, wqs, wk, wvt = _pack_mla_weights(mla_w_uq[l], mla_w_ukv[l])
        (pool_u, ca, saq, sakv, iq, ik, iwt, svt, cavt, mq, mk, mvt) = _inproj(
            x2, mod, row(g_mix[l]), w1, wt, row(mla_g_cq[l]), row(mla_g_ckv[l]), wq, wqs, wk, wvt, rope_tabs, S)
        gg = g_group[l].reshape(4, 1, GROUP_W).astype(F32)
        wbd = jax.scipy.linalg.block_diag(*[pool_w[l, gi] for gi in range(len(POOL_WINDOWS))]).astype(BF16)
        bsw = lambda a: a.reshape(B, S, a.shape[-1])
        y_a = _pool(bsw(pool_u), wbd, row(pool_scale[l]), gg[0])
        y_b = _chunk_attention(bsw(ca), cavt, _band_bias(ca_rel[l]), gg[1])
        y_c = _sparse_attention(bsw(saq), bsw(sakv), svt, bsw(iq), bsw(ik), iwt, nbias, gg[2])
        y_d = _latent_attention(bsw(mq), bsw(mk), mvt, gg[3])
        ys = [y.reshape(N, GROUP_W) for y in (y_a, y_b, y_c, y_d)]
        x2 = _out_ffn(ys, x2, mod, w_out[l].astype(BF16), row(g_ffn[l]), ffn_w1[l].astype(BF16),
                      ffn_w3[l].astype(BF16), ffn_w2[l].astype(BF16), row(g_final), S,
                      final=(l == DEPTH - 1))
    return x2.reshape(B, S, D)
```

```python
import functools
import math
from statistics import NormalDist

import jax
import jax.numpy as jnp
from jax import lax
import numpy as np
from jax.experimental import pallas as pl
from jax.experimental.pallas import tpu as pltpu

F32 = jnp.float32
BF16 = jnp.bfloat16

D_MODEL = 1024
DEPTH = 2
CHUNK = 64
EPS = 1e-6
NEG_INF = -1e30
GROUP_W = 256
HEAD_DIM = 64
POOL_WINDOWS = (2, 4, 8, 16)
POOL_HALO = 16
CA_HEADS = 4
CA_LEFT_CHUNKS = 8
CA_MAX_REL = 256
SA_HEADS = 4
IDX_HEADS = 8
IDX_DIM = 64
TOPK_MAX = 256
MLA_HEADS = 4
MLA_NOPE = 64
MLA_ROPE = 32
MLA_V = 64
ROPE_BASE = 10000.0
T5_BUCKETS = 32
T5_MAX_DIST = 128
D_FF = 2816
IN_WIDTHS = (256, 256, 256, 256, 256, 64, 64, 512, 64, 8, 256, 128, 32)
IN_OFFS = tuple(int(v) for v in np.cumsum((0,) + IN_WIDTHS))

LANES = 128
VMEM_LIMIT = 56 * 1024 * 1024

TM_PROJ = 512
TM_FFN = 512
TQ_CA = 256
CA_WIN = TQ_CA + CA_LEFT_CHUNKS * CHUNK
CA_NBLK = CA_WIN // TQ_CA
IWT_ROWS = 16
TQ_SA = 256
KB_SA = 256
COUNT_CHAINS = 2
SEARCH_FIRST_ROUND = 16
SEARCH_ROUND = 4
GUESS_SPREAD = 0.3
TQ_MLA = 512
KB_MLA = 256
FF_CHUNK = 256

C_POOL = 0
C_CA = C_POOL + 256
C_SAQ = C_CA + 2 * GROUP_W
C_SAKV = C_SAQ + SA_HEADS * LANES
C_IQ = C_SAKV + LANES
C_IK = C_IQ + IDX_HEADS * IDX_DIM
C_CQ = C_IK + 2 * LANES
C_CKV = C_CQ + 256
C_KRF = C_CKV + LANES
C_KRS = C_KRF + LANES
C_END = C_KRS + LANES

INT_MIN = -2 ** 31
KEY_ALL = INT_MIN - int(np.array(-np.inf, np.float32).view(np.int32)) + 1


def _cparams(n_axes):
    return pltpu.CompilerParams(dimension_semantics=("arbitrary",) * n_axes,
                                vmem_limit_bytes=VMEM_LIMIT)


def _rms(x, g):
    return x * lax.rsqrt(jnp.mean(x * x, axis=-1, keepdims=True) + EPS) * g


def _dot(a, b):
    return jnp.dot(a, b, preferred_element_type=F32)


def _dot_t(a, b):
    return lax.dot_general(a, b, (((1,), (1,)), ((), ())), preferred_element_type=F32)


def _mod_kernel(c_ref, w_ref, b_ref, o_ref):
    c = c_ref[...]
    act = c * jax.nn.sigmoid(c)
    o_ref[0] = jnp.dot(act, w_ref[0], precision=lax.Precision.HIGHEST,
                       preferred_element_type=F32) + b_ref[0]


def _modulation(c, w_mod, b_mod):
    L, D, W = w_mod.shape
    B = c.shape[0]
    nj = W // D
    return pl.pallas_call(
        _mod_kernel,
        grid=(L, nj),
        in_specs=[pl.BlockSpec((B, D), lambda l, j: (0, 0)),
                  pl.BlockSpec((1, D, D), lambda l, j: (l, 0, j)),
                  pl.BlockSpec((1, 1, D), lambda l, j: (l, 0, j))],
        out_specs=pl.BlockSpec((1, B, D), lambda l, j: (l, 0, j)),
        out_shape=jax.ShapeDtypeStruct((L, B, W), F32),
        compiler_params=_cparams(2),
        name="modulation",
    )(c, w_mod, b_mod.reshape(L, 1, W))


def _pool_mix(u, first_tile, t0, pad_scr, halo_scr, w_ref, scale_ref, g_ref):
    TP = u.shape[0]
    pad_scr[0:POOL_HALO, :] = jnp.where(first_tile, 0.0, halo_scr[...])
    pad_scr[POOL_HALO:, :] = u
    halo_scr[...] = u[TP - POOL_HALO:, :]

    def shifted(j):
        return pad_scr[POOL_HALO - j:POOL_HALO - j + TP, :]

    lane = lax.broadcasted_iota(jnp.int32, (TP, GROUP_W), 1)
    w2 = u + shifted(1)
    w4 = w2 + shifted(2) + shifted(3)
    w8 = w4
    for j in range(4, 8):
        w8 = w8 + shifted(j)
    w16 = w8
    for j in range(8, 16):
        w16 = w16 + shifted(j)
    win = jnp.where(lane < 64, w2, jnp.where(lane < 128, w4, jnp.where(lane < 192, w8, w16)))
    wlen = jnp.where(lane < 64, 2, jnp.where(lane < 128, 4, jnp.where(lane < 192, 8, 16)))
    t = t0 + lax.broadcasted_iota(jnp.int32, (TP, GROUP_W), 0)
    cnt = jnp.minimum(t + 1, wlen).astype(F32)
    d = (win / cnt - u).astype(BF16)
    y = _dot(d, w_ref[...]) * scale_ref[...]
    return _rms(y, g_ref[...]).astype(BF16)


def _inproj_kernel(x_ref, mod_ref, gmix_ref, w_ref, wt_ref, gcq_ref, gckv_ref, wq_ref, wqs_ref, wk_ref, wvt_ref,
                   cosq_ref, sinq_ref, cosk_ref, sink_ref, wpool_ref, spool_ref, gpool_ref,
                   ya_o, ca_o, saq_o, sakv_o, iq_o, ik_o, iwt_o, svt_o, cavt_o, mq_o, mk_o, mvt_o,
                   pad_scr, halo_scr, *, tiles_per_seq):
    sh1 = mod_ref[0, 0:1, :]
    sc1 = mod_ref[0, 1:2, :]
    h = (_rms(x_ref[...], gmix_ref[...]) * (1.0 + sc1) + sh1).astype(BF16)

    def seg(a, b):
        return _dot(h, w_ref[:, a:b])

    qn = _rms(seg(C_CQ, C_CKV), gcq_ref[...]).astype(BF16)
    kvn = _rms(seg(C_CKV, C_KRF), gckv_ref[...]).astype(BF16)
    krope = seg(C_KRF, C_KRS) * cosk_ref[...] + seg(C_KRS, C_END) * sink_ref[...]

    u_pool = seg(C_POOL, C_CA)
    ca_o[...] = seg(C_CA, C_SAQ).astype(BF16)
    qf = _dot(qn, wq_ref[...])
    qs = _dot(qn, wqs_ref[...])
    saq_o[...] = seg(C_SAQ, C_SAKV).astype(BF16)
    cosq = jnp.concatenate([cosq_ref[...]] * MLA_HEADS, axis=1)
    sinq = jnp.concatenate([sinq_ref[...]] * MLA_HEADS, axis=1)
    mq_o[...] = (qf * cosq + qs * sinq).astype(BF16)
    kvf = _dot(kvn, wk_ref[...])
    sakv_o[...] = seg(C_SAKV, C_IQ).astype(BF16)
    iq_o[...] = seg(C_IQ, C_IK).astype(BF16)
    for hd in range(MLA_HEADS):
        mk_o[:, hd * LANES:(hd + 1) * LANES] = (kvf[:, hd * LANES:(hd + 1) * LANES] + krope).astype(BF16)
    mvt_o[...] = _dot_t(wvt_ref[...], kvn).astype(BF16)
    ik_o[...] = seg(C_IK, C_CQ).astype(BF16)
    tr = _dot_t(wt_ref[...], h)
    iwt_o[...] = tr[0:IWT_ROWS] * ((IDX_HEADS ** -0.5) * (IDX_DIM ** -0.5))
    svt_o[...] = tr[IWT_ROWS:IWT_ROWS + HEAD_DIM].astype(BF16)
    cavt_o[...] = tr[IWT_ROWS + HEAD_DIM:].astype(BF16)
    tile = pl.program_id(0) % tiles_per_seq
    ya_o[...] = _pool_mix(u_pool, tile == 0, tile * x_ref.shape[0], pad_scr, halo_scr,
                          wpool_ref, spool_ref, gpool_ref)


def _inproj(x2, mod, gmix, w1, wt, gcq, gckv, wq, wqs, wk, wvt, rope_tabs, wpool, spool, gpool, S):
    N, D = x2.shape
    TM = TM_PROJ
    nt = S // TM
    cosq, sinq, cosk, sink = rope_tabs

    def full(a):
        return pl.BlockSpec(a.shape, lambda i: (0,) * a.ndim)

    def tok(w):
        return pl.BlockSpec((TM, w), lambda i: (i, 0))

    tab = pl.BlockSpec((TM, LANES), lambda i: (i % nt, 0))
    def tokt(rows):
        return pl.BlockSpec((rows, TM), lambda i: (0, i))

    outs = [(GROUP_W, BF16, True), (C_SAQ - C_CA, BF16, True), (C_SAKV - C_SAQ, BF16, True),
            (C_IQ - C_SAKV, BF16, True), (C_IK - C_IQ, BF16, True), (C_CQ - C_IK, BF16, True),
            (IWT_ROWS, F32, False), (HEAD_DIM, BF16, False), (GROUP_W, BF16, False),
            (MLA_HEADS * LANES, BF16, True), (MLA_HEADS * LANES, BF16, True), (GROUP_W, BF16, False)]
    return pl.pallas_call(
        functools.partial(_inproj_kernel, tiles_per_seq=nt),
        grid=(N // TM,),
        in_specs=[tok(D),
                  pl.BlockSpec((1, 6, D), lambda i: (i // nt, 0, 0)),
                  full(gmix), full(w1), full(wt), full(gcq), full(gckv), full(wq), full(wqs), full(wk), full(wvt),
                  tab, tab, tab, tab, full(wpool), full(spool), full(gpool)],
        out_specs=[tok(w) if tm else tokt(w) for w, _, tm in outs],
        out_shape=[jax.ShapeDtypeStruct((N, w) if tm else (w, N), dt) for w, dt, tm in outs],
        scratch_shapes=[pltpu.VMEM((POOL_HALO + TM, GROUP_W), F32), pltpu.VMEM((POOL_HALO, GROUP_W), F32)],
        compiler_params=_cparams(1),
        name="inproj",
    )(x2, mod, gmix, w1, wt, gcq, gckv, wq, wqs, wk, wvt, cosq, sinq, cosk, sink, wpool, spool, gpool)


def _ca_kernel(q_ref, k_ref, vt_ref, bias_ref, g_ref, o_ref):
    i = pl.program_id(1)
    TQ = TQ_CA
    lane = lax.broadcasted_iota(jnp.int32, (TQ, LANES), 1)
    starts = []
    for j in range(CA_NBLK):
        kb = i - (CA_NBLK - 1) + j
        starts.append((kb >= 0, pl.multiple_of(jnp.maximum(kb, 0) * TQ, TQ)))
    scored = []
    for hd in range(CA_HEADS):
        cols = slice((hd // 2) * LANES, (hd // 2 + 1) * LANES)
        keep = (lane < HEAD_DIM) if hd % 2 == 0 else (lane >= HEAD_DIM)
        qh = jnp.where(keep, q_ref[0, :, cols].astype(F32), 0.0).astype(BF16)
        parts = []
        for j, (present, start) in enumerate(starts):
            s = _dot_t(k_ref[0, pl.ds(start, TQ), cols], qh) + bias_ref[hd, j * TQ:(j + 1) * TQ, :]
            parts.append(jnp.where(present, s, NEG_INF))
        m = parts[0].max(axis=0, keepdims=True)
        for s in parts[1:]:
            m = jnp.maximum(m, s.max(axis=0, keepdims=True))
        scored.append((parts, m))
    outs = []
    for hd, (parts, m) in enumerate(scored):
        l = jnp.zeros((1, TQ), F32)
        acc = jnp.zeros((HEAD_DIM, TQ), F32)
        for j, (_, start) in enumerate(starts):
            p = jnp.exp(parts[j] - m)
            l = l + p.sum(axis=0, keepdims=True)
            acc = acc + _dot(vt_ref[hd * HEAD_DIM:(hd + 1) * HEAD_DIM, pl.ds(start, TQ)], p.astype(BF16))
        outs.append(acc / l)
    o_ref[0] = _group_norm_t(jnp.concatenate(outs, axis=0), g_ref[...])


def _chunk_attention(caqk, cavt, bias, g):
    B, S, _ = caqk.shape
    W = GROUP_W
    TQ = TQ_CA
    return pl.pallas_call(
        _ca_kernel,
        grid=(B, S // TQ),
        in_specs=[pl.BlockSpec((1, TQ, W), lambda b, i: (b, i, 0)),
                  pl.BlockSpec((1, S, W), lambda b, i: (b, 0, 1)),
                  pl.BlockSpec((W, S), lambda b, i: (0, b)),
                  pl.BlockSpec(bias.shape, lambda b, i: (0, 0, 0)),
                  pl.BlockSpec((1, W), lambda b, i: (0, 0))],
        out_specs=pl.BlockSpec((1, TQ, W), lambda b, i: (b, i, 0)),
        out_shape=jax.ShapeDtypeStruct((B, S, W), BF16),
        compiler_params=_cparams(2),
        name="band_attention",
    )(caqk, caqk, cavt, bias, g)


def _score_key(score):
    b = lax.bitcast_convert_type(score, jnp.int32)
    return jnp.where(b < 0, jnp.int32(INT_MIN) - b, b)


def _sa_kernel(q_ref, kv_ref, vt_ref, iq_ref, ik_ref, iwt_ref, zq_ref, nbias_ref, g_ref, o_ref,
               key_scr, s0_scr, s1_scr, smax0_scr, smax1_scr):
    s_scr, smax_scr = (s0_scr, s1_scr), (smax0_scr, smax1_scr)
    i = pl.program_id(1)
    TQ, KB = TQ_SA, KB_SA
    K = float(TOPK_MAX)
    nb = i + 1
    q0 = i * TQ
    cshift = CHUNK.bit_length() - 1
    kchunk = lax.broadcasted_iota(jnp.int32, (KB, TQ), 0) >> cshift
    qchunk = (q0 + lax.broadcasted_iota(jnp.int32, (1, TQ), 1)) >> cshift

    iwt = iwt_ref[...]

    def score_block(j, carry, tail, moments):
        smax, s1, s2 = carry
        k0 = pl.multiple_of(j * KB, KB)
        ik = ik_ref[0, pl.ds(k0, KB), :]
        ik2 = jnp.concatenate([ik[:, :LANES], ik[:, LANES:]], axis=0)
        sc = jnp.zeros((KB, TQ), F32)
        for p in range(IDX_HEADS // 2):
            logits = _dot_t(ik2, iq_ref[0, :, p * LANES:(p + 1) * LANES])
            sc = sc + iwt[2 * p:2 * p + 1, :] * jnp.maximum(logits[:KB], 0.0)
            sc = sc + iwt[2 * p + 1:2 * p + 2, :] * jnp.maximum(logits[KB:], 0.0)
        if tail:
            sc = jnp.where(kchunk <= qchunk - (k0 >> cshift), sc, -jnp.inf)
        key_scr[pl.ds(k0, KB), :] = _score_key(sc)
        smax = jnp.maximum(smax, sc.max(axis=0, keepdims=True))
        if moments:
            s1 = s1 + sc.sum(axis=0, keepdims=True)
            s2 = s2 + (sc * sc).sum(axis=0, keepdims=True)
        return smax, s1, s2

    n_pairs = (nb + 1) // 2
    carry = lax.fori_loop(
        0, n_pairs - 1, lambda j, c: score_block(2 * j + 1, score_block(2 * j, c, False, True), False, False),
        (jnp.full((1, TQ), -jnp.inf, F32), jnp.zeros((1, TQ), F32), jnp.zeros((1, TQ), F32)))
    last = 2 * (n_pairs - 1)
    carry = score_block(last, carry, True, False)

    def pad_block(c):
        key_scr[pl.ds(pl.multiple_of((last + 1) * KB, KB), KB), :] = jnp.full((KB, TQ), KEY_ALL - 1, jnp.int32)
        return c

    smax, s1, s2 = lax.cond(nb % 2 == 1, pad_block, lambda c: score_block(last + 1, c, True, False), carry)
    n_moments = ((n_pairs - 1) * KB).astype(F32)

    def count_ge(cand):
        def body(j, acc):
            blk = key_scr[pl.ds(pl.multiple_of(j * (2 * KB), 2 * KB), 2 * KB), :]
            ones = jnp.where(blk >= cand, 1.0, 0.0)
            return acc + ones.reshape(COUNT_CHAINS, -1, 8, TQ).sum(axis=1)
        acc = lax.fori_loop(0, (nb + 1) // 2, body, jnp.zeros((COUNT_CHAINS, 8, TQ), F32))
        return acc.sum(axis=0).sum(axis=0, keepdims=True)

    def max_below(bound):
        def body(j, acc):
            blk = key_scr[pl.ds(pl.multiple_of(j * (2 * KB), 2 * KB), 2 * KB), :]
            below = jnp.where(blk < bound, blk, jnp.int32(INT_MIN))
            return jnp.maximum(acc, below.reshape(COUNT_CHAINS, -1, 8, TQ).max(axis=1))
        acc = lax.fori_loop(0, (nb + 1) // 2, body, jnp.full((COUNT_CHAINS, 8, TQ), INT_MIN, jnp.int32))
        return acc.max(axis=0).max(axis=0, keepdims=True)

    def search():
        def unkey(k):
            return lax.bitcast_convert_type(jnp.where(k < 0, jnp.int32(INT_MIN) - k, k), F32)

        def is_active(lo, hi, clo):
            return jnp.logical_and(clo > K, hi > lo + 1)

        def cond(st):
            _, lo, hi, clo, _ = st
            act = jnp.where(is_active(lo, hi, clo), 1.0, 0.0)
            return jnp.max(jnp.maximum(act[:, :LANES], act[:, LANES:])) > 0.0

        mean = s1 / n_moments
        std = jnp.sqrt(jnp.maximum(s2 / n_moments - mean * mean, 0.0))
        zq = jnp.max(zq_ref[...], axis=0, keepdims=True)
        guess_lo = _score_key(mean + (zq - GUESS_SPREAD) * std)
        guess_hi = _score_key(mean + (zq + GUESS_SPREAD) * std)

        def step(st, peel):
            it, lo, hi, clo, chi = st
            active = is_active(lo, hi, clo)
            if peel:
                cand = max_below(hi)
            else:
                lf, hf = unkey(lo), unkey(hi)
                lc = jnp.log(clo)
                frac = jnp.clip((lc - math.log(K - 0.5)) / (lc - jnp.log(jnp.maximum(chi, 0.5))), 0.05, 0.95)
                cand = _score_key(lf + frac * (hf - lf))
                cand = jnp.where(it % 3 == 2, (lo >> 1) + (hi >> 1) + (lo & hi & 1), cand)
                cand = jnp.where(it == 0, guess_lo, cand)
                cand = jnp.where(it == 1, guess_hi, cand)
                cand = jnp.clip(cand, lo + 1, hi - 1)
            cand = jnp.where(active, cand, lo)
            cnt = count_ge(cand)
            up = jnp.logical_and(active, cnt >= K)
            down = jnp.logical_and(active, cnt < K)
            hi = jnp.where(down, cand, jnp.where(up, cand + 1, hi) if peel else hi)
            return (it + 1, jnp.where(up, cand, lo), hi, jnp.where(up, cnt, clo), jnp.where(down, cnt, chi))

        def steps(n, st):
            return lax.fori_loop(0, n, lambda _, s: step(s, False), st)

        lo0 = jnp.full((1, TQ), KEY_ALL - 1, jnp.int32)
        hi0 = _score_key(smax) + 1
        clo0 = jnp.zeros((1, TQ), F32) + ((nb + 1) // 2 * (2 * KB)).astype(F32)
        st = (jnp.int32(0), lo0, hi0, clo0, jnp.zeros((1, TQ), F32))
        st = steps(SEARCH_FIRST_ROUND - 1, st)
        st = step(st, True)
        st = lax.while_loop(cond, lambda s: step(steps(SEARCH_ROUND - 1, s), True), st)
        return st[1], st[3]

    def no_search():
        return jnp.full((1, TQ), KEY_ALL, jnp.int32), jnp.full((1, TQ), K, F32)

    t, cnt_t = lax.cond(i > 0, search, no_search)
    t = jnp.maximum(t, KEY_ALL)

    @pl.when(jnp.max(cnt_t) > K)
    def _():
        allowed = K - count_ge(t + 1)
        r = lax.broadcasted_iota(jnp.int32, (KB, KB), 0)
        c = lax.broadcasted_iota(jnp.int32, (KB, KB), 1)
        earlier = jnp.where(c < r, 1.0, 0.0).astype(BF16)

        def body(jj, seen):
            sls = [pl.ds(pl.multiple_of((2 * jj + u) * KB, KB), KB) for u in range(2)]
            blks = [key_scr[sl, :] for sl in sls]
            eqs = [jnp.where(blk == t, 1.0, 0.0) for blk in blks]
            seens = [seen, seen + eqs[0].sum(axis=0, keepdims=True)]
            ranks = [_dot(earlier, eq.astype(BF16)) + sn for eq, sn in zip(eqs, seens)]
            for sl, blk, eq, rank in zip(sls, blks, eqs, ranks):
                demote = eq * jnp.where(rank >= allowed, 1.0, 0.0)
                key_scr[sl, :] = jnp.where(demote > 0.5, t - 1, blk)
            return seens[1] + eqs[1].sum(axis=0, keepdims=True)

        lax.fori_loop(0, (nb + 1) // 2, body, jnp.zeros((1, TQ), F32))

    def scores_to(slot, j, bias_rows, present):
        k0 = pl.multiple_of(j * KB, KB)
        kblk = kv_ref[0, pl.ds(k0, KB), :]
        sel = key_scr[pl.ds(k0, KB), :] >= (t if present is True else jnp.where(present, t, jnp.int32(2 ** 31 - 1)))
        for hd in range(SA_HEADS):
            s = _dot_t(kblk, q_ref[0, :, hd * LANES:(hd + 1) * LANES])
            if bias_rows is not None:
                s = s + nbias_ref[hd, bias_rows, :]
            s = jnp.where(sel, s, NEG_INF)
            s_scr[slot][hd] = s
            smax_scr[slot][hd] = jnp.broadcast_to(s.max(axis=0, keepdims=True), smax_scr[slot].shape[1:])

    def softmax_pv(slot, j, st):
        vt = vt_ref[:, pl.ds(pl.multiple_of(j * KB, KB), KB)]
        out = []
        for hd in range(SA_HEADS):
            m, l, acc = st[hd]
            mn = jnp.maximum(m, smax_scr[slot][hd][0:1])
            p = jnp.exp(s_scr[slot][hd] - mn)
            alpha = jnp.exp(m - mn)
            out.append((mn, alpha * l + p.sum(axis=0, keepdims=True), alpha * acc + _dot(vt, p.astype(BF16))))
        return tuple(out)

    st = tuple((jnp.full((1, TQ), NEG_INF, F32), jnp.zeros((1, TQ), F32), jnp.zeros((HEAD_DIM, TQ), F32))
               for _ in range(SA_HEADS))
    left = jnp.maximum(i - 1, 0)
    tails = [(left, lambda slot: scores_to(slot, left, slice(0, KB), i >= 1)),
             (i, lambda slot: scores_to(slot, i, slice(KB, 2 * KB), True))]
    st = _block_pipeline(left, i, lambda slot, j, present: scores_to(slot, j, None, present), tails, softmax_pv, st)
    y_t = jnp.concatenate([acc / l for _, l, acc in st], axis=0)
    o_ref[0] = _group_norm_t(y_t, g_ref[...])


def _sparse_attention(saq, sakv, svt, iq, ik, iwt, nbias, g):
    B, S, _ = saq.shape
    TQ = TQ_SA
    W = GROUP_W
    nt = S // TQ
    n_adm = (np.arange(S) // CHUNK + 1) * CHUNK
    zq = np.array([NormalDist().inv_cdf(1.0 - TOPK_MAX / n) if n > TOPK_MAX else 0.0 for n in n_adm], np.float32)
    zq = jnp.asarray(np.tile(zq[None, :], (8, 1)))
    return pl.pallas_call(
        _sa_kernel,
        grid=(B, nt),
        in_specs=[pl.BlockSpec((1, TQ, saq.shape[2]), lambda b, i: (b, i, 0)),
                  pl.BlockSpec((1, S, sakv.shape[2]), lambda b, i: (b, 0, 0)),
                  pl.BlockSpec((HEAD_DIM, S), lambda b, i: (0, b)),
                  pl.BlockSpec((1, TQ, iq.shape[2]), lambda b, i: (b, i, 0)),
                  pl.BlockSpec((1, S, ik.shape[2]), lambda b, i: (b, 0, 0)),
                  pl.BlockSpec((IWT_ROWS, TQ), lambda b, i: (0, b * nt + i)),
                  pl.BlockSpec((8, TQ), lambda b, i: (0, i)),
                  pl.BlockSpec(nbias.shape, lambda b, i: (0, 0, 0)),
                  pl.BlockSpec((1, W), lambda b, i: (0, 0))],
        out_specs=pl.BlockSpec((1, TQ, W), lambda b, i: (b, i, 0)),
        out_shape=jax.ShapeDtypeStruct((B, S, W), BF16),
        scratch_shapes=([pltpu.VMEM((S, TQ), jnp.int32)] + [pltpu.VMEM((SA_HEADS, KB_SA, TQ), F32)] * 2
                        + [pltpu.VMEM((SA_HEADS, 8, TQ), F32)] * 2),
        compiler_params=_cparams(2),
        name="sparse_attention",
    )(saq, sakv, svt, iq, ik, iwt, zq, nbias, g)


def _group_norm_t(y_t, g):
    inv = lax.rsqrt(jnp.mean(y_t * y_t, axis=0, keepdims=True) + EPS)
    return ((y_t * inv).T * g).astype(BF16)


def _block_pipeline(n_plain, last_blk, score_plain, tails, softmax, st):
    off = n_plain % 2
    n_loop = jnp.maximum((n_plain + off) // 2 - 1, 0)

    def blk(pos):
        return jnp.clip(pos - off, 0, last_blk)

    def body(pp, st):
        pos = 2 * pp
        score_plain(1, blk(pos + 1), True)
        st = softmax(0, blk(pos), st)
        score_plain(0, blk(pos + 2), True)
        return softmax(1, blk(pos + 1), st)

    score_plain(0, blk(0), jnp.logical_and(n_plain >= 1, off == 0))
    st = lax.fori_loop(0, n_loop, body, st)
    e0 = 2 * n_loop
    slot, pending = 0, blk(e0)
    steps = [(blk(e0 + 1), lambda s: score_plain(s, blk(e0 + 1), n_plain >= 1))] + list(tails)
    for nxt, score_fn in steps:
        score_fn(1 - slot)
        st = softmax(slot, pending, st)
        slot, pending = 1 - slot, nxt
    return softmax(slot, pending, st)


def _mla_kernel(q_ref, k_ref, vt_ref, g_ref, o_ref, s0_scr, s1_scr, smax0_scr, smax1_scr):
    i = pl.program_id(1)
    TQ, KB = TQ_MLA, KB_MLA
    R = TQ // KB
    cshift = CHUNK.bit_length() - 1
    kch = lax.broadcasted_iota(jnp.int32, (KB, TQ), 0) >> cshift
    qch = lax.broadcasted_iota(jnp.int32, (KB, TQ), 1) >> cshift
    s_scr, smax_scr = (s0_scr, s1_scr), (smax0_scr, smax1_scr)

    def scores_to(slot, j, keep):
        k0 = pl.multiple_of(j * KB, KB)
        for hd in range(MLA_HEADS):
            cols = slice(hd * LANES, (hd + 1) * LANES)
            s = _dot_t(k_ref[0, pl.ds(k0, KB), cols], q_ref[0, :, cols])
            if keep is not None:
                s = jnp.where(keep, s, NEG_INF)
            s_scr[slot][hd] = s
            smax_scr[slot][hd] = jnp.broadcast_to(s.max(axis=0, keepdims=True), smax_scr[slot].shape[1:])

    def score_plain(slot, j, present):
        scores_to(slot, j, None if present is True else kch >= jnp.where(present, 0, TQ))

    def softmax_pv(slot, j, st):
        k0 = pl.multiple_of(j * KB, KB)
        out = []
        for hd in range(MLA_HEADS):
            m, l, acc = st[hd]
            mn = jnp.maximum(m, smax_scr[slot][hd][0:1])
            p = jnp.exp(s_scr[slot][hd] - mn)
            alpha = jnp.exp(m - mn)
            vt = vt_ref[hd * MLA_V:(hd + 1) * MLA_V, pl.ds(k0, KB)]
            out.append((mn, alpha * l + p.sum(axis=0, keepdims=True),
                        alpha * acc + _dot(vt, p.astype(BF16))))
        return tuple(out)

    st = tuple((jnp.full((1, TQ), NEG_INF, F32), jnp.zeros((1, TQ), F32), jnp.zeros((MLA_V, TQ), F32))
               for _ in range(MLA_HEADS))
    tails = [(R * i + d, functools.partial(
        lambda slot, d: scores_to(slot, R * i + d, kch + d * (KB // CHUNK) <= qch), d=d)) for d in range(R)]
    st = _block_pipeline(R * i, R * i + R - 1, score_plain, tails, softmax_pv, st)
    o_ref[0] = _group_norm_t(jnp.concatenate([acc / l for _, l, acc in st], axis=0), g_ref[...])


def _latent_attention(mq, mk, mvt, g):
    B, S, _ = mq.shape
    TQ = TQ_MLA
    W = GROUP_W
    return pl.pallas_call(
        _mla_kernel,
        grid=(B, S // TQ),
        in_specs=[pl.BlockSpec((1, TQ, mq.shape[2]), lambda b, i: (b, i, 0)),
                  pl.BlockSpec((1, S, mk.shape[2]), lambda b, i: (b, 0, 0)),
                  pl.BlockSpec((W, S), lambda b, i: (0, b)),
                  pl.BlockSpec((1, W), lambda b, i: (0, 0))],
        out_specs=pl.BlockSpec((1, TQ, W), lambda b, i: (b, i, 0)),
        out_shape=jax.ShapeDtypeStruct((B, S, W), BF16),
        scratch_shapes=[pltpu.VMEM((MLA_HEADS, KB_MLA, TQ), F32)] * 2 + [pltpu.VMEM((MLA_HEADS, 8, TQ), F32)] * 2,
        compiler_params=_cparams(2),
        name="latent_attention",
    )(mq, mk, mvt, g)


def _ffn_kernel(ya_ref, yb_ref, yc_ref, yd_ref, x_ref, mod_ref, wout_ref, gffn_ref, w1_ref, w3_ref, w2_ref,
                gfin_ref, o_ref, acc_scr, *, final):
    gt1 = mod_ref[0, 2:3, :]
    sh2 = mod_ref[0, 3:4, :]
    sc2 = mod_ref[0, 4:5, :]
    gt2 = mod_ref[0, 5:6, :]
    attn = _dot(ya_ref[...], wout_ref[0:GROUP_W, :])
    for gi, y_ref in enumerate((yb_ref, yc_ref, yd_ref), start=1):
        attn = attn + _dot(y_ref[...], wout_ref[gi * GROUP_W:(gi + 1) * GROUP_W, :])
    x1 = x_ref[...] + gt1 * attn
    h = (_rms(x1, gffn_ref[...]) * (1.0 + sc2) + sh2).astype(BF16)
    for ci in range(D_FF // FF_CHUNK):
        cols = slice(ci * FF_CHUNK, (ci + 1) * FF_CHUNK)
        a = _dot(h, w1_ref[:, cols])
        gate = (a * jax.nn.sigmoid(a) * _dot(h, w3_ref[:, cols])).astype(BF16)
        part = _dot(gate, w2_ref[cols, :])
        if ci == 0:
            acc_scr[...] = part
        else:
            acc_scr[...] += part
    x2 = x1 + gt2 * acc_scr[...]
    o_ref[...] = _rms(x2, gfin_ref[...]) if final else x2


def _out_ffn(ys, x2, mod, wout, gffn, w1, w3, w2, gfin, S, final):
    N, D = x2.shape
    TM = TM_FFN
    nt = S // TM

    def full(a):
        return pl.BlockSpec(a.shape, lambda i: (0,) * a.ndim, pipeline_mode=pl.Buffered(1))

    def tok(w):
        return pl.BlockSpec((TM, w), lambda i: (i, 0))

    return pl.pallas_call(
        functools.partial(_ffn_kernel, final=final),
        grid=(N // TM,),
        in_specs=[tok(GROUP_W)] * 4 + [tok(D), pl.BlockSpec((1, 6, D), lambda i: (i // nt, 0, 0)),
                                       full(wout), full(gffn), full(w1), full(w3), full(w2), full(gfin)],
        out_specs=tok(D),
        out_shape=jax.ShapeDtypeStruct((N, D), F32),
        scratch_shapes=[pltpu.VMEM((TM, D), F32)],
        compiler_params=_cparams(1),
        name="out_ffn_final" if final else "out_ffn",
    )(*ys, x2, mod, wout, gffn, w1, w3, w2, gfin)


def _t5_bucket(rel):
    nb = T5_BUCKETS // 2
    max_exact = nb // 2
    ret = jnp.where(rel > 0, nb, 0)
    n = jnp.abs(rel)
    nf = jnp.maximum(n, 1).astype(jnp.float32)
    large = max_exact + (jnp.log(nf / max_exact) / math.log(T5_MAX_DIST / max_exact)
                         * (nb - max_exact)).astype(jnp.int32)
    large = jnp.minimum(large, nb - 1)
    return ret + jnp.where(n < max_exact, n, large)


def _rope_tables(S):
    half = MLA_ROPE // 2
    freqs = ROPE_BASE ** (-jnp.arange(half, dtype=F32) / half)
    ang = jnp.arange(S, dtype=jnp.int32).astype(F32)[:, None] * freqs[None, :]
    cos, sin = jnp.cos(ang), jnp.sin(ang)
    cos2 = jnp.concatenate([cos, cos], axis=1)
    sin2 = jnp.concatenate([-sin, sin], axis=1)
    zeros = jnp.zeros((S, LANES - MLA_NOPE - MLA_ROPE), F32)
    scale = (MLA_NOPE + MLA_ROPE) ** -0.5
    cosq = jnp.concatenate([jnp.full((S, MLA_NOPE), scale, F32), cos2 * scale, zeros], axis=1)
    sinq = jnp.concatenate([jnp.zeros((S, MLA_NOPE), F32), sin2 * scale, zeros], axis=1)
    cosk = jnp.concatenate([jnp.zeros((S, MLA_NOPE), F32), cos2, zeros], axis=1)
    sink = jnp.concatenate([jnp.zeros((S, MLA_NOPE), F32), sin2, zeros], axis=1)
    return cosq, sinq, cosk, sink


def _pack_in_weight(w):
    part = {n: w[:, IN_OFFS[k]:IN_OFFS[k + 1]] for k, n in enumerate(
        ('pool_u', 'ca_q', 'ca_k', 'ca_v', 'sa_q', 'sa_k', 'sa_v', 'idx_q', 'idx_k', 'idx_w',
         'mla_cq', 'mla_ckv', 'mla_kr'))}
    D = w.shape[0]
    z = lambda n: jnp.zeros((D, n), F32)
    qscale = HEAD_DIM ** -0.5
    saq = part['sa_q'].reshape(D, SA_HEADS, HEAD_DIM) * qscale
    saq = jnp.concatenate([saq, jnp.zeros_like(saq)], axis=2).reshape(D, SA_HEADS * LANES)
    kr = part['mla_kr']
    kr_swap = jnp.concatenate([kr[:, MLA_ROPE // 2:], kr[:, :MLA_ROPE // 2]], axis=1)
    pad_r = LANES - MLA_NOPE - MLA_ROPE
    cols = [part['pool_u'], part['ca_q'] * qscale, part['ca_k'], saq,
            part['sa_k'], part['sa_v'], part['idx_q'],
            part['idx_k'], z(IDX_DIM), z(IDX_DIM), part['idx_k'],
            part['mla_cq'], part['mla_ckv'],
            z(MLA_NOPE), kr, z(pad_r), z(MLA_NOPE), kr_swap, z(pad_r)]
    out = jnp.concatenate(cols, axis=1)
    assert out.shape[1] == C_END
    wt = jnp.concatenate([part['idx_w'].T, jnp.zeros((IWT_ROWS - IDX_HEADS, D), F32), part['sa_v'].T,
                          part['ca_v'].T], axis=0)
    return out.astype(BF16), wt.astype(BF16)


def _pack_mla_weights(w_uq, w_ukv):
    R = w_uq.shape[0]
    pad = jnp.zeros((R, MLA_HEADS, LANES - MLA_NOPE - MLA_ROPE), F32)
    rope_w = w_uq[:, :, MLA_NOPE:]
    rope_sw = jnp.concatenate([rope_w[:, :, MLA_ROPE // 2:], rope_w[:, :, :MLA_ROPE // 2]], axis=2)
    wq = jnp.concatenate([w_uq, pad], axis=2).reshape(R, MLA_HEADS * LANES)
    wqs = jnp.concatenate([jnp.zeros((R, MLA_HEADS, MLA_NOPE), F32), rope_sw, pad],
                          axis=2).reshape(R, MLA_HEADS * LANES)
    Rk = w_ukv.shape[0]
    wk = jnp.concatenate([w_ukv[:, :, :MLA_NOPE], jnp.zeros((Rk, MLA_HEADS, LANES - MLA_NOPE), F32)],
                         axis=2).reshape(Rk, MLA_HEADS * LANES)
    wvt = w_ukv[:, :, MLA_NOPE:].reshape(Rk, MLA_HEADS * MLA_V).T
    return wq.astype(BF16), wqs.astype(BF16), wk.astype(BF16), wvt.astype(BF16)


def _toeplitz(vec, rows, cols):
    L = vec.shape[-1]
    assert cols <= L - 1
    flat = jnp.tile(vec, (1, rows))[:, :rows * (L - 1)]
    return flat.reshape(vec.shape[0], rows, L - 1)[:, :, :cols]


def _signed_mod_range(L, hi):
    d = np.arange(L)
    return np.where(d <= hi, d, d - L)


def _band_bias(rel_table):
    L = CA_WIN + TQ_CA
    e = _signed_mod_range(L, TQ_CA - 1)
    ridx = np.clip(CA_LEFT_CHUNKS * CHUNK + e, -(CHUNK - 1), CA_MAX_REL) + (CHUNK - 1)
    bias = _toeplitz(rel_table[:, ridx].astype(F32), CA_WIN, TQ_CA)
    kc = np.arange(CA_WIN)[:, None] // CHUNK
    qc = np.arange(TQ_CA)[None, :] // CHUNK + CA_LEFT_CHUNKS
    valid = (kc <= qc) & (kc >= qc - CA_LEFT_CHUNKS)
    return jnp.where(valid[None], bias, NEG_INF)


def _t5_bias(t5_table):
    TQ = TQ_SA
    L = 3 * TQ
    e = _signed_mod_range(L, TQ - 1)
    rel = jnp.asarray(-e - TQ, jnp.int32)
    far = t5_table[_t5_bucket(jnp.int32(-(TQ + 1)))].astype(F32)
    vec = (t5_table[_t5_bucket(rel)].astype(F32) - far[None, :]).T
    return _toeplitz(vec, 2 * TQ, TQ)


def kernel(x, c, t5_table, w_mod, b_mod, g_mix, w_in, pool_w, pool_scale, ca_rel, mla_g_cq, mla_g_ckv,
           mla_w_uq, mla_w_ukv, g_group, w_out, g_ffn, ffn_w1, ffn_w3, ffn_w2, g_final):
    B, S, D = x.shape
    assert D == D_MODEL and S % TM_PROJ == 0 and S % TQ_SA == 0 and S >= 4 * TOPK_MAX
    N = B * S
    mod_all = _modulation(c, w_mod, b_mod)
    rope_tabs = _rope_tables(S)
    nbias = _t5_bias(t5_table)
    row = lambda v: v.reshape(1, -1).astype(F32)
    x2 = x.reshape(N, D)
    for l in range(DEPTH):
        mod = mod_all[l].reshape(B, 6, D)
        w1, wt = _pack_in_weight(w_in[l])
        wq, wqs, wk, wvt = _pack_mla_weights(mla_w_uq[l], mla_w_ukv[l])
        gg = g_group[l].reshape(4, 1, GROUP_W).astype(F32)
        wbd = jax.scipy.linalg.block_diag(*[pool_w[l, gi] for gi in range(len(POOL_WINDOWS))]).astype(BF16)
        (y_a, ca, saq, sakv, iq, ik, iwt, svt, cavt, mq, mk, mvt) = _inproj(
            x2, mod, row(g_mix[l]), w1, wt, row(mla_g_cq[l]), row(mla_g_ckv[l]), wq, wqs, wk, wvt, rope_tabs,
            wbd, row(pool_scale[l]), gg[0], S)
        bsw = lambda a: a.reshape(B, S, a.shape[-1])
        y_b = _chunk_attention(bsw(ca), cavt, _band_bias(ca_rel[l]), gg[1])
        y_c = _sparse_attention(bsw(saq), bsw(sakv), svt, bsw(iq), bsw(ik), iwt, nbias, gg[2])
        y_d = _latent_attention(bsw(mq), bsw(mk), mvt, gg[3])
        ys = [y.reshape(N, GROUP_W) for y in (y_a, y_b, y_c, y_d)]
        x2 = _out_ffn(ys, x2, mod, w_out[l].astype(BF16), row(g_ffn[l]), ffn_w1[l].astype(BF16),
                      ffn_w3[l].astype(BF16), ffn_w2[l].astype(BF16), row(g_final), S,
                      final=(l == DEPTH - 1))
    return x2.reshape(B, S, D)
```

```python
import functools
import math
from statistics import NormalDist

import jax
import jax.numpy as jnp
from jax import lax
import numpy as np
from jax.experimental import pallas as pl
from jax.experimental.pallas import tpu as pltpu

F32 = jnp.float32
BF16 = jnp.bfloat16

D_MODEL = 1024
DEPTH = 2
CHUNK = 64
EPS = 1e-6
NEG_INF = -1e30
GROUP_W = 256
HEAD_DIM = 64
POOL_WINDOWS = (2, 4, 8, 16)
POOL_HALO = 16
CA_HEADS = 4
CA_LEFT_CHUNKS = 8
CA_MAX_REL = 256
SA_HEADS = 4
IDX_HEADS = 8
IDX_DIM = 64
TOPK_MAX = 256
MLA_HEADS = 4
MLA_NOPE = 64
MLA_ROPE = 32
MLA_V = 64
ROPE_BASE = 10000.0
T5_BUCKETS = 32
T5_MAX_DIST = 128
D_FF = 2816
IN_WIDTHS = (256, 256, 256, 256, 256, 64, 64, 512, 64, 8, 256, 128, 32)
IN_OFFS = tuple(int(v) for v in np.cumsum((0,) + IN_WIDTHS))

LANES = 128
VMEM_LIMIT = 56 * 1024 * 1024

TM_PROJ = 1024
TM_FFN = 512
TQ_CA = 256
CA_WIN = TQ_CA + CA_LEFT_CHUNKS * CHUNK
CA_NBLK = CA_WIN // TQ_CA
IWT_ROWS = 16
TQ_SA = 256
KB_SA = 256
COUNT_CHAINS = 2
SEARCH_FIRST_ROUND = 16
SEARCH_ROUND = 4
GUESS_SPREAD = 0.3
TQ_MLA = 512
KB_MLA = 256
FF_CHUNK = 256

C_POOL = 0
C_CA = C_POOL + 256
C_SAQ = C_CA + 2 * GROUP_W
C_SAKV = C_SAQ + SA_HEADS * LANES
C_IQ = C_SAKV + LANES
C_IK = C_IQ + IDX_HEADS * IDX_DIM
C_CQ = C_IK + 2 * LANES
C_CKV = C_CQ + 256
C_KRF = C_CKV + LANES
C_KRS = C_KRF + LANES
C_END = C_KRS + LANES

INT_MIN = -2 ** 31
KEY_ALL = INT_MIN - int(np.array(-np.inf, np.float32).view(np.int32)) + 1


def _cparams(n_axes):
    return pltpu.CompilerParams(dimension_semantics=("arbitrary",) * n_axes,
                                vmem_limit_bytes=VMEM_LIMIT)


def _rms(x, g):
    return x * lax.rsqrt(jnp.mean(x * x, axis=-1, keepdims=True) + EPS) * g


def _dot(a, b):
    return jnp.dot(a, b, preferred_element_type=F32)


def _dot_t(a, b):
    return lax.dot_general(a, b, (((1,), (1,)), ((), ())), preferred_element_type=F32)


def _mod_kernel(c_ref, w_ref, b_ref, o_ref):
    c = c_ref[...]
    act = c * jax.nn.sigmoid(c)
    o_ref[0] = jnp.dot(act, w_ref[0], precision=lax.Precision.HIGHEST,
                       preferred_element_type=F32) + b_ref[0]


def _modulation(c, w_mod, b_mod):
    L, D, W = w_mod.shape
    B = c.shape[0]
    nj = W // D
    return pl.pallas_call(
        _mod_kernel,
        grid=(L, nj),
        in_specs=[pl.BlockSpec((B, D), lambda l, j: (0, 0)),
                  pl.BlockSpec((1, D, D), lambda l, j: (l, 0, j)),
                  pl.BlockSpec((1, 1, D), lambda l, j: (l, 0, j))],
        out_specs=pl.BlockSpec((1, B, D), lambda l, j: (l, 0, j)),
        out_shape=jax.ShapeDtypeStruct((L, B, W), F32),
        compiler_params=_cparams(2),
        name="modulation",
    )(c, w_mod, b_mod.reshape(L, 1, W))


def _pool_mix(u, first_tile, t0, pad_scr, halo_scr, w_ref, scale_ref, g_ref):
    TP = u.shape[0]
    pad_scr[0:POOL_HALO, :] = jnp.where(first_tile, 0.0, halo_scr[...])
    pad_scr[POOL_HALO:, :] = u
    halo_scr[...] = u[TP - POOL_HALO:, :]

    def shifted(j):
        return pad_scr[POOL_HALO - j:POOL_HALO - j + TP, :]

    lane = lax.broadcasted_iota(jnp.int32, (TP, GROUP_W), 1)
    w2 = u + shifted(1)
    w4 = w2 + shifted(2) + shifted(3)
    w8 = w4
    for j in range(4, 8):
        w8 = w8 + shifted(j)
    w16 = w8
    for j in range(8, 16):
        w16 = w16 + shifted(j)
    win = jnp.where(lane < 64, w2, jnp.where(lane < 128, w4, jnp.where(lane < 192, w8, w16)))
    wlen = jnp.where(lane < 64, 2, jnp.where(lane < 128, 4, jnp.where(lane < 192, 8, 16)))
    t = t0 + lax.broadcasted_iota(jnp.int32, (TP, GROUP_W), 0)
    cnt = jnp.minimum(t + 1, wlen).astype(F32)
    d = (win / cnt - u).astype(BF16)
    y = _dot(d, w_ref[...]) * scale_ref[...]
    return _rms(y, g_ref[...]).astype(BF16)


def _inproj_kernel(x_ref, mod_ref, gmix_ref, w_ref, wt_ref, gcq_ref, gckv_ref, wq_ref, wqs_ref, wk_ref, wvt_ref,
                   cosq_ref, sinq_ref, cosk_ref, sink_ref, wpool_ref, spool_ref, gpool_ref,
                   ya_o, ca_o, saq_o, sakv_o, iq_o, ik_o, iwt_o, svt_o, cavt_o, mq_o, mk_o, mvt_o,
                   pad_scr, halo_scr, *, tiles_per_seq):
    sh1 = mod_ref[0, 0:1, :]
    sc1 = mod_ref[0, 1:2, :]
    h = (_rms(x_ref[...], gmix_ref[...]) * (1.0 + sc1) + sh1).astype(BF16)

    def seg(a, b):
        return _dot(h, w_ref[:, a:b])

    qn = _rms(seg(C_CQ, C_CKV), gcq_ref[...]).astype(BF16)
    kvn = _rms(seg(C_CKV, C_KRF), gckv_ref[...]).astype(BF16)
    krope = seg(C_KRF, C_KRS) * cosk_ref[...] + seg(C_KRS, C_END) * sink_ref[...]

    u_pool = seg(C_POOL, C_CA)
    ca_o[...] = seg(C_CA, C_SAQ).astype(BF16)
    qf = _dot(qn, wq_ref[...])
    qs = _dot(qn, wqs_ref[...])
    saq_o[...] = seg(C_SAQ, C_SAKV).astype(BF16)
    cosq = jnp.concatenate([cosq_ref[...]] * MLA_HEADS, axis=1)
    sinq = jnp.concatenate([sinq_ref[...]] * MLA_HEADS, axis=1)
    mq_o[...] = (qf * cosq + qs * sinq).astype(BF16)
    kvf = _dot(kvn, wk_ref[...])
    sakv_o[...] = seg(C_SAKV, C_IQ).astype(BF16)
    iq_o[...] = seg(C_IQ, C_IK).astype(BF16)
    for hd in range(MLA_HEADS):
        mk_o[:, hd * LANES:(hd + 1) * LANES] = (kvf[:, hd * LANES:(hd + 1) * LANES] + krope).astype(BF16)
    mvt_o[...] = _dot_t(wvt_ref[...], kvn).astype(BF16)
    ik_o[...] = seg(C_IK, C_CQ).astype(BF16)
    tr = _dot_t(wt_ref[...], h)
    iwt_o[...] = tr[0:IWT_ROWS] * ((IDX_HEADS ** -0.5) * (IDX_DIM ** -0.5))
    svt_o[...] = tr[IWT_ROWS:IWT_ROWS + HEAD_DIM].astype(BF16)
    cavt_o[...] = tr[IWT_ROWS + HEAD_DIM:].astype(BF16)
    tile = pl.program_id(0) % tiles_per_seq
    ya_o[...] = _pool_mix(u_pool, tile == 0, tile * x_ref.shape[0], pad_scr, halo_scr,
                          wpool_ref, spool_ref, gpool_ref)


def _inproj(x2, mod, gmix, w1, wt, gcq, gckv, wq, wqs, wk, wvt, rope_tabs, wpool, spool, gpool, S):
    N, D = x2.shape
    TM = TM_PROJ
    nt = S // TM
    cosq, sinq, cosk, sink = rope_tabs

    def full(a):
        return pl.BlockSpec(a.shape, lambda i: (0,) * a.ndim)

    def tok(w):
        return pl.BlockSpec((TM, w), lambda i: (i, 0))

    tab = pl.BlockSpec((TM, LANES), lambda i: (i % nt, 0))
    def tokt(rows):
        return pl.BlockSpec((rows, TM), lambda i: (0, i))

    outs = [(GROUP_W, BF16, True), (C_SAQ - C_CA, BF16, True), (C_SAKV - C_SAQ, BF16, True),
            (C_IQ - C_SAKV, BF16, True), (C_IK - C_IQ, BF16, True), (C_CQ - C_IK, BF16, True),
            (IWT_ROWS, F32, False), (HEAD_DIM, BF16, False), (GROUP_W, BF16, False),
            (MLA_HEADS * LANES, BF16, True), (MLA_HEADS * LANES, BF16, True), (GROUP_W, BF16, False)]
    return pl.pallas_call(
        functools.partial(_inproj_kernel, tiles_per_seq=nt),
        grid=(N // TM,),
        in_specs=[tok(D),
                  pl.BlockSpec((1, 6, D), lambda i: (i // nt, 0, 0)),
                  full(gmix), full(w1), full(wt), full(gcq), full(gckv), full(wq), full(wqs), full(wk), full(wvt),
                  tab, tab, tab, tab, full(wpool), full(spool), full(gpool)],
        out_specs=[tok(w) if tm else tokt(w) for w, _, tm in outs],
        out_shape=[jax.ShapeDtypeStruct((N, w) if tm else (w, N), dt) for w, dt, tm in outs],
        scratch_shapes=[pltpu.VMEM((POOL_HALO + TM, GROUP_W), F32), pltpu.VMEM((POOL_HALO, GROUP_W), F32)],
        compiler_params=_cparams(1),
        name="inproj",
    )(x2, mod, gmix, w1, wt, gcq, gckv, wq, wqs, wk, wvt, cosq, sinq, cosk, sink, wpool, spool, gpool)


def _ca_kernel(q_ref, k_ref, vt_ref, bias_ref, g_ref, o_ref):
    i = pl.program_id(1)
    TQ = TQ_CA
    lane = lax.broadcasted_iota(jnp.int32, (TQ, LANES), 1)
    starts = []
    for j in range(CA_NBLK):
        kb = i - (CA_NBLK - 1) + j
        starts.append((kb >= 0, pl.multiple_of(jnp.maximum(kb, 0) * TQ, TQ)))
    scored = []
    for hd in range(CA_HEADS):
        cols = slice((hd // 2) * LANES, (hd // 2 + 1) * LANES)
        keep = (lane < HEAD_DIM) if hd % 2 == 0 else (lane >= HEAD_DIM)
        qh = jnp.where(keep, q_ref[0, :, cols].astype(F32), 0.0).astype(BF16)
        parts = []
        for j, (present, start) in enumerate(starts):
            s = _dot_t(k_ref[0, pl.ds(start, TQ), cols], qh) + bias_ref[hd, j * TQ:(j + 1) * TQ, :]
            parts.append(jnp.where(present, s, NEG_INF))
        m = parts[0].max(axis=0, keepdims=True)
        for s in parts[1:]:
            m = jnp.maximum(m, s.max(axis=0, keepdims=True))
        scored.append((parts, m))
    outs = []
    for hd, (parts, m) in enumerate(scored):
        l = jnp.zeros((1, TQ), F32)
        acc = jnp.zeros((HEAD_DIM, TQ), F32)
        for j, (_, start) in enumerate(starts):
            p = jnp.exp(parts[j] - m)
            l = l + p.sum(axis=0, keepdims=True)
            acc = acc + _dot(vt_ref[hd * HEAD_DIM:(hd + 1) * HEAD_DIM, pl.ds(start, TQ)], p.astype(BF16))
        outs.append(acc / l)
    o_ref[0] = _group_norm_t(jnp.concatenate(outs, axis=0), g_ref[...])


def _chunk_attention(caqk, cavt, bias, g):
    B, S, _ = caqk.shape
    W = GROUP_W
    TQ = TQ_CA
    return pl.pallas_call(
        _ca_kernel,
        grid=(B, S // TQ),
        in_specs=[pl.BlockSpec((1, TQ, W), lambda b, i: (b, i, 0)),
                  pl.BlockSpec((1, S, W), lambda b, i: (b, 0, 1)),
                  pl.BlockSpec((W, S), lambda b, i: (0, b)),
                  pl.BlockSpec(bias.shape, lambda b, i: (0, 0, 0)),
                  pl.BlockSpec((1, W), lambda b, i: (0, 0))],
        out_specs=pl.BlockSpec((1, TQ, W), lambda b, i: (b, i, 0)),
        out_shape=jax.ShapeDtypeStruct((B, S, W), BF16),
        compiler_params=_cparams(2),
        name="band_attention",
    )(caqk, caqk, cavt, bias, g)


def _score_key(score):
    b = lax.bitcast_convert_type(score, jnp.int32)
    return jnp.where(b < 0, jnp.int32(INT_MIN) - b, b)


def _sa_kernel(q_ref, kv_ref, vt_ref, iq_ref, ik_ref, iwt_ref, zq_ref, nbias_ref, g_ref, o_ref,
               key_scr, s0_scr, s1_scr, smax0_scr, smax1_scr):
    s_scr, smax_scr = (s0_scr, s1_scr), (smax0_scr, smax1_scr)
    i = pl.program_id(1)
    TQ, KB = TQ_SA, KB_SA
    K = float(TOPK_MAX)
    nb = i + 1
    q0 = i * TQ
    cshift = CHUNK.bit_length() - 1
    kchunk = lax.broadcasted_iota(jnp.int32, (KB, TQ), 0) >> cshift
    qchunk = (q0 + lax.broadcasted_iota(jnp.int32, (1, TQ), 1)) >> cshift

    iwt = iwt_ref[...]

    def score_block(j, carry, tail, moments):
        smax, s1, s2 = carry
        k0 = pl.multiple_of(j * KB, KB)
        ik = ik_ref[0, pl.ds(k0, KB), :]
        ik2 = jnp.concatenate([ik[:, :LANES], ik[:, LANES:]], axis=0)
        sc = jnp.zeros((KB, TQ), F32)
        for p in range(IDX_HEADS // 2):
            logits = _dot_t(ik2, iq_ref[0, :, p * LANES:(p + 1) * LANES])
            sc = sc + iwt[2 * p:2 * p + 1, :] * jnp.maximum(logits[:KB], 0.0)
            sc = sc + iwt[2 * p + 1:2 * p + 2, :] * jnp.maximum(logits[KB:], 0.0)
        if tail:
            sc = jnp.where(kchunk <= qchunk - (k0 >> cshift), sc, -jnp.inf)
        key_scr[pl.ds(k0, KB), :] = _score_key(sc)
        smax = jnp.maximum(smax, sc.max(axis=0, keepdims=True))
        if moments:
            s1 = s1 + sc.sum(axis=0, keepdims=True)
            s2 = s2 + (sc * sc).sum(axis=0, keepdims=True)
        return smax, s1, s2

    n_pairs = (nb + 1) // 2
    carry = lax.fori_loop(
        0, n_pairs - 1, lambda j, c: score_block(2 * j + 1, score_block(2 * j, c, False, True), False, False),
        (jnp.full((1, TQ), -jnp.inf, F32), jnp.zeros((1, TQ), F32), jnp.zeros((1, TQ), F32)))
    last = 2 * (n_pairs - 1)
    carry = score_block(last, carry, True, False)

    def pad_block(c):
        key_scr[pl.ds(pl.multiple_of((last + 1) * KB, KB), KB), :] = jnp.full((KB, TQ), KEY_ALL - 1, jnp.int32)
        return c

    smax, s1, s2 = lax.cond(nb % 2 == 1, pad_block, lambda c: score_block(last + 1, c, True, False), carry)
    n_moments = ((n_pairs - 1) * KB).astype(F32)

    def count_ge(cand):
        def body(j, acc):
            blk = key_scr[pl.ds(pl.multiple_of(j * (2 * KB), 2 * KB), 2 * KB), :]
            ones = jnp.where(blk >= cand, 1.0, 0.0)
            return acc + ones.reshape(COUNT_CHAINS, -1, 8, TQ).sum(axis=1)
        acc = lax.fori_loop(0, (nb + 1) // 2, body, jnp.zeros((COUNT_CHAINS, 8, TQ), F32))
        return acc.sum(axis=0).sum(axis=0, keepdims=True)

    def max_below(bound):
        def body(j, acc):
            blk = key_scr[pl.ds(pl.multiple_of(j * (2 * KB), 2 * KB), 2 * KB), :]
            below = jnp.where(blk < bound, blk, jnp.int32(INT_MIN))
            return jnp.maximum(acc, below.reshape(COUNT_CHAINS, -1, 8, TQ).max(axis=1))
        acc = lax.fori_loop(0, (nb + 1) // 2, body, jnp.full((COUNT_CHAINS, 8, TQ), INT_MIN, jnp.int32))
        return acc.max(axis=0).max(axis=0, keepdims=True)

    def search():
        def unkey(k):
            return lax.bitcast_convert_type(jnp.where(k < 0, jnp.int32(INT_MIN) - k, k), F32)

        def is_active(lo, hi, clo):
            return jnp.logical_and(clo > K, hi > lo + 1)

        def cond(st):
            _, lo, hi, clo, _ = st
            act = jnp.where(is_active(lo, hi, clo), 1.0, 0.0)
            return jnp.max(jnp.maximum(act[:, :LANES], act[:, LANES:])) > 0.0

        mean = s1 / n_moments
        std = jnp.sqrt(jnp.maximum(s2 / n_moments - mean * mean, 0.0))
        zq = jnp.max(zq_ref[...], axis=0, keepdims=True)
        guess_lo = _score_key(mean + (zq - GUESS_SPREAD) * std)
        guess_hi = _score_key(mean + (zq + GUESS_SPREAD) * std)

        def step(st, peel):
            it, lo, hi, clo, chi = st
            active = is_active(lo, hi, clo)
            if peel:
                cand = max_below(hi)
            else:
                lf, hf = unkey(lo), unkey(hi)
                lc = jnp.log(clo)
                frac = jnp.clip((lc - math.log(K - 0.5)) / (lc - jnp.log(jnp.maximum(chi, 0.5))), 0.05, 0.95)
                cand = _score_key(lf + frac * (hf - lf))
                cand = jnp.where(it % 3 == 2, (lo >> 1) + (hi >> 1) + (lo & hi & 1), cand)
                cand = jnp.where(it == 0, guess_lo, cand)
                cand = jnp.where(it == 1, guess_hi, cand)
                cand = jnp.clip(cand, lo + 1, hi - 1)
            cand = jnp.where(active, cand, lo)
            cnt = count_ge(cand)
            up = jnp.logical_and(active, cnt >= K)
            down = jnp.logical_and(active, cnt < K)
            hi = jnp.where(down, cand, jnp.where(up, cand + 1, hi) if peel else hi)
            return (it + 1, jnp.where(up, cand, lo), hi, jnp.where(up, cnt, clo), jnp.where(down, cnt, chi))

        def steps(n, st):
            return lax.fori_loop(0, n, lambda _, s: step(s, False), st)

        lo0 = jnp.full((1, TQ), KEY_ALL - 1, jnp.int32)
        hi0 = _score_key(smax) + 1
        clo0 = jnp.zeros((1, TQ), F32) + ((nb + 1) // 2 * (2 * KB)).astype(F32)
        st = (jnp.int32(0), lo0, hi0, clo0, jnp.zeros((1, TQ), F32))
        st = steps(SEARCH_FIRST_ROUND - 1, st)
        st = step(st, True)
        st = lax.while_loop(cond, lambda s: step(steps(SEARCH_ROUND - 1, s), True), st)
        return st[1], st[3]

    def no_search():
        return jnp.full((1, TQ), KEY_ALL, jnp.int32), jnp.full((1, TQ), K, F32)

    t, cnt_t = lax.cond(i > 0, search, no_search)
    t = jnp.maximum(t, KEY_ALL)

    @pl.when(jnp.max(cnt_t) > K)
    def _():
        allowed = K - count_ge(t + 1)
        r = lax.broadcasted_iota(jnp.int32, (KB, KB), 0)
        c = lax.broadcasted_iota(jnp.int32, (KB, KB), 1)
        earlier = jnp.where(c < r, 1.0, 0.0).astype(BF16)

        def body(jj, seen):
            sls = [pl.ds(pl.multiple_of((2 * jj + u) * KB, KB), KB) for u in range(2)]
            blks = [key_scr[sl, :] for sl in sls]
            eqs = [jnp.where(blk == t, 1.0, 0.0) for blk in blks]
            seens = [seen, seen + eqs[0].sum(axis=0, keepdims=True)]
            ranks = [_dot(earlier, eq.astype(BF16)) + sn for eq, sn in zip(eqs, seens)]
            for sl, blk, eq, rank in zip(sls, blks, eqs, ranks):
                demote = eq * jnp.where(rank >= allowed, 1.0, 0.0)
                key_scr[sl, :] = jnp.where(demote > 0.5, t - 1, blk)
            return seens[1] + eqs[1].sum(axis=0, keepdims=True)

        lax.fori_loop(0, (nb + 1) // 2, body, jnp.zeros((1, TQ), F32))

    def scores_to(slot, j, bias_rows, present):
        k0 = pl.multiple_of(j * KB, KB)
        kblk = kv_ref[0, pl.ds(k0, KB), :]
        sel = key_scr[pl.ds(k0, KB), :] >= (t if present is True else jnp.where(present, t, jnp.int32(2 ** 31 - 1)))
        for hd in range(SA_HEADS):
            s = _dot_t(kblk, q_ref[0, :, hd * LANES:(hd + 1) * LANES])
            if bias_rows is not None:
                s = s + nbias_ref[hd, bias_rows, :]
            s = jnp.where(sel, s, NEG_INF)
            s_scr[slot][hd] = s
            smax_scr[slot][hd] = jnp.broadcast_to(s.max(axis=0, keepdims=True), smax_scr[slot].shape[1:])

    def softmax_pv(slot, j, st):
        vt = vt_ref[:, pl.ds(pl.multiple_of(j * KB, KB), KB)]
        out = []
        for hd in range(SA_HEADS):
            m, l, acc = st[hd]
            mn = jnp.maximum(m, smax_scr[slot][hd][0:1])
            p = jnp.exp(s_scr[slot][hd] - mn)
            alpha = jnp.exp(m - mn)
            out.append((mn, alpha * l + p.sum(axis=0, keepdims=True), alpha * acc + _dot(vt, p.astype(BF16))))
        return tuple(out)

    st = tuple((jnp.full((1, TQ), NEG_INF, F32), jnp.zeros((1, TQ), F32), jnp.zeros((HEAD_DIM, TQ), F32))
               for _ in range(SA_HEADS))
    left = jnp.maximum(i - 1, 0)
    tails = [(left, lambda slot: scores_to(slot, left, slice(0, KB), i >= 1)),
             (i, lambda slot: scores_to(slot, i, slice(KB, 2 * KB), True))]
    st = _block_pipeline(left, i, lambda slot, j, present: scores_to(slot, j, None, present), tails, softmax_pv, st)
    y_t = jnp.concatenate([acc / l for _, l, acc in st], axis=0)
    o_ref[0] = _group_norm_t(y_t, g_ref[...])


def _sparse_attention(saq, sakv, svt, iq, ik, iwt, nbias, g):
    B, S, _ = saq.shape
    TQ = TQ_SA
    W = GROUP_W
    nt = S // TQ
    n_adm = (np.arange(S) // CHUNK + 1) * CHUNK
    zq = np.array([NormalDist().inv_cdf(1.0 - TOPK_MAX / n) if n > TOPK_MAX else 0.0 for n in n_adm], np.float32)
    zq = jnp.asarray(np.tile(zq[None, :], (8, 1)))
    return pl.pallas_call(
        _sa_kernel,
        grid=(B, nt),
        in_specs=[pl.BlockSpec((1, TQ, saq.shape[2]), lambda b, i: (b, i, 0)),
                  pl.BlockSpec((1, S, sakv.shape[2]), lambda b, i: (b, 0, 0)),
                  pl.BlockSpec((HEAD_DIM, S), lambda b, i: (0, b)),
                  pl.BlockSpec((1, TQ, iq.shape[2]), lambda b, i: (b, i, 0)),
                  pl.BlockSpec((1, S, ik.shape[2]), lambda b, i: (b, 0, 0)),
                  pl.BlockSpec((IWT_ROWS, TQ), lambda b, i: (0, b * nt + i)),
                  pl.BlockSpec((8, TQ), lambda b, i: (0, i)),
                  pl.BlockSpec(nbias.shape, lambda b, i: (0, 0, 0)),
                  pl.BlockSpec((1, W), lambda b, i: (0, 0))],
        out_specs=pl.BlockSpec((1, TQ, W), lambda b, i: (b, i, 0)),
        out_shape=jax.ShapeDtypeStruct((B, S, W), BF16),
        scratch_shapes=([pltpu.VMEM((S, TQ), jnp.int32)] + [pltpu.VMEM((SA_HEADS, KB_SA, TQ), F32)] * 2
                        + [pltpu.VMEM((SA_HEADS, 8, TQ), F32)] * 2),
        compiler_params=_cparams(2),
        name="sparse_attention",
    )(saq, sakv, svt, iq, ik, iwt, zq, nbias, g)


def _group_norm_t(y_t, g):
    inv = lax.rsqrt(jnp.mean(y_t * y_t, axis=0, keepdims=True) + EPS)
    return ((y_t * inv).T * g).astype(BF16)


def _block_pipeline(n_plain, last_blk, score_plain, tails, softmax, st):
    off = n_plain % 2
    n_loop = jnp.maximum((n_plain + off) // 2 - 1, 0)

    def blk(pos):
        return jnp.clip(pos - off, 0, last_blk)

    def body(pp, st):
        pos = 2 * pp
        score_plain(1, blk(pos + 1), True)
        st = softmax(0, blk(pos), st)
        score_plain(0, blk(pos + 2), True)
        return softmax(1, blk(pos + 1), st)

    score_plain(0, blk(0), jnp.logical_and(n_plain >= 1, off == 0))
    st = lax.fori_loop(0, n_loop, body, st)
    e0 = 2 * n_loop
    slot, pending = 0, blk(e0)
    steps = [(blk(e0 + 1), lambda s: score_plain(s, blk(e0 + 1), n_plain >= 1))] + list(tails)
    for nxt, score_fn in steps:
        score_fn(1 - slot)
        st = softmax(slot, pending, st)
        slot, pending = 1 - slot, nxt
    return softmax(slot, pending, st)


def _mla_kernel(q_ref, k_ref, vt_ref, g_ref, o_ref, s0_scr, s1_scr, smax0_scr, smax1_scr):
    i = pl.program_id(1)
    TQ, KB = TQ_MLA, KB_MLA
    R = TQ // KB
    cshift = CHUNK.bit_length() - 1
    kch = lax.broadcasted_iota(jnp.int32, (KB, TQ), 0) >> cshift
    qch = lax.broadcasted_iota(jnp.int32, (KB, TQ), 1) >> cshift
    s_scr, smax_scr = (s0_scr, s1_scr), (smax0_scr, smax1_scr)

    def scores_to(slot, j, keep):
        k0 = pl.multiple_of(j * KB, KB)
        for hd in range(MLA_HEADS):
            cols = slice(hd * LANES, (hd + 1) * LANES)
            s = _dot_t(k_ref[0, pl.ds(k0, KB), cols], q_ref[0, :, cols])
            if keep is not None:
                s = jnp.where(keep, s, NEG_INF)
            s_scr[slot][hd] = s
            smax_scr[slot][hd] = jnp.broadcast_to(s.max(axis=0, keepdims=True), smax_scr[slot].shape[1:])

    def score_plain(slot, j, present):
        scores_to(slot, j, None if present is True else kch >= jnp.where(present, 0, TQ))

    def softmax_pv(slot, j, st):
        k0 = pl.multiple_of(j * KB, KB)
        out = []
        for hd in range(MLA_HEADS):
            m, l, acc = st[hd]
            mn = jnp.maximum(m, smax_scr[slot][hd][0:1])
            p = jnp.exp(s_scr[slot][hd] - mn)
            alpha = jnp.exp(m - mn)
            vt = vt_ref[hd * MLA_V:(hd + 1) * MLA_V, pl.ds(k0, KB)]
            out.append((mn, alpha * l + p.sum(axis=0, keepdims=True),
                        alpha * acc + _dot(vt, p.astype(BF16))))
        return tuple(out)

    st = tuple((jnp.full((1, TQ), NEG_INF, F32), jnp.zeros((1, TQ), F32), jnp.zeros((MLA_V, TQ), F32))
               for _ in range(MLA_HEADS))
    tails = [(R * i + d, functools.partial(
        lambda slot, d: scores_to(slot, R * i + d, kch + d * (KB // CHUNK) <= qch), d=d)) for d in range(R)]
    st = _block_pipeline(R * i, R * i + R - 1, score_plain, tails, softmax_pv, st)
    o_ref[0] = _group_norm_t(jnp.concatenate([acc / l for _, l, acc in st], axis=0), g_ref[...])


def _latent_attention(mq, mk, mvt, g):
    B, S, _ = mq.shape
    TQ = TQ_MLA
    W = GROUP_W
    return pl.pallas_call(
        _mla_kernel,
        grid=(B, S // TQ),
        in_specs=[pl.BlockSpec((1, TQ, mq.shape[2]), lambda b, i: (b, i, 0)),
                  pl.BlockSpec((1, S, mk.shape[2]), lambda b, i: (b, 0, 0)),
                  pl.BlockSpec((W, S), lambda b, i: (0, b)),
                  pl.BlockSpec((1, W), lambda b, i: (0, 0))],
        out_specs=pl.BlockSpec((1, TQ, W), lambda b, i: (b, i, 0)),
        out_shape=jax.ShapeDtypeStruct((B, S, W), BF16),
        scratch_shapes=[pltpu.VMEM((MLA_HEADS, KB_MLA, TQ), F32)] * 2 + [pltpu.VMEM((MLA_HEADS, 8, TQ), F32)] * 2,
        compiler_params=_cparams(2),
        name="latent_attention",
    )(mq, mk, mvt, g)


def _ffn_kernel(ya_ref, yb_ref, yc_ref, yd_ref, x_ref, mod_ref, wout_ref, gffn_ref, w1_ref, w3_ref, w2_ref,
                gfin_ref, o_ref, acc_scr, *, final):
    gt1 = mod_ref[0, 2:3, :]
    sh2 = mod_ref[0, 3:4, :]
    sc2 = mod_ref[0, 4:5, :]
    gt2 = mod_ref[0, 5:6, :]
    attn = _dot(ya_ref[...], wout_ref[0:GROUP_W, :])
    for gi, y_ref in enumerate((yb_ref, yc_ref, yd_ref), start=1):
        attn = attn + _dot(y_ref[...], wout_ref[gi * GROUP_W:(gi + 1) * GROUP_W, :])
    x1 = x_ref[...] + gt1 * attn
    h = (_rms(x1, gffn_ref[...]) * (1.0 + sc2) + sh2).astype(BF16)
    for ci in range(D_FF // FF_CHUNK):
        cols = slice(ci * FF_CHUNK, (ci + 1) * FF_CHUNK)
        a = _dot(h, w1_ref[:, cols])
        gate = (a * jax.nn.sigmoid(a) * _dot(h, w3_ref[:, cols])).astype(BF16)
        part = _dot(gate, w2_ref[cols, :])
        if ci == 0:
            acc_scr[...] = part
        else:
            acc_scr[...] += part
    x2 = x1 + gt2 * acc_scr[...]
    o_ref[...] = _rms(x2, gfin_ref[...]) if final else x2


def _out_ffn(ys, x2, mod, wout, gffn, w1, w3, w2, gfin, S, final):
    N, D = x2.shape
    TM = TM_FFN
    nt = S // TM

    def full(a):
        return pl.BlockSpec(a.shape, lambda i: (0,) * a.ndim, pipeline_mode=pl.Buffered(1))

    def tok(w):
        return pl.BlockSpec((TM, w), lambda i: (i, 0))

    return pl.pallas_call(
        functools.partial(_ffn_kernel, final=final),
        grid=(N // TM,),
        in_specs=[tok(GROUP_W)] * 4 + [tok(D), pl.BlockSpec((1, 6, D), lambda i: (i // nt, 0, 0)),
                                       full(wout), full(gffn), full(w1), full(w3), full(w2), full(gfin)],
        out_specs=tok(D),
        out_shape=jax.ShapeDtypeStruct((N, D), F32),
        scratch_shapes=[pltpu.VMEM((TM, D), F32)],
        compiler_params=_cparams(1),
        name="out_ffn_final" if final else "out_ffn",
    )(*ys, x2, mod, wout, gffn, w1, w3, w2, gfin)


def _t5_bucket(rel):
    nb = T5_BUCKETS // 2
    max_exact = nb // 2
    ret = jnp.where(rel > 0, nb, 0)
    n = jnp.abs(rel)
    nf = jnp.maximum(n, 1).astype(jnp.float32)
    large = max_exact + (jnp.log(nf / max_exact) / math.log(T5_MAX_DIST / max_exact)
                         * (nb - max_exact)).astype(jnp.int32)
    large = jnp.minimum(large, nb - 1)
    return ret + jnp.where(n < max_exact, n, large)


def _rope_tables(S):
    half = MLA_ROPE // 2
    freqs = ROPE_BASE ** (-jnp.arange(half, dtype=F32) / half)
    ang = jnp.arange(S, dtype=jnp.int32).astype(F32)[:, None] * freqs[None, :]
    cos, sin = jnp.cos(ang), jnp.sin(ang)
    cos2 = jnp.concatenate([cos, cos], axis=1)
    sin2 = jnp.concatenate([-sin, sin], axis=1)
    zeros = jnp.zeros((S, LANES - MLA_NOPE - MLA_ROPE), F32)
    scale = (MLA_NOPE + MLA_ROPE) ** -0.5
    cosq = jnp.concatenate([jnp.full((S, MLA_NOPE), scale, F32), cos2 * scale, zeros], axis=1)
    sinq = jnp.concatenate([jnp.zeros((S, MLA_NOPE), F32), sin2 * scale, zeros], axis=1)
    cosk = jnp.concatenate([jnp.zeros((S, MLA_NOPE), F32), cos2, zeros], axis=1)
    sink = jnp.concatenate([jnp.zeros((S, MLA_NOPE), F32), sin2, zeros], axis=1)
    return cosq, sinq, cosk, sink


def _pack_in_weight(w):
    part = {n: w[:, IN_OFFS[k]:IN_OFFS[k + 1]] for k, n in enumerate(
        ('pool_u', 'ca_q', 'ca_k', 'ca_v', 'sa_q', 'sa_k', 'sa_v', 'idx_q', 'idx_k', 'idx_w',
         'mla_cq', 'mla_ckv', 'mla_kr'))}
    D = w.shape[0]
    z = lambda n: jnp.zeros((D, n), F32)
    qscale = HEAD_DIM ** -0.5
    saq = part['sa_q'].reshape(D, SA_HEADS, HEAD_DIM) * qscale
    saq = jnp.concatenate([saq, jnp.zeros_like(saq)], axis=2).reshape(D, SA_HEADS * LANES)
    kr = part['mla_kr']
    kr_swap = jnp.concatenate([kr[:, MLA_ROPE // 2:], kr[:, :MLA_ROPE // 2]], axis=1)
    pad_r = LANES - MLA_NOPE - MLA_ROPE
    cols = [part['pool_u'], part['ca_q'] * qscale, part['ca_k'], saq,
            part['sa_k'], part['sa_v'], part['idx_q'],
            part['idx_k'], z(IDX_DIM), z(IDX_DIM), part['idx_k'],
            part['mla_cq'], part['mla_ckv'],
            z(MLA_NOPE), kr, z(pad_r), z(MLA_NOPE), kr_swap, z(pad_r)]
    out = jnp.concatenate(cols, axis=1)
    assert out.shape[1] == C_END
    wt = jnp.concatenate([part['idx_w'].T, jnp.zeros((IWT_ROWS - IDX_HEADS, D), F32), part['sa_v'].T,
                          part['ca_v'].T], axis=0)
    return out.astype(BF16), wt.astype(BF16)


def _pack_mla_weights(w_uq, w_ukv):
    R = w_uq.shape[0]
    pad = jnp.zeros((R, MLA_HEADS, LANES - MLA_NOPE - MLA_ROPE), F32)
    rope_w = w_uq[:, :, MLA_NOPE:]
    rope_sw = jnp.concatenate([rope_w[:, :, MLA_ROPE // 2:], rope_w[:, :, :MLA_ROPE // 2]], axis=2)
    wq = jnp.concatenate([w_uq, pad], axis=2).reshape(R, MLA_HEADS * LANES)
    wqs = jnp.concatenate([jnp.zeros((R, MLA_HEADS, MLA_NOPE), F32), rope_sw, pad],
                          axis=2).reshape(R, MLA_HEADS * LANES)
    Rk = w_ukv.shape[0]
    wk = jnp.concatenate([w_ukv[:, :, :MLA_NOPE], jnp.zeros((Rk, MLA_HEADS, LANES - MLA_NOPE), F32)],
                         axis=2).reshape(Rk, MLA_HEADS * LANES)
    wvt = w_ukv[:, :, MLA_NOPE:].reshape(Rk, MLA_HEADS * MLA_V).T
    return wq.astype(BF16), wqs.astype(BF16), wk.astype(BF16), wvt.astype(BF16)


def _toeplitz(vec, rows, cols):
    L = vec.shape[-1]
    assert cols <= L - 1
    flat = jnp.tile(vec, (1, rows))[:, :rows * (L - 1)]
    return flat.reshape(vec.shape[0], rows, L - 1)[:, :, :cols]


def _signed_mod_range(L, hi):
    d = np.arange(L)
    return np.where(d <= hi, d, d - L)


def _band_bias(rel_table):
    L = CA_WIN + TQ_CA
    e = _signed_mod_range(L, TQ_CA - 1)
    ridx = np.clip(CA_LEFT_CHUNKS * CHUNK + e, -(CHUNK - 1), CA_MAX_REL) + (CHUNK - 1)
    bias = _toeplitz(rel_table[:, ridx].astype(F32), CA_WIN, TQ_CA)
    kc = np.arange(CA_WIN)[:, None] // CHUNK
    qc = np.arange(TQ_CA)[None, :] // CHUNK + CA_LEFT_CHUNKS
    valid = (kc <= qc) & (kc >= qc - CA_LEFT_CHUNKS)
    return jnp.where(valid[None], bias, NEG_INF)


def _t5_bias(t5_table):
    TQ = TQ_SA
    L = 3 * TQ
    e = _signed_mod_range(L, TQ - 1)
    rel = jnp.asarray(-e - TQ, jnp.int32)
    far = t5_table[_t5_bucket(jnp.int32(-(TQ + 1)))].astype(F32)
    vec = (t5_table[_t5_bucket(rel)].astype(F32) - far[None, :]).T
    return _toeplitz(vec, 2 * TQ, TQ)


def kernel(x, c, t5_table, w_mod, b_mod, g_mix, w_in, pool_w, pool_scale, ca_rel, mla_g_cq, mla_g_ckv,
           mla_w_uq, mla_w_ukv, g_group, w_out, g_ffn, ffn_w1, ffn_w3, ffn_w2, g_final):
    B, S, D = x.shape
    assert D == D_MODEL and S % TM_PROJ == 0 and S % TQ_SA == 0 and S >= 4 * TOPK_MAX
    N = B * S
    mod_all = _modulation(c, w_mod, b_mod)
    rope_tabs = _rope_tables(S)
    nbias = _t5_bias(t5_table)
    row = lambda v: v.reshape(1, -1).astype(F32)
    x2 = x.reshape(N, D)
    for l in range(DEPTH):
        mod = mod_all[l].reshape(B, 6, D)
        w1, wt = _pack_in_weight(w_in[l])
        wq, wqs, wk, wvt = _pack_mla_weights(mla_w_uq[l], mla_w_ukv[l])
        gg = g_group[l].reshape(4, 1, GROUP_W).astype(F32)
        wbd = jax.scipy.linalg.block_diag(*[pool_w[l, gi] for gi in range(len(POOL_WINDOWS))]).astype(BF16)
        (y_a, ca, saq, sakv, iq, ik, iwt, svt, cavt, mq, mk, mvt) = _inproj(
            x2, mod, row(g_mix[l]), w1, wt, row(mla_g_cq[l]), row(mla_g_ckv[l]), wq, wqs, wk, wvt, rope_tabs,
            wbd, row(pool_scale[l]), gg[0], S)
        bsw = lambda a: a.reshape(B, S, a.shape[-1])
        y_b = _chunk_attention(bsw(ca), cavt, _band_bias(ca_rel[l]), gg[1])
        y_c = _sparse_attention(bsw(saq), bsw(sakv), svt, bsw(iq), bsw(ik), iwt, nbias, gg[2])
        y_d = _latent_attention(bsw(mq), bsw(mk), mvt, gg[3])
        ys = [y.reshape(N, GROUP_W) for y in (y_a, y_b, y_c, y_d)]
        x2 = _out_ffn(ys, x2, mod, w_out[l].astype(BF16), row(g_ffn[l]), ffn_w1[l].astype(BF16),
                      ffn_w3[l].astype(BF16), ffn_w2[l].astype(BF16), row(g_final), S,
                      final=(l == DEPTH - 1))
    return x2.reshape(B, S, D)
```
